```python
import math
import jax, jax.numpy as jnp
from jax import lax
import numpy as np

D_MODEL = 1024
BATCH = 16
SEQ = 4096
DEPTH = 2

CONV_WIDTH = D_MODEL // 2
CONV_KERNEL = 31
N_HEADS = 8
HEAD_DIM = 64
ATT_WIDTH = N_HEADS * HEAD_DIM
N_BRANCH = 2
D_FF = -(-8 * D_MODEL // (3 * 256)) * 256
IN_COLS = 2 * CONV_WIDTH + 3 * ATT_WIDTH + N_BRANCH * D_MODEL
Q_BLOCK = 128
EPS = 1e-6

kernel_name = "hybrid_conformer_stickbreaking_block"


def rmsnorm(x, g):
    xf = x.astype(jnp.float32)
    y = xf * lax.rsqrt(jnp.mean(xf * xf, axis=-1, keepdims=True) + EPS)
    return (y * g.astype(jnp.float32)).astype(x.dtype)


def layernorm(x, g, b):
    xf = x.astype(jnp.float32)
    mu = jnp.mean(xf, axis=-1, keepdims=True)
    xc = xf - mu
    y = xc * lax.rsqrt(jnp.mean(xc * xc, axis=-1, keepdims=True) + EPS)
    return (y * g.astype(jnp.float32) + b.astype(jnp.float32)).astype(x.dtype)


def causal_depthwise_conv(u, w, b):
    ch = u.shape[-1]
    y = lax.conv_general_dilated(
        u, w[:, None, :].astype(u.dtype), window_strides=(1,), padding=[(CONV_KERNEL - 1, 0)],
        dimension_numbers=("NWC", "WIO", "NWC"), feature_group_count=ch)
    return y + b


def stick_breaking_attention(q, k, v):
    s_len = q.shape[1]
    scale = 1.0 / math.sqrt(q.shape[-1])
    qf = q.astype(jnp.float32) * scale
    kf = k.astype(jnp.float32)
    vf = v.astype(jnp.float32)
    outs = []
    for start in range(0, s_len, Q_BLOCK):
        end = start + Q_BLOCK
        z = jnp.einsum("bqhd,bkhd->bhqk", qf[:, start:end], kf[:, :end])
        t_idx = start + jnp.arange(Q_BLOCK)[:, None]
        s_idx = jnp.arange(end)[None, :]
        causal = s_idx < t_idx
        log_keep = jnp.where(causal, jax.nn.log_sigmoid(-z), 0.0)
        log_after = lax.cumsum(log_keep, axis=3, reverse=True) - log_keep
        a = jnp.where(causal, jnp.exp(jax.nn.log_sigmoid(z) + log_after), 0.0)
        outs.append(jnp.einsum("bhqk,bkhd->bqhd", a, vf[:, :end]))
    return jnp.concatenate(outs, axis=1).astype(q.dtype)


def hybrid_mixer(h, w_in, conv_w, conv_b, conv_ln_g, conv_ln_b, w_conv_out, w_att_out, w_o):
    b, s, _ = h.shape
    proj = h @ w_in
    glu_in, qkv, gates = jnp.split(proj, [2 * CONV_WIDTH, 2 * CONV_WIDTH + 3 * ATT_WIDTH], axis=-1)
    u_val, u_gate = jnp.split(glu_in, 2, axis=-1)
    u = u_val * jax.nn.sigmoid(u_gate)
    u = causal_depthwise_conv(u, conv_w, conv_b)
    u = jax.nn.silu(layernorm(u, conv_ln_g, conv_ln_b))
    y_conv = u @ w_conv_out
    qkv = qkv.reshape(b, s, 3, N_HEADS, HEAD_DIM)
    o = stick_breaking_attention(qkv[:, :, 0], qkv[:, :, 1], qkv[:, :, 2])
    y_att = o.reshape(b, s, ATT_WIDTH) @ w_att_out
    g_conv, g_att = jnp.split(gates, 2, axis=-1)
    merged = jax.nn.sigmoid(g_conv) * y_conv + jax.nn.sigmoid(g_att) * y_att
    return merged @ w_o


def swiglu_ffn(h, w_ffn_in, w_ffn_out):
    gate, up = jnp.split(h @ w_ffn_in, 2, axis=-1)
    return (jax.nn.silu(gate) * up) @ w_ffn_out


def _normal(k, shape, scale):
    return jax.random.normal(k, shape, jnp.float32) * scale


def _fwd_setup_inputs(seed: int = 0) -> dict:
    key = jax.random.key(seed)
    ks = jax.random.split(key, 20)
    D = D_MODEL
    return {
        "x": _normal(ks[0], (BATCH, SEQ, D), 1.0),
        "c": _normal(ks[1], (BATCH, D), 1.0),
        "ada_w": _normal(ks[2], (DEPTH, D, 6 * D), D ** -0.5),
        "ada_b": _normal(ks[3], (DEPTH, 6 * D), 0.02),
        "pre_mix_g": 1.0 + _normal(ks[4], (DEPTH, D), 0.02),
        "post_mix_g": 1.0 + _normal(ks[5], (DEPTH, D), 0.02),
        "pre_ffn_g": 1.0 + _normal(ks[6], (DEPTH, D), 0.02),
        "post_ffn_g": 1.0 + _normal(ks[7], (DEPTH, D), 0.02),
        "w_in": _normal(ks[8], (DEPTH, D, IN_COLS), D ** -0.5),
        "conv_w": _normal(ks[9], (DEPTH, CONV_KERNEL, CONV_WIDTH), CONV_KERNEL ** -0.5),
        "conv_b": _normal(ks[10], (DEPTH, CONV_WIDTH), 0.02),
        "conv_ln_g": 1.0 + _normal(ks[11], (DEPTH, CONV_WIDTH), 0.02),
        "conv_ln_b": _normal(ks[12], (DEPTH, CONV_WIDTH), 0.02),
        "w_conv_out": _normal(ks[13], (DEPTH, CONV_WIDTH, D), CONV_WIDTH ** -0.5),
        "w_att_out": _normal(ks[14], (DEPTH, ATT_WIDTH, D), ATT_WIDTH ** -0.5),
        "w_o": _normal(ks[15], (DEPTH, D, D), D ** -0.5),
        "w_ffn_in": _normal(ks[16], (DEPTH, D, 2 * D_FF), D ** -0.5),
        "w_ffn_out": _normal(ks[17], (DEPTH, D_FF, D), D_FF ** -0.5),
    }


def _fwd_reference(x, c, ada_w, ada_b, pre_mix_g, post_mix_g, pre_ffn_g, post_ffn_g, w_in, conv_w, conv_b,
              conv_ln_g, conv_ln_b, w_conv_out, w_att_out, w_o, w_ffn_in, w_ffn_out):
    c_act = jax.nn.silu(c)
    for l in range(DEPTH):
        mod = c_act @ ada_w[l] + ada_b[l]
        sh1, sc1, ga1, sh2, sc2, ga2 = [m[:, None, :] for m in jnp.split(mod, 6, axis=-1)]
        h = rmsnorm(x, pre_mix_g[l]) * (1.0 + sc1) + sh1
        y = hybrid_mixer(h, w_in[l], conv_w[l], conv_b[l], conv_ln_g[l], conv_ln_b[l],
                         w_conv_out[l], w_att_out[l], w_o[l])
        x = x + ga1 * rmsnorm(y, post_mix_g[l])
        h = rmsnorm(x, pre_ffn_g[l]) * (1.0 + sc2) + sh2
        y = swiglu_ffn(h, w_ffn_in[l], w_ffn_out[l])
        x = x + ga2 * rmsnorm(y, post_ffn_g[l])
    return x


import jax as _jax
import jax.numpy as _jnp

TWIN_FORMAT = 'train_step'
FWD_PARAMS = ['x', 'c', 'ada_w', 'ada_b', 'pre_mix_g', 'post_mix_g', 'pre_ffn_g', 'post_ffn_g', 'w_in', 'conv_w', 'conv_b', 'conv_ln_g', 'conv_ln_b', 'w_conv_out', 'w_att_out', 'w_o', 'w_ffn_in', 'w_ffn_out']
TWIN_WEIGHTS = ['ada_w', 'ada_b', 'pre_mix_g', 'post_mix_g', 'pre_ffn_g', 'post_ffn_g', 'w_in', 'conv_w', 'conv_b', 'conv_ln_g', 'conv_ln_b', 'w_conv_out', 'w_att_out', 'w_o', 'w_ffn_in', 'w_ffn_out']
TWIN_DIFF_INPUT = 'x'
TWIN_INPUTS = ['x', 'c', 'ada_w', 'ada_b', 'pre_mix_g', 'post_mix_g', 'pre_ffn_g', 'post_ffn_g', 'w_in', 'conv_w', 'conv_b', 'conv_ln_g', 'conv_ln_b', 'w_conv_out', 'w_att_out', 'w_o', 'w_ffn_in', 'w_ffn_out', 'loss_target', 'm_ada_w', 'm_ada_b', 'm_pre_mix_g', 'm_post_mix_g', 'm_pre_ffn_g', 'm_post_ffn_g', 'm_w_in', 'm_conv_w', 'm_conv_b', 'm_conv_ln_g', 'm_conv_ln_b', 'm_w_conv_out', 'm_w_att_out', 'm_w_o', 'm_w_ffn_in', 'm_w_ffn_out', 'v_ada_w', 'v_ada_b', 'v_pre_mix_g', 'v_post_mix_g', 'v_pre_ffn_g', 'v_post_ffn_g', 'v_w_in', 'v_conv_w', 'v_conv_b', 'v_conv_ln_g', 'v_conv_ln_b', 'v_w_conv_out', 'v_w_att_out', 'v_w_o', 'v_w_ffn_in', 'v_w_ffn_out']
TWIN_OUTPUTS = ['loss', 'grad_x', 'grad_ada_w', 'grad_ada_b', 'grad_pre_mix_g', 'grad_post_mix_g', 'grad_pre_ffn_g', 'grad_post_ffn_g', 'grad_w_in', 'grad_conv_w', 'grad_conv_b', 'grad_conv_ln_g', 'grad_conv_ln_b', 'grad_w_conv_out', 'grad_w_att_out', 'grad_w_o', 'grad_w_ffn_in', 'grad_w_ffn_out', 'delta_ada_w', 'delta_ada_b', 'delta_pre_mix_g', 'delta_post_mix_g', 'delta_pre_ffn_g', 'delta_post_ffn_g', 'delta_w_in', 'delta_conv_w', 'delta_conv_b', 'delta_conv_ln_g', 'delta_conv_ln_b', 'delta_w_conv_out', 'delta_w_att_out', 'delta_w_o', 'delta_w_ffn_in', 'delta_w_ffn_out', 'new_m_ada_w', 'new_m_ada_b', 'new_m_pre_mix_g', 'new_m_post_mix_g', 'new_m_pre_ffn_g', 'new_m_post_ffn_g', 'new_m_w_in', 'new_m_conv_w', 'new_m_conv_b', 'new_m_conv_ln_g', 'new_m_conv_ln_b', 'new_m_w_conv_out', 'new_m_w_att_out', 'new_m_w_o', 'new_m_w_ffn_in', 'new_m_w_ffn_out', 'new_v_ada_w', 'new_v_ada_b', 'new_v_pre_mix_g', 'new_v_post_mix_g', 'new_v_pre_ffn_g', 'new_v_post_ffn_g', 'new_v_w_in', 'new_v_conv_w', 'new_v_conv_b', 'new_v_conv_ln_g', 'new_v_conv_ln_b', 'new_v_w_conv_out', 'new_v_w_att_out', 'new_v_w_o', 'new_v_w_ffn_in', 'new_v_w_ffn_out']
TWIN_LEAF_KINDS = {'loss': 'loss', 'grad_x': 'grad_x', 'grad_ada_w': 'grad_w', 'grad_ada_b': 'grad_w', 'grad_pre_mix_g': 'grad_w', 'grad_post_mix_g': 'grad_w', 'grad_pre_ffn_g': 'grad_w', 'grad_post_ffn_g': 'grad_w', 'grad_w_in': 'grad_w', 'grad_conv_w': 'grad_w', 'grad_conv_b': 'grad_w', 'grad_conv_ln_g': 'grad_w', 'grad_conv_ln_b': 'grad_w', 'grad_w_conv_out': 'grad_w', 'grad_w_att_out': 'grad_w', 'grad_w_o': 'grad_w', 'grad_w_ffn_in': 'grad_w', 'grad_w_ffn_out': 'grad_w', 'delta_ada_w': 'delta_w', 'delta_ada_b': 'delta_w', 'delta_pre_mix_g': 'delta_w', 'delta_post_mix_g': 'delta_w', 'delta_pre_ffn_g': 'delta_w', 'delta_post_ffn_g': 'delta_w', 'delta_w_in': 'delta_w', 'delta_conv_w': 'delta_w', 'delta_conv_b': 'delta_w', 'delta_conv_ln_g': 'delta_w', 'delta_conv_ln_b': 'delta_w', 'delta_w_conv_out': 'delta_w', 'delta_w_att_out': 'delta_w', 'delta_w_o': 'delta_w', 'delta_w_ffn_in': 'delta_w', 'delta_w_ffn_out': 'delta_w', 'new_m_ada_w': 'new_m', 'new_m_ada_b': 'new_m', 'new_m_pre_mix_g': 'new_m', 'new_m_post_mix_g': 'new_m', 'new_m_pre_ffn_g': 'new_m', 'new_m_post_ffn_g': 'new_m', 'new_m_w_in': 'new_m', 'new_m_conv_w': 'new_m', 'new_m_conv_b': 'new_m', 'new_m_conv_ln_g': 'new_m', 'new_m_conv_ln_b': 'new_m', 'new_m_w_conv_out': 'new_m', 'new_m_w_att_out': 'new_m', 'new_m_w_o': 'new_m', 'new_m_w_ffn_in': 'new_m', 'new_m_w_ffn_out': 'new_m', 'new_v_ada_w': 'new_v', 'new_v_ada_b': 'new_v', 'new_v_pre_mix_g': 'new_v', 'new_v_post_mix_g': 'new_v', 'new_v_pre_ffn_g': 'new_v', 'new_v_post_ffn_g': 'new_v', 'new_v_w_in': 'new_v', 'new_v_conv_w': 'new_v', 'new_v_conv_b': 'new_v', 'new_v_conv_ln_g': 'new_v', 'new_v_conv_ln_b': 'new_v', 'new_v_w_conv_out': 'new_v', 'new_v_w_att_out': 'new_v', 'new_v_w_o': 'new_v', 'new_v_w_ffn_in': 'new_v', 'new_v_w_ffn_out': 'new_v'}


def _forward(args):
    return _fwd_reference(*[args[k] for k in FWD_PARAMS])


def _output_shape():
    out = _jax.eval_shape(lambda: _forward(_fwd_setup_inputs(0)))
    return out.shape, out.dtype

N_MICROBATCH = 1
ADAM_LR = 0.001
ADAM_B1 = 0.9
ADAM_B2 = 0.999
ADAM_EPS = 1e-08
ADAM_WD = 0.01
ADAM_STEP = 10
PER_EXAMPLE_BATCH_AXIS = {'x': 0, 'c': 0, 'loss_target': 0}
SHARED_INPUTS = []
_WEIGHT_DTYPES = {'ada_w': _jnp.float32, 'ada_b': _jnp.float32, 'pre_mix_g': _jnp.float32, 'post_mix_g': _jnp.float32, 'pre_ffn_g': _jnp.float32, 'post_ffn_g': _jnp.float32, 'w_in': _jnp.float32, 'conv_w': _jnp.float32, 'conv_b': _jnp.float32, 'conv_ln_g': _jnp.float32, 'conv_ln_b': _jnp.float32, 'w_conv_out': _jnp.float32, 'w_att_out': _jnp.float32, 'w_o': _jnp.float32, 'w_ffn_in': _jnp.float32, 'w_ffn_out': _jnp.float32}
MOMENT_SCALE = {'ada_w': 8.650301e+00, 'ada_b': 1.555932e+01, 'pre_mix_g': 1.620582e+00, 'post_mix_g': 3.201479e+01, 'pre_ffn_g': 2.368825e+00, 'post_ffn_g': 3.046912e+01, 'w_in': 3.392613e+00, 'conv_w': 3.291845e+00, 'conv_b': 1.016808e+01, 'conv_ln_g': 6.494046e+00, 'conv_ln_b': 8.637231e+00, 'w_conv_out': 3.392518e+00, 'w_att_out': 6.981920e+00, 'w_o': 7.845902e+00, 'w_ffn_in': 1.886132e+00, 'w_ffn_out': 3.525204e+00}


def _to_microbatches(a, axis):
    t = _jnp.moveaxis(a, axis, 0)
    t = t.reshape((N_MICROBATCH, t.shape[0] // N_MICROBATCH) + t.shape[1:])
    return _jnp.moveaxis(t, 1, axis + 1)


def setup_inputs(seed: int = 0) -> dict:
    inp = _fwd_setup_inputs(seed)
    key = _jax.random.fold_in(_jax.random.key(seed), 7919)
    shape, _ = _output_shape()
    out = dict(inp)
    out["loss_target"] = _jax.random.normal(_jax.random.fold_in(key, 0), shape, _jnp.float32)
    for i, name in enumerate(TWIN_WEIGHTS):
        w = inp[name].astype(_jnp.float32)
        if MOMENT_SCALE is None:
            s = _jnp.sqrt(_jnp.mean(_jnp.square(w)) + 1e-30)
        else:
            s = MOMENT_SCALE[name]
        km, kv = _jax.random.split(_jax.random.fold_in(key, i + 1))
        out[name] = w
        out["m_" + name] = s * _jax.random.normal(km, w.shape, _jnp.float32)
        out["v_" + name] = (s * s) * _jax.random.uniform(kv, w.shape, _jnp.float32, 0.5, 1.5)
    if N_MICROBATCH > 1:
        for name, axis in PER_EXAMPLE_BATCH_AXIS.items():
            out[name] = _to_microbatches(out[name], axis)
    return {'x': out['x'], 'c': out['c'], 'ada_w': out['ada_w'], 'ada_b': out['ada_b'], 'pre_mix_g': out['pre_mix_g'], 'post_mix_g': out['post_mix_g'], 'pre_ffn_g': out['pre_ffn_g'], 'post_ffn_g': out['post_ffn_g'], 'w_in': out['w_in'], 'conv_w': out['conv_w'], 'conv_b': out['conv_b'], 'conv_ln_g': out['conv_ln_g'], 'conv_ln_b': out['conv_ln_b'], 'w_conv_out': out['w_conv_out'], 'w_att_out': out['w_att_out'], 'w_o': out['w_o'], 'w_ffn_in': out['w_ffn_in'], 'w_ffn_out': out['w_ffn_out'], 'loss_target': out['loss_target'], 'm_ada_w': out['m_ada_w'], 'm_ada_b': out['m_ada_b'], 'm_pre_mix_g': out['m_pre_mix_g'], 'm_post_mix_g': out['m_post_mix_g'], 'm_pre_ffn_g': out['m_pre_ffn_g'], 'm_post_ffn_g': out['m_post_ffn_g'], 'm_w_in': out['m_w_in'], 'm_conv_w': out['m_conv_w'], 'm_conv_b': out['m_conv_b'], 'm_conv_ln_g': out['m_conv_ln_g'], 'm_conv_ln_b': out['m_conv_ln_b'], 'm_w_conv_out': out['m_w_conv_out'], 'm_w_att_out': out['m_w_att_out'], 'm_w_o': out['m_w_o'], 'm_w_ffn_in': out['m_w_ffn_in'], 'm_w_ffn_out': out['m_w_ffn_out'], 'v_ada_w': out['v_ada_w'], 'v_ada_b': out['v_ada_b'], 'v_pre_mix_g': out['v_pre_mix_g'], 'v_post_mix_g': out['v_post_mix_g'], 'v_pre_ffn_g': out['v_pre_ffn_g'], 'v_post_ffn_g': out['v_post_ffn_g'], 'v_w_in': out['v_w_in'], 'v_conv_w': out['v_conv_w'], 'v_conv_b': out['v_conv_b'], 'v_conv_ln_g': out['v_conv_ln_g'], 'v_conv_ln_b': out['v_conv_ln_b'], 'v_w_conv_out': out['v_w_conv_out'], 'v_w_att_out': out['v_w_att_out'], 'v_w_o': out['v_w_o'], 'v_w_ffn_in': out['v_w_ffn_in'], 'v_w_ffn_out': out['v_w_ffn_out']}


def _loss(weights, diff, rest, loss_target):
    with _jax.named_scope("forward"):
        args = {**rest, TWIN_DIFF_INPUT: diff, **{k: w.astype(_WEIGHT_DTYPES[k]) for k, w in weights.items()}}
        y = _forward(args)
    with _jax.named_scope("loss_head"):
        err = _jnp.square(y.astype(_jnp.float32) - loss_target)
        return 0.5 * _jnp.sum(_jnp.mean(err, axis=-1)) if err.ndim else 0.5 * err


def _adamw(w, g, m, v):
    m = ADAM_B1 * m + (1.0 - ADAM_B1) * g
    v = ADAM_B2 * v + (1.0 - ADAM_B2) * _jnp.square(g)
    m_hat = m / (1.0 - ADAM_B1 ** ADAM_STEP)
    v_hat = v / (1.0 - ADAM_B2 ** ADAM_STEP)
    delta = -ADAM_LR * (m_hat / (_jnp.sqrt(v_hat) + ADAM_EPS) + ADAM_WD * w)
    return delta, m, v


def reference(x, c, ada_w, ada_b, pre_mix_g, post_mix_g, pre_ffn_g, post_ffn_g, w_in, conv_w, conv_b, conv_ln_g, conv_ln_b, w_conv_out, w_att_out, w_o, w_ffn_in, w_ffn_out, loss_target, m_ada_w, m_ada_b, m_pre_mix_g, m_post_mix_g, m_pre_ffn_g, m_post_ffn_g, m_w_in, m_conv_w, m_conv_b, m_conv_ln_g, m_conv_ln_b, m_w_conv_out, m_w_att_out, m_w_o, m_w_ffn_in, m_w_ffn_out, v_ada_w, v_ada_b, v_pre_mix_g, v_post_mix_g, v_pre_ffn_g, v_post_ffn_g, v_w_in, v_conv_w, v_conv_b, v_conv_ln_g, v_conv_ln_b, v_w_conv_out, v_w_att_out, v_w_o, v_w_ffn_in, v_w_ffn_out):
    given = dict(x=x, c=c, ada_w=ada_w, ada_b=ada_b, pre_mix_g=pre_mix_g, post_mix_g=post_mix_g, pre_ffn_g=pre_ffn_g, post_ffn_g=post_ffn_g, w_in=w_in, conv_w=conv_w, conv_b=conv_b, conv_ln_g=conv_ln_g, conv_ln_b=conv_ln_b, w_conv_out=w_conv_out, w_att_out=w_att_out, w_o=w_o, w_ffn_in=w_ffn_in, w_ffn_out=w_ffn_out, loss_target=loss_target, m_ada_w=m_ada_w, m_ada_b=m_ada_b, m_pre_mix_g=m_pre_mix_g, m_post_mix_g=m_post_mix_g, m_pre_ffn_g=m_pre_ffn_g, m_post_ffn_g=m_post_ffn_g, m_w_in=m_w_in, m_conv_w=m_conv_w, m_conv_b=m_conv_b, m_conv_ln_g=m_conv_ln_g, m_conv_ln_b=m_conv_ln_b, m_w_conv_out=m_w_conv_out, m_w_att_out=m_w_att_out, m_w_o=m_w_o, m_w_ffn_in=m_w_ffn_in, m_w_ffn_out=m_w_ffn_out, v_ada_w=v_ada_w, v_ada_b=v_ada_b, v_pre_mix_g=v_pre_mix_g, v_post_mix_g=v_post_mix_g, v_pre_ffn_g=v_pre_ffn_g, v_post_ffn_g=v_post_ffn_g, v_w_in=v_w_in, v_conv_w=v_conv_w, v_conv_b=v_conv_b, v_conv_ln_g=v_conv_ln_g, v_conv_ln_b=v_conv_ln_b, v_w_conv_out=v_w_conv_out, v_w_att_out=v_w_att_out, v_w_o=v_w_o, v_w_ffn_in=v_w_ffn_in, v_w_ffn_out=v_w_ffn_out)
    weights = {n: given[n] for n in TWIN_WEIGHTS}
    shared = {n: given[n] for n in SHARED_INPUTS}
    per_example = {n: given[n] for n in ['x', 'c']}
    grad_fn = _jax.value_and_grad(_loss, argnums=(0, 1))

    def one_microbatch(ex, loss_target):
        ex = dict(ex)
        diff = ex.pop(TWIN_DIFF_INPUT)
        return grad_fn(weights, diff, {**shared, **ex}, loss_target)

    if N_MICROBATCH == 1:
        loss, (grad_w, grad_x) = one_microbatch(per_example, given["loss_target"])
    else:
        def body(carry, xs):
            loss_sum, grad_sum = carry
            l_k, (gw_k, gx_k) = one_microbatch(xs[0], xs[1])
            with _jax.named_scope("update"):
                return (loss_sum + l_k, _jax.tree.map(_jnp.add, grad_sum, gw_k)), gx_k

        init = (_jnp.zeros((), _jnp.float32), _jax.tree.map(_jnp.zeros_like, weights))
        (loss, grad_w), grad_x = _jax.lax.scan(body, init, (per_example, given["loss_target"]))
    with _jax.named_scope("update"):
        delta_w, new_m, new_v = {}, {}, {}
        for n in TWIN_WEIGHTS:
            delta_w[n], new_m[n], new_v[n] = _adamw(weights[n], grad_w[n], given["m_" + n], given["v_" + n])
    return (loss, grad_x, *[grad_w[n] for n in TWIN_WEIGHTS], *[delta_w[n] for n in TWIN_WEIGHTS],
            *[new_m[n] for n in TWIN_WEIGHTS], *[new_v[n] for n in TWIN_WEIGHTS])
```

```python
import functools
import math

import jax
import jax.numpy as jnp
from jax import lax
from jax.experimental import pallas as pl
from jax.experimental.pallas import tpu as pltpu

F32 = jnp.float32
BF16 = jnp.bfloat16
MESH = pl.DeviceIdType.MESH
N_DEV = 8
EPS = 1e-6
HEAD_DIM = 64
CONV_K = 31
HALO = 32
KEY_BLK = 128
ADAM_LR, ADAM_B1, ADAM_B2, ADAM_EPS, ADAM_WD, ADAM_STEP = 0.001, 0.9, 0.999, 1e-08, 0.01, 10

NT = (((1,), (1,)), ((), ()))
NN = (((1,), (0,)), ((), ()))
TN = (((0,), (0,)), ((), ()))


def _tile(n, cands):
    for t in cands:
        if n % t == 0:
            return t
    return n


def _sig(x):
    return 1.0 / (1.0 + jnp.exp(-x))


def _pos():
    return lax.axis_index("x"), lax.axis_index("y"), lax.axis_index("c")


def _my_index():
    x, y, c = _pos()
    return 4 * x + 2 * y + c


def _all_gather(arrs, name, in_vmem):
    n = len(arrs)

    def body(*refs):
        ins, outs = refs[:n], refs[n:2 * n]
        send_sems, recv_sems, local_sems = refs[2 * n:]
        x, y, c = _pos()
        me, sib = (x, y, c), (x, y, 1 - c)
        chips = [(1 - x, y), (x, 1 - y), (1 - x, 1 - y)]

        def rows(i, p):
            r = ins[i].shape[0]
            return outs[i].at[pl.ds((4 * p[0] + 2 * p[1] + p[2]) * r, r), :]

        def copy(i, k, block, to, src=None):
            return pltpu.make_async_remote_copy(
                src_ref=rows(i, block) if src is None else src, dst_ref=rows(i, block),
                send_sem=send_sems.at[i, k], recv_sem=recv_sems.at[i, k],
                device_id=to, device_id_type=MESH)

        mine = [pltpu.make_async_copy(ins[i], rows(i, me), local_sems.at[i]) for i in range(n)]
        for cp in mine:
            cp.start()
        first = []
        for j, chip in enumerate(chips):
            for i in range(n):
                first.append(copy(i, 1 + j, me, (*chip, c), src=ins[i]))
        for i in range(n):
            first.append(copy(i, 0, me, sib, src=ins[i]))
        for cp in first:
            cp.start()
        passed = []
        for j, chip in enumerate(chips):
            for i in range(n):
                copy(i, 1 + j, (*chip, c), me).wait_recv()
                p = copy(i, 4 + j, (*chip, c), sib)
                p.start()
                passed.append(p)
        for i in range(n):
            copy(i, 0, sib, me).wait_recv()
            for j, chip in enumerate(chips):
                copy(i, 4 + j, (*chip, 1 - c), me).wait_recv()
        for cp in first + passed:
            cp.wait_send()
        for cp in mine:
            cp.wait()

    space = pltpu.VMEM if in_vmem else pl.ANY
    spec = pl.BlockSpec(memory_space=space)
    return pl.pallas_call(
        body, name=name,
        out_shape=[jax.ShapeDtypeStruct((N_DEV * a.shape[0], a.shape[1]), a.dtype) for a in arrs],
        in_specs=[spec] * n, out_specs=[spec] * n,
        scratch_shapes=[pltpu.SemaphoreType.DMA((n, 7)), pltpu.SemaphoreType.DMA((n, 7)),
                        pltpu.SemaphoreType.DMA((n,))],
    )(*arrs)


def _exchange_blocks(arrs, name):
    n = len(arrs)

    def body(*refs):
        ins, outs = refs[:n], refs[n:2 * n]
        send_sems, recv_sems, local_sems = refs[2 * n:]
        x, y, c = _pos()

        def peer(k):
            px = 1 - x if k & 4 else x
            py = 1 - y if k & 2 else y
            pc = 1 - c if k & 1 else c
            return (px, py, pc)

        def block(i, p):
            r = outs[i].shape[1]
            return ins[i].at[pl.ds((4 * p[0] + 2 * p[1] + p[2]) * r, r), :]

        def copy(i, k):
            return pltpu.make_async_remote_copy(
                src_ref=block(i, peer(k)), dst_ref=outs[i].at[k],
                send_sem=send_sems.at[i, k - 1], recv_sem=recv_sems.at[i, k - 1],
                device_id=peer(k), device_id_type=MESH)

        mine = [pltpu.make_async_copy(block(i, peer(0)), outs[i].at[0], local_sems.at[i]) for i in range(n)]
        for cp in mine:
            cp.start()
        sends = [copy(i, k) for k in (4, 2, 6, 5, 3, 7, 1) for i in range(n)]
        for cp in sends:
            cp.start()
        for cp in sends:
            cp.wait_recv()
        for cp in sends:
            cp.wait_send()
        for cp in mine:
            cp.wait()

    spec = pl.BlockSpec(memory_space=pl.ANY)
    return pl.pallas_call(
        body, name=name,
        out_shape=[jax.ShapeDtypeStruct((N_DEV, a.shape[0] // N_DEV, a.shape[1]), a.dtype) for a in arrs],
        in_specs=[spec] * n, out_specs=[spec] * n,
        scratch_shapes=[pltpu.SemaphoreType.DMA((n, 7)), pltpu.SemaphoreType.DMA((n, 7)),
                        pltpu.SemaphoreType.DMA((n,))],
    )(*arrs)


def _sum_slots(r3, name):
    _, r, c = r3.shape
    tr = _tile(r, (128, 96, 88, 64, 48, 32, 24, 16, 8))

    def body(r_ref, o_ref):
        acc = r_ref[0]
        for k in range(1, N_DEV):
            acc = acc + r_ref[k]
        o_ref[...] = acc

    return pl.pallas_call(
        body, name=name, grid=(r // tr,),
        out_shape=jax.ShapeDtypeStruct((r, c), F32),
        in_specs=[pl.BlockSpec((N_DEV, tr, c), lambda i: (0, i, 0))],
        out_specs=pl.BlockSpec((tr, c), lambda i: (i, 0)),
    )(r3)


def _mm(a, b, dims, out_dtype, name, n0=0, n=None):
    if dims is NT:
        m, k = a.shape
        n = b.shape[0] if n is None else n
    elif dims is NN:
        m, k = a.shape
        n = b.shape[1]
    else:
        k, m = a.shape
        n = b.shape[1]
    if dims is TN:
        tm = _tile(m, (512, 256, 128))
        tn = n if n <= 1024 else _tile(n, (1024, 512, 256, 128))
        tk = _tile(k, (1024, 512, 256, 128))
    else:
        tm = _tile(m, (1024, 512, 256, 128))
        tn = _tile(math.gcd(n, n0) if n0 else n, (512, 256, 128))
        tk = k if k <= 2816 else _tile(k, (2816, 2048, 1024, 512))
    nk = k // tk
    noff = n0 // tn

    def body(a_ref, b_ref, o_ref, *scratch):
        part = lax.dot_general(a_ref[...].astype(BF16), b_ref[...].astype(BF16), dims,
                               preferred_element_type=F32)
        if nk == 1:
            o_ref[...] = part.astype(o_ref.dtype)
        else:
            acc_ref, = scratch
            kk = pl.program_id(2)

            @pl.when(kk == 0)
            def _():
                acc_ref[...] = part

            @pl.when(kk > 0)
            def _():
                acc_ref[...] += part

            @pl.when(kk == nk - 1)
            def _():
                o_ref[...] = acc_ref[...].astype(o_ref.dtype)

    if dims is NT:
        in_specs = [pl.BlockSpec((tm, tk), lambda i, j, kk: (i, kk)),
                    pl.BlockSpec((tn, tk), lambda i, j, kk: (j + noff, kk))]
    elif dims is NN:
        in_specs = [pl.BlockSpec((tm, tk), lambda i, j, kk: (i, kk)),
                    pl.BlockSpec((tk, tn), lambda i, j, kk: (kk, j))]
    else:
        in_specs = [pl.BlockSpec((tk, tm), lambda i, j, kk: (kk, i)),
                    pl.BlockSpec((tk, tn), lambda i, j, kk: (kk, j))]
    return pl.pallas_call(
        body, name=name, grid=(m // tm, n // tn, nk),
        out_shape=jax.ShapeDtypeStruct((m, n), out_dtype),
        in_specs=in_specs,
        out_specs=pl.BlockSpec((tm, tn), lambda i, j, kk: (i, j)),
        scratch_shapes=[] if nk == 1 else [pltpu.VMEM((tm, tn), F32)],
        compiler_params=pltpu.CompilerParams(dimension_semantics=("parallel", "parallel", "arbitrary")),
    )(a, b)


def _tok_tile(s):
    return _tile(s, (512, 256, 128))


def _row_spec(tm, d):
    return pl.BlockSpec((tm, d), lambda i: (i, 0))


def _vec_spec(d):
    return pl.BlockSpec((1, d), lambda i: (0, 0))


def _seq_spec(d, tiles_per_seq):
    return pl.BlockSpec((1, 1, d), lambda i: (i // tiles_per_seq, 0, 0))


def _rms(x):
    return lax.rsqrt(jnp.mean(x * x, axis=-1, keepdims=True) + EPS)


def _norm_mod(x, g, sc, sh, s, name):
    t, d = x.shape
    tm = _tok_tile(s)

    def body(x_ref, g_ref, sc_ref, sh_ref, h_ref):
        xv = x_ref[...]
        h = (xv * _rms(xv)) * g_ref[...]
        h_ref[...] = (h * (1.0 + sc_ref[0]) + sh_ref[0]).astype(BF16)

    return pl.pallas_call(
        body, name=name, grid=(t // tm,),
        out_shape=jax.ShapeDtypeStruct((t, d), BF16),
        in_specs=[_row_spec(tm, d), _vec_spec(d), _seq_spec(d, s // tm), _seq_spec(d, s // tm)],
        out_specs=_row_spec(tm, d),
    )(x, g, sc, sh)


def _post_res_norm(x, y, g_post, ga, g_pre, sc, sh, s, name):
    t, d = x.shape
    tm = _tok_tile(s)

    def body(x_ref, y_ref, gp_ref, ga_ref, g_ref, sc_ref, sh_ref, xn_ref, h_ref):
        yv = y_ref[...]
        xn = x_ref[...] + ga_ref[0] * ((yv * _rms(yv)) * gp_ref[...])
        xn_ref[...] = xn
        h = (xn * _rms(xn)) * g_ref[...]
        h_ref[...] = (h * (1.0 + sc_ref[0]) + sh_ref[0]).astype(BF16)

    tps = s // tm
    return pl.pallas_call(
        body, name=name, grid=(t // tm,),
        out_shape=[jax.ShapeDtypeStruct((t, d), F32), jax.ShapeDtypeStruct((t, d), BF16)],
        in_specs=[_row_spec(tm, d), _row_spec(tm, d), _vec_spec(d), _seq_spec(d, tps),
                  _vec_spec(d), _seq_spec(d, tps), _seq_spec(d, tps)],
        out_specs=[_row_spec(tm, d), _row_spec(tm, d)],
    )(x, y, g_post, ga, g_pre, sc, sh)


def _post_res_loss(x, y, g_post, ga, target, s, name):
    t, d = x.shape
    tm = _tok_tile(s)

    def body(x_ref, y_ref, gp_ref, ga_ref, t_ref, dx_ref, loss_ref):
        yv = y_ref[...]
        err = x_ref[...] + ga_ref[0] * ((yv * _rms(yv)) * gp_ref[...]) - t_ref[...]
        dx_ref[...] = err * (1.0 / d)
        part = 0.5 * jnp.sum(jnp.mean(err * err, axis=-1, keepdims=True), axis=0, keepdims=True)

        @pl.when(pl.program_id(0) == 0)
        def _():
            loss_ref[...] = jnp.zeros_like(loss_ref)

        loss_ref[...] += part

    return pl.pallas_call(
        body, name=name, grid=(t // tm,),
        out_shape=[jax.ShapeDtypeStruct((t, d), F32), jax.ShapeDtypeStruct((8, 128), F32)],
        in_specs=[_row_spec(tm, d), _row_spec(tm, d), _vec_spec(d), _seq_spec(d, s // tm), _row_spec(tm, d)],
        out_specs=[_row_spec(tm, d), pl.BlockSpec((8, 128), lambda i: (0, 0))],
    )(x, y, g_post, ga, target)


def _post_bwd(dxn, y, g, ga, s, name):
    t, d = y.shape
    tm = _tok_tile(s)
    tps = s // tm

    def body(dx_ref, y_ref, g_ref, ga_ref, dy_ref, dg_ref, dga_ref):
        i = pl.program_id(0)
        yv = y_ref[...]
        r = _rms(yv)
        yh = yv * r
        dxv = dx_ref[...]
        dn = dxv * ga_ref[0]
        dyh = dn * g_ref[...]
        dy_ref[...] = (r * (dyh - yh * jnp.mean(dyh * yh, axis=-1, keepdims=True))).astype(BF16)

        @pl.when(i == 0)
        def _():
            dg_ref[...] = jnp.zeros_like(dg_ref)

        @pl.when(i % tps == 0)
        def _():
            dga_ref[...] = jnp.zeros_like(dga_ref)

        dg_ref[...] += jnp.sum(dn * yh, axis=0, keepdims=True)
        dga_ref[0] += jnp.sum(dxv * (yh * g_ref[...]), axis=0, keepdims=True)

    return pl.pallas_call(
        body, name=name, grid=(t // tm,),
        out_shape=[jax.ShapeDtypeStruct((t, d), BF16), jax.ShapeDtypeStruct((1, d), F32),
                   jax.ShapeDtypeStruct((t // s, 1, d), F32)],
        in_specs=[_row_spec(tm, d), _row_spec(tm, d), _vec_spec(d), _seq_spec(d, tps)],
        out_specs=[_row_spec(tm, d), _vec_spec(d), _seq_spec(d, tps)],
    )(dxn, y, g, ga)


def _pre_bwd(dh, x, g, sc, dres, s, name):
    t, d = x.shape
    tm = _tok_tile(s)
    tps = s // tm

    def body(dh_ref, x_ref, g_ref, sc_ref, dr_ref, dx_ref, dsh_ref, dsc_ref, dg_ref):
        i = pl.program_id(0)
        xv = x_ref[...]
        r = _rms(xv)
        xh = xv * r
        dhv = dh_ref[...]
        one_sc = 1.0 + sc_ref[0]
        dxh = dhv * one_sc * g_ref[...]
        dx_ref[...] = dr_ref[...] + r * (dxh - xh * jnp.mean(dxh * xh, axis=-1, keepdims=True))

        @pl.when(i == 0)
        def _():
            dg_ref[...] = jnp.zeros_like(dg_ref)

        @pl.when(i % tps == 0)
        def _():
            dsh_ref[...] = jnp.zeros_like(dsh_ref)
            dsc_ref[...] = jnp.zeros_like(dsc_ref)

        dg_ref[...] += jnp.sum(dhv * one_sc * xh, axis=0, keepdims=True)
        dsh_ref[0] += jnp.sum(dhv, axis=0, keepdims=True)
        dsc_ref[0] += jnp.sum(dhv * (xh * g_ref[...]), axis=0, keepdims=True)

    nb = t // s
    return pl.pallas_call(
        body, name=name, grid=(t // tm,),
        out_shape=[jax.ShapeDtypeStruct((t, d), F32), jax.ShapeDtypeStruct((nb, 1, d), F32),
                   jax.ShapeDtypeStruct((nb, 1, d), F32), jax.ShapeDtypeStruct((1, d), F32)],
        in_specs=[_row_spec(tm, d), _row_spec(tm, d), _vec_spec(d), _seq_spec(d, tps), _row_spec(tm, d)],
        out_specs=[_row_spec(tm, d), _seq_spec(d, tps), _seq_spec(d, tps), _vec_spec(d)],
    )(dh, x, g, sc, dres)


def _merge(y_conv, y_att, gates, s, name):
    t, d = y_conv.shape
    tm = _tok_tile(s)

    def body(yc_ref, ya_ref, gc_ref, gt_ref, o_ref):
        o_ref[...] = (_sig(gc_ref[...]) * yc_ref[...] + _sig(gt_ref[...]) * ya_ref[...]).astype(BF16)

    return pl.pallas_call(
        body, name=name, grid=(t // tm,),
        out_shape=jax.ShapeDtypeStruct((t, d), BF16),
        in_specs=[_row_spec(tm, d), _row_spec(tm, d), pl.BlockSpec((tm, d), lambda i: (i, 0)),
                  pl.BlockSpec((tm, d), lambda i: (i, 1))],
        out_specs=_row_spec(tm, d),
    )(y_conv, y_att, gates, gates)


def _merge_bwd(dm, y_conv, y_att, gates, s, name):
    t, d = y_conv.shape
    tm = _tok_tile(s)

    def body(dm_ref, yc_ref, ya_ref, gc_ref, gt_ref, dyc_ref, dya_ref, dg_ref):
        dmv = dm_ref[...]
        sc_, st_ = _sig(gc_ref[...]), _sig(gt_ref[...])
        dyc_ref[...] = (dmv * sc_).astype(BF16)
        dya_ref[...] = (dmv * st_).astype(BF16)
        dg_ref[:, :d] = (dmv * yc_ref[...] * (sc_ * (1.0 - sc_))).astype(BF16)
        dg_ref[:, d:] = (dmv * ya_ref[...] * (st_ * (1.0 - st_))).astype(BF16)

    return pl.pallas_call(
        body, name=name, grid=(t // tm,),
        out_shape=[jax.ShapeDtypeStruct((t, d), BF16), jax.ShapeDtypeStruct((t, d), BF16),
                   jax.ShapeDtypeStruct((t, 2 * d), BF16)],
        in_specs=[_row_spec(tm, d), _row_spec(tm, d), _row_spec(tm, d),
                  pl.BlockSpec((tm, d), lambda i: (i, 0)), pl.BlockSpec((tm, d), lambda i: (i, 1))],
        out_specs=[_row_spec(tm, d), _row_spec(tm, d), _row_spec(tm, 2 * d)],
    )(dm, y_conv, y_att, gates, gates)


def _swiglu(f, s, name):
    t, two = f.shape
    dff = two // 2
    tm = _tok_tile(s)

    def body(g_ref, u_ref, a_ref):
        gv = g_ref[...]
        a_ref[...] = (gv * _sig(gv) * u_ref[...]).astype(BF16)

    return pl.pallas_call(
        body, name=name, grid=(t // tm,),
        out_shape=jax.ShapeDtypeStruct((t, dff), BF16),
        in_specs=[pl.BlockSpec((tm, dff), lambda i: (i, 0)), pl.BlockSpec((tm, dff), lambda i: (i, 1))],
        out_specs=_row_spec(tm, dff),
    )(f, f)


def _swiglu_bwd(da, f, s, name):
    t, two = f.shape
    dff = two // 2
    tm = _tok_tile(s)

    def body(da_ref, g_ref, u_ref, df_ref):
        gv, dav = g_ref[...], da_ref[...]
        sg = _sig(gv)
        df_ref[:, :dff] = (dav * u_ref[...] * (sg * (1.0 + gv * (1.0 - sg)))).astype(BF16)
        df_ref[:, dff:] = (dav * (gv * sg)).astype(BF16)

    return pl.pallas_call(
        body, name=name, grid=(t // tm,),
        out_shape=jax.ShapeDtypeStruct((t, two), BF16),
        in_specs=[_row_spec(tm, dff), pl.BlockSpec((tm, dff), lambda i: (i, 0)),
                  pl.BlockSpec((tm, dff), lambda i: (i, 1))],
        out_specs=_row_spec(tm, two),
    )(da, f, f)


CONV_CHUNK = 64


def _conv_fwd(glu_in, conv_w, conv_b, ln_g, ln_b, nb, s, name):
    t, two = glu_in.shape
    cw = two // 2
    tt = _tile(s, (256, 128, 64))
    spt = s // tt
    ch = min(CONV_CHUNK, tt)

    def body(cur_ref, halo_ref, w_ref, b_ref, g_ref, be_ref, u_ref, cv_ref, ua_ref, ext_ref):
        j = pl.program_id(1)
        cur = cur_ref[...]
        u_cur = cur[:, :cw] * _sig(cur[:, cw:])
        hal = halo_ref[...]
        u_hal = hal[:, :cw] * _sig(hal[:, cw:])
        ext_ref[0:HALO, :] = jnp.where(j == 0, 0.0, u_hal)
        ext_ref[HALO:HALO + tt, :] = u_cur
        u_ref[...] = u_cur
        for c0 in range(0, tt, ch):
            acc = jnp.zeros((ch, cw), F32) + b_ref[...]
            for k in range(CONV_K):
                st = c0 + HALO - (CONV_K - 1) + k
                acc = acc + w_ref[k:k + 1, :] * ext_ref[st:st + ch, :]
            cv_ref[c0:c0 + ch, :] = acc
            mu = jnp.mean(acc, axis=-1, keepdims=True)
            xc = acc - mu
            yv = xc * lax.rsqrt(jnp.mean(xc * xc, axis=-1, keepdims=True) + EPS) * g_ref[...] + be_ref[...]
            ua_ref[c0:c0 + ch, :] = (yv * _sig(yv)).astype(BF16)

    cur_map = lambda b, j: (b * spt + j, 0)
    halo_map = lambda b, j: (jnp.maximum((b * s + j * tt) // HALO - 1, 0), 0)
    vec = lambda d0: pl.BlockSpec((d0, cw), lambda b, j: (0, 0))
    return pl.pallas_call(
        body, name=name, grid=(nb, spt),
        out_shape=[jax.ShapeDtypeStruct((t, cw), F32), jax.ShapeDtypeStruct((t, cw), F32),
                   jax.ShapeDtypeStruct((t, cw), BF16)],
        in_specs=[pl.BlockSpec((tt, two), cur_map), pl.BlockSpec((HALO, two), halo_map),
                  vec(CONV_K), vec(1), vec(1), vec(1)],
        out_specs=[pl.BlockSpec((tt, cw), cur_map)] * 3,
        scratch_shapes=[pltpu.VMEM((tt + HALO, cw), F32)],
    )(glu_in, glu_in, conv_w, conv_b, ln_g, ln_b)


def _convln_bwd(dua, cv, ln_g, ln_b, s, name):
    t, cw = cv.shape
    tm = _tok_tile(s)

    def body(du_ref, cv_ref, g_ref, be_ref, dcv_ref, dg_ref, dbe_ref, db_ref):
        cvv = cv_ref[...]
        mu = jnp.mean(cvv, axis=-1, keepdims=True)
        xc = cvv - mu
        rstd = lax.rsqrt(jnp.mean(xc * xc, axis=-1, keepdims=True) + EPS)
        xh = xc * rstd
        yv = xh * g_ref[...] + be_ref[...]
        sg = _sig(yv)
        dy = du_ref[...] * (sg * (1.0 + yv * (1.0 - sg)))
        dxh = dy * g_ref[...]
        dcv = rstd * (dxh - jnp.mean(dxh, axis=-1, keepdims=True)
                      - xh * jnp.mean(dxh * xh, axis=-1, keepdims=True))
        dcv_ref[...] = dcv

        @pl.when(pl.program_id(0) == 0)
        def _():
            dg_ref[...] = jnp.zeros_like(dg_ref)
            dbe_ref[...] = jnp.zeros_like(dbe_ref)
            db_ref[...] = jnp.zeros_like(db_ref)

        dg_ref[...] += jnp.sum(dy * xh, axis=0, keepdims=True)
        dbe_ref[...] += jnp.sum(dy, axis=0, keepdims=True)
        db_ref[...] += jnp.sum(dcv, axis=0, keepdims=True)

    return pl.pallas_call(
        body, name=name, grid=(t // tm,),
        out_shape=[jax.ShapeDtypeStruct((t, cw), F32)] + [jax.ShapeDtypeStruct((1, cw), F32)] * 3,
        in_specs=[_row_spec(tm, cw), _row_spec(tm, cw), _vec_spec(cw), _vec_spec(cw)],
        out_specs=[_row_spec(tm, cw), _vec_spec(cw), _vec_spec(cw), _vec_spec(cw)],
    )(dua, cv, ln_g, ln_b)


def _conv_bwd(dcv, u, glu_in, conv_w, nb, s, name):
    t, cw = dcv.shape
    tt = _tile(s, (256, 128, 64))
    spt = s // tt
    ch = min(CONV_CHUNK, tt)
    nblk = t // HALO

    def body(d_ref, dn_ref, u_ref, up_ref, glu_ref, w_ref, dglu_ref, dw_ref, dext_ref, uext_ref):
        b, j = pl.program_id(0), pl.program_id(1)
        dcur = d_ref[...]
        dext_ref[0:tt, :] = dcur
        dext_ref[tt:tt + HALO, :] = jnp.where(j == spt - 1, 0.0, dn_ref[...])
        uext_ref[0:HALO, :] = jnp.where(j == 0, 0.0, up_ref[...])
        uext_ref[HALO:HALO + tt, :] = u_ref[...]

        @pl.when((b == 0) & (j == 0))
        def _():
            dw_ref[...] = jnp.zeros_like(dw_ref)

        for c0 in range(0, tt, ch):
            dc = dext_ref[c0:c0 + ch, :]
            du = jnp.zeros((ch, cw), F32)
            for k in range(CONV_K):
                sd = c0 + (CONV_K - 1) - k
                du = du + w_ref[k:k + 1, :] * dext_ref[sd:sd + ch, :]
                su = c0 + HALO - (CONV_K - 1) + k
                dw_ref[k:k + 1, :] += jnp.sum(dc * uext_ref[su:su + ch, :], axis=0, keepdims=True)
            val = glu_ref[c0:c0 + ch, :cw]
            sg = _sig(glu_ref[c0:c0 + ch, cw:])
            dglu_ref[c0:c0 + ch, :cw] = (du * sg).astype(BF16)
            dglu_ref[c0:c0 + ch, cw:] = (du * val * (sg * (1.0 - sg))).astype(BF16)

    cur_map = lambda b, j: (b * spt + j, 0)
    prev_map = lambda b, j: (jnp.maximum((b * s + j * tt) // HALO - 1, 0), 0)
    next_map = lambda b, j: (jnp.minimum((b * s + (j + 1) * tt) // HALO, nblk - 1), 0)
    return pl.pallas_call(
        body, name=name, grid=(nb, spt),
        out_shape=[jax.ShapeDtypeStruct((t, 2 * cw), BF16), jax.ShapeDtypeStruct((HALO, cw), F32)],
        in_specs=[pl.BlockSpec((tt, cw), cur_map), pl.BlockSpec((HALO, cw), next_map),
                  pl.BlockSpec((tt, cw), cur_map), pl.BlockSpec((HALO, cw), prev_map),
                  pl.BlockSpec((tt, 2 * cw), cur_map), pl.BlockSpec((CONV_K, cw), lambda b, j: (0, 0))],
        out_specs=[pl.BlockSpec((tt, 2 * cw), cur_map), pl.BlockSpec((HALO, cw), lambda b, j: (0, 0))],
        scratch_shapes=[pltpu.VMEM((tt + HALO, cw), F32), pltpu.VMEM((tt + HALO, cw), F32)],
    )(dcv, dcv, u, u, glu_in, conv_w)


def _split(x):
    hi = x.astype(BF16)
    return hi, (x - hi.astype(F32)).astype(BF16)


def _tri(upper):
    j = lax.broadcasted_iota(jnp.int32, (KEY_BLK, KEY_BLK), 0)
    s_ = lax.broadcasted_iota(jnp.int32, (KEY_BLK, KEY_BLK), 1)
    return jnp.where(j > s_ if upper else j < s_, 1.0, 0.0).astype(BF16)


def _log_keep(z, causal):
    sp = jnp.maximum(z, 0.0) + jnp.log(1.0 + jnp.exp(-jnp.abs(z)))
    return jnp.where(causal, -sp, 0.0)


def _attn_fwd(q, k, v, name):
    bh, s, dh = q.shape
    tq = _tile(s, (256, 128))
    scale = 1.0 / math.sqrt(dh)

    def body(q_ref, k_ref, v_ref, o_ref, car_ref, acc_ref, run_ref):
        qi = pl.program_id(1)
        qv = q_ref[0]
        lane = lax.broadcasted_iota(jnp.int32, (tq, KEY_BLK), 1)
        row = qi * tq + lax.broadcasted_iota(jnp.int32, (tq, KEY_BLK), 0)
        m_suf = _tri(True)
        nk = (qi * tq + tq) // KEY_BLK
        acc_ref[...] = jnp.zeros_like(acc_ref)
        run_ref[...] = jnp.zeros_like(run_ref)
        car_ref[...] = jnp.zeros_like(car_ref)

        def step(it, _):
            kb = nk - 1 - it
            off = pl.multiple_of(kb * KEY_BLK, KEY_BLK)
            kblk = k_ref[0, pl.ds(off, KEY_BLK), :]
            vblk = v_ref[0, pl.ds(off, KEY_BLK), :]
            z = lax.dot_general(qv, kblk, NT, preferred_element_type=F32) * scale
            causal = (off + lane) < row
            lk = _log_keep(z, causal)
            hi, lo = _split(lk)
            cs = (jnp.dot(hi, m_suf, preferred_element_type=F32)
                  + jnp.dot(lo, m_suf, preferred_element_type=F32))
            run = run_ref[...]
            a = jnp.where(causal, jnp.exp(z + lk + cs + run), 0.0)
            acc_ref[...] += jnp.dot(a.astype(BF16), vblk, preferred_element_type=F32)
            car_ref[0] = jnp.where(lane == kb, run, car_ref[0])
            run_ref[...] = run + jnp.sum(lk, axis=1, keepdims=True)
            return 0

        lax.fori_loop(0, nk, step, 0)
        o_ref[0] = acc_ref[...].astype(BF16)

    full = pl.BlockSpec((1, s, dh), lambda b, i: (b, 0, 0))
    return pl.pallas_call(
        body, name=name, grid=(bh, s // tq),
        out_shape=[jax.ShapeDtypeStruct((bh, s, dh), BF16), jax.ShapeDtypeStruct((bh, s, KEY_BLK), F32)],
        in_specs=[pl.BlockSpec((1, tq, dh), lambda b, i: (b, i, 0)), full, full],
        out_specs=[pl.BlockSpec((1, tq, dh), lambda b, i: (b, i, 0)),
                   pl.BlockSpec((1, tq, KEY_BLK), lambda b, i: (b, i, 0))],
        scratch_shapes=[pltpu.VMEM((tq, dh), F32), pltpu.VMEM((tq, KEY_BLK), F32)],
        compiler_params=pltpu.CompilerParams(dimension_semantics=("parallel", "arbitrary")),
    )(q, k, v)


def _attn_bwd(q, k, v, do, car, name):
    bh, s, dh = q.shape
    tq = _tile(s, (256, 128))
    scale = 1.0 / math.sqrt(dh)
    nkb, nqb = s // KEY_BLK, s // tq

    def body(q_ref, k_ref, v_ref, do_ref, car_ref, dq_ref, dk_ref, dv_ref, dq_acc, g_acc, dk_acc, dv_acc):
        lane = lax.broadcasted_iota(jnp.int32, (tq, KEY_BLK), 1)
        rowi = lax.broadcasted_iota(jnp.int32, (tq, KEY_BLK), 0)
        m_suf, m_pre = _tri(True), _tri(False)
        dq_acc[...] = jnp.zeros_like(dq_acc)
        g_acc[...] = jnp.zeros_like(g_acc)

        def kloop(kb, _):
            off = pl.multiple_of(kb * KEY_BLK, KEY_BLK)
            kblk = k_ref[0, pl.ds(off, KEY_BLK), :]
            vblk = v_ref[0, pl.ds(off, KEY_BLK), :]

            dk_acc[...] = jnp.zeros_like(dk_acc)
            dv_acc[...] = jnp.zeros_like(dv_acc)

            def qloop(qb, _):
                r0 = pl.multiple_of(qb * tq, tq)
                qblk = q_ref[0, pl.ds(r0, tq), :]
                doblk = do_ref[0, pl.ds(r0, tq), :]
                z = lax.dot_general(qblk, kblk, NT, preferred_element_type=F32) * scale
                causal = (off + lane) < (r0 + rowi)
                lk = _log_keep(z, causal)
                hi, lo = _split(lk)
                cs = (jnp.dot(hi, m_suf, preferred_element_type=F32)
                      + jnp.dot(lo, m_suf, preferred_element_type=F32))
                cpast = jnp.sum(jnp.where(lane == kb, car_ref[0, pl.ds(r0, tq), :], 0.0), axis=1, keepdims=True)
                lsig = z + lk
                a = jnp.where(causal, jnp.exp(lsig + cs + cpast), 0.0)
                da = lax.dot_general(doblk, vblk, NT, preferred_element_type=F32)
                g = a * da
                ghi, glo = _split(g)
                gpre = (jnp.dot(ghi, m_pre, preferred_element_type=F32)
                        + jnp.dot(glo, m_pre, preferred_element_type=F32)
                        + g_acc[pl.ds(r0, tq), :])
                dz = jnp.where(causal, g - jnp.exp(lsig) * (g + gpre), 0.0)
                ds_ = (dz * scale).astype(BF16)
                dv_acc[...] += lax.dot_general(a.astype(BF16), doblk, TN, preferred_element_type=F32)
                dk_acc[...] += lax.dot_general(ds_, qblk, TN, preferred_element_type=F32)
                dq_acc[pl.ds(r0, tq), :] += jnp.dot(ds_, kblk, preferred_element_type=F32)
                g_acc[pl.ds(r0, tq), :] += jnp.broadcast_to(jnp.sum(g, axis=1, keepdims=True), (tq, KEY_BLK))
                return 0

            lax.fori_loop(off // tq, nqb, qloop, 0)
            dk_ref[0, pl.ds(off, KEY_BLK), :] = dk_acc[...].astype(BF16)
            dv_ref[0, pl.ds(off, KEY_BLK), :] = dv_acc[...].astype(BF16)
            return 0

        lax.fori_loop(0, nkb, kloop, 0)
        dq_ref[0] = dq_acc[...].astype(BF16)

    full = pl.BlockSpec((1, s, dh), lambda b: (b, 0, 0))
    return pl.pallas_call(
        body, name=name, grid=(bh,),
        out_shape=[jax.ShapeDtypeStruct((bh, s, dh), BF16)] * 3,
        in_specs=[full, full, full, full, pl.BlockSpec((1, s, KEY_BLK), lambda b: (b, 0, 0))],
        out_specs=[full, full, full],
        scratch_shapes=[pltpu.VMEM((s, dh), F32), pltpu.VMEM((s, KEY_BLK), F32),
                        pltpu.VMEM((KEY_BLK, dh), F32), pltpu.VMEM((KEY_BLK, dh), F32)],
        compiler_params=pltpu.CompilerParams(dimension_semantics=("parallel",)),
    )(q, k, v, do, car)


def _ada_fwd(c_all, ada_w, ada_b_loc, name):
    nl, d, n6 = ada_w.shape
    nb = c_all.shape[0]

    def body(c_ref, w_ref, b_ref, o_ref):
        cv = c_ref[...]
        o_ref[0] = jnp.dot(cv * _sig(cv), w_ref[0], preferred_element_type=F32) + b_ref[0]

    return pl.pallas_call(
        body, name=name, grid=(nl,),
        out_shape=jax.ShapeDtypeStruct((nl, nb, n6), F32),
        in_specs=[pl.BlockSpec((nb, d), lambda l: (0, 0)), pl.BlockSpec((1, d, n6), lambda l: (l, 0, 0)),
                  pl.BlockSpec((1, 1, n6), lambda l: (l, 0, 0))],
        out_specs=pl.BlockSpec((1, nb, n6), lambda l: (l, 0, 0)),
    )(c_all, ada_w, ada_b_loc)


def _ada_bwd(c_all_t, dmod_loc, dmod_all, name):
    nl, nb, n6 = dmod_loc.shape
    d = c_all_t.shape[0]
    n_all = dmod_all.shape[2]

    def body(c_ref, dl_ref, da_ref, gw_ref, gb_ref):
        cv = c_ref[...]
        gw_ref[0] = jnp.dot(cv * _sig(cv), dl_ref[0], preferred_element_type=F32)
        gb_ref[0] = jnp.sum(da_ref[0], axis=0, keepdims=True)

    return pl.pallas_call(
        body, name=name, grid=(nl,),
        out_shape=[jax.ShapeDtypeStruct((nl, d, n6), F32), jax.ShapeDtypeStruct((nl, 1, n_all), F32)],
        in_specs=[pl.BlockSpec((d, nb), lambda l: (0, 0)), pl.BlockSpec((1, nb, n6), lambda l: (l, 0, 0)),
                  pl.BlockSpec((1, nb, n_all), lambda l: (l, 0, 0))],
        out_specs=[pl.BlockSpec((1, d, n6), lambda l: (l, 0, 0)), pl.BlockSpec((1, 1, n_all), lambda l: (l, 0, 0))],
    )(c_all_t, dmod_loc, dmod_all)


def _adamw(w, g, m, v, name):
    shape = w.shape
    cols = shape[-1]
    rows = w.size // cols
    tr = _tile(rows, (512, 256, 128, 64, 32, 16, 8))
    flat = lambda a: a.reshape(rows, cols)

    def body(w_ref, g_ref, m_ref, v_ref, d_ref, mo_ref, vo_ref):
        gv = g_ref[...]
        mn = ADAM_B1 * m_ref[...] + (1.0 - ADAM_B1) * gv
        vn = ADAM_B2 * v_ref[...] + (1.0 - ADAM_B2) * (gv * gv)
        m_hat = mn / (1.0 - ADAM_B1 ** ADAM_STEP)
        v_hat = vn / (1.0 - ADAM_B2 ** ADAM_STEP)
        d_ref[...] = -ADAM_LR * (m_hat / (jnp.sqrt(v_hat) + ADAM_EPS) + ADAM_WD * w_ref[...])
        mo_ref[...] = mn
        vo_ref[...] = vn

    spec = pl.BlockSpec((tr, cols), lambda i: (i, 0))
    outs = pl.pallas_call(
        body, name=name, grid=(rows // tr,),
        out_shape=[jax.ShapeDtypeStruct((rows, cols), F32)] * 3,
        in_specs=[spec] * 4, out_specs=[spec] * 3,
    )(flat(w), flat(g), flat(m), flat(v))
    return tuple(o.reshape(shape) for o in outs)


def _pack_rows(parts, d):
    flat = jnp.concatenate([p.reshape(-1) for p in parts])
    rows = -(-flat.size // d)
    rows = -(-rows // 8) * 8
    return jnp.pad(flat, (0, rows * d - flat.size)).reshape(rows, d)


def _to_heads(a, nb, s):
    h = a.shape[1] // HEAD_DIM
    return a.reshape(nb, s, h, HEAD_DIM).transpose(0, 2, 1, 3).reshape(nb * h, s, HEAD_DIM)


def _from_heads(a, nb, s):
    h = a.shape[0] // nb
    return a.reshape(nb, h, s, HEAD_DIM).transpose(0, 2, 1, 3).reshape(nb * s, h * HEAD_DIM)


def kernel(x, c, ada_w, ada_b, pre_mix_g, post_mix_g, pre_ffn_g, post_ffn_g, w_in, conv_w, conv_b, conv_ln_g, conv_ln_b, w_conv_out, w_att_out, w_o, w_ffn_in, w_ffn_out, loss_target, m_ada_w, m_ada_b, m_pre_mix_g, m_post_mix_g, m_pre_ffn_g, m_post_ffn_g, m_w_in, m_conv_w, m_conv_b, m_conv_ln_g, m_conv_ln_b, m_w_conv_out, m_w_att_out, m_w_o, m_w_ffn_in, m_w_ffn_out, v_ada_w, v_ada_b, v_pre_mix_g, v_post_mix_g, v_pre_ffn_g, v_post_ffn_g, v_w_in, v_conv_w, v_conv_b, v_conv_ln_g, v_conv_ln_b, v_w_conv_out, v_w_att_out, v_w_o, v_w_ffn_in, v_w_ffn_out):
    nb, s, d = x.shape
    nl = ada_w.shape[0]
    cw = conv_b.shape[1]
    aw = w_att_out.shape[1]
    cl = conv_w.shape[2]
    n6 = ada_w.shape[2]
    t = nb * s
    me = _my_index()

    conv_flat = conv_w.reshape(-1)
    p1 = _pack_rows([c, conv_flat], d)
    r1 = p1.shape[0]
    (p1_all,) = _all_gather([p1], "gather_c_convw", True)
    p1_all = p1_all.reshape(N_DEV, r1 * d)
    c_all = p1_all[:, :nb * d].reshape(N_DEV * nb, d)
    conv_w_all = p1_all[:, nb * d:nb * d + conv_flat.size].reshape(N_DEV, nl, CONV_K, cl)
    conv_w_all = conv_w_all.transpose(1, 2, 0, 3).reshape(nl, CONV_K, cw)

    ada_b_loc = lax.dynamic_slice_in_dim(ada_b, me * n6, n6, axis=1).reshape(nl, 1, n6)
    mod_cols = _ada_fwd(c_all, ada_w, ada_b_loc, "ada_fwd")
    (mod_g,) = _all_gather([mod_cols.reshape(-1, d)], "gather_mod", True)
    mod_g = mod_g.reshape(N_DEV, nl, N_DEV * nb, n6)
    mod_mine = lax.dynamic_slice_in_dim(mod_g, me * nb, nb, axis=2)
    mod = mod_mine.transpose(1, 2, 0, 3).reshape(nl, nb, 6 * d)
    mods = [[mod[l, :, i * d:(i + 1) * d].reshape(nb, 1, d) for i in range(6)] for l in range(nl)]

    tr_ = lambda w: jnp.swapaxes(w, 0, 1).astype(BF16)
    shards = []
    for l in range(nl):
        shards += [tr_(w_in[l]), tr_(w_conv_out[l]), tr_(w_att_out[l]), w_o[l].astype(BF16),
                   tr_(w_ffn_in[l]), w_ffn_out[l].astype(BF16)]
    full = _all_gather(shards, "gather_weights", False)
    weights = [full[6 * l:6 * l + 6] for l in range(nl)]

    vec = lambda a, l: a[l].reshape(1, -1)
    x2 = x.reshape(t, d)
    tgt = loss_target.reshape(t, d)

    saved = []
    xin = x2
    h = _norm_mod(xin, vec(pre_mix_g, 0), mods[0][1], mods[0][0], s, "norm_mod_0")
    for l in range(nl):
        win_t, wconv_t, watt_t, wo, wffn_in_t, wffn_out = weights[l]
        sh1, sc1, ga1, sh2, sc2, ga2 = mods[l]
        glu_in = _mm(h, win_t, NT, F32, f"proj_glu_{l}", 0, 2 * cw)
        qkv = _mm(h, win_t, NT, BF16, f"proj_qkv_{l}", 2 * cw, 3 * aw)
        gates = _mm(h, win_t, NT, F32, f"proj_gates_{l}", 2 * cw + 3 * aw, 2 * d)
        u, cv, ua = _conv_fwd(glu_in, conv_w_all[l], vec(conv_b, l), vec(conv_ln_g, l), vec(conv_ln_b, l),
                              nb, s, f"conv_fwd_{l}")
        qh, kh, vh = (_to_heads(qkv[:, i * aw:(i + 1) * aw], nb, s) for i in range(3))
        oh, car = _attn_fwd(qh, kh, vh, f"attn_fwd_{l}")
        o = _from_heads(oh, nb, s)
        y_conv = _mm(ua, wconv_t, NT, F32, f"conv_out_{l}")
        y_att = _mm(o, watt_t, NT, F32, f"att_out_{l}")
        merged = _merge(y_conv, y_att, gates, s, f"merge_{l}")
        y = _mm(merged, wo, NN, F32, f"w_o_{l}")
        x1, h2 = _post_res_norm(xin, y, vec(post_mix_g, l), ga1, vec(pre_ffn_g, l), sc2, sh2, s, f"post_mix_{l}")
        f = _mm(h2, wffn_in_t, NT, F32, f"ffn_in_{l}")
        a = _swiglu(f, s, f"swiglu_{l}")
        y2 = _mm(a, wffn_out, NN, F32, f"ffn_out_{l}")
        saved.append(dict(xin=xin, h=h, glu_in=glu_in, gates=gates, u=u, cv=cv, ua=ua, qh=qh, kh=kh, vh=vh,
                          car=car, o=o, y_conv=y_conv, y_att=y_att, merged=merged, y=y, x1=x1, h2=h2, f=f,
                          a=a, y2=y2))
        if l + 1 < nl:
            nsh1, nsc1 = mods[l + 1][0], mods[l + 1][1]
            xin, h = _post_res_norm(x1, y2, vec(post_ffn_g, l), ga2, vec(pre_mix_g, l + 1), nsc1, nsh1, s,
                                    f"post_ffn_{l}")
        else:
            dx, loss_acc = _post_res_loss(x1, y2, vec(post_ffn_g, l), ga2, tgt, s, "loss")

    loss = lax.psum(loss_acc[0, 0], ("x", "y", "c"))

    big_grads = [None] * nl
    small = [None] * nl
    for l in reversed(range(nl)):
        win_t, wconv_t, watt_t, wo, wffn_in_t, wffn_out = weights[l]
        sh1, sc1, ga1, sh2, sc2, ga2 = mods[l]
        sv = saved[l]
        dy2, dg_post_ffn, dga2 = _post_bwd(dx, sv["y2"], vec(post_ffn_g, l), ga2, s, f"post_ffn_bwd_{l}")
        d_wffn_out = _mm(sv["a"], dy2, TN, F32, f"d_w_ffn_out_{l}")
        da = _mm(dy2, wffn_out, NT, F32, f"d_a_{l}")
        df = _swiglu_bwd(da, sv["f"], s, f"swiglu_bwd_{l}")
        d_wffn_in_t = _mm(df, sv["h2"], TN, F32, f"d_w_ffn_in_{l}")
        dh2 = _mm(df, wffn_in_t, NN, F32, f"d_h2_{l}")
        dx1, dsh2, dsc2, dg_pre_ffn = _pre_bwd(dh2, sv["x1"], vec(pre_ffn_g, l), sc2, dx, s, f"pre_ffn_bwd_{l}")
        dy, dg_post_mix, dga1 = _post_bwd(dx1, sv["y"], vec(post_mix_g, l), ga1, s, f"post_mix_bwd_{l}")
        d_wo = _mm(sv["merged"], dy, TN, F32, f"d_w_o_{l}")
        dmerged = _mm(dy, wo, NT, F32, f"d_merged_{l}")
        dyc, dya, dgates = _merge_bwd(dmerged, sv["y_conv"], sv["y_att"], sv["gates"], s, f"merge_bwd_{l}")
        d_wconv_t = _mm(dyc, sv["ua"], TN, F32, f"d_w_conv_out_{l}")
        dua = _mm(dyc, wconv_t, NN, F32, f"d_ua_{l}")
        d_watt_t = _mm(dya, sv["o"], TN, F32, f"d_w_att_out_{l}")
        do = _mm(dya, watt_t, NN, BF16, f"d_o_{l}")
        dqh, dkh, dvh = _attn_bwd(sv["qh"], sv["kh"], sv["vh"], _to_heads(do, nb, s), sv["car"], f"attn_bwd_{l}")
        dcv, dln_g, dln_b, dconv_b = _convln_bwd(dua, sv["cv"], vec(conv_ln_g, l), vec(conv_ln_b, l), s,
                                                 f"convln_bwd_{l}")
        dglu, dconv_w = _conv_bwd(dcv, sv["u"], sv["glu_in"], conv_w_all[l], nb, s, f"conv_bwd_{l}")
        dproj = jnp.concatenate([dglu, _from_heads(dqh, nb, s), _from_heads(dkh, nb, s),
                                 _from_heads(dvh, nb, s), dgates], axis=1)
        d_win_t = _mm(dproj, sv["h"], TN, F32, f"d_w_in_{l}")
        dh = _mm(dproj, win_t, NN, F32, f"d_h_{l}")
        dx, dsh1, dsc1, dg_pre_mix = _pre_bwd(dh, sv["xin"], vec(pre_mix_g, l), sc1, dx1, s, f"pre_mix_bwd_{l}")
        big_grads[l] = [d_win_t, d_wconv_t, d_watt_t, d_wo, d_wffn_in_t, d_wffn_out]
        dmod = jnp.concatenate([dsh1, dsc1, dga1, dsh2, dsc2, dga2], axis=2)
        small[l] = [dg_pre_mix, dg_post_mix, dg_pre_ffn, dg_post_ffn, dconv_b, dln_g, dln_b, dconv_w, dmod]
    grad_x = dx.reshape(nb, s, d)

    sizes = [a.size for a in small[0]]
    p3 = _pack_rows([a for l in range(nl) for a in small[l]], d)
    r3 = p3.shape[0]
    (p3_all,) = _all_gather([p3], "gather_small_grads", True)
    p3_all = p3_all.reshape(N_DEV, r3, d)
    p3_sum = _sum_slots(p3_all, "sum_small_grads").reshape(-1)
    per_layer = sum(sizes)
    small_sum, dmod_all = [], []
    for l in range(nl):
        off, parts = l * per_layer, []
        for sz in sizes[:-1]:
            parts.append(p3_sum[off:off + sz])
            off += sz
        small_sum.append(parts)
        dm = p3_all.reshape(N_DEV, r3 * d)[:, off:off + sizes[-1]]
        dmod_all.append(dm.reshape(N_DEV * nb, 6 * d))
    dmod_all = jnp.stack(dmod_all)
    dmod_loc = lax.dynamic_slice_in_dim(dmod_all, me * n6, n6, axis=2)
    g_ada_w, g_ada_b = _ada_bwd(c_all.T, dmod_loc, dmod_all, "ada_bwd")
    g_ada_b = g_ada_b.reshape(nl, 6 * d)

    stack_small = lambda i, shape: jnp.stack([small_sum[l][i].reshape(shape) for l in range(nl)])
    g_pre_mix, g_post_mix, g_pre_ffn, g_post_ffn = (stack_small(i, (d,)) for i in range(4))
    g_conv_b, g_ln_g, g_ln_b = (stack_small(i, (cw,)) for i in range(4, 7))
    g_conv_w_all = stack_small(7, (HALO, cw))[:, :CONV_K]
    g_conv_w = lax.dynamic_slice_in_dim(g_conv_w_all, me * cl, cl, axis=2)

    slots = _exchange_blocks([g for l in range(nl) for g in big_grads[l]], "exchange_grads")
    sums = [_sum_slots(r, f"sum_grads_{i}") for i, r in enumerate(slots)]
    un_t = lambda i: jnp.stack([jnp.swapaxes(sums[6 * l + i], 0, 1) for l in range(nl)])
    keep = lambda i: jnp.stack([sums[6 * l + i] for l in range(nl)])
    g_w_in, g_w_conv_out, g_w_att_out, g_w_o, g_w_ffn_in, g_w_ffn_out = (
        un_t(0), un_t(1), un_t(2), keep(3), un_t(4), keep(5))

    grads = [g_ada_w, g_ada_b, g_pre_mix, g_post_mix, g_pre_ffn, g_post_ffn, g_w_in, g_conv_w, g_conv_b,
             g_ln_g, g_ln_b, g_w_conv_out, g_w_att_out, g_w_o, g_w_ffn_in, g_w_ffn_out]
    ws = [ada_w, ada_b, pre_mix_g, post_mix_g, pre_ffn_g, post_ffn_g, w_in, conv_w, conv_b, conv_ln_g,
          conv_ln_b, w_conv_out, w_att_out, w_o, w_ffn_in, w_ffn_out]
    ms = [m_ada_w, m_ada_b, m_pre_mix_g, m_post_mix_g, m_pre_ffn_g, m_post_ffn_g, m_w_in, m_conv_w, m_conv_b,
          m_conv_ln_g, m_conv_ln_b, m_w_conv_out, m_w_att_out, m_w_o, m_w_ffn_in, m_w_ffn_out]
    vs = [v_ada_w, v_ada_b, v_pre_mix_g, v_post_mix_g, v_pre_ffn_g, v_post_ffn_g, v_w_in, v_conv_w, v_conv_b,
          v_conv_ln_g, v_conv_ln_b, v_w_conv_out, v_w_att_out, v_w_o, v_w_ffn_in, v_w_ffn_out]
    deltas, new_m, new_v = [], [], []
    for i, (w, g, m, v) in enumerate(zip(ws, grads, ms, vs)):
        dlt, mn, vn = _adamw(w, g, m, v, f"adamw_{i}")
        deltas.append(dlt)
        new_m.append(mn)
        new_v.append(vn)
    return (loss, grad_x, *grads, *deltas, *new_m, *new_v)
```

```python
import functools
import math

import jax
import jax.numpy as jnp
from jax import lax
from jax.experimental import pallas as pl
from jax.experimental.pallas import tpu as pltpu

F32 = jnp.float32
BF16 = jnp.bfloat16
MESH = pl.DeviceIdType.MESH
N_DEV = 8
EPS = 1e-6
HEAD_DIM = 64
CONV_K = 31
HALO = 32
KEY_BLK = 256
ADAM_LR, ADAM_B1, ADAM_B2, ADAM_EPS, ADAM_WD, ADAM_STEP = 0.001, 0.9, 0.999, 1e-08, 0.01, 10

NT = (((1,), (1,)), ((), ()))
NN = (((1,), (0,)), ((), ()))
TN = (((0,), (0,)), ((), ()))


def _tile(n, cands):
    for t in cands:
        if n % t == 0:
            return t
    return n


def _lane_tile(n, cap):
    best = n
    for t in range(128, min(n, cap) + 1, 128):
        if n % t == 0:
            best = t
    return best


def _sig(x):
    return 1.0 / (1.0 + jnp.exp(-x))


def _pos():
    return lax.axis_index("x"), lax.axis_index("y"), lax.axis_index("c")


def _my_index():
    x, y, c = _pos()
    return 4 * x + 2 * y + c


def _all_gather(arrs, name, in_vmem):
    n = len(arrs)

    def body(*refs):
        ins, outs = refs[:n], refs[n:2 * n]
        send_sems, recv_sems, local_sems = refs[2 * n:]
        x, y, c = _pos()
        me, sib = (x, y, c), (x, y, 1 - c)
        chips = [(1 - x, y), (x, 1 - y), (1 - x, 1 - y)]

        def rows(i, p):
            r = ins[i].shape[0]
            return outs[i].at[pl.ds((4 * p[0] + 2 * p[1] + p[2]) * r, r), :]

        def copy(i, k, block, to, src=None):
            return pltpu.make_async_remote_copy(
                src_ref=rows(i, block) if src is None else src, dst_ref=rows(i, block),
                send_sem=send_sems.at[i, k], recv_sem=recv_sems.at[i, k],
                device_id=to, device_id_type=MESH)

        mine = [pltpu.make_async_copy(ins[i], rows(i, me), local_sems.at[i]) for i in range(n)]
        for cp in mine:
            cp.start()
        first = []
        for j, chip in enumerate(chips):
            for i in range(n):
                first.append(copy(i, 1 + j, me, (*chip, c), src=ins[i]))
        for i in range(n):
            first.append(copy(i, 0, me, sib, src=ins[i]))
        for cp in first:
            cp.start()
        passed = []
        for j, chip in enumerate(chips):
            for i in range(n):
                copy(i, 1 + j, (*chip, c), me).wait_recv()
                p = copy(i, 4 + j, (*chip, c), sib)
                p.start()
                passed.append(p)
        for i in range(n):
            copy(i, 0, sib, me).wait_recv()
            for j, chip in enumerate(chips):
                copy(i, 4 + j, (*chip, 1 - c), me).wait_recv()
        for cp in first + passed:
            cp.wait_send()
        for cp in mine:
            cp.wait()

    space = pltpu.VMEM if in_vmem else pl.ANY
    spec = pl.BlockSpec(memory_space=space)
    return pl.pallas_call(
        body, name=name,
        out_shape=[jax.ShapeDtypeStruct((N_DEV * a.shape[0], a.shape[1]), a.dtype) for a in arrs],
        in_specs=[spec] * n, out_specs=[spec] * n,
        scratch_shapes=[pltpu.SemaphoreType.DMA((n, 7)), pltpu.SemaphoreType.DMA((n, 7)),
                        pltpu.SemaphoreType.DMA((n,))],
    )(*arrs)


def _exchange_blocks(arrs, name):
    n = len(arrs)

    def body(*refs):
        ins, outs = refs[:n], refs[n:2 * n]
        send_sems, recv_sems, local_sems = refs[2 * n:]
        x, y, c = _pos()

        def peer(k):
            px = 1 - x if k & 4 else x
            py = 1 - y if k & 2 else y
            pc = 1 - c if k & 1 else c
            return (px, py, pc)

        def block(i, p):
            r = outs[i].shape[1]
            return ins[i].at[pl.ds((4 * p[0] + 2 * p[1] + p[2]) * r, r), :]

        def copy(i, k):
            return pltpu.make_async_remote_copy(
                src_ref=block(i, peer(k)), dst_ref=outs[i].at[k],
                send_sem=send_sems.at[i, k - 1], recv_sem=recv_sems.at[i, k - 1],
                device_id=peer(k), device_id_type=MESH)

        mine = [pltpu.make_async_copy(block(i, peer(0)), outs[i].at[0], local_sems.at[i]) for i in range(n)]
        for cp in mine:
            cp.start()
        sends = [copy(i, k) for k in (4, 2, 6, 5, 3, 7, 1) for i in range(n)]
        for cp in sends:
            cp.start()
        for cp in sends:
            cp.wait_recv()
        for cp in sends:
            cp.wait_send()
        for cp in mine:
            cp.wait()

    spec = pl.BlockSpec(memory_space=pl.ANY)
    return pl.pallas_call(
        body, name=name,
        out_shape=[jax.ShapeDtypeStruct((N_DEV, a.shape[0] // N_DEV, a.shape[1]), a.dtype) for a in arrs],
        in_specs=[spec] * n, out_specs=[spec] * n,
        scratch_shapes=[pltpu.SemaphoreType.DMA((n, 7)), pltpu.SemaphoreType.DMA((n, 7)),
                        pltpu.SemaphoreType.DMA((n,))],
    )(*arrs)


def _sum_slots(r3, name):
    _, r, c = r3.shape
    tr = _tile(r, (128, 96, 64, 32, 16))

    def body(r_ref, o_ref):
        acc = r_ref[0].astype(F32)
        for k in range(1, N_DEV):
            acc = acc + r_ref[k].astype(F32)
        o_ref[...] = acc

    return pl.pallas_call(
        body, name=name, grid=(r // tr,),
        out_shape=jax.ShapeDtypeStruct((r, c), F32),
        in_specs=[pl.BlockSpec((N_DEV, tr, c), lambda i: (0, i, 0))],
        out_specs=pl.BlockSpec((tr, c), lambda i: (i, 0)),
    )(r3)


def _mm(a, b, dims, out_dtype, name, n0=0, n=None):
    if dims is NT:
        m, k = a.shape
        n = b.shape[0] if n is None else n
    elif dims is NN:
        m, k = a.shape
        n = b.shape[1]
    else:
        k, m = a.shape
        n = b.shape[1]
    if dims is TN:
        tm = _lane_tile(m, 1536)
        tn = _lane_tile(n, 1024)
        tk = _tile(k, (1024, 512, 256, 128))
    else:
        tm = _tile(m, (1024, 512, 256, 128))
        tn = _lane_tile(math.gcd(n, n0) if n0 else n, 1536)
        tk = _lane_tile(k, 2816)
    nk = k // tk
    noff = n0 // tn

    def body(a_ref, b_ref, o_ref, *scratch):
        part = lax.dot_general(a_ref[...].astype(BF16), b_ref[...].astype(BF16), dims,
                               preferred_element_type=F32)
        if nk == 1:
            o_ref[...] = part.astype(o_ref.dtype)
        else:
            acc_ref, = scratch
            kk = pl.program_id(2)

            @pl.when(kk == 0)
            def _():
                acc_ref[...] = part

            @pl.when(kk > 0)
            def _():
                acc_ref[...] += part

            @pl.when(kk == nk - 1)
            def _():
                o_ref[...] = acc_ref[...].astype(o_ref.dtype)

    if dims is NT:
        in_specs = [pl.BlockSpec((tm, tk), lambda i, j, kk: (i, kk)),
                    pl.BlockSpec((tn, tk), lambda i, j, kk: (j + noff, kk))]
    elif dims is NN:
        in_specs = [pl.BlockSpec((tm, tk), lambda i, j, kk: (i, kk)),
                    pl.BlockSpec((tk, tn), lambda i, j, kk: (kk, j))]
    else:
        in_specs = [pl.BlockSpec((tk, tm), lambda i, j, kk: (kk, i)),
                    pl.BlockSpec((tk, tn), lambda i, j, kk: (kk, j))]
    return pl.pallas_call(
        body, name=name, grid=(m // tm, n // tn, nk),
        out_shape=jax.ShapeDtypeStruct((m, n), out_dtype),
        in_specs=in_specs,
        out_specs=pl.BlockSpec((tm, tn), lambda i, j, kk: (i, j)),
        scratch_shapes=[] if nk == 1 else [pltpu.VMEM((tm, tn), F32)],
        compiler_params=pltpu.CompilerParams(dimension_semantics=("parallel", "parallel", "arbitrary")),
    )(a, b)


def _tok_tile(s):
    return _tile(s, (512, 256, 128))


def _row_spec(tm, d):
    return pl.BlockSpec((tm, d), lambda i: (i, 0))


def _vec_spec(d):
    return pl.BlockSpec((1, d), lambda i: (0, 0))


def _seq_spec(d, tiles_per_seq):
    return pl.BlockSpec((1, 1, d), lambda i: (i // tiles_per_seq, 0, 0))


def _rms(x):
    return lax.rsqrt(jnp.mean(x * x, axis=-1, keepdims=True) + EPS)


def _norm_mod(x, g, sc, sh, s, name):
    t, d = x.shape
    tm = _tok_tile(s)

    def body(x_ref, g_ref, sc_ref, sh_ref, h_ref):
        xv = x_ref[...]
        h = (xv * _rms(xv)) * g_ref[...]
        h_ref[...] = (h * (1.0 + sc_ref[0]) + sh_ref[0]).astype(BF16)

    return pl.pallas_call(
        body, name=name, grid=(t // tm,),
        out_shape=jax.ShapeDtypeStruct((t, d), BF16),
        in_specs=[_row_spec(tm, d), _vec_spec(d), _seq_spec(d, s // tm), _seq_spec(d, s // tm)],
        out_specs=_row_spec(tm, d),
    )(x, g, sc, sh)


def _post_res_norm(x, y, g_post, ga, g_pre, sc, sh, s, name):
    t, d = x.shape
    tm = _tok_tile(s)

    def body(x_ref, y_ref, gp_ref, ga_ref, g_ref, sc_ref, sh_ref, xn_ref, h_ref):
        yv = y_ref[...]
        xn = x_ref[...] + ga_ref[0] * ((yv * _rms(yv)) * gp_ref[...])
        xn_ref[...] = xn
        h = (xn * _rms(xn)) * g_ref[...]
        h_ref[...] = (h * (1.0 + sc_ref[0]) + sh_ref[0]).astype(BF16)

    tps = s // tm
    return pl.pallas_call(
        body, name=name, grid=(t // tm,),
        out_shape=[jax.ShapeDtypeStruct((t, d), F32), jax.ShapeDtypeStruct((t, d), BF16)],
        in_specs=[_row_spec(tm, d), _row_spec(tm, d), _vec_spec(d), _seq_spec(d, tps),
                  _vec_spec(d), _seq_spec(d, tps), _seq_spec(d, tps)],
        out_specs=[_row_spec(tm, d), _row_spec(tm, d)],
    )(x, y, g_post, ga, g_pre, sc, sh)


def _post_res_loss(x, y, g_post, ga, target, s, name):
    t, d = x.shape
    tm = _tok_tile(s)

    def body(x_ref, y_ref, gp_ref, ga_ref, t_ref, dx_ref, loss_ref):
        yv = y_ref[...]
        err = x_ref[...] + ga_ref[0] * ((yv * _rms(yv)) * gp_ref[...]) - t_ref[...]
        dx_ref[...] = err * (1.0 / d)
        part = 0.5 * jnp.sum(jnp.mean(err * err, axis=-1, keepdims=True), axis=0, keepdims=True)

        @pl.when(pl.program_id(0) == 0)
        def _():
            loss_ref[...] = jnp.zeros_like(loss_ref)

        loss_ref[...] += part

    return pl.pallas_call(
        body, name=name, grid=(t // tm,),
        out_shape=[jax.ShapeDtypeStruct((t, d), F32), jax.ShapeDtypeStruct((8, 128), F32)],
        in_specs=[_row_spec(tm, d), _row_spec(tm, d), _vec_spec(d), _seq_spec(d, s // tm), _row_spec(tm, d)],
        out_specs=[_row_spec(tm, d), pl.BlockSpec((8, 128), lambda i: (0, 0))],
    )(x, y, g_post, ga, target)


def _post_bwd(dxn, y, g, ga, s, name):
    t, d = y.shape
    tm = _tok_tile(s)
    tps = s // tm

    def body(dx_ref, y_ref, g_ref, ga_ref, dy_ref, dg_ref, dga_ref):
        i = pl.program_id(0)
        yv = y_ref[...]
        r = _rms(yv)
        yh = yv * r
        dxv = dx_ref[...]
        dn = dxv * ga_ref[0]
        dyh = dn * g_ref[...]
        dy_ref[...] = (r * (dyh - yh * jnp.mean(dyh * yh, axis=-1, keepdims=True))).astype(BF16)

        @pl.when(i == 0)
        def _():
            dg_ref[...] = jnp.zeros_like(dg_ref)

        @pl.when(i % tps == 0)
        def _():
            dga_ref[...] = jnp.zeros_like(dga_ref)

        dg_ref[...] += jnp.sum(dn * yh, axis=0, keepdims=True)
        dga_ref[0] += jnp.sum(dxv * (yh * g_ref[...]), axis=0, keepdims=True)

    return pl.pallas_call(
        body, name=name, grid=(t // tm,),
        out_shape=[jax.ShapeDtypeStruct((t, d), BF16), jax.ShapeDtypeStruct((1, d), F32),
                   jax.ShapeDtypeStruct((t // s, 1, d), F32)],
        in_specs=[_row_spec(tm, d), _row_spec(tm, d), _vec_spec(d), _seq_spec(d, tps)],
        out_specs=[_row_spec(tm, d), _vec_spec(d), _seq_spec(d, tps)],
    )(dxn, y, g, ga)


def _pre_bwd(dh, x, g, sc, dres, s, name):
    t, d = x.shape
    tm = _tok_tile(s)
    tps = s // tm

    def body(dh_ref, x_ref, g_ref, sc_ref, dr_ref, dx_ref, dsh_ref, dsc_ref, dg_ref):
        i = pl.program_id(0)
        xv = x_ref[...]
        r = _rms(xv)
        xh = xv * r
        dhv = dh_ref[...]
        one_sc = 1.0 + sc_ref[0]
        dxh = dhv * one_sc * g_ref[...]
        dx_ref[...] = dr_ref[...] + r * (dxh - xh * jnp.mean(dxh * xh, axis=-1, keepdims=True))

        @pl.when(i == 0)
        def _():
            dg_ref[...] = jnp.zeros_like(dg_ref)

        @pl.when(i % tps == 0)
        def _():
            dsh_ref[...] = jnp.zeros_like(dsh_ref)
            dsc_ref[...] = jnp.zeros_like(dsc_ref)

        dg_ref[...] += jnp.sum(dhv * one_sc * xh, axis=0, keepdims=True)
        dsh_ref[0] += jnp.sum(dhv, axis=0, keepdims=True)
        dsc_ref[0] += jnp.sum(dhv * (xh * g_ref[...]), axis=0, keepdims=True)

    nb = t // s
    return pl.pallas_call(
        body, name=name, grid=(t // tm,),
        out_shape=[jax.ShapeDtypeStruct((t, d), F32), jax.ShapeDtypeStruct((nb, 1, d), F32),
                   jax.ShapeDtypeStruct((nb, 1, d), F32), jax.ShapeDtypeStruct((1, d), F32)],
        in_specs=[_row_spec(tm, d), _row_spec(tm, d), _vec_spec(d), _seq_spec(d, tps), _row_spec(tm, d)],
        out_specs=[_row_spec(tm, d), _seq_spec(d, tps), _seq_spec(d, tps), _vec_spec(d)],
    )(dh, x, g, sc, dres)


def _merge(y_conv, y_att, gates, s, name):
    t, d = y_conv.shape
    tm = _tok_tile(s)

    def body(yc_ref, ya_ref, gc_ref, gt_ref, o_ref):
        o_ref[...] = (_sig(gc_ref[...]) * yc_ref[...] + _sig(gt_ref[...]) * ya_ref[...]).astype(BF16)

    return pl.pallas_call(
        body, name=name, grid=(t // tm,),
        out_shape=jax.ShapeDtypeStruct((t, d), BF16),
        in_specs=[_row_spec(tm, d), _row_spec(tm, d), pl.BlockSpec((tm, d), lambda i: (i, 0)),
                  pl.BlockSpec((tm, d), lambda i: (i, 1))],
        out_specs=_row_spec(tm, d),
    )(y_conv, y_att, gates, gates)


def _merge_bwd(dm, y_conv, y_att, gates, s, name):
    t, d = y_conv.shape
    tm = _tok_tile(s)

    def body(dm_ref, yc_ref, ya_ref, gc_ref, gt_ref, dyc_ref, dya_ref, dg_ref):
        dmv = dm_ref[...]
        sc_, st_ = _sig(gc_ref[...]), _sig(gt_ref[...])
        dyc_ref[...] = (dmv * sc_).astype(BF16)
        dya_ref[...] = (dmv * st_).astype(BF16)
        dg_ref[:, :d] = (dmv * yc_ref[...] * (sc_ * (1.0 - sc_))).astype(BF16)
        dg_ref[:, d:] = (dmv * ya_ref[...] * (st_ * (1.0 - st_))).astype(BF16)

    return pl.pallas_call(
        body, name=name, grid=(t // tm,),
        out_shape=[jax.ShapeDtypeStruct((t, d), BF16), jax.ShapeDtypeStruct((t, d), BF16),
                   jax.ShapeDtypeStruct((t, 2 * d), BF16)],
        in_specs=[_row_spec(tm, d), _row_spec(tm, d), _row_spec(tm, d),
                  pl.BlockSpec((tm, d), lambda i: (i, 0)), pl.BlockSpec((tm, d), lambda i: (i, 1))],
        out_specs=[_row_spec(tm, d), _row_spec(tm, d), _row_spec(tm, 2 * d)],
    )(dm, y_conv, y_att, gates, gates)


def _swiglu(f, s, name):
    t, two = f.shape
    dff = two // 2
    tm = _tok_tile(s)

    def body(g_ref, u_ref, a_ref):
        gv = g_ref[...]
        a_ref[...] = (gv * _sig(gv) * u_ref[...]).astype(BF16)

    return pl.pallas_call(
        body, name=name, grid=(t // tm,),
        out_shape=jax.ShapeDtypeStruct((t, dff), BF16),
        in_specs=[pl.BlockSpec((tm, dff), lambda i: (i, 0)), pl.BlockSpec((tm, dff), lambda i: (i, 1))],
        out_specs=_row_spec(tm, dff),
    )(f, f)


def _swiglu_bwd(da, f, s, name):
    t, two = f.shape
    dff = two // 2
    tm = _tok_tile(s)

    def body(da_ref, g_ref, u_ref, df_ref):
        gv, dav = g_ref[...], da_ref[...]
        sg = _sig(gv)
        df_ref[:, :dff] = (dav * u_ref[...] * (sg * (1.0 + gv * (1.0 - sg)))).astype(BF16)
        df_ref[:, dff:] = (dav * (gv * sg)).astype(BF16)

    return pl.pallas_call(
        body, name=name, grid=(t // tm,),
        out_shape=jax.ShapeDtypeStruct((t, two), BF16),
        in_specs=[_row_spec(tm, dff), pl.BlockSpec((tm, dff), lambda i: (i, 0)),
                  pl.BlockSpec((tm, dff), lambda i: (i, 1))],
        out_specs=_row_spec(tm, two),
    )(da, f, f)


CONV_CHUNK = 64


def _conv_fwd(glu_in, conv_w, conv_b, ln_g, ln_b, nb, s, name):
    t, two = glu_in.shape
    cw = two // 2
    tt = _tile(s, (256, 128, 64))
    spt = s // tt
    ch = min(CONV_CHUNK, tt)

    def body(cur_ref, halo_ref, w_ref, b_ref, g_ref, be_ref, u_ref, cv_ref, ua_ref, ext_ref):
        j = pl.program_id(1)
        cur = cur_ref[...]
        u_cur = cur[:, :cw] * _sig(cur[:, cw:])
        hal = halo_ref[...]
        u_hal = hal[:, :cw] * _sig(hal[:, cw:])
        ext_ref[0:HALO, :] = jnp.where(j == 0, 0.0, u_hal)
        ext_ref[HALO:HALO + tt, :] = u_cur
        u_ref[...] = u_cur
        for c0 in range(0, tt, ch):
            acc = jnp.zeros((ch, cw), F32) + b_ref[...]
            for k in range(CONV_K):
                st = c0 + HALO - (CONV_K - 1) + k
                acc = acc + w_ref[k:k + 1, :] * ext_ref[st:st + ch, :]
            cv_ref[c0:c0 + ch, :] = acc
            mu = jnp.mean(acc, axis=-1, keepdims=True)
            xc = acc - mu
            yv = xc * lax.rsqrt(jnp.mean(xc * xc, axis=-1, keepdims=True) + EPS) * g_ref[...] + be_ref[...]
            ua_ref[c0:c0 + ch, :] = (yv * _sig(yv)).astype(BF16)

    cur_map = lambda b, j: (b * spt + j, 0)
    halo_map = lambda b, j: (jnp.maximum((b * s + j * tt) // HALO - 1, 0), 0)
    vec = lambda d0: pl.BlockSpec((d0, cw), lambda b, j: (0, 0))
    return pl.pallas_call(
        body, name=name, grid=(nb, spt),
        out_shape=[jax.ShapeDtypeStruct((t, cw), F32), jax.ShapeDtypeStruct((t, cw), F32),
                   jax.ShapeDtypeStruct((t, cw), BF16)],
        in_specs=[pl.BlockSpec((tt, two), cur_map), pl.BlockSpec((HALO, two), halo_map),
                  vec(CONV_K), vec(1), vec(1), vec(1)],
        out_specs=[pl.BlockSpec((tt, cw), cur_map)] * 3,
        scratch_shapes=[pltpu.VMEM((tt + HALO, cw), F32)],
    )(glu_in, glu_in, conv_w, conv_b, ln_g, ln_b)


def _convln_bwd(dua, cv, ln_g, ln_b, s, name):
    t, cw = cv.shape
    tm = _tok_tile(s)

    def body(du_ref, cv_ref, g_ref, be_ref, dcv_ref, dg_ref, dbe_ref, db_ref):
        cvv = cv_ref[...]
        mu = jnp.mean(cvv, axis=-1, keepdims=True)
        xc = cvv - mu
        rstd = lax.rsqrt(jnp.mean(xc * xc, axis=-1, keepdims=True) + EPS)
        xh = xc * rstd
        yv = xh * g_ref[...] + be_ref[...]
        sg = _sig(yv)
        dy = du_ref[...] * (sg * (1.0 + yv * (1.0 - sg)))
        dxh = dy * g_ref[...]
        dcv = rstd * (dxh - jnp.mean(dxh, axis=-1, keepdims=True)
                      - xh * jnp.mean(dxh * xh, axis=-1, keepdims=True))
        dcv_ref[...] = dcv

        @pl.when(pl.program_id(0) == 0)
        def _():
            dg_ref[...] = jnp.zeros_like(dg_ref)
            dbe_ref[...] = jnp.zeros_like(dbe_ref)
            db_ref[...] = jnp.zeros_like(db_ref)

        dg_ref[...] += jnp.sum(dy * xh, axis=0, keepdims=True)
        dbe_ref[...] += jnp.sum(dy, axis=0, keepdims=True)
        db_ref[...] += jnp.sum(dcv, axis=0, keepdims=True)

    return pl.pallas_call(
        body, name=name, grid=(t // tm,),
        out_shape=[jax.ShapeDtypeStruct((t, cw), F32)] + [jax.ShapeDtypeStruct((1, cw), F32)] * 3,
        in_specs=[_row_spec(tm, cw), _row_spec(tm, cw), _vec_spec(cw), _vec_spec(cw)],
        out_specs=[_row_spec(tm, cw), _vec_spec(cw), _vec_spec(cw), _vec_spec(cw)],
    )(dua, cv, ln_g, ln_b)


def _conv_bwd(dcv, u, glu_in, conv_w, nb, s, name):
    t, cw = dcv.shape
    tt = _tile(s, (256, 128, 64))
    spt = s // tt
    ch = min(CONV_CHUNK, tt)
    nblk = t // HALO

    def body(d_ref, dn_ref, u_ref, up_ref, glu_ref, w_ref, dglu_ref, dw_ref, dext_ref, uext_ref):
        b, j = pl.program_id(0), pl.program_id(1)
        dcur = d_ref[...]
        dext_ref[0:tt, :] = dcur
        dext_ref[tt:tt + HALO, :] = jnp.where(j == spt - 1, 0.0, dn_ref[...])
        uext_ref[0:HALO, :] = jnp.where(j == 0, 0.0, up_ref[...])
        uext_ref[HALO:HALO + tt, :] = u_ref[...]

        @pl.when((b == 0) & (j == 0))
        def _():
            dw_ref[...] = jnp.zeros_like(dw_ref)

        for c0 in range(0, tt, ch):
            dc = dext_ref[c0:c0 + ch, :]
            du = jnp.zeros((ch, cw), F32)
            for k in range(CONV_K):
                sd = c0 + (CONV_K - 1) - k
                du = du + w_ref[k:k + 1, :] * dext_ref[sd:sd + ch, :]
                su = c0 + HALO - (CONV_K - 1) + k
                dw_ref[k:k + 1, :] += jnp.sum(dc * uext_ref[su:su + ch, :], axis=0, keepdims=True)
            val = glu_ref[c0:c0 + ch, :cw]
            sg = _sig(glu_ref[c0:c0 + ch, cw:])
            dglu_ref[c0:c0 + ch, :cw] = (du * sg).astype(BF16)
            dglu_ref[c0:c0 + ch, cw:] = (du * val * (sg * (1.0 - sg))).astype(BF16)

    cur_map = lambda b, j: (b * spt + j, 0)
    prev_map = lambda b, j: (jnp.maximum((b * s + j * tt) // HALO - 1, 0), 0)
    next_map = lambda b, j: (jnp.minimum((b * s + (j + 1) * tt) // HALO, nblk - 1), 0)
    return pl.pallas_call(
        body, name=name, grid=(nb, spt),
        out_shape=[jax.ShapeDtypeStruct((t, 2 * cw), BF16), jax.ShapeDtypeStruct((HALO, cw), F32)],
        in_specs=[pl.BlockSpec((tt, cw), cur_map), pl.BlockSpec((HALO, cw), next_map),
                  pl.BlockSpec((tt, cw), cur_map), pl.BlockSpec((HALO, cw), prev_map),
                  pl.BlockSpec((tt, 2 * cw), cur_map), pl.BlockSpec((CONV_K, cw), lambda b, j: (0, 0))],
        out_specs=[pl.BlockSpec((tt, 2 * cw), cur_map), pl.BlockSpec((HALO, cw), lambda b, j: (0, 0))],
        scratch_shapes=[pltpu.VMEM((tt + HALO, cw), F32), pltpu.VMEM((tt + HALO, cw), F32)],
    )(dcv, dcv, u, u, glu_in, conv_w)


def _split(x):
    hi = x.astype(BF16)
    return hi, (x - hi.astype(F32)).astype(BF16)


def _tri(upper):
    j = lax.broadcasted_iota(jnp.int32, (KEY_BLK, KEY_BLK), 0)
    s_ = lax.broadcasted_iota(jnp.int32, (KEY_BLK, KEY_BLK), 1)
    return jnp.where(j > s_ if upper else j < s_, 1.0, 0.0).astype(BF16)


def _log_keep(z, causal):
    sp = jnp.maximum(z, 0.0) + jnp.log(1.0 + jnp.exp(-jnp.abs(z)))
    return -sp if causal is None else jnp.where(causal, -sp, 0.0)


def _attn_fwd(q, k, v, name):
    bh, s, dh = q.shape
    tq = _tile(s, (256, 128))
    scale = 1.0 / math.sqrt(dh)

    def body(q_ref, k_ref, v_ref, o_ref, car_ref, acc_ref, run_ref):
        qi = pl.program_id(1)
        qv = q_ref[0]
        lane = lax.broadcasted_iota(jnp.int32, (tq, KEY_BLK), 1)
        row = qi * tq + lax.broadcasted_iota(jnp.int32, (tq, KEY_BLK), 0)
        m_suf = _tri(True)
        nk = (qi * tq + tq) // KEY_BLK
        acc_ref[...] = jnp.zeros_like(acc_ref)
        run_ref[...] = jnp.zeros_like(run_ref)
        car_ref[...] = jnp.zeros_like(car_ref)

        def step(it, _):
            kb = nk - 1 - it
            off = pl.multiple_of(kb * KEY_BLK, KEY_BLK)
            kblk = k_ref[0, pl.ds(off, KEY_BLK), :]
            vblk = v_ref[0, pl.ds(off, KEY_BLK), :]
            z = lax.dot_general(qv, kblk, NT, preferred_element_type=F32) * scale
            causal = (off + lane) < row
            lk = _log_keep(z, causal)
            hi, lo = _split(lk)
            cs = (jnp.dot(hi, m_suf, preferred_element_type=F32)
                  + jnp.dot(lo, m_suf, preferred_element_type=F32))
            run = run_ref[...]
            a = jnp.where(causal, jnp.exp(z + lk + cs + run), 0.0)
            acc_ref[...] += jnp.dot(a.astype(BF16), vblk, preferred_element_type=F32)
            car_ref[0] = jnp.where(lane == kb, run, car_ref[0])
            run_ref[...] = run + jnp.sum(lk, axis=1, keepdims=True)
            return 0

        lax.fori_loop(0, nk, step, 0)
        o_ref[0] = acc_ref[...].astype(BF16)

    full = pl.BlockSpec((1, s, dh), lambda b, i: (b, 0, 0))
    return pl.pallas_call(
        body, name=name, grid=(bh, s // tq),
        out_shape=[jax.ShapeDtypeStruct((bh, s, dh), BF16), jax.ShapeDtypeStruct((bh, s, KEY_BLK), F32)],
        in_specs=[pl.BlockSpec((1, tq, dh), lambda b, i: (b, i, 0)), full, full],
        out_specs=[pl.BlockSpec((1, tq, dh), lambda b, i: (b, i, 0)),
                   pl.BlockSpec((1, tq, KEY_BLK), lambda b, i: (b, i, 0))],
        scratch_shapes=[pltpu.VMEM((tq, dh), F32), pltpu.VMEM((tq, KEY_BLK), F32)],
        compiler_params=pltpu.CompilerParams(dimension_semantics=("parallel", "arbitrary")),
    )(q, k, v)


def _attn_bwd(q, k, v, do, car, name):
    bh, s, dh = q.shape
    tq = _tile(s, (256, 128))
    scale = 1.0 / math.sqrt(dh)
    nkb, nqb = s // KEY_BLK, s // tq

    def body(q_ref, k_ref, v_ref, do_ref, car_ref, dq_ref, dk_ref, dv_ref, dq_acc, g_acc, dk_acc, dv_acc):
        lane = lax.broadcasted_iota(jnp.int32, (tq, KEY_BLK), 1)
        rowi = lax.broadcasted_iota(jnp.int32, (tq, KEY_BLK), 0)
        m_suf, m_pre = _tri(True), _tri(False)
        dq_acc[...] = jnp.zeros_like(dq_acc)
        g_acc[...] = jnp.zeros_like(g_acc)

        def kloop(kb, _):
            off = pl.multiple_of(kb * KEY_BLK, KEY_BLK)
            kblk = k_ref[0, pl.ds(off, KEY_BLK), :]
            vblk = v_ref[0, pl.ds(off, KEY_BLK), :]

            dk_acc[...] = jnp.zeros_like(dk_acc)
            dv_acc[...] = jnp.zeros_like(dv_acc)

            def qloop(qb, _):
                r0 = pl.multiple_of(qb * tq, tq)
                qblk = q_ref[0, pl.ds(r0, tq), :]
                doblk = do_ref[0, pl.ds(r0, tq), :]
                z = lax.dot_general(qblk, kblk, NT, preferred_element_type=F32) * scale
                causal = (off + lane) < (r0 + rowi)
                lk = _log_keep(z, causal)
                hi, lo = _split(lk)
                cs = (jnp.dot(hi, m_suf, preferred_element_type=F32)
                      + jnp.dot(lo, m_suf, preferred_element_type=F32))
                cpast = jnp.sum(jnp.where(lane == kb, car_ref[0, pl.ds(r0, tq), :], 0.0), axis=1, keepdims=True)
                lsig = z + lk
                a = jnp.where(causal, jnp.exp(lsig + cs + cpast), 0.0)
                da = lax.dot_general(doblk, vblk, NT, preferred_element_type=F32)
                g = a * da
                ghi, glo = _split(g)
                gpre = (jnp.dot(ghi, m_pre, preferred_element_type=F32)
                        + jnp.dot(glo, m_pre, preferred_element_type=F32)
                        + g_acc[pl.ds(r0, tq), :])
                dz = jnp.where(causal, g - jnp.exp(lsig) * (g + gpre), 0.0)
                ds_ = (dz * scale).astype(BF16)
                dv_acc[...] += lax.dot_general(a.astype(BF16), doblk, TN, preferred_element_type=F32)
                dk_acc[...] += lax.dot_general(ds_, qblk, TN, preferred_element_type=F32)
                dq_acc[pl.ds(r0, tq), :] += jnp.dot(ds_, kblk, preferred_element_type=F32)
                g_acc[pl.ds(r0, tq), :] += jnp.broadcast_to(jnp.sum(g, axis=1, keepdims=True), (tq, KEY_BLK))
                return 0

            lax.fori_loop(off // tq, nqb, qloop, 0)
            dk_ref[0, pl.ds(off, KEY_BLK), :] = dk_acc[...].astype(BF16)
            dv_ref[0, pl.ds(off, KEY_BLK), :] = dv_acc[...].astype(BF16)
            return 0

        lax.fori_loop(0, nkb, kloop, 0)
        dq_ref[0] = dq_acc[...].astype(BF16)

    full = pl.BlockSpec((1, s, dh), lambda b: (b, 0, 0))
    return pl.pallas_call(
        body, name=name, grid=(bh,),
        out_shape=[jax.ShapeDtypeStruct((bh, s, dh), BF16)] * 3,
        in_specs=[full, full, full, full, pl.BlockSpec((1, s, KEY_BLK), lambda b: (b, 0, 0))],
        out_specs=[full, full, full],
        scratch_shapes=[pltpu.VMEM((s, dh), F32), pltpu.VMEM((s, KEY_BLK), F32),
                        pltpu.VMEM((KEY_BLK, dh), F32), pltpu.VMEM((KEY_BLK, dh), F32)],
        compiler_params=pltpu.CompilerParams(dimension_semantics=("parallel",)),
    )(q, k, v, do, car)


LANES = 128


def _sb_fwd(qkv, nb, s, name):
    t, three_aw = qkv.shape
    aw = three_aw // 3
    npair = aw // LANES
    tq = _tile(s, (256, 128))
    ndiag = tq // KEY_BLK
    spt = s // tq
    scale = 1.0 / math.sqrt(HEAD_DIM)

    def body(q_ref, k_ref, v_ref, o_ref, car_ref, acc_ref, run_ref):
        qi = pl.program_id(2)
        lane = lax.broadcasted_iota(jnp.int32, (tq, LANES), 1)
        col = lax.broadcasted_iota(jnp.int32, (tq, KEY_BLK), 1)
        row = qi * tq + lax.broadcasted_iota(jnp.int32, (tq, KEY_BLK), 0)
        first = lane < HEAD_DIM
        q2 = q_ref[...]
        zero = jnp.zeros_like(q2)
        qh = (jnp.where(first, q2, zero), jnp.where(first, zero, q2))
        m_suf = _tri(True)
        nk = (qi * tq + tq) // KEY_BLK
        acc_ref[...] = jnp.zeros_like(acc_ref)
        run_ref[...] = jnp.zeros_like(run_ref)
        car_ref[...] = jnp.zeros_like(car_ref)

        def step(it, masked):
            kb = nk - 1 - it
            off = pl.multiple_of(kb * KEY_BLK, KEY_BLK)
            kblk = k_ref[pl.ds(off, KEY_BLK), :]
            vblk = v_ref[pl.ds(off, KEY_BLK), :]
            causal = ((off + col) < row) if masked else None
            hs = range(2)
            zs = [lax.dot_general(qh[h], kblk, NT, preferred_element_type=F32) * scale for h in hs]
            lks = [_log_keep(zs[h], causal) for h in hs]
            sp = [_split(lks[h]) for h in hs]
            css = [jnp.dot(sp[h][0], m_suf, preferred_element_type=F32)
                   + jnp.dot(sp[h][1], m_suf, preferred_element_type=F32) for h in hs]
            runs = [run_ref[h] for h in hs]
            aa = [jnp.exp(zs[h] + lks[h] + css[h] + jnp.concatenate([runs[h]] * (KEY_BLK // LANES), axis=1))
                  for h in hs]
            if masked:
                aa = [jnp.where(causal, aa[h], 0.0) for h in hs]
            for h in hs:
                acc_ref[h] += jnp.dot(aa[h].astype(BF16), vblk, preferred_element_type=F32)
            for h in hs:
                car_ref[h] = jnp.where(lane == kb, runs[h], car_ref[h])
                run_ref[h] = runs[h] + jnp.sum(lks[h], axis=1, keepdims=True)

        for it in range(ndiag):
            step(it, True)

        def rest(it, _):
            step(it, False)
            return 0

        lax.fori_loop(ndiag, nk, rest, 0)
        o_ref[...] = jnp.where(first, acc_ref[0], acc_ref[1]).astype(BF16)

    return pl.pallas_call(
        body, name=name, grid=(nb, npair, spt),
        out_shape=[jax.ShapeDtypeStruct((t, aw), BF16),
                   jax.ShapeDtypeStruct((nb * npair * 2, s, LANES), F32)],
        in_specs=[pl.BlockSpec((tq, LANES), lambda b, p, i: (b * spt + i, p)),
                  pl.BlockSpec((s, LANES), lambda b, p, i: (b, npair + p)),
                  pl.BlockSpec((s, LANES), lambda b, p, i: (b, 2 * npair + p))],
        out_specs=[pl.BlockSpec((tq, LANES), lambda b, p, i: (b * spt + i, p)),
                   pl.BlockSpec((2, tq, LANES), lambda b, p, i: (b * npair + p, i, 0))],
        scratch_shapes=[pltpu.VMEM((2, tq, LANES), F32), pltpu.VMEM((2, tq, LANES), F32)],
        compiler_params=pltpu.CompilerParams(dimension_semantics=("parallel", "parallel", "arbitrary")),
    )(qkv, qkv, qkv)


def _sb_bwd(qkv, do, car, nb, s, name):
    t, three_aw = qkv.shape
    aw = three_aw // 3
    npair = aw // LANES
    tq = _tile(s, (256, 128))
    scale = 1.0 / math.sqrt(HEAD_DIM)
    nkb, nqb = s // KEY_BLK, s // tq

    def body(q_ref, k_ref, v_ref, do_ref, car_ref, dq_ref, dk_ref, dv_ref, dq_acc, g_acc, dk_acc, dv_acc):
        lane = lax.broadcasted_iota(jnp.int32, (tq, LANES), 1)
        col = lax.broadcasted_iota(jnp.int32, (tq, KEY_BLK), 1)
        rowi = lax.broadcasted_iota(jnp.int32, (tq, KEY_BLK), 0)
        first_k = lax.broadcasted_iota(jnp.int32, (KEY_BLK, LANES), 1) < HEAD_DIM
        m_suf, m_pre = _tri(True), _tri(False)
        dq_acc[...] = jnp.zeros_like(dq_acc)
        g_acc[...] = jnp.zeros_like(g_acc)

        def kloop(kb, _):
            off = pl.multiple_of(kb * KEY_BLK, KEY_BLK)
            kblk = k_ref[pl.ds(off, KEY_BLK), :]
            vblk = v_ref[pl.ds(off, KEY_BLK), :]
            zero = jnp.zeros_like(kblk)
            kh = (jnp.where(first_k, kblk, zero), jnp.where(first_k, zero, kblk))
            vh = (jnp.where(first_k, vblk, zero), jnp.where(first_k, zero, vblk))
            dk_acc[...] = jnp.zeros_like(dk_acc)
            dv_acc[...] = jnp.zeros_like(dv_acc)

            def tile(qb, masked):
                r0 = pl.multiple_of(qb * tq, tq)
                q2 = q_ref[pl.ds(r0, tq), :]
                do2 = do_ref[pl.ds(r0, tq), :]
                causal = ((off + col) < (r0 + rowi)) if masked else None
                hs = range(2)
                wide = lambda x: jnp.concatenate([x] * (KEY_BLK // LANES), axis=1)
                zs = [lax.dot_general(q2, kh[h], NT, preferred_element_type=F32) * scale for h in hs]
                das = [lax.dot_general(do2, vh[h], NT, preferred_element_type=F32) for h in hs]
                lks = [_log_keep(zs[h], causal) for h in hs]
                sp = [_split(lks[h]) for h in hs]
                css = [jnp.dot(sp[h][0], m_suf, preferred_element_type=F32)
                       + jnp.dot(sp[h][1], m_suf, preferred_element_type=F32) for h in hs]
                cpast = [jnp.sum(jnp.where(lane == kb, car_ref[h, pl.ds(r0, tq), :], 0.0), axis=1, keepdims=True)
                         for h in hs]
                lsig = [zs[h] + lks[h] for h in hs]
                aa = [jnp.exp(lsig[h] + css[h] + cpast[h]) for h in hs]
                if masked:
                    aa = [jnp.where(causal, aa[h], 0.0) for h in hs]
                gs = [aa[h] * das[h] for h in hs]
                gsp = [_split(gs[h]) for h in hs]
                gpre = [jnp.dot(gsp[h][0], m_pre, preferred_element_type=F32)
                        + jnp.dot(gsp[h][1], m_pre, preferred_element_type=F32)
                        + wide(g_acc[h, pl.ds(r0, tq), :]) for h in hs]
                dzs = [gs[h] - jnp.exp(lsig[h]) * (gs[h] + gpre[h]) for h in hs]
                if masked:
                    dzs = [jnp.where(causal, dzs[h], 0.0) for h in hs]
                dss = [(dzs[h] * scale).astype(BF16) for h in hs]
                for h in hs:
                    dv_acc[h] += lax.dot_general(aa[h].astype(BF16), do2, TN, preferred_element_type=F32)
                for h in hs:
                    dk_acc[h] += lax.dot_general(dss[h], q2, TN, preferred_element_type=F32)
                for h in hs:
                    dq_acc[h, pl.ds(r0, tq), :] += jnp.dot(dss[h], kblk, preferred_element_type=F32)
                for h in hs:
                    g_acc[h, pl.ds(r0, tq), :] += jnp.broadcast_to(jnp.sum(gs[h], axis=1, keepdims=True),
                                                                   (tq, LANES))

            q0 = off // tq
            tile(q0, True)

            def rest(qb, _):
                tile(qb, False)
                return 0

            lax.fori_loop(q0 + 1, nqb, rest, 0)
            dk_ref[pl.ds(off, KEY_BLK), :] = jnp.where(first_k, dk_acc[0], dk_acc[1]).astype(BF16)
            dv_ref[pl.ds(off, KEY_BLK), :] = jnp.where(first_k, dv_acc[0], dv_acc[1]).astype(BF16)
            return 0

        lax.fori_loop(0, nkb, kloop, 0)
        first_q = lax.broadcasted_iota(jnp.int32, (s, LANES), 1) < HEAD_DIM
        dq_ref[...] = jnp.where(first_q, dq_acc[0], dq_acc[1]).astype(BF16)

    col = lambda j: pl.BlockSpec((s, LANES), lambda b, p: (b, j * npair + p))
    return pl.pallas_call(
        body, name=name, grid=(nb, npair),
        out_shape=[jax.ShapeDtypeStruct((t, aw), BF16)] * 3,
        in_specs=[col(0), col(1), col(2), col(0),
                  pl.BlockSpec((2, s, LANES), lambda b, p: (b * npair + p, 0, 0))],
        out_specs=[col(0), col(0), col(0)],
        scratch_shapes=[pltpu.VMEM((2, s, LANES), F32), pltpu.VMEM((2, s, LANES), F32),
                        pltpu.VMEM((2, KEY_BLK, LANES), F32), pltpu.VMEM((2, KEY_BLK, LANES), F32)],
        compiler_params=pltpu.CompilerParams(dimension_semantics=("parallel", "parallel")),
    )(qkv, qkv, qkv, do, car)


def _ada_fwd(c_all, ada_w, ada_b_loc, name):
    nl, d, n6 = ada_w.shape
    nb = c_all.shape[0]

    def body(c_ref, w_ref, b_ref, o_ref):
        cv = c_ref[...]
        o_ref[0] = jnp.dot(cv * _sig(cv), w_ref[0], preferred_element_type=F32) + b_ref[0]

    return pl.pallas_call(
        body, name=name, grid=(nl,),
        out_shape=jax.ShapeDtypeStruct((nl, nb, n6), F32),
        in_specs=[pl.BlockSpec((nb, d), lambda l: (0, 0)), pl.BlockSpec((1, d, n6), lambda l: (l, 0, 0)),
                  pl.BlockSpec((1, 1, n6), lambda l: (l, 0, 0))],
        out_specs=pl.BlockSpec((1, nb, n6), lambda l: (l, 0, 0)),
    )(c_all, ada_w, ada_b_loc)


def _ada_bwd(c_all_t, dmod_loc, dmod_all, name):
    nl, nb, n6 = dmod_loc.shape
    d = c_all_t.shape[0]
    n_all = dmod_all.shape[2]

    def body(c_ref, dl_ref, da_ref, gw_ref, gb_ref):
        cv = c_ref[...]
        gw_ref[0] = jnp.dot(cv * _sig(cv), dl_ref[0], preferred_element_type=F32)
        gb_ref[0] = jnp.sum(da_ref[0], axis=0, keepdims=True)

    return pl.pallas_call(
        body, name=name, grid=(nl,),
        out_shape=[jax.ShapeDtypeStruct((nl, d, n6), F32), jax.ShapeDtypeStruct((nl, 1, n_all), F32)],
        in_specs=[pl.BlockSpec((d, nb), lambda l: (0, 0)), pl.BlockSpec((1, nb, n6), lambda l: (l, 0, 0)),
                  pl.BlockSpec((1, nb, n_all), lambda l: (l, 0, 0))],
        out_specs=[pl.BlockSpec((1, d, n6), lambda l: (l, 0, 0)), pl.BlockSpec((1, 1, n_all), lambda l: (l, 0, 0))],
    )(c_all_t, dmod_loc, dmod_all)


def _adamw(w, g, m, v, name):
    shape = w.shape
    cols = shape[-1]
    rows = w.size // cols
    tr = _tile(rows, (512, 256, 128, 64, 32, 16, 8))
    flat = lambda a: a.reshape(rows, cols)

    def body(w_ref, g_ref, m_ref, v_ref, d_ref, mo_ref, vo_ref):
        gv = g_ref[...]
        mn = ADAM_B1 * m_ref[...] + (1.0 - ADAM_B1) * gv
        vn = ADAM_B2 * v_ref[...] + (1.0 - ADAM_B2) * (gv * gv)
        m_hat = mn / (1.0 - ADAM_B1 ** ADAM_STEP)
        v_hat = vn / (1.0 - ADAM_B2 ** ADAM_STEP)
        d_ref[...] = -ADAM_LR * (m_hat / (jnp.sqrt(v_hat) + ADAM_EPS) + ADAM_WD * w_ref[...])
        mo_ref[...] = mn
        vo_ref[...] = vn

    spec = pl.BlockSpec((tr, cols), lambda i: (i, 0))
    outs = pl.pallas_call(
        body, name=name, grid=(rows // tr,),
        out_shape=[jax.ShapeDtypeStruct((rows, cols), F32)] * 3,
        in_specs=[spec] * 4, out_specs=[spec] * 3,
    )(flat(w), flat(g), flat(m), flat(v))
    return tuple(o.reshape(shape) for o in outs)


def _pack_rows(parts, d):
    flat = jnp.concatenate([p.reshape(-1) for p in parts])
    rows = -(-flat.size // d)
    rows = -(-rows // 8) * 8
    return jnp.pad(flat, (0, rows * d - flat.size)).reshape(rows, d)


def _to_heads(a, nb, s):
    h = a.shape[1] // HEAD_DIM
    return a.reshape(nb, s, h, HEAD_DIM).transpose(0, 2, 1, 3).reshape(nb * h, s, HEAD_DIM)


def _from_heads(a, nb, s):
    h = a.shape[0] // nb
    return a.reshape(nb, h, s, HEAD_DIM).transpose(0, 2, 1, 3).reshape(nb * s, h * HEAD_DIM)


def kernel(x, c, ada_w, ada_b, pre_mix_g, post_mix_g, pre_ffn_g, post_ffn_g, w_in, conv_w, conv_b, conv_ln_g, conv_ln_b, w_conv_out, w_att_out, w_o, w_ffn_in, w_ffn_out, loss_target, m_ada_w, m_ada_b, m_pre_mix_g, m_post_mix_g, m_pre_ffn_g, m_post_ffn_g, m_w_in, m_conv_w, m_conv_b, m_conv_ln_g, m_conv_ln_b, m_w_conv_out, m_w_att_out, m_w_o, m_w_ffn_in, m_w_ffn_out, v_ada_w, v_ada_b, v_pre_mix_g, v_post_mix_g, v_pre_ffn_g, v_post_ffn_g, v_w_in, v_conv_w, v_conv_b, v_conv_ln_g, v_conv_ln_b, v_w_conv_out, v_w_att_out, v_w_o, v_w_ffn_in, v_w_ffn_out):
    nb, s, d = x.shape
    nl = ada_w.shape[0]
    cw = conv_b.shape[1]
    aw = w_att_out.shape[1]
    cl = conv_w.shape[2]
    n6 = ada_w.shape[2]
    t = nb * s
    me = _my_index()

    conv_flat = conv_w.reshape(-1)
    p1 = _pack_rows([c, conv_flat], d)
    r1 = p1.shape[0]
    (p1_all,) = _all_gather([p1], "gather_c_convw", True)
    p1_all = p1_all.reshape(N_DEV, r1 * d)
    c_all = p1_all[:, :nb * d].reshape(N_DEV * nb, d)
    conv_w_all = p1_all[:, nb * d:nb * d + conv_flat.size].reshape(N_DEV, nl, CONV_K, cl)
    conv_w_all = conv_w_all.transpose(1, 2, 0, 3).reshape(nl, CONV_K, cw)

    ada_b_loc = lax.dynamic_slice_in_dim(ada_b, me * n6, n6, axis=1).reshape(nl, 1, n6)
    mod_cols = _ada_fwd(c_all, ada_w, ada_b_loc, "ada_fwd")
    (mod_g,) = _all_gather([mod_cols.reshape(-1, d)], "gather_mod", True)
    mod_g = mod_g.reshape(N_DEV, nl, N_DEV * nb, n6)
    mod_mine = lax.dynamic_slice_in_dim(mod_g, me * nb, nb, axis=2)
    mod = mod_mine.transpose(1, 2, 0, 3).reshape(nl, nb, 6 * d)
    mods = [[mod[l, :, i * d:(i + 1) * d].reshape(nb, 1, d) for i in range(6)] for l in range(nl)]

    tr_ = lambda w: jnp.swapaxes(w, 0, 1).astype(BF16)
    shards = []
    for l in range(nl):
        shards += [tr_(w_in[l]), tr_(w_conv_out[l]), tr_(w_att_out[l]), w_o[l].astype(BF16),
                   tr_(w_ffn_in[l]), w_ffn_out[l].astype(BF16)]
    full = _all_gather(shards, "gather_weights", False)
    weights = [full[6 * l:6 * l + 6] for l in range(nl)]

    vec = lambda a, l: a[l].reshape(1, -1)
    x2 = x.reshape(t, d)
    tgt = loss_target.reshape(t, d)

    saved = []
    xin = x2
    h = _norm_mod(xin, vec(pre_mix_g, 0), mods[0][1], mods[0][0], s, "norm_mod_0")
    for l in range(nl):
        win_t, wconv_t, watt_t, wo, wffn_in_t, wffn_out = weights[l]
        sh1, sc1, ga1, sh2, sc2, ga2 = mods[l]
        glu_in = _mm(h, win_t, NT, F32, f"proj_glu_{l}", 0, 2 * cw)
        qkv = _mm(h, win_t, NT, BF16, f"proj_qkv_{l}", 2 * cw, 3 * aw)
        gates = _mm(h, win_t, NT, F32, f"proj_gates_{l}", 2 * cw + 3 * aw, 2 * d)
        u, cv, ua = _conv_fwd(glu_in, conv_w_all[l], vec(conv_b, l), vec(conv_ln_g, l), vec(conv_ln_b, l),
                              nb, s, f"conv_fwd_{l}")
        o, car = _sb_fwd(qkv, nb, s, f"attn_fwd_{l}")
        y_conv = _mm(ua, wconv_t, NT, F32, f"conv_out_{l}")
        y_att = _mm(o, watt_t, NT, F32, f"att_out_{l}")
        merged = _merge(y_conv, y_att, gates, s, f"merge_{l}")
        y = _mm(merged, wo, NN, F32, f"w_o_{l}")
        x1, h2 = _post_res_norm(xin, y, vec(post_mix_g, l), ga1, vec(pre_ffn_g, l), sc2, sh2, s, f"post_mix_{l}")
        f = _mm(h2, wffn_in_t, NT, F32, f"ffn_in_{l}")
        a = _swiglu(f, s, f"swiglu_{l}")
        y2 = _mm(a, wffn_out, NN, F32, f"ffn_out_{l}")
        saved.append(dict(xin=xin, h=h, glu_in=glu_in, gates=gates, u=u, cv=cv, ua=ua, qkv=qkv,
                          car=car, o=o, y_conv=y_conv, y_att=y_att, merged=merged, y=y, x1=x1, h2=h2, f=f,
                          a=a, y2=y2))
        if l + 1 < nl:
            nsh1, nsc1 = mods[l + 1][0], mods[l + 1][1]
            xin, h = _post_res_norm(x1, y2, vec(post_ffn_g, l), ga2, vec(pre_mix_g, l + 1), nsc1, nsh1, s,
                                    f"post_ffn_{l}")
        else:
            dx, loss_acc = _post_res_loss(x1, y2, vec(post_ffn_g, l), ga2, tgt, s, "loss")

    loss = lax.psum(loss_acc[0, 0], ("x", "y", "c"))

    big_grads = [None] * nl
    small = [None] * nl
    for l in reversed(range(nl)):
        win_t, wconv_t, watt_t, wo, wffn_in_t, wffn_out = weights[l]
        sh1, sc1, ga1, sh2, sc2, ga2 = mods[l]
        sv = saved[l]
        dy2, dg_post_ffn, dga2 = _post_bwd(dx, sv["y2"], vec(post_ffn_g, l), ga2, s, f"post_ffn_bwd_{l}")
        d_wffn_out = _mm(sv["a"], dy2, TN, BF16, f"d_w_ffn_out_{l}")
        da = _mm(dy2, wffn_out, NT, F32, f"d_a_{l}")
        df = _swiglu_bwd(da, sv["f"], s, f"swiglu_bwd_{l}")
        d_wffn_in_t = _mm(df, sv["h2"], TN, BF16, f"d_w_ffn_in_{l}")
        dh2 = _mm(df, wffn_in_t, NN, F32, f"d_h2_{l}")
        dx1, dsh2, dsc2, dg_pre_ffn = _pre_bwd(dh2, sv["x1"], vec(pre_ffn_g, l), sc2, dx, s, f"pre_ffn_bwd_{l}")
        dy, dg_post_mix, dga1 = _post_bwd(dx1, sv["y"], vec(post_mix_g, l), ga1, s, f"post_mix_bwd_{l}")
        d_wo = _mm(sv["merged"], dy, TN, BF16, f"d_w_o_{l}")
        dmerged = _mm(dy, wo, NT, F32, f"d_merged_{l}")
        dyc, dya, dgates = _merge_bwd(dmerged, sv["y_conv"], sv["y_att"], sv["gates"], s, f"merge_bwd_{l}")
        d_wconv_t = _mm(dyc, sv["ua"], TN, BF16, f"d_w_conv_out_{l}")
        dua = _mm(dyc, wconv_t, NN, F32, f"d_ua_{l}")
        d_watt_t = _mm(dya, sv["o"], TN, BF16, f"d_w_att_out_{l}")
        do = _mm(dya, watt_t, NN, BF16, f"d_o_{l}")
        dq, dk, dv = _sb_bwd(sv["qkv"], do, sv["car"], nb, s, f"attn_bwd_{l}")
        dcv, dln_g, dln_b, dconv_b = _convln_bwd(dua, sv["cv"], vec(conv_ln_g, l), vec(conv_ln_b, l), s,
                                                 f"convln_bwd_{l}")
        dglu, dconv_w = _conv_bwd(dcv, sv["u"], sv["glu_in"], conv_w_all[l], nb, s, f"conv_bwd_{l}")
        dproj = jnp.concatenate([dglu, dq, dk, dv, dgates], axis=1)
        d_win_t = _mm(dproj, sv["h"], TN, BF16, f"d_w_in_{l}")
        dh = _mm(dproj, win_t, NN, F32, f"d_h_{l}")
        dx, dsh1, dsc1, dg_pre_mix = _pre_bwd(dh, sv["xin"], vec(pre_mix_g, l), sc1, dx1, s, f"pre_mix_bwd_{l}")
        big_grads[l] = [d_win_t, d_wconv_t, d_watt_t, d_wo, d_wffn_in_t, d_wffn_out]
        dmod = jnp.concatenate([dsh1, dsc1, dga1, dsh2, dsc2, dga2], axis=2)
        small[l] = [dg_pre_mix, dg_post_mix, dg_pre_ffn, dg_post_ffn, dconv_b, dln_g, dln_b, dconv_w, dmod]
    grad_x = dx.reshape(nb, s, d)

    sizes = [a.size for a in small[0]]
    p3 = _pack_rows([a for l in range(nl) for a in small[l]], d)
    r3 = p3.shape[0]
    (p3_all,) = _all_gather([p3], "gather_small_grads", True)
    p3_all = p3_all.reshape(N_DEV, r3, d)
    p3_sum = _sum_slots(p3_all, "sum_small_grads").reshape(-1)
    per_layer = sum(sizes)
    small_sum, dmod_all = [], []
    for l in range(nl):
        off, parts = l * per_layer, []
        for sz in sizes[:-1]:
            parts.append(p3_sum[off:off + sz])
            off += sz
        small_sum.append(parts)
        dm = p3_all.reshape(N_DEV, r3 * d)[:, off:off + sizes[-1]]
        dmod_all.append(dm.reshape(N_DEV * nb, 6 * d))
    dmod_all = jnp.stack(dmod_all)
    dmod_loc = lax.dynamic_slice_in_dim(dmod_all, me * n6, n6, axis=2)
    g_ada_w, g_ada_b = _ada_bwd(c_all.T, dmod_loc, dmod_all, "ada_bwd")
    g_ada_b = g_ada_b.reshape(nl, 6 * d)

    stack_small = lambda i, shape: jnp.stack([small_sum[l][i].reshape(shape) for l in range(nl)])
    g_pre_mix, g_post_mix, g_pre_ffn, g_post_ffn = (stack_small(i, (d,)) for i in range(4))
    g_conv_b, g_ln_g, g_ln_b = (stack_small(i, (cw,)) for i in range(4, 7))
    g_conv_w_all = stack_small(7, (HALO, cw))[:, :CONV_K]
    g_conv_w = lax.dynamic_slice_in_dim(g_conv_w_all, me * cl, cl, axis=2)

    slots = _exchange_blocks([g for l in range(nl) for g in big_grads[l]], "exchange_grads")
    sums = [_sum_slots(r, f"sum_grads_{i}") for i, r in enumerate(slots)]
    un_t = lambda i: jnp.stack([jnp.swapaxes(sums[6 * l + i], 0, 1) for l in range(nl)])
    keep = lambda i: jnp.stack([sums[6 * l + i] for l in range(nl)])
    g_w_in, g_w_conv_out, g_w_att_out, g_w_o, g_w_ffn_in, g_w_ffn_out = (
        un_t(0), un_t(1), un_t(2), keep(3), un_t(4), keep(5))

    grads = [g_ada_w, g_ada_b, g_pre_mix, g_post_mix, g_pre_ffn, g_post_ffn, g_w_in, g_conv_w, g_conv_b,
             g_ln_g, g_ln_b, g_w_conv_out, g_w_att_out, g_w_o, g_w_ffn_in, g_w_ffn_out]
    ws = [ada_w, ada_b, pre_mix_g, post_mix_g, pre_ffn_g, post_ffn_g, w_in, conv_w, conv_b, conv_ln_g,
          conv_ln_b, w_conv_out, w_att_out, w_o, w_ffn_in, w_ffn_out]
    ms = [m_ada_w, m_ada_b, m_pre_mix_g, m_post_mix_g, m_pre_ffn_g, m_post_ffn_g, m_w_in, m_conv_w, m_conv_b,
          m_conv_ln_g, m_conv_ln_b, m_w_conv_out, m_w_att_out, m_w_o, m_w_ffn_in, m_w_ffn_out]
    vs = [v_ada_w, v_ada_b, v_pre_mix_g, v_post_mix_g, v_pre_ffn_g, v_post_ffn_g, v_w_in, v_conv_w, v_conv_b,
          v_conv_ln_g, v_conv_ln_b, v_w_conv_out, v_w_att_out, v_w_o, v_w_ffn_in, v_w_ffn_out]
    deltas, new_m, new_v = [], [], []
    for i, (w, g, m, v) in enumerate(zip(ws, grads, ms, vs)):
        dlt, mn, vn = _adamw(w, g, m, v, f"adamw_{i}")
        deltas.append(dlt)
        new_m.append(mn)
        new_v.append(vn)
    return (loss, grad_x, *grads, *deltas, *new_m, *new_v)
```

```python
import functools
import math

import jax
import jax.numpy as jnp
from jax import lax
from jax.experimental import pallas as pl
from jax.experimental.pallas import tpu as pltpu

F32 = jnp.float32
BF16 = jnp.bfloat16
MESH = pl.DeviceIdType.MESH
N_DEV = 8
EPS = 1e-6
HEAD_DIM = 64
CONV_K = 31
HALO = 32
KEY_BLK = 256
ADAM_LR, ADAM_B1, ADAM_B2, ADAM_EPS, ADAM_WD, ADAM_STEP = 0.001, 0.9, 0.999, 1e-08, 0.01, 10

NT = (((1,), (1,)), ((), ()))
NN = (((1,), (0,)), ((), ()))
TN = (((0,), (0,)), ((), ()))


def _tile(n, cands):
    for t in cands:
        if n % t == 0:
            return t
    return n


def _lane_tile(n, cap):
    best = n
    for t in range(128, min(n, cap) + 1, 128):
        if n % t == 0:
            best = t
    return best


def _sig(x):
    return 1.0 / (1.0 + jnp.exp(-x))


def _pos():
    return lax.axis_index("x"), lax.axis_index("y"), lax.axis_index("c")


def _my_index():
    x, y, c = _pos()
    return 4 * x + 2 * y + c


def _all_gather(arrs, name, in_vmem):
    n = len(arrs)

    def body(*refs):
        ins, outs = refs[:n], refs[n:2 * n]
        send_sems, recv_sems, local_sems = refs[2 * n:]
        x, y, c = _pos()
        me, sib = (x, y, c), (x, y, 1 - c)
        chips = [(1 - x, y), (x, 1 - y), (1 - x, 1 - y)]

        def rows(i, p):
            r = ins[i].shape[0]
            return outs[i].at[pl.ds((4 * p[0] + 2 * p[1] + p[2]) * r, r), :]

        def copy(i, k, block, to, src=None):
            return pltpu.make_async_remote_copy(
                src_ref=rows(i, block) if src is None else src, dst_ref=rows(i, block),
                send_sem=send_sems.at[i, k], recv_sem=recv_sems.at[i, k],
                device_id=to, device_id_type=MESH)

        mine = [pltpu.make_async_copy(ins[i], rows(i, me), local_sems.at[i]) for i in range(n)]
        for cp in mine:
            cp.start()
        first = []
        for j, chip in enumerate(chips):
            for i in range(n):
                first.append(copy(i, 1 + j, me, (*chip, c), src=ins[i]))
        for i in range(n):
            first.append(copy(i, 0, me, sib, src=ins[i]))
        for cp in first:
            cp.start()
        passed = []
        for j, chip in enumerate(chips):
            for i in range(n):
                copy(i, 1 + j, (*chip, c), me).wait_recv()
                p = copy(i, 4 + j, (*chip, c), sib)
                p.start()
                passed.append(p)
        for i in range(n):
            copy(i, 0, sib, me).wait_recv()
            for j, chip in enumerate(chips):
                copy(i, 4 + j, (*chip, 1 - c), me).wait_recv()
        for cp in first + passed:
            cp.wait_send()
        for cp in mine:
            cp.wait()

    space = pltpu.VMEM if in_vmem else pl.ANY
    spec = pl.BlockSpec(memory_space=space)
    return pl.pallas_call(
        body, name=name,
        out_shape=[jax.ShapeDtypeStruct((N_DEV * a.shape[0], a.shape[1]), a.dtype) for a in arrs],
        in_specs=[spec] * n, out_specs=[spec] * n,
        scratch_shapes=[pltpu.SemaphoreType.DMA((n, 7)), pltpu.SemaphoreType.DMA((n, 7)),
                        pltpu.SemaphoreType.DMA((n,))],
    )(*arrs)


def _exchange_blocks(arrs, name):
    n = len(arrs)

    def body(*refs):
        copies = _exchange_copies(refs[:n], refs[n:2 * n], *refs[2 * n:])
        _start_copies(copies)
        _finish_copies(copies)

    spec = pl.BlockSpec(memory_space=pl.ANY)
    return pl.pallas_call(
        body, name=name, out_shape=_exchange_out_shapes(arrs),
        in_specs=[spec] * n, out_specs=[spec] * n, scratch_shapes=_copy_sems(n),
    )(*arrs)


PEER_ORDER = (4, 2, 6, 5, 3, 7, 1)


def _peer(k):
    x, y, c = _pos()
    return (1 - x if k & 4 else x, 1 - y if k & 2 else y, 1 - c if k & 1 else c)


def _mesh_index(p):
    return 4 * p[0] + 2 * p[1] + p[2]


def _copy_sems(n):
    return [pltpu.SemaphoreType.DMA((n, 7)), pltpu.SemaphoreType.DMA((n, 7)), pltpu.SemaphoreType.DMA((n,))]


def _exchange_out_shapes(arrs):
    return [jax.ShapeDtypeStruct((N_DEV, a.shape[0] // N_DEV, a.shape[1]), a.dtype) for a in arrs]


def _gather_out_shapes(arrs):
    return [jax.ShapeDtypeStruct((N_DEV * a.shape[0], a.shape[1]), a.dtype) for a in arrs]


def _exchange_copies(ins, outs, send_sems, recv_sems, local_sems):
    n = len(ins)

    def block(i, p):
        r = outs[i].shape[1]
        return ins[i].at[pl.ds(_mesh_index(p) * r, r), :]

    mine = [pltpu.make_async_copy(block(i, _peer(0)), outs[i].at[0], local_sems.at[i]) for i in range(n)]
    sends = [pltpu.make_async_remote_copy(
        src_ref=block(i, _peer(k)), dst_ref=outs[i].at[k], send_sem=send_sems.at[i, k - 1],
        recv_sem=recv_sems.at[i, k - 1], device_id=_peer(k), device_id_type=MESH)
        for k in PEER_ORDER for i in range(n)]
    return mine, sends, sends


def _gather_copies(ins, outs, send_sems, recv_sems, local_sems):
    n = len(ins)

    def rows(i, p):
        r = ins[i].shape[0]
        return outs[i].at[pl.ds(_mesh_index(p) * r, r), :]

    def copy(i, k, landing):
        return pltpu.make_async_remote_copy(
            src_ref=ins[i], dst_ref=rows(i, landing), send_sem=send_sems.at[i, k - 1],
            recv_sem=recv_sems.at[i, k - 1], device_id=_peer(k), device_id_type=MESH)

    mine = [pltpu.make_async_copy(ins[i], rows(i, _peer(0)), local_sems.at[i]) for i in range(n)]
    sends = [copy(i, k, _peer(0)) for k in PEER_ORDER for i in range(n)]
    recvs = [copy(i, k, _peer(k)) for k in PEER_ORDER for i in range(n)]
    return mine, sends, recvs


def _start_copies(copies):
    mine, sends, _ = copies
    for cp in mine + sends:
        cp.start()


def _finish_copies(copies):
    mine, sends, recvs = copies
    for cp in recvs:
        cp.wait_recv()
    for cp in sends:
        cp.wait_send()
    for cp in mine:
        cp.wait()


def _sum_slots(r3, name):
    _, r, c = r3.shape
    tr = _tile(r, (128, 96, 64, 32, 16))

    def body(r_ref, o_ref):
        acc = r_ref[0].astype(F32)
        for k in range(1, N_DEV):
            acc = acc + r_ref[k].astype(F32)
        o_ref[...] = acc

    return pl.pallas_call(
        body, name=name, grid=(r // tr,),
        out_shape=jax.ShapeDtypeStruct((r, c), F32),
        in_specs=[pl.BlockSpec((N_DEV, tr, c), lambda i: (0, i, 0))],
        out_specs=pl.BlockSpec((tr, c), lambda i: (i, 0)),
    )(r3)


def _mm(a, b, dims, out_dtype, name, n0=0, n=None):
    if dims is NT:
        m, k = a.shape
        n = b.shape[0] if n is None else n
    elif dims is NN:
        m, k = a.shape
        n = b.shape[1]
    else:
        k, m = a.shape
        n = b.shape[1]
    if dims is TN:
        tm = _lane_tile(m, 1536)
        tn = _lane_tile(n, 1024)
        tk = _tile(k, (1024, 512, 256, 128))
    else:
        tm = _tile(m, (1024, 512, 256, 128))
        tn = _lane_tile(math.gcd(n, n0) if n0 else n, 1536)
        tk = _lane_tile(k, 2816)
    nk = k // tk
    noff = n0 // tn

    def body(a_ref, b_ref, o_ref, *scratch):
        part = lax.dot_general(a_ref[...].astype(BF16), b_ref[...].astype(BF16), dims,
                               preferred_element_type=F32)
        if nk == 1:
            o_ref[...] = part.astype(o_ref.dtype)
        else:
            acc_ref, = scratch
            kk = pl.program_id(2)

            @pl.when(kk == 0)
            def _():
                acc_ref[...] = part

            @pl.when(kk > 0)
            def _():
                acc_ref[...] += part

            @pl.when(kk == nk - 1)
            def _():
                o_ref[...] = acc_ref[...].astype(o_ref.dtype)

    if dims is NT:
        in_specs = [pl.BlockSpec((tm, tk), lambda i, j, kk: (i, kk)),
                    pl.BlockSpec((tn, tk), lambda i, j, kk: (j + noff, kk))]
    elif dims is NN:
        in_specs = [pl.BlockSpec((tm, tk), lambda i, j, kk: (i, kk)),
                    pl.BlockSpec((tk, tn), lambda i, j, kk: (kk, j))]
    else:
        in_specs = [pl.BlockSpec((tk, tm), lambda i, j, kk: (kk, i)),
                    pl.BlockSpec((tk, tn), lambda i, j, kk: (kk, j))]
    return pl.pallas_call(
        body, name=name, grid=(m // tm, n // tn, nk),
        out_shape=jax.ShapeDtypeStruct((m, n), out_dtype),
        in_specs=in_specs,
        out_specs=pl.BlockSpec((tm, tn), lambda i, j, kk: (i, j)),
        scratch_shapes=[] if nk == 1 else [pltpu.VMEM((tm, tn), F32)],
        compiler_params=pltpu.CompilerParams(dimension_semantics=("parallel", "parallel", "arbitrary")),
    )(a, b)


def _tok_tile(s):
    return _tile(s, (512, 256, 128))


def _row_spec(tm, d):
    return pl.BlockSpec((tm, d), lambda i: (i, 0))


def _vec_spec(d):
    return pl.BlockSpec((1, d), lambda i: (0, 0))


def _seq_spec(d, tiles_per_seq):
    return pl.BlockSpec((1, 1, d), lambda i: (i // tiles_per_seq, 0, 0))


def _rms(x):
    return lax.rsqrt(jnp.mean(x * x, axis=-1, keepdims=True) + EPS)


def _norm_mod(x, g, sc, sh, s, name):
    t, d = x.shape
    tm = _tok_tile(s)

    def body(x_ref, g_ref, sc_ref, sh_ref, h_ref):
        xv = x_ref[...]
        h = (xv * _rms(xv)) * g_ref[...]
        h_ref[...] = (h * (1.0 + sc_ref[0]) + sh_ref[0]).astype(BF16)

    return pl.pallas_call(
        body, name=name, grid=(t // tm,),
        out_shape=jax.ShapeDtypeStruct((t, d), BF16),
        in_specs=[_row_spec(tm, d), _vec_spec(d), _seq_spec(d, s // tm), _seq_spec(d, s // tm)],
        out_specs=_row_spec(tm, d),
    )(x, g, sc, sh)


def _post_res_norm(x, y, g_post, ga, g_pre, sc, sh, s, name):
    t, d = x.shape
    tm = _tok_tile(s)

    def body(x_ref, y_ref, gp_ref, ga_ref, g_ref, sc_ref, sh_ref, xn_ref, h_ref):
        yv = y_ref[...]
        xn = x_ref[...] + ga_ref[0] * ((yv * _rms(yv)) * gp_ref[...])
        xn_ref[...] = xn
        h = (xn * _rms(xn)) * g_ref[...]
        h_ref[...] = (h * (1.0 + sc_ref[0]) + sh_ref[0]).astype(BF16)

    tps = s // tm
    return pl.pallas_call(
        body, name=name, grid=(t // tm,),
        out_shape=[jax.ShapeDtypeStruct((t, d), F32), jax.ShapeDtypeStruct((t, d), BF16)],
        in_specs=[_row_spec(tm, d), _row_spec(tm, d), _vec_spec(d), _seq_spec(d, tps),
                  _vec_spec(d), _seq_spec(d, tps), _seq_spec(d, tps)],
        out_specs=[_row_spec(tm, d), _row_spec(tm, d)],
    )(x, y, g_post, ga, g_pre, sc, sh)


def _post_res_loss(x, y, g_post, ga, target, s, name):
    t, d = x.shape
    tm = _tok_tile(s)

    def body(x_ref, y_ref, gp_ref, ga_ref, t_ref, dx_ref, loss_ref):
        yv = y_ref[...]
        err = x_ref[...] + ga_ref[0] * ((yv * _rms(yv)) * gp_ref[...]) - t_ref[...]
        dx_ref[...] = err * (1.0 / d)
        part = 0.5 * jnp.sum(jnp.mean(err * err, axis=-1, keepdims=True), axis=0, keepdims=True)

        @pl.when(pl.program_id(0) == 0)
        def _():
            loss_ref[...] = jnp.zeros_like(loss_ref)

        loss_ref[...] += part

    return pl.pallas_call(
        body, name=name, grid=(t // tm,),
        out_shape=[jax.ShapeDtypeStruct((t, d), F32), jax.ShapeDtypeStruct((8, 128), F32)],
        in_specs=[_row_spec(tm, d), _row_spec(tm, d), _vec_spec(d), _seq_spec(d, s // tm), _row_spec(tm, d)],
        out_specs=[_row_spec(tm, d), pl.BlockSpec((8, 128), lambda i: (0, 0))],
    )(x, y, g_post, ga, target)


def _post_bwd(dxn, y, g, ga, s, name):
    t, d = y.shape
    tm = _tok_tile(s)
    tps = s // tm

    def body(dx_ref, y_ref, g_ref, ga_ref, dy_ref, dg_ref, dga_ref):
        i = pl.program_id(0)
        yv = y_ref[...]
        r = _rms(yv)
        yh = yv * r
        dxv = dx_ref[...]
        dn = dxv * ga_ref[0]
        dyh = dn * g_ref[...]
        dy_ref[...] = (r * (dyh - yh * jnp.mean(dyh * yh, axis=-1, keepdims=True))).astype(BF16)

        @pl.when(i == 0)
        def _():
            dg_ref[...] = jnp.zeros_like(dg_ref)

        @pl.when(i % tps == 0)
        def _():
            dga_ref[...] = jnp.zeros_like(dga_ref)

        dg_ref[...] += jnp.sum(dn * yh, axis=0, keepdims=True)
        dga_ref[0] += jnp.sum(dxv * (yh * g_ref[...]), axis=0, keepdims=True)

    return pl.pallas_call(
        body, name=name, grid=(t // tm,),
        out_shape=[jax.ShapeDtypeStruct((t, d), BF16), jax.ShapeDtypeStruct((1, d), F32),
                   jax.ShapeDtypeStruct((t // s, 1, d), F32)],
        in_specs=[_row_spec(tm, d), _row_spec(tm, d), _vec_spec(d), _seq_spec(d, tps)],
        out_specs=[_row_spec(tm, d), _vec_spec(d), _seq_spec(d, tps)],
    )(dxn, y, g, ga)


def _pre_bwd(dh, x, g, sc, dres, s, name):
    t, d = x.shape
    tm = _tok_tile(s)
    tps = s // tm

    def body(dh_ref, x_ref, g_ref, sc_ref, dr_ref, dx_ref, dsh_ref, dsc_ref, dg_ref):
        i = pl.program_id(0)
        xv = x_ref[...]
        r = _rms(xv)
        xh = xv * r
        dhv = dh_ref[...]
        one_sc = 1.0 + sc_ref[0]
        dxh = dhv * one_sc * g_ref[...]
        dx_ref[...] = dr_ref[...] + r * (dxh - xh * jnp.mean(dxh * xh, axis=-1, keepdims=True))

        @pl.when(i == 0)
        def _():
            dg_ref[...] = jnp.zeros_like(dg_ref)

        @pl.when(i % tps == 0)
        def _():
            dsh_ref[...] = jnp.zeros_like(dsh_ref)
            dsc_ref[...] = jnp.zeros_like(dsc_ref)

        dg_ref[...] += jnp.sum(dhv * one_sc * xh, axis=0, keepdims=True)
        dsh_ref[0] += jnp.sum(dhv, axis=0, keepdims=True)
        dsc_ref[0] += jnp.sum(dhv * (xh * g_ref[...]), axis=0, keepdims=True)

    nb = t // s
    return pl.pallas_call(
        body, name=name, grid=(t // tm,),
        out_shape=[jax.ShapeDtypeStruct((t, d), F32), jax.ShapeDtypeStruct((nb, 1, d), F32),
                   jax.ShapeDtypeStruct((nb, 1, d), F32), jax.ShapeDtypeStruct((1, d), F32)],
        in_specs=[_row_spec(tm, d), _row_spec(tm, d), _vec_spec(d), _seq_spec(d, tps), _row_spec(tm, d)],
        out_specs=[_row_spec(tm, d), _seq_spec(d, tps), _seq_spec(d, tps), _vec_spec(d)],
    )(dh, x, g, sc, dres)


def _merge(y_conv, y_att, gates, s, name):
    t, d = y_conv.shape
    tm = _tok_tile(s)

    def body(yc_ref, ya_ref, gc_ref, gt_ref, o_ref):
        o_ref[...] = (_sig(gc_ref[...]) * yc_ref[...] + _sig(gt_ref[...]) * ya_ref[...]).astype(BF16)

    return pl.pallas_call(
        body, name=name, grid=(t // tm,),
        out_shape=jax.ShapeDtypeStruct((t, d), BF16),
        in_specs=[_row_spec(tm, d), _row_spec(tm, d), pl.BlockSpec((tm, d), lambda i: (i, 0)),
                  pl.BlockSpec((tm, d), lambda i: (i, 1))],
        out_specs=_row_spec(tm, d),
    )(y_conv, y_att, gates, gates)


def _merge_bwd(dm, y_conv, y_att, gates, s, name):
    t, d = y_conv.shape
    tm = _tok_tile(s)

    def body(dm_ref, yc_ref, ya_ref, gc_ref, gt_ref, dyc_ref, dya_ref, dg_ref):
        dmv = dm_ref[...]
        sc_, st_ = _sig(gc_ref[...]), _sig(gt_ref[...])
        dyc_ref[...] = (dmv * sc_).astype(BF16)
        dya_ref[...] = (dmv * st_).astype(BF16)
        dg_ref[:, :d] = (dmv * yc_ref[...] * (sc_ * (1.0 - sc_))).astype(BF16)
        dg_ref[:, d:] = (dmv * ya_ref[...] * (st_ * (1.0 - st_))).astype(BF16)

    return pl.pallas_call(
        body, name=name, grid=(t // tm,),
        out_shape=[jax.ShapeDtypeStruct((t, d), BF16), jax.ShapeDtypeStruct((t, d), BF16),
                   jax.ShapeDtypeStruct((t, 2 * d), BF16)],
        in_specs=[_row_spec(tm, d), _row_spec(tm, d), _row_spec(tm, d),
                  pl.BlockSpec((tm, d), lambda i: (i, 0)), pl.BlockSpec((tm, d), lambda i: (i, 1))],
        out_specs=[_row_spec(tm, d), _row_spec(tm, d), _row_spec(tm, 2 * d)],
    )(dm, y_conv, y_att, gates, gates)


def _swiglu(f, s, name):
    t, two = f.shape
    dff = two // 2
    tm = _tok_tile(s)

    def body(g_ref, u_ref, a_ref):
        gv = g_ref[...]
        a_ref[...] = (gv * _sig(gv) * u_ref[...]).astype(BF16)

    return pl.pallas_call(
        body, name=name, grid=(t // tm,),
        out_shape=jax.ShapeDtypeStruct((t, dff), BF16),
        in_specs=[pl.BlockSpec((tm, dff), lambda i: (i, 0)), pl.BlockSpec((tm, dff), lambda i: (i, 1))],
        out_specs=_row_spec(tm, dff),
    )(f, f)


def _swiglu_bwd(da, f, s, name):
    t, two = f.shape
    dff = two // 2
    tm = _tok_tile(s)

    def body(da_ref, g_ref, u_ref, df_ref):
        gv, dav = g_ref[...], da_ref[...]
        sg = _sig(gv)
        df_ref[:, :dff] = (dav * u_ref[...] * (sg * (1.0 + gv * (1.0 - sg)))).astype(BF16)
        df_ref[:, dff:] = (dav * (gv * sg)).astype(BF16)

    return pl.pallas_call(
        body, name=name, grid=(t // tm,),
        out_shape=jax.ShapeDtypeStruct((t, two), BF16),
        in_specs=[_row_spec(tm, dff), pl.BlockSpec((tm, dff), lambda i: (i, 0)),
                  pl.BlockSpec((tm, dff), lambda i: (i, 1))],
        out_specs=_row_spec(tm, two),
    )(da, f, f)


CONV_CHUNK = 64


def _conv_fwd(glu_in, conv_w, conv_b, ln_g, ln_b, nb, s, name):
    t, two = glu_in.shape
    cw = two // 2
    tt = _tile(s, (256, 128, 64))
    spt = s // tt
    ch = min(CONV_CHUNK, tt)

    def body(cur_ref, halo_ref, w_ref, b_ref, g_ref, be_ref, u_ref, cv_ref, ua_ref, ext_ref):
        j = pl.program_id(1)
        cur = cur_ref[...]
        u_cur = cur[:, :cw] * _sig(cur[:, cw:])
        hal = halo_ref[...]
        u_hal = hal[:, :cw] * _sig(hal[:, cw:])
        ext_ref[0:HALO, :] = jnp.where(j == 0, 0.0, u_hal)
        ext_ref[HALO:HALO + tt, :] = u_cur
        u_ref[...] = u_cur
        for c0 in range(0, tt, ch):
            acc = jnp.zeros((ch, cw), F32) + b_ref[...]
            for k in range(CONV_K):
                st = c0 + HALO - (CONV_K - 1) + k
                acc = acc + w_ref[k:k + 1, :] * ext_ref[st:st + ch, :]
            cv_ref[c0:c0 + ch, :] = acc
            mu = jnp.mean(acc, axis=-1, keepdims=True)
            xc = acc - mu
            yv = xc * lax.rsqrt(jnp.mean(xc * xc, axis=-1, keepdims=True) + EPS) * g_ref[...] + be_ref[...]
            ua_ref[c0:c0 + ch, :] = (yv * _sig(yv)).astype(BF16)

    cur_map = lambda b, j: (b * spt + j, 0)
    halo_map = lambda b, j: (jnp.maximum((b * s + j * tt) // HALO - 1, 0), 0)
    vec = lambda d0: pl.BlockSpec((d0, cw), lambda b, j: (0, 0))
    return pl.pallas_call(
        body, name=name, grid=(nb, spt),
        out_shape=[jax.ShapeDtypeStruct((t, cw), F32), jax.ShapeDtypeStruct((t, cw), F32),
                   jax.ShapeDtypeStruct((t, cw), BF16)],
        in_specs=[pl.BlockSpec((tt, two), cur_map), pl.BlockSpec((HALO, two), halo_map),
                  vec(CONV_K), vec(1), vec(1), vec(1)],
        out_specs=[pl.BlockSpec((tt, cw), cur_map)] * 3,
        scratch_shapes=[pltpu.VMEM((tt + HALO, cw), F32)],
    )(glu_in, glu_in, conv_w, conv_b, ln_g, ln_b)


def _convln_bwd(dua, cv, ln_g, ln_b, s, name):
    t, cw = cv.shape
    tm = _tok_tile(s)

    def body(du_ref, cv_ref, g_ref, be_ref, dcv_ref, dg_ref, dbe_ref, db_ref):
        cvv = cv_ref[...]
        mu = jnp.mean(cvv, axis=-1, keepdims=True)
        xc = cvv - mu
        rstd = lax.rsqrt(jnp.mean(xc * xc, axis=-1, keepdims=True) + EPS)
        xh = xc * rstd
        yv = xh * g_ref[...] + be_ref[...]
        sg = _sig(yv)
        dy = du_ref[...] * (sg * (1.0 + yv * (1.0 - sg)))
        dxh = dy * g_ref[...]
        dcv = rstd * (dxh - jnp.mean(dxh, axis=-1, keepdims=True)
                      - xh * jnp.mean(dxh * xh, axis=-1, keepdims=True))
        dcv_ref[...] = dcv

        @pl.when(pl.program_id(0) == 0)
        def _():
            dg_ref[...] = jnp.zeros_like(dg_ref)
            dbe_ref[...] = jnp.zeros_like(dbe_ref)
            db_ref[...] = jnp.zeros_like(db_ref)

        dg_ref[...] += jnp.sum(dy * xh, axis=0, keepdims=True)
        dbe_ref[...] += jnp.sum(dy, axis=0, keepdims=True)
        db_ref[...] += jnp.sum(dcv, axis=0, keepdims=True)

    return pl.pallas_call(
        body, name=name, grid=(t // tm,),
        out_shape=[jax.ShapeDtypeStruct((t, cw), F32)] + [jax.ShapeDtypeStruct((1, cw), F32)] * 3,
        in_specs=[_row_spec(tm, cw), _row_spec(tm, cw), _vec_spec(cw), _vec_spec(cw)],
        out_specs=[_row_spec(tm, cw), _vec_spec(cw), _vec_spec(cw), _vec_spec(cw)],
    )(dua, cv, ln_g, ln_b)


def _conv_bwd(dcv, u, glu_in, conv_w, nb, s, name):
    t, cw = dcv.shape
    tt = _tile(s, (256, 128, 64))
    spt = s // tt
    ch = min(CONV_CHUNK, tt)
    nblk = t // HALO

    def body(d_ref, dn_ref, u_ref, up_ref, glu_ref, w_ref, dglu_ref, dw_ref, dext_ref, uext_ref):
        b, j = pl.program_id(0), pl.program_id(1)
        dcur = d_ref[...]
        dext_ref[0:tt, :] = dcur
        dext_ref[tt:tt + HALO, :] = jnp.where(j == spt - 1, 0.0, dn_ref[...])
        uext_ref[0:HALO, :] = jnp.where(j == 0, 0.0, up_ref[...])
        uext_ref[HALO:HALO + tt, :] = u_ref[...]

        @pl.when((b == 0) & (j == 0))
        def _():
            dw_ref[...] = jnp.zeros_like(dw_ref)

        for c0 in range(0, tt, ch):
            dc = dext_ref[c0:c0 + ch, :]
            du = jnp.zeros((ch, cw), F32)
            for k in range(CONV_K):
                sd = c0 + (CONV_K - 1) - k
                du = du + w_ref[k:k + 1, :] * dext_ref[sd:sd + ch, :]
                su = c0 + HALO - (CONV_K - 1) + k
                dw_ref[k:k + 1, :] += jnp.sum(dc * uext_ref[su:su + ch, :], axis=0, keepdims=True)
            val = glu_ref[c0:c0 + ch, :cw]
            sg = _sig(glu_ref[c0:c0 + ch, cw:])
            dglu_ref[c0:c0 + ch, :cw] = (du * sg).astype(BF16)
            dglu_ref[c0:c0 + ch, cw:] = (du * val * (sg * (1.0 - sg))).astype(BF16)

    cur_map = lambda b, j: (b * spt + j, 0)
    prev_map = lambda b, j: (jnp.maximum((b * s + j * tt) // HALO - 1, 0), 0)
    next_map = lambda b, j: (jnp.minimum((b * s + (j + 1) * tt) // HALO, nblk - 1), 0)
    return pl.pallas_call(
        body, name=name, grid=(nb, spt),
        out_shape=[jax.ShapeDtypeStruct((t, 2 * cw), BF16), jax.ShapeDtypeStruct((HALO, cw), F32)],
        in_specs=[pl.BlockSpec((tt, cw), cur_map), pl.BlockSpec((HALO, cw), next_map),
                  pl.BlockSpec((tt, cw), cur_map), pl.BlockSpec((HALO, cw), prev_map),
                  pl.BlockSpec((tt, 2 * cw), cur_map), pl.BlockSpec((CONV_K, cw), lambda b, j: (0, 0))],
        out_specs=[pl.BlockSpec((tt, 2 * cw), cur_map), pl.BlockSpec((HALO, cw), lambda b, j: (0, 0))],
        scratch_shapes=[pltpu.VMEM((tt + HALO, cw), F32), pltpu.VMEM((tt + HALO, cw), F32)],
    )(dcv, dcv, u, u, glu_in, conv_w)


def _split(x):
    hi = x.astype(BF16)
    return hi, (x - hi.astype(F32)).astype(BF16)


def _tri(upper):
    j = lax.broadcasted_iota(jnp.int32, (KEY_BLK, KEY_BLK), 0)
    s_ = lax.broadcasted_iota(jnp.int32, (KEY_BLK, KEY_BLK), 1)
    return jnp.where(j > s_ if upper else j < s_, 1.0, 0.0).astype(BF16)


def _log_keep(z, causal):
    sp = jnp.maximum(z, 0.0) + jnp.log(1.0 + jnp.exp(-jnp.abs(z)))
    return -sp if causal is None else jnp.where(causal, -sp, 0.0)


def _attn_fwd(q, k, v, name):
    bh, s, dh = q.shape
    tq = _tile(s, (256, 128))
    scale = 1.0 / math.sqrt(dh)

    def body(q_ref, k_ref, v_ref, o_ref, car_ref, acc_ref, run_ref):
        qi = pl.program_id(1)
        qv = q_ref[0]
        lane = lax.broadcasted_iota(jnp.int32, (tq, KEY_BLK), 1)
        row = qi * tq + lax.broadcasted_iota(jnp.int32, (tq, KEY_BLK), 0)
        m_suf = _tri(True)
        nk = (qi * tq + tq) // KEY_BLK
        acc_ref[...] = jnp.zeros_like(acc_ref)
        run_ref[...] = jnp.zeros_like(run_ref)
        car_ref[...] = jnp.zeros_like(car_ref)

        def step(it, _):
            kb = nk - 1 - it
            off = pl.multiple_of(kb * KEY_BLK, KEY_BLK)
            kblk = k_ref[0, pl.ds(off, KEY_BLK), :]
            vblk = v_ref[0, pl.ds(off, KEY_BLK), :]
            z = lax.dot_general(qv, kblk, NT, preferred_element_type=F32) * scale
            causal = (off + lane) < row
            lk = _log_keep(z, causal)
            hi, lo = _split(lk)
            cs = (jnp.dot(hi, m_suf, preferred_element_type=F32)
                  + jnp.dot(lo, m_suf, preferred_element_type=F32))
            run = run_ref[...]
            a = jnp.where(causal, jnp.exp(z + lk + cs + run), 0.0)
            acc_ref[...] += jnp.dot(a.astype(BF16), vblk, preferred_element_type=F32)
            car_ref[0] = jnp.where(lane == kb, run, car_ref[0])
            run_ref[...] = run + jnp.sum(lk, axis=1, keepdims=True)
            return 0

        lax.fori_loop(0, nk, step, 0)
        o_ref[0] = acc_ref[...].astype(BF16)

    full = pl.BlockSpec((1, s, dh), lambda b, i: (b, 0, 0))
    return pl.pallas_call(
        body, name=name, grid=(bh, s // tq),
        out_shape=[jax.ShapeDtypeStruct((bh, s, dh), BF16), jax.ShapeDtypeStruct((bh, s, KEY_BLK), F32)],
        in_specs=[pl.BlockSpec((1, tq, dh), lambda b, i: (b, i, 0)), full, full],
        out_specs=[pl.BlockSpec((1, tq, dh), lambda b, i: (b, i, 0)),
                   pl.BlockSpec((1, tq, KEY_BLK), lambda b, i: (b, i, 0))],
        scratch_shapes=[pltpu.VMEM((tq, dh), F32), pltpu.VMEM((tq, KEY_BLK), F32)],
        compiler_params=pltpu.CompilerParams(dimension_semantics=("parallel", "arbitrary")),
    )(q, k, v)


def _attn_bwd(q, k, v, do, car, name):
    bh, s, dh = q.shape
    tq = _tile(s, (256, 128))
    scale = 1.0 / math.sqrt(dh)
    nkb, nqb = s // KEY_BLK, s // tq

    def body(q_ref, k_ref, v_ref, do_ref, car_ref, dq_ref, dk_ref, dv_ref, dq_acc, g_acc, dk_acc, dv_acc):
        lane = lax.broadcasted_iota(jnp.int32, (tq, KEY_BLK), 1)
        rowi = lax.broadcasted_iota(jnp.int32, (tq, KEY_BLK), 0)
        m_suf, m_pre = _tri(True), _tri(False)
        dq_acc[...] = jnp.zeros_like(dq_acc)
        g_acc[...] = jnp.zeros_like(g_acc)

        def kloop(kb, _):
            off = pl.multiple_of(kb * KEY_BLK, KEY_BLK)
            kblk = k_ref[0, pl.ds(off, KEY_BLK), :]
            vblk = v_ref[0, pl.ds(off, KEY_BLK), :]

            dk_acc[...] = jnp.zeros_like(dk_acc)
            dv_acc[...] = jnp.zeros_like(dv_acc)

            def qloop(qb, _):
                r0 = pl.multiple_of(qb * tq, tq)
                qblk = q_ref[0, pl.ds(r0, tq), :]
                doblk = do_ref[0, pl.ds(r0, tq), :]
                z = lax.dot_general(qblk, kblk, NT, preferred_element_type=F32) * scale
                causal = (off + lane) < (r0 + rowi)
                lk = _log_keep(z, causal)
                hi, lo = _split(lk)
                cs = (jnp.dot(hi, m_suf, preferred_element_type=F32)
                      + jnp.dot(lo, m_suf, preferred_element_type=F32))
                cpast = jnp.sum(jnp.where(lane == kb, car_ref[0, pl.ds(r0, tq), :], 0.0), axis=1, keepdims=True)
                lsig = z + lk
                a = jnp.where(causal, jnp.exp(lsig + cs + cpast), 0.0)
                da = lax.dot_general(doblk, vblk, NT, preferred_element_type=F32)
                g = a * da
                ghi, glo = _split(g)
                gpre = (jnp.dot(ghi, m_pre, preferred_element_type=F32)
                        + jnp.dot(glo, m_pre, preferred_element_type=F32)
                        + g_acc[pl.ds(r0, tq), :])
                dz = jnp.where(causal, g - jnp.exp(lsig) * (g + gpre), 0.0)
                ds_ = (dz * scale).astype(BF16)
                dv_acc[...] += lax.dot_general(a.astype(BF16), doblk, TN, preferred_element_type=F32)
                dk_acc[...] += lax.dot_general(ds_, qblk, TN, preferred_element_type=F32)
                dq_acc[pl.ds(r0, tq), :] += jnp.dot(ds_, kblk, preferred_element_type=F32)
                g_acc[pl.ds(r0, tq), :] += jnp.broadcast_to(jnp.sum(g, axis=1, keepdims=True), (tq, KEY_BLK))
                return 0

            lax.fori_loop(off // tq, nqb, qloop, 0)
            dk_ref[0, pl.ds(off, KEY_BLK), :] = dk_acc[...].astype(BF16)
            dv_ref[0, pl.ds(off, KEY_BLK), :] = dv_acc[...].astype(BF16)
            return 0

        lax.fori_loop(0, nkb, kloop, 0)
        dq_ref[0] = dq_acc[...].astype(BF16)

    full = pl.BlockSpec((1, s, dh), lambda b: (b, 0, 0))
    return pl.pallas_call(
        body, name=name, grid=(bh,),
        out_shape=[jax.ShapeDtypeStruct((bh, s, dh), BF16)] * 3,
        in_specs=[full, full, full, full, pl.BlockSpec((1, s, KEY_BLK), lambda b: (b, 0, 0))],
        out_specs=[full, full, full],
        scratch_shapes=[pltpu.VMEM((s, dh), F32), pltpu.VMEM((s, KEY_BLK), F32),
                        pltpu.VMEM((KEY_BLK, dh), F32), pltpu.VMEM((KEY_BLK, dh), F32)],
        compiler_params=pltpu.CompilerParams(dimension_semantics=("parallel",)),
    )(q, k, v, do, car)


LANES = 128


def _sb_fwd(qkv, nb, s, name):
    t, three_aw = qkv.shape
    aw = three_aw // 3
    npair = aw // LANES
    tq = _tile(s, (256, 128))
    ndiag = tq // KEY_BLK
    spt = s // tq
    scale = 1.0 / math.sqrt(HEAD_DIM)

    def body(q_ref, k_ref, v_ref, o_ref, car_ref, acc_ref, run_ref):
        qi = pl.program_id(2)
        lane = lax.broadcasted_iota(jnp.int32, (tq, LANES), 1)
        col = lax.broadcasted_iota(jnp.int32, (tq, KEY_BLK), 1)
        row = qi * tq + lax.broadcasted_iota(jnp.int32, (tq, KEY_BLK), 0)
        first = lane < HEAD_DIM
        q2 = q_ref[...]
        zero = jnp.zeros_like(q2)
        qh = (jnp.where(first, q2, zero), jnp.where(first, zero, q2))
        m_suf = _tri(True)
        nk = (qi * tq + tq) // KEY_BLK
        acc_ref[...] = jnp.zeros_like(acc_ref)
        run_ref[...] = jnp.zeros_like(run_ref)
        car_ref[...] = jnp.zeros_like(car_ref)

        def step(it, masked):
            kb = nk - 1 - it
            off = pl.multiple_of(kb * KEY_BLK, KEY_BLK)
            kblk = k_ref[pl.ds(off, KEY_BLK), :]
            vblk = v_ref[pl.ds(off, KEY_BLK), :]
            causal = ((off + col) < row) if masked else None
            hs = range(2)
            zs = [lax.dot_general(qh[h], kblk, NT, preferred_element_type=F32) * scale for h in hs]
            lks = [_log_keep(zs[h], causal) for h in hs]
            sp = [_split(lks[h]) for h in hs]
            css = [jnp.dot(sp[h][0], m_suf, preferred_element_type=F32)
                   + jnp.dot(sp[h][1], m_suf, preferred_element_type=F32) for h in hs]
            runs = [run_ref[h] for h in hs]
            aa = [jnp.exp(zs[h] + lks[h] + css[h] + jnp.concatenate([runs[h]] * (KEY_BLK // LANES), axis=1))
                  for h in hs]
            if masked:
                aa = [jnp.where(causal, aa[h], 0.0) for h in hs]
            for h in hs:
                acc_ref[h] += jnp.dot(aa[h].astype(BF16), vblk, preferred_element_type=F32)
            for h in hs:
                car_ref[h] = jnp.where(lane == kb, runs[h], car_ref[h])
                run_ref[h] = runs[h] + jnp.sum(lks[h], axis=1, keepdims=True)

        for it in range(ndiag):
            step(it, True)

        def rest(it, _):
            step(it, False)
            return 0

        lax.fori_loop(ndiag, nk, rest, 0)
        o_ref[...] = jnp.where(first, acc_ref[0], acc_ref[1]).astype(BF16)

    return pl.pallas_call(
        body, name=name, grid=(nb, npair, spt),
        out_shape=[jax.ShapeDtypeStruct((t, aw), BF16),
                   jax.ShapeDtypeStruct((nb * npair * 2, s, LANES), F32)],
        in_specs=[pl.BlockSpec((tq, LANES), lambda b, p, i: (b * spt + i, p)),
                  pl.BlockSpec((s, LANES), lambda b, p, i: (b, npair + p)),
                  pl.BlockSpec((s, LANES), lambda b, p, i: (b, 2 * npair + p))],
        out_specs=[pl.BlockSpec((tq, LANES), lambda b, p, i: (b * spt + i, p)),
                   pl.BlockSpec((2, tq, LANES), lambda b, p, i: (b * npair + p, i, 0))],
        scratch_shapes=[pltpu.VMEM((2, tq, LANES), F32), pltpu.VMEM((2, tq, LANES), F32)],
        compiler_params=pltpu.CompilerParams(dimension_semantics=("parallel", "parallel", "arbitrary")),
    )(qkv, qkv, qkv)


def _sb_bwd(qkv, do, car, nb, s, name):
    t, three_aw = qkv.shape
    aw = three_aw // 3
    npair = aw // LANES
    tq = _tile(s, (256, 128))
    scale = 1.0 / math.sqrt(HEAD_DIM)
    nkb, nqb = s // KEY_BLK, s // tq

    def body(q_ref, k_ref, v_ref, do_ref, car_ref, dq_ref, dk_ref, dv_ref, dq_acc, g_acc, dk_acc, dv_acc):
        lane = lax.broadcasted_iota(jnp.int32, (tq, LANES), 1)
        col = lax.broadcasted_iota(jnp.int32, (tq, KEY_BLK), 1)
        rowi = lax.broadcasted_iota(jnp.int32, (tq, KEY_BLK), 0)
        first_k = lax.broadcasted_iota(jnp.int32, (KEY_BLK, LANES), 1) < HEAD_DIM
        m_suf, m_pre = _tri(True), _tri(False)
        dq_acc[...] = jnp.zeros_like(dq_acc)
        g_acc[...] = jnp.zeros_like(g_acc)

        def kloop(kb, _):
            off = pl.multiple_of(kb * KEY_BLK, KEY_BLK)
            kblk = k_ref[pl.ds(off, KEY_BLK), :]
            vblk = v_ref[pl.ds(off, KEY_BLK), :]
            zero = jnp.zeros_like(kblk)
            kh = (jnp.where(first_k, kblk, zero), jnp.where(first_k, zero, kblk))
            vh = (jnp.where(first_k, vblk, zero), jnp.where(first_k, zero, vblk))
            dk_acc[...] = jnp.zeros_like(dk_acc)
            dv_acc[...] = jnp.zeros_like(dv_acc)

            def tile(qb, masked):
                r0 = pl.multiple_of(qb * tq, tq)
                q2 = q_ref[pl.ds(r0, tq), :]
                do2 = do_ref[pl.ds(r0, tq), :]
                causal = ((off + col) < (r0 + rowi)) if masked else None
                hs = range(2)
                wide = lambda x: jnp.concatenate([x] * (KEY_BLK // LANES), axis=1)
                zs = [lax.dot_general(q2, kh[h], NT, preferred_element_type=F32) * scale for h in hs]
                das = [lax.dot_general(do2, vh[h], NT, preferred_element_type=F32) for h in hs]
                lks = [_log_keep(zs[h], causal) for h in hs]
                sp = [_split(lks[h]) for h in hs]
                css = [jnp.dot(sp[h][0], m_suf, preferred_element_type=F32)
                       + jnp.dot(sp[h][1], m_suf, preferred_element_type=F32) for h in hs]
                cpast = [jnp.sum(jnp.where(lane == kb, car_ref[h, pl.ds(r0, tq), :], 0.0), axis=1, keepdims=True)
                         for h in hs]
                lsig = [zs[h] + lks[h] for h in hs]
                aa = [jnp.exp(lsig[h] + css[h] + cpast[h]) for h in hs]
                if masked:
                    aa = [jnp.where(causal, aa[h], 0.0) for h in hs]
                gs = [aa[h] * das[h] for h in hs]
                gsp = [_split(gs[h]) for h in hs]
                gpre = [jnp.dot(gsp[h][0], m_pre, preferred_element_type=F32)
                        + jnp.dot(gsp[h][1], m_pre, preferred_element_type=F32)
                        + wide(g_acc[h, pl.ds(r0, tq), :]) for h in hs]
                dzs = [gs[h] - jnp.exp(lsig[h]) * (gs[h] + gpre[h]) for h in hs]
                if masked:
                    dzs = [jnp.where(causal, dzs[h], 0.0) for h in hs]
                dss = [(dzs[h] * scale).astype(BF16) for h in hs]
                for h in hs:
                    dv_acc[h] += lax.dot_general(aa[h].astype(BF16), do2, TN, preferred_element_type=F32)
                for h in hs:
                    dk_acc[h] += lax.dot_general(dss[h], q2, TN, preferred_element_type=F32)
                for h in hs:
                    dq_acc[h, pl.ds(r0, tq), :] += jnp.dot(dss[h], kblk, preferred_element_type=F32)
                for h in hs:
                    g_acc[h, pl.ds(r0, tq), :] += jnp.broadcast_to(jnp.sum(gs[h], axis=1, keepdims=True),
                                                                   (tq, LANES))

            q0 = off // tq
            tile(q0, True)

            def rest(qb, _):
                tile(qb, False)
                return 0

            lax.fori_loop(q0 + 1, nqb, rest, 0)
            dk_ref[pl.ds(off, KEY_BLK), :] = jnp.where(first_k, dk_acc[0], dk_acc[1]).astype(BF16)
            dv_ref[pl.ds(off, KEY_BLK), :] = jnp.where(first_k, dv_acc[0], dv_acc[1]).astype(BF16)
            return 0

        lax.fori_loop(0, nkb, kloop, 0)
        first_q = lax.broadcasted_iota(jnp.int32, (s, LANES), 1) < HEAD_DIM
        dq_ref[...] = jnp.where(first_q, dq_acc[0], dq_acc[1]).astype(BF16)

    col = lambda j: pl.BlockSpec((s, LANES), lambda b, p: (b, j * npair + p))
    return pl.pallas_call(
        body, name=name, grid=(nb, npair),
        out_shape=[jax.ShapeDtypeStruct((t, aw), BF16)] * 3,
        in_specs=[col(0), col(1), col(2), col(0),
                  pl.BlockSpec((2, s, LANES), lambda b, p: (b * npair + p, 0, 0))],
        out_specs=[col(0), col(0), col(0)],
        scratch_shapes=[pltpu.VMEM((2, s, LANES), F32), pltpu.VMEM((2, s, LANES), F32),
                        pltpu.VMEM((2, KEY_BLK, LANES), F32), pltpu.VMEM((2, KEY_BLK, LANES), F32)],
        compiler_params=pltpu.CompilerParams(dimension_semantics=("parallel", "parallel")),
    )(qkv, qkv, qkv, do, car)


HALF_Q = 256


def _cumsum_mat(upper):
    m = _tri(upper)
    return jnp.concatenate([m, m], axis=0)


def _cum(x, mat):
    hi, lo = _split(x)
    return jnp.dot(jnp.concatenate([hi, lo], axis=1), mat, preferred_element_type=F32)


def _wide(x):
    return jnp.concatenate([x] * (KEY_BLK // LANES), axis=1)


def _sb4_fwd(qkv, nb, s, name, gather=()):
    t, three_aw = qkv.shape
    aw = three_aw // 3
    npair = aw // LANES
    tq = 2 * HALF_Q
    spt = s // tq
    scale = 1.0 / math.sqrt(HEAD_DIM)

    ng = len(gather)

    def body(*refs):
        q_ref, k_ref, v_ref = refs[:3]
        o_ref, car_ref = refs[3 + ng:5 + ng]
        acc_ref, run_ref = refs[5 + 2 * ng:7 + 2 * ng]
        if ng:
            copies = _gather_copies(refs[3:3 + ng], refs[5 + ng:5 + 2 * ng], *refs[7 + 2 * ng:])
            step_id = (pl.program_id(0) * npair + pl.program_id(1)) * spt + pl.program_id(2)
            pl.when(step_id == 0)(lambda: _start_copies(copies))
        qi = pl.program_id(2)
        lane = lax.broadcasted_iota(jnp.int32, (HALF_Q, LANES), 1)
        col = lax.broadcasted_iota(jnp.int32, (HALF_Q, KEY_BLK), 1)
        rowi = lax.broadcasted_iota(jnp.int32, (HALF_Q, KEY_BLK), 0)
        first = lane < HEAD_DIM
        mat = _cumsum_mat(True)
        qs = []
        for half in range(2):
            q2 = q_ref[half * HALF_Q:(half + 1) * HALF_Q, :] * scale
            zero = jnp.zeros_like(q2)
            qs.append((jnp.where(first, q2, zero), jnp.where(first, zero, q2)))
        acc_ref[...] = jnp.zeros_like(acc_ref)
        run_ref[...] = jnp.zeros_like(run_ref)
        car_ref[...] = jnp.zeros_like(car_ref)
        nk = (qi * tq + tq) // KEY_BLK

        def step(kb, halves):
            off = pl.multiple_of(kb * KEY_BLK, KEY_BLK)
            kblk = k_ref[pl.ds(off, KEY_BLK), :]
            vblk = v_ref[pl.ds(off, KEY_BLK), :]
            chains = [(half, masked, h) for half, masked in halves for h in range(2)]
            causal = {half: ((off + col) < (qi * tq + half * HALF_Q + rowi)) if masked else None
                      for half, masked in halves}
            zs = [lax.dot_general(qs[half][h], kblk, NT, preferred_element_type=F32) for half, _, h in chains]
            lks = [_log_keep(z, causal[c[0]]) for z, c in zip(zs, chains)]
            css = [_cum(lk, mat) for lk in lks]
            runs = [run_ref[2 * half + h] for half, _, h in chains]
            aa = [jnp.exp(z + lk + cs + _wide(run)) for z, lk, cs, run in zip(zs, lks, css, runs)]
            aa = [jnp.where(causal[c[0]], a, 0.0) if c[1] else a for a, c in zip(aa, chains)]
            for a, (half, _, h) in zip(aa, chains):
                acc_ref[2 * half + h] += jnp.dot(a.astype(BF16), vblk, preferred_element_type=F32)
            for lk, run, (half, _, h) in zip(lks, runs, chains):
                rows = pl.ds(half * HALF_Q, HALF_Q)
                car_ref[h, rows, :] = jnp.where(lane == kb, run, car_ref[h, rows, :])
                run_ref[2 * half + h] = run + jnp.sum(lk, axis=1, keepdims=True)

        step(nk - 1, ((1, True),))
        step(nk - 2, ((0, True), (1, False)))

        def rest(it, _):
            step(nk - 1 - it, ((0, False), (1, False)))
            return 0

        lax.fori_loop(2, nk, rest, 0)
        for half in range(2):
            o_ref[half * HALF_Q:(half + 1) * HALF_Q, :] = jnp.where(
                first, acc_ref[2 * half], acc_ref[2 * half + 1]).astype(BF16)
        if ng:
            pl.when(step_id == nb * npair * spt - 1)(lambda: _finish_copies(copies))

    hbm = pl.BlockSpec(memory_space=pl.ANY)
    outs = pl.pallas_call(
        body, name=name, grid=(nb, npair, spt),
        out_shape=[jax.ShapeDtypeStruct((t, aw), BF16),
                   jax.ShapeDtypeStruct((nb * npair * 2, s, LANES), F32)] + _gather_out_shapes(gather),
        in_specs=[pl.BlockSpec((tq, LANES), lambda b, p, i: (b * spt + i, p)),
                  pl.BlockSpec((s, LANES), lambda b, p, i: (b, npair + p)),
                  pl.BlockSpec((s, LANES), lambda b, p, i: (b, 2 * npair + p))] + [hbm] * ng,
        out_specs=[pl.BlockSpec((tq, LANES), lambda b, p, i: (b * spt + i, p)),
                   pl.BlockSpec((2, tq, LANES), lambda b, p, i: (b * npair + p, i, 0))] + [hbm] * ng,
        scratch_shapes=[pltpu.VMEM((4, HALF_Q, LANES), F32), pltpu.VMEM((4, HALF_Q, LANES), F32)]
        + (_copy_sems(ng) if ng else []),
        compiler_params=pltpu.CompilerParams(dimension_semantics=("arbitrary", "arbitrary", "arbitrary")),
    )(qkv, qkv, qkv, *gather)
    return outs[0], outs[1], list(outs[2:])


def _sb4_bwd(qkv, do, car, nb, s, name, exchange=()):
    t, three_aw = qkv.shape
    aw = three_aw // 3
    npair = aw // LANES
    tq = HALF_Q
    scale = 1.0 / math.sqrt(HEAD_DIM)
    nkp, nqb = s // (2 * KEY_BLK), s // tq
    nx = len(exchange)

    def body(*refs):
        q_ref, k_ref, v_ref, do_ref, car_ref = refs[:5]
        dq_ref, dk_ref, dv_ref = refs[5 + nx:8 + nx]
        dq_acc, g_acc, dk_acc, dv_acc = refs[8 + 2 * nx:12 + 2 * nx]
        if nx:
            copies = _exchange_copies(refs[5:5 + nx], refs[8 + nx:8 + 2 * nx], *refs[12 + 2 * nx:])
            step_id = pl.program_id(0) * npair + pl.program_id(1)
            pl.when(step_id == 0)(lambda: _start_copies(copies))
        lane = lax.broadcasted_iota(jnp.int32, (tq, LANES), 1)
        col = lax.broadcasted_iota(jnp.int32, (tq, KEY_BLK), 1)
        rowi = lax.broadcasted_iota(jnp.int32, (tq, KEY_BLK), 0)
        first_k = lax.broadcasted_iota(jnp.int32, (KEY_BLK, LANES), 1) < HEAD_DIM
        m_suf, m_pre = _cumsum_mat(True), _cumsum_mat(False)
        dq_acc[...] = jnp.zeros_like(dq_acc)
        g_acc[...] = jnp.zeros_like(g_acc)

        def kloop(kp, _):
            offs = [pl.multiple_of((2 * kp + blk) * KEY_BLK, KEY_BLK) for blk in range(2)]
            kblk = [k_ref[pl.ds(off, KEY_BLK), :] for off in offs]
            vblk = [v_ref[pl.ds(off, KEY_BLK), :] for off in offs]
            zero = jnp.zeros_like(kblk[0])
            kh = [(jnp.where(first_k, kb_, zero), jnp.where(first_k, zero, kb_)) for kb_ in kblk]
            vh = [(jnp.where(first_k, vb_, zero), jnp.where(first_k, zero, vb_)) for vb_ in vblk]
            k_scaled = jnp.concatenate([kblk[0] * scale, kblk[1] * scale], axis=0)
            dk_acc[...] = jnp.zeros_like(dk_acc)
            dv_acc[...] = jnp.zeros_like(dv_acc)

            def tile(qb, blocks):
                r0 = pl.multiple_of(qb * tq, tq)
                rows = pl.ds(r0, tq)
                q2 = q_ref[rows, :] * scale
                do2 = do_ref[rows, :]
                chains = [(blk, masked, h) for blk, masked in blocks for h in range(2)]
                causal = {blk: ((offs[blk] + col) < (r0 + rowi)) if masked else None for blk, masked in blocks}
                zs = [lax.dot_general(q2, kh[blk][h], NT, preferred_element_type=F32) for blk, _, h in chains]
                das = [lax.dot_general(do2, vh[blk][h], NT, preferred_element_type=F32) for blk, _, h in chains]
                lks = [_log_keep(z, causal[c[0]]) for z, c in zip(zs, chains)]
                css = [_cum(lk, m_suf) for lk in lks]
                cars = [car_ref[h, rows, :] for h in range(2)]
                cpast = [jnp.sum(jnp.where(lane == 2 * kp + blk, cars[h], 0.0), axis=1, keepdims=True)
                         for blk, _, h in chains]
                lsig = [z + lk for z, lk in zip(zs, lks)]
                aa = [jnp.exp(ls + cs + cp) for ls, cs, cp in zip(lsig, css, cpast)]
                aa = [jnp.where(causal[c[0]], a, 0.0) if c[1] else a for a, c in zip(aa, chains)]
                gs = [a * da for a, da in zip(aa, das)]
                gsum = [jnp.sum(g, axis=1, keepdims=True) for g in gs]
                gin = [_cum(g, m_pre) for g in gs]
                gold = [_wide(g_acc[h, rows, :]) for h in range(2)]
                gpre = []
                for i, (blk, _, h) in enumerate(chains):
                    left = gold[h] if blk == 0 else gold[h] + gsum[h]
                    gpre.append(gin[i] + left)
                dzs = [g - jnp.exp(ls) * (g + gp) for g, ls, gp in zip(gs, lsig, gpre)]
                dzs = [jnp.where(causal[c[0]], dz, 0.0) if c[1] else dz for dz, c in zip(dzs, chains)]
                dss = [dz.astype(BF16) for dz in dzs]
                for a, (blk, _, h) in zip(aa, chains):
                    dv_acc[2 * blk + h] += lax.dot_general(a.astype(BF16), do2, TN, preferred_element_type=F32)
                for ds_, (blk, _, h) in zip(dss, chains):
                    dk_acc[2 * blk + h] += lax.dot_general(ds_, q2, TN, preferred_element_type=F32)
                for h in range(2):
                    if len(blocks) == 2:
                        both = jnp.concatenate([dss[h], dss[2 + h]], axis=1)
                        dq_acc[h, rows, :] += jnp.dot(both, k_scaled, preferred_element_type=F32)
                        tot = gsum[h] + gsum[2 + h]
                    else:
                        dq_acc[h, rows, :] += jnp.dot(dss[h], k_scaled[:KEY_BLK], preferred_element_type=F32)
                        tot = gsum[h]
                    g_acc[h, rows, :] += jnp.broadcast_to(tot, (tq, LANES))

            tile(2 * kp, ((0, True),))
            tile(2 * kp + 1, ((0, False), (1, True)))

            def rest(qb, _):
                tile(qb, ((0, False), (1, False)))
                return 0

            lax.fori_loop(2 * kp + 2, nqb, rest, 0)
            for blk in range(2):
                dk_ref[pl.ds(offs[blk], KEY_BLK), :] = jnp.where(
                    first_k, dk_acc[2 * blk], dk_acc[2 * blk + 1]).astype(BF16)
                dv_ref[pl.ds(offs[blk], KEY_BLK), :] = jnp.where(
                    first_k, dv_acc[2 * blk], dv_acc[2 * blk + 1]).astype(BF16)
            return 0

        lax.fori_loop(0, nkp, kloop, 0)
        first_q = lax.broadcasted_iota(jnp.int32, (s, LANES), 1) < HEAD_DIM
        dq_ref[...] = jnp.where(first_q, dq_acc[0], dq_acc[1]).astype(BF16)
        if nx:
            pl.when(step_id == nb * npair - 1)(lambda: _finish_copies(copies))

    colspec = lambda j: pl.BlockSpec((s, LANES), lambda b, p: (b, j * npair + p))
    hbm = pl.BlockSpec(memory_space=pl.ANY)
    outs = pl.pallas_call(
        body, name=name, grid=(nb, npair),
        out_shape=[jax.ShapeDtypeStruct((t, aw), BF16)] * 3 + _exchange_out_shapes(exchange),
        in_specs=[colspec(0), colspec(1), colspec(2), colspec(0),
                  pl.BlockSpec((2, s, LANES), lambda b, p: (b * npair + p, 0, 0))] + [hbm] * nx,
        out_specs=[colspec(0), colspec(0), colspec(0)] + [hbm] * nx,
        scratch_shapes=[pltpu.VMEM((2, s, LANES), F32), pltpu.VMEM((2, s, LANES), F32),
                        pltpu.VMEM((4, KEY_BLK, LANES), F32), pltpu.VMEM((4, KEY_BLK, LANES), F32)]
        + (_copy_sems(nx) if nx else []),
        compiler_params=pltpu.CompilerParams(dimension_semantics=("arbitrary", "arbitrary")),
    )(qkv, qkv, qkv, do, car, *exchange)
    return outs[0], outs[1], outs[2], list(outs[3:])


def _ada_fwd(c_all, ada_w, ada_b_loc, name):
    nl, d, n6 = ada_w.shape
    nb = c_all.shape[0]

    def body(c_ref, w_ref, b_ref, o_ref):
        cv = c_ref[...]
        o_ref[0] = jnp.dot(cv * _sig(cv), w_ref[0], preferred_element_type=F32) + b_ref[0]

    return pl.pallas_call(
        body, name=name, grid=(nl,),
        out_shape=jax.ShapeDtypeStruct((nl, nb, n6), F32),
        in_specs=[pl.BlockSpec((nb, d), lambda l: (0, 0)), pl.BlockSpec((1, d, n6), lambda l: (l, 0, 0)),
                  pl.BlockSpec((1, 1, n6), lambda l: (l, 0, 0))],
        out_specs=pl.BlockSpec((1, nb, n6), lambda l: (l, 0, 0)),
    )(c_all, ada_w, ada_b_loc)


def _ada_bwd(c_all_t, dmod_loc, dmod_all, name):
    nl, nb, n6 = dmod_loc.shape
    d = c_all_t.shape[0]
    n_all = dmod_all.shape[2]

    def body(c_ref, dl_ref, da_ref, gw_ref, gb_ref):
        cv = c_ref[...]
        gw_ref[0] = jnp.dot(cv * _sig(cv), dl_ref[0], preferred_element_type=F32)
        gb_ref[0] = jnp.sum(da_ref[0], axis=0, keepdims=True)

    return pl.pallas_call(
        body, name=name, grid=(nl,),
        out_shape=[jax.ShapeDtypeStruct((nl, d, n6), F32), jax.ShapeDtypeStruct((nl, 1, n_all), F32)],
        in_specs=[pl.BlockSpec((d, nb), lambda l: (0, 0)), pl.BlockSpec((1, nb, n6), lambda l: (l, 0, 0)),
                  pl.BlockSpec((1, nb, n_all), lambda l: (l, 0, 0))],
        out_specs=[pl.BlockSpec((1, d, n6), lambda l: (l, 0, 0)), pl.BlockSpec((1, 1, n_all), lambda l: (l, 0, 0))],
    )(c_all_t, dmod_loc, dmod_all)


def _adamw(w, g, m, v, name):
    shape = w.shape
    cols = shape[-1]
    rows = w.size // cols
    tr = _tile(rows, (512, 256, 128, 64, 32, 16, 8))
    flat = lambda a: a.reshape(rows, cols)

    def body(w_ref, g_ref, m_ref, v_ref, d_ref, mo_ref, vo_ref):
        gv = g_ref[...]
        mn = ADAM_B1 * m_ref[...] + (1.0 - ADAM_B1) * gv
        vn = ADAM_B2 * v_ref[...] + (1.0 - ADAM_B2) * (gv * gv)
        m_hat = mn / (1.0 - ADAM_B1 ** ADAM_STEP)
        v_hat = vn / (1.0 - ADAM_B2 ** ADAM_STEP)
        d_ref[...] = -ADAM_LR * (m_hat / (jnp.sqrt(v_hat) + ADAM_EPS) + ADAM_WD * w_ref[...])
        mo_ref[...] = mn
        vo_ref[...] = vn

    spec = pl.BlockSpec((tr, cols), lambda i: (i, 0))
    outs = pl.pallas_call(
        body, name=name, grid=(rows // tr,),
        out_shape=[jax.ShapeDtypeStruct((rows, cols), F32)] * 3,
        in_specs=[spec] * 4, out_specs=[spec] * 3,
    )(flat(w), flat(g), flat(m), flat(v))
    return tuple(o.reshape(shape) for o in outs)


def _pack_rows(parts, d):
    flat = jnp.concatenate([p.reshape(-1) for p in parts])
    rows = -(-flat.size // d)
    rows = -(-rows // 8) * 8
    return jnp.pad(flat, (0, rows * d - flat.size)).reshape(rows, d)


def _to_heads(a, nb, s):
    h = a.shape[1] // HEAD_DIM
    return a.reshape(nb, s, h, HEAD_DIM).transpose(0, 2, 1, 3).reshape(nb * h, s, HEAD_DIM)


def _from_heads(a, nb, s):
    h = a.shape[0] // nb
    return a.reshape(nb, h, s, HEAD_DIM).transpose(0, 2, 1, 3).reshape(nb * s, h * HEAD_DIM)


def kernel(x, c, ada_w, ada_b, pre_mix_g, post_mix_g, pre_ffn_g, post_ffn_g, w_in, conv_w, conv_b, conv_ln_g, conv_ln_b, w_conv_out, w_att_out, w_o, w_ffn_in, w_ffn_out, loss_target, m_ada_w, m_ada_b, m_pre_mix_g, m_post_mix_g, m_pre_ffn_g, m_post_ffn_g, m_w_in, m_conv_w, m_conv_b, m_conv_ln_g, m_conv_ln_b, m_w_conv_out, m_w_att_out, m_w_o, m_w_ffn_in, m_w_ffn_out, v_ada_w, v_ada_b, v_pre_mix_g, v_post_mix_g, v_pre_ffn_g, v_post_ffn_g, v_w_in, v_conv_w, v_conv_b, v_conv_ln_g, v_conv_ln_b, v_w_conv_out, v_w_att_out, v_w_o, v_w_ffn_in, v_w_ffn_out):
    nb, s, d = x.shape
    nl = ada_w.shape[0]
    cw = conv_b.shape[1]
    aw = w_att_out.shape[1]
    cl = conv_w.shape[2]
    n6 = ada_w.shape[2]
    t = nb * s
    me = _my_index()

    conv_flat = conv_w.reshape(-1)
    p1 = _pack_rows([c, conv_flat], d)
    r1 = p1.shape[0]
    (p1_all,) = _all_gather([p1], "gather_c_convw", True)
    p1_all = p1_all.reshape(N_DEV, r1 * d)
    c_all = p1_all[:, :nb * d].reshape(N_DEV * nb, d)
    conv_w_all = p1_all[:, nb * d:nb * d + conv_flat.size].reshape(N_DEV, nl, CONV_K, cl)
    conv_w_all = conv_w_all.transpose(1, 2, 0, 3).reshape(nl, CONV_K, cw)

    ada_b_loc = lax.dynamic_slice_in_dim(ada_b, me * n6, n6, axis=1).reshape(nl, 1, n6)
    mod_cols = _ada_fwd(c_all, ada_w, ada_b_loc, "ada_fwd")
    (mod_g,) = _all_gather([mod_cols.reshape(-1, d)], "gather_mod", True)
    mod_g = mod_g.reshape(N_DEV, nl, N_DEV * nb, n6)
    mod_mine = lax.dynamic_slice_in_dim(mod_g, me * nb, nb, axis=2)
    mod = mod_mine.transpose(1, 2, 0, 3).reshape(nl, nb, 6 * d)
    mods = [[mod[l, :, i * d:(i + 1) * d].reshape(nb, 1, d) for i in range(6)] for l in range(nl)]

    tr_ = lambda w: jnp.swapaxes(w, 0, 1).astype(BF16)
    shards = [[tr_(w_in[l]), tr_(w_conv_out[l]), tr_(w_att_out[l]), w_o[l].astype(BF16),
               tr_(w_ffn_in[l]), w_ffn_out[l].astype(BF16)] for l in range(nl)]
    weights = [_all_gather(shards[0], "gather_weights", False)]

    vec = lambda a, l: a[l].reshape(1, -1)
    x2 = x.reshape(t, d)
    tgt = loss_target.reshape(t, d)

    saved = []
    xin = x2
    h = _norm_mod(xin, vec(pre_mix_g, 0), mods[0][1], mods[0][0], s, "norm_mod_0")
    for l in range(nl):
        win_t, wconv_t, watt_t, wo, wffn_in_t, wffn_out = weights[l]
        sh1, sc1, ga1, sh2, sc2, ga2 = mods[l]
        glu_in = _mm(h, win_t, NT, F32, f"proj_glu_{l}", 0, 2 * cw)
        qkv = _mm(h, win_t, NT, BF16, f"proj_qkv_{l}", 2 * cw, 3 * aw)
        gates = _mm(h, win_t, NT, F32, f"proj_gates_{l}", 2 * cw + 3 * aw, 2 * d)
        u, cv, ua = _conv_fwd(glu_in, conv_w_all[l], vec(conv_b, l), vec(conv_ln_g, l), vec(conv_ln_b, l),
                              nb, s, f"conv_fwd_{l}")
        o, car, nxt = _sb4_fwd(qkv, nb, s, f"attn_fwd_{l}", shards[l + 1] if l + 1 < nl else ())
        if l + 1 < nl:
            weights.append(nxt)
        y_conv = _mm(ua, wconv_t, NT, F32, f"conv_out_{l}")
        y_att = _mm(o, watt_t, NT, F32, f"att_out_{l}")
        merged = _merge(y_conv, y_att, gates, s, f"merge_{l}")
        y = _mm(merged, wo, NN, F32, f"w_o_{l}")
        x1, h2 = _post_res_norm(xin, y, vec(post_mix_g, l), ga1, vec(pre_ffn_g, l), sc2, sh2, s, f"post_mix_{l}")
        f = _mm(h2, wffn_in_t, NT, F32, f"ffn_in_{l}")
        a = _swiglu(f, s, f"swiglu_{l}")
        y2 = _mm(a, wffn_out, NN, F32, f"ffn_out_{l}")
        saved.append(dict(xin=xin, h=h, glu_in=glu_in, gates=gates, u=u, cv=cv, ua=ua, qkv=qkv,
                          car=car, o=o, y_conv=y_conv, y_att=y_att, merged=merged, y=y, x1=x1, h2=h2, f=f,
                          a=a, y2=y2))
        if l + 1 < nl:
            nsh1, nsc1 = mods[l + 1][0], mods[l + 1][1]
            xin, h = _post_res_norm(x1, y2, vec(post_ffn_g, l), ga2, vec(pre_mix_g, l + 1), nsc1, nsh1, s,
                                    f"post_ffn_{l}")
        else:
            dx, loss_acc = _post_res_loss(x1, y2, vec(post_ffn_g, l), ga2, tgt, s, "loss")

    loss = lax.psum(loss_acc[0, 0], ("x", "y", "c"))

    slots = [None] * (6 * nl)
    pending = []
    small = [None] * nl
    for l in reversed(range(nl)):
        win_t, wconv_t, watt_t, wo, wffn_in_t, wffn_out = weights[l]
        sh1, sc1, ga1, sh2, sc2, ga2 = mods[l]
        sv = saved[l]
        dy2, dg_post_ffn, dga2 = _post_bwd(dx, sv["y2"], vec(post_ffn_g, l), ga2, s, f"post_ffn_bwd_{l}")
        d_wffn_out = _mm(sv["a"], dy2, TN, BF16, f"d_w_ffn_out_{l}")
        da = _mm(dy2, wffn_out, NT, F32, f"d_a_{l}")
        df = _swiglu_bwd(da, sv["f"], s, f"swiglu_bwd_{l}")
        d_wffn_in_t = _mm(df, sv["h2"], TN, BF16, f"d_w_ffn_in_{l}")
        dh2 = _mm(df, wffn_in_t, NN, F32, f"d_h2_{l}")
        dx1, dsh2, dsc2, dg_pre_ffn = _pre_bwd(dh2, sv["x1"], vec(pre_ffn_g, l), sc2, dx, s, f"pre_ffn_bwd_{l}")
        dy, dg_post_mix, dga1 = _post_bwd(dx1, sv["y"], vec(post_mix_g, l), ga1, s, f"post_mix_bwd_{l}")
        d_wo = _mm(sv["merged"], dy, TN, BF16, f"d_w_o_{l}")
        dmerged = _mm(dy, wo, NT, F32, f"d_merged_{l}")
        dyc, dya, dgates = _merge_bwd(dmerged, sv["y_conv"], sv["y_att"], sv["gates"], s, f"merge_bwd_{l}")
        d_wconv_t = _mm(dyc, sv["ua"], TN, BF16, f"d_w_conv_out_{l}")
        dua = _mm(dyc, wconv_t, NN, F32, f"d_ua_{l}")
        d_watt_t = _mm(dya, sv["o"], TN, BF16, f"d_w_att_out_{l}")
        do = _mm(dya, watt_t, NN, BF16, f"d_o_{l}")
        ready = [(l, 5, d_wffn_out), (l, 4, d_wffn_in_t)] + pending
        dq, dk, dv, got = _sb4_bwd(sv["qkv"], do, sv["car"], nb, s, f"attn_bwd_{l}", [g for _, _, g in ready])
        for (gl, gi, _), r in zip(ready, got):
            slots[6 * gl + gi] = r
        dcv, dln_g, dln_b, dconv_b = _convln_bwd(dua, sv["cv"], vec(conv_ln_g, l), vec(conv_ln_b, l), s,
                                                 f"convln_bwd_{l}")
        dglu, dconv_w = _conv_bwd(dcv, sv["u"], sv["glu_in"], conv_w_all[l], nb, s, f"conv_bwd_{l}")
        dproj = jnp.concatenate([dglu, dq, dk, dv, dgates], axis=1)
        d_win_t = _mm(dproj, sv["h"], TN, BF16, f"d_w_in_{l}")
        dh = _mm(dproj, win_t, NN, F32, f"d_h_{l}")
        dx, dsh1, dsc1, dg_pre_mix = _pre_bwd(dh, sv["xin"], vec(pre_mix_g, l), sc1, dx1, s, f"pre_mix_bwd_{l}")
        pending = [(l, 0, d_win_t), (l, 1, d_wconv_t), (l, 2, d_watt_t), (l, 3, d_wo)]
        dmod =jnp.concatenate([dsh1, dsc1, dga1, dsh2, dsc2, dga2], axis=2)
        small[l] = [dg_pre_mix, dg_post_mix, dg_pre_ffn, dg_post_ffn, dconv_b, dln_g, dln_b, dconv_w, dmod]
    grad_x = dx.reshape(nb, s, d)

    sizes = [a.size for a in small[0]]
    p3 = _pack_rows([a for l in range(nl) for a in small[l]], d)
    r3 = p3.shape[0]
    (p3_all,) = _all_gather([p3], "gather_small_grads", True)
    p3_all = p3_all.reshape(N_DEV, r3, d)
    p3_sum = _sum_slots(p3_all, "sum_small_grads").reshape(-1)
    per_layer = sum(sizes)
    small_sum, dmod_all = [], []
    for l in range(nl):
        off, parts = l * per_layer, []
        for sz in sizes[:-1]:
            parts.append(p3_sum[off:off + sz])
            off += sz
        small_sum.append(parts)
        dm = p3_all.reshape(N_DEV, r3 * d)[:, off:off + sizes[-1]]
        dmod_all.append(dm.reshape(N_DEV * nb, 6 * d))
    dmod_all = jnp.stack(dmod_all)
    dmod_loc = lax.dynamic_slice_in_dim(dmod_all, me * n6, n6, axis=2)
    g_ada_w, g_ada_b = _ada_bwd(c_all.T, dmod_loc, dmod_all, "ada_bwd")
    g_ada_b = g_ada_b.reshape(nl, 6 * d)

    stack_small = lambda i, shape: jnp.stack([small_sum[l][i].reshape(shape) for l in range(nl)])
    g_pre_mix, g_post_mix, g_pre_ffn, g_post_ffn = (stack_small(i, (d,)) for i in range(4))
    g_conv_b, g_ln_g, g_ln_b = (stack_small(i, (cw,)) for i in range(4, 7))
    g_conv_w_all = stack_small(7, (HALO, cw))[:, :CONV_K]
    g_conv_w = lax.dynamic_slice_in_dim(g_conv_w_all, me * cl, cl, axis=2)

    for (gl, gi, _), r in zip(pending, _exchange_blocks([g for _, _, g in pending], "exchange_grads")):
        slots[6 * gl + gi] = r
    sums =[_sum_slots(r, f"sum_grads_{i}") for i, r in enumerate(slots)]
    un_t = lambda i: jnp.stack([jnp.swapaxes(sums[6 * l + i], 0, 1) for l in range(nl)])
    keep = lambda i: jnp.stack([sums[6 * l + i] for l in range(nl)])
    g_w_in, g_w_conv_out, g_w_att_out, g_w_o, g_w_ffn_in, g_w_ffn_out = (
        un_t(0), un_t(1), un_t(2), keep(3), un_t(4), keep(5))

    grads = [g_ada_w, g_ada_b, g_pre_mix, g_post_mix, g_pre_ffn, g_post_ffn, g_w_in, g_conv_w, g_conv_b,
             g_ln_g, g_ln_b, g_w_conv_out, g_w_att_out, g_w_o, g_w_ffn_in, g_w_ffn_out]
    ws = [ada_w, ada_b, pre_mix_g, post_mix_g, pre_ffn_g, post_ffn_g, w_in, conv_w, conv_b, conv_ln_g,
          conv_ln_b, w_conv_out, w_att_out, w_o, w_ffn_in, w_ffn_out]
    ms = [m_ada_w, m_ada_b, m_pre_mix_g, m_post_mix_g, m_pre_ffn_g, m_post_ffn_g, m_w_in, m_conv_w, m_conv_b,
          m_conv_ln_g, m_conv_ln_b, m_w_conv_out, m_w_att_out, m_w_o, m_w_ffn_in, m_w_ffn_out]
    vs = [v_ada_w, v_ada_b, v_pre_mix_g, v_post_mix_g, v_pre_ffn_g, v_post_ffn_g, v_w_in, v_conv_w, v_conv_b,
          v_conv_ln_g, v_conv_ln_b, v_w_conv_out, v_w_att_out, v_w_o, v_w_ffn_in, v_w_ffn_out]
    deltas, new_m, new_v = [], [], []
    for i, (w, g, m, v) in enumerate(zip(ws, grads, ms, vs)):
        dlt, mn, vn = _adamw(w, g, m, v, f"adamw_{i}")
        deltas.append(dlt)
        new_m.append(mn)
        new_v.append(vn)
    return (loss, grad_x, *grads, *deltas, *new_m, *new_v)
```

```python
import functools
import math

import jax
import jax.numpy as jnp
from jax import lax
from jax.experimental import pallas as pl
from jax.experimental.pallas import tpu as pltpu

F32 = jnp.float32
BF16 = jnp.bfloat16
MESH = pl.DeviceIdType.MESH
N_DEV = 8
EPS = 1e-6
HEAD_DIM = 64
CONV_K = 31
HALO = 32
KEY_BLK = 256
ADAM_LR, ADAM_B1, ADAM_B2, ADAM_EPS, ADAM_WD, ADAM_STEP = 0.001, 0.9, 0.999, 1e-08, 0.01, 10

NT = (((1,), (1,)), ((), ()))
NN = (((1,), (0,)), ((), ()))
TN = (((0,), (0,)), ((), ()))


def _tile(n, cands):
    for t in cands:
        if n % t == 0:
            return t
    return n


def _lane_tile(n, cap):
    best = n
    for t in range(128, min(n, cap) + 1, 128):
        if n % t == 0:
            best = t
    return best


def _sig(x):
    return 1.0 / (1.0 + jnp.exp(-x))


def _pos():
    return lax.axis_index("x"), lax.axis_index("y"), lax.axis_index("c")


def _my_index():
    x, y, c = _pos()
    return 4 * x + 2 * y + c


def _all_gather(arrs, name, in_vmem):
    n = len(arrs)

    def body(*refs):
        ins, outs = refs[:n], refs[n:2 * n]
        send_sems, recv_sems, local_sems = refs[2 * n:]
        x, y, c = _pos()
        me, sib = (x, y, c), (x, y, 1 - c)
        chips = [(1 - x, y), (x, 1 - y), (1 - x, 1 - y)]

        def rows(i, p):
            r = ins[i].shape[0]
            return outs[i].at[pl.ds((4 * p[0] + 2 * p[1] + p[2]) * r, r), :]

        def copy(i, k, block, to, src=None):
            return pltpu.make_async_remote_copy(
                src_ref=rows(i, block) if src is None else src, dst_ref=rows(i, block),
                send_sem=send_sems.at[i, k], recv_sem=recv_sems.at[i, k],
                device_id=to, device_id_type=MESH)

        mine = [pltpu.make_async_copy(ins[i], rows(i, me), local_sems.at[i]) for i in range(n)]
        for cp in mine:
            cp.start()
        first = []
        for j, chip in enumerate(chips):
            for i in range(n):
                first.append(copy(i, 1 + j, me, (*chip, c), src=ins[i]))
        for i in range(n):
            first.append(copy(i, 0, me, sib, src=ins[i]))
        for cp in first:
            cp.start()
        passed = []
        for j, chip in enumerate(chips):
            for i in range(n):
                copy(i, 1 + j, (*chip, c), me).wait_recv()
                p = copy(i, 4 + j, (*chip, c), sib)
                p.start()
                passed.append(p)
        for i in range(n):
            copy(i, 0, sib, me).wait_recv()
            for j, chip in enumerate(chips):
                copy(i, 4 + j, (*chip, 1 - c), me).wait_recv()
        for cp in first + passed:
            cp.wait_send()
        for cp in mine:
            cp.wait()

    space = pltpu.VMEM if in_vmem else pl.ANY
    spec = pl.BlockSpec(memory_space=space)
    return pl.pallas_call(
        body, name=name,
        out_shape=[jax.ShapeDtypeStruct((N_DEV * a.shape[0], a.shape[1]), a.dtype) for a in arrs],
        in_specs=[spec] * n, out_specs=[spec] * n,
        scratch_shapes=[pltpu.SemaphoreType.DMA((n, 7)), pltpu.SemaphoreType.DMA((n, 7)),
                        pltpu.SemaphoreType.DMA((n,))],
    )(*arrs)


def _exchange_blocks(arrs, name):
    n = len(arrs)

    def body(*refs):
        copies = _exchange_copies(refs[:n], refs[n:2 * n], *refs[2 * n:])
        _start_copies(copies)
        _finish_copies(copies)

    spec = pl.BlockSpec(memory_space=pl.ANY)
    return pl.pallas_call(
        body, name=name, out_shape=_exchange_out_shapes(arrs),
        in_specs=[spec] * n, out_specs=[spec] * n, scratch_shapes=_copy_sems(n),
    )(*arrs)


PEER_ORDER = (4, 2, 6, 5, 3, 7, 1)


def _peer(k):
    x, y, c = _pos()
    return (1 - x if k & 4 else x, 1 - y if k & 2 else y, 1 - c if k & 1 else c)


def _mesh_index(p):
    return 4 * p[0] + 2 * p[1] + p[2]


def _copy_sems(n):
    return [pltpu.SemaphoreType.DMA((n, 7)), pltpu.SemaphoreType.DMA((n, 7)), pltpu.SemaphoreType.DMA((n,))]


def _exchange_out_shapes(arrs):
    return [jax.ShapeDtypeStruct((N_DEV, a.shape[0] // N_DEV, a.shape[1]), a.dtype) for a in arrs]


def _gather_out_shapes(arrs):
    return [jax.ShapeDtypeStruct((N_DEV * a.shape[0], a.shape[1]), a.dtype) for a in arrs]


def _exchange_copies(ins, outs, send_sems, recv_sems, local_sems):
    n = len(ins)

    def block(i, p):
        r = outs[i].shape[1]
        return ins[i].at[pl.ds(_mesh_index(p) * r, r), :]

    mine = [pltpu.make_async_copy(block(i, _peer(0)), outs[i].at[0], local_sems.at[i]) for i in range(n)]
    sends = [pltpu.make_async_remote_copy(
        src_ref=block(i, _peer(k)), dst_ref=outs[i].at[k], send_sem=send_sems.at[i, k - 1],
        recv_sem=recv_sems.at[i, k - 1], device_id=_peer(k), device_id_type=MESH)
        for k in PEER_ORDER for i in range(n)]
    return mine, sends, sends


def _gather_copies(ins, outs, send_sems, recv_sems, local_sems):
    n = len(ins)

    def rows(i, p):
        r = ins[i].shape[0]
        return outs[i].at[pl.ds(_mesh_index(p) * r, r), :]

    def copy(i, k, landing):
        return pltpu.make_async_remote_copy(
            src_ref=ins[i], dst_ref=rows(i, landing), send_sem=send_sems.at[i, k - 1],
            recv_sem=recv_sems.at[i, k - 1], device_id=_peer(k), device_id_type=MESH)

    mine = [pltpu.make_async_copy(ins[i], rows(i, _peer(0)), local_sems.at[i]) for i in range(n)]
    sends = [copy(i, k, _peer(0)) for k in PEER_ORDER for i in range(n)]
    recvs = [copy(i, k, _peer(k)) for k in PEER_ORDER for i in range(n)]
    return mine, sends, recvs


def _start_copies(copies):
    mine, sends, _ = copies
    for cp in mine + sends:
        cp.start()


def _finish_copies(copies):
    mine, sends, recvs = copies
    for cp in recvs:
        cp.wait_recv()
    for cp in sends:
        cp.wait_send()
    for cp in mine:
        cp.wait()


def _sum_slots(r3, name):
    _, r, c = r3.shape
    tr = _tile(r, (128, 96, 64, 32, 16))

    def body(r_ref, o_ref):
        acc = r_ref[0].astype(F32)
        for k in range(1, N_DEV):
            acc = acc + r_ref[k].astype(F32)
        o_ref[...] = acc

    return pl.pallas_call(
        body, name=name, grid=(r // tr,),
        out_shape=jax.ShapeDtypeStruct((r, c), F32),
        in_specs=[pl.BlockSpec((N_DEV, tr, c), lambda i: (0, i, 0))],
        out_specs=pl.BlockSpec((tr, c), lambda i: (i, 0)),
    )(r3)


def _mm(a, b, dims, out_dtype, name, n0=0, n=None, exchange=()):
    if dims is NT:
        m, k = a.shape
        n = b.shape[0] if n is None else n
    elif dims is NN:
        m, k = a.shape
        n = b.shape[1]
    else:
        k, m = a.shape
        n = b.shape[1]
    if dims is TN:
        tm = _lane_tile(m, 1536)
        tn = _lane_tile(n, 1024)
        tk = _tile(k, (1024, 512, 256, 128))
    else:
        tm = _tile(m, (1024, 512, 256, 128))
        tn = _lane_tile(math.gcd(n, n0) if n0 else n, 1536)
        tk = _lane_tile(k, 2816)
    nk = k // tk
    noff = n0 // tn

    nx = len(exchange)
    grid = (m // tm, n // tn, nk)

    def body(*refs):
        a_ref, b_ref = refs[:2]
        o_ref = refs[2 + nx]
        scratch = refs[3 + 2 * nx:]
        if nx:
            copies = _exchange_copies(refs[2:2 + nx], refs[3 + nx:3 + 2 * nx], *scratch[-3:])
            step_id = (pl.program_id(0) * grid[1] + pl.program_id(1)) * grid[2] + pl.program_id(2)
            pl.when(step_id == 0)(lambda: _start_copies(copies))
        part =lax.dot_general(a_ref[...].astype(BF16), b_ref[...].astype(BF16), dims,
                               preferred_element_type=F32)
        if nk == 1:
            o_ref[...] = part.astype(o_ref.dtype)
        else:
            acc_ref = scratch[0]
            kk = pl.program_id(2)

            @pl.when(kk == 0)
            def _():
                acc_ref[...] = part

            @pl.when(kk > 0)
            def _():
                acc_ref[...] += part

            @pl.when(kk == nk - 1)
            def _():
                o_ref[...] = acc_ref[...].astype(o_ref.dtype)

        if nx:
            pl.when(step_id == grid[0] * grid[1] * grid[2] - 1)(lambda: _finish_copies(copies))

    if dims is NT:
        in_specs = [pl.BlockSpec((tm, tk), lambda i, j, kk: (i, kk)),
                    pl.BlockSpec((tn, tk), lambda i, j, kk: (j + noff, kk))]
    elif dims is NN:
        in_specs = [pl.BlockSpec((tm, tk), lambda i, j, kk: (i, kk)),
                    pl.BlockSpec((tk, tn), lambda i, j, kk: (kk, j))]
    else:
        in_specs = [pl.BlockSpec((tk, tm), lambda i, j, kk: (kk, i)),
                    pl.BlockSpec((tk, tn), lambda i, j, kk: (kk, j))]
    hbm = pl.BlockSpec(memory_space=pl.ANY)
    outs = pl.pallas_call(
        body, name=name, grid=grid,
        out_shape=[jax.ShapeDtypeStruct((m, n), out_dtype)] + _exchange_out_shapes(exchange),
        in_specs=in_specs + [hbm] * nx,
        out_specs=[pl.BlockSpec((tm, tn), lambda i, j, kk: (i, j))] + [hbm] * nx,
        scratch_shapes=([] if nk == 1 else [pltpu.VMEM((tm, tn), F32)]) + (_copy_sems(nx) if nx else []),
        compiler_params=pltpu.CompilerParams(
            dimension_semantics=("arbitrary",) * 3 if nx else ("parallel", "parallel", "arbitrary")),
    )(a, b, *exchange)
    return (outs[0], list(outs[1:])) if nx else outs[0]


def _tok_tile(s):
    return _tile(s, (512, 256, 128))


def _row_spec(tm, d):
    return pl.BlockSpec((tm, d), lambda i: (i, 0))


def _vec_spec(d):
    return pl.BlockSpec((1, d), lambda i: (0, 0))


def _seq_spec(d, tiles_per_seq):
    return pl.BlockSpec((1, 1, d), lambda i: (i // tiles_per_seq, 0, 0))


def _rms(x):
    return lax.rsqrt(jnp.mean(x * x, axis=-1, keepdims=True) + EPS)


def _norm_mod(x, g, sc, sh, s, name):
    t, d = x.shape
    tm = _tok_tile(s)

    def body(x_ref, g_ref, sc_ref, sh_ref, h_ref):
        xv = x_ref[...]
        h = (xv * _rms(xv)) * g_ref[...]
        h_ref[...] = (h * (1.0 + sc_ref[0]) + sh_ref[0]).astype(BF16)

    return pl.pallas_call(
        body, name=name, grid=(t // tm,),
        out_shape=jax.ShapeDtypeStruct((t, d), BF16),
        in_specs=[_row_spec(tm, d), _vec_spec(d), _seq_spec(d, s // tm), _seq_spec(d, s // tm)],
        out_specs=_row_spec(tm, d),
    )(x, g, sc, sh)


def _post_res_norm(x, y, g_post, ga, g_pre, sc, sh, s, name):
    t, d = x.shape
    tm = _tok_tile(s)

    def body(x_ref, y_ref, gp_ref, ga_ref, g_ref, sc_ref, sh_ref, xn_ref, h_ref):
        yv = y_ref[...]
        xn = x_ref[...] + ga_ref[0] * ((yv * _rms(yv)) * gp_ref[...])
        xn_ref[...] = xn
        h = (xn * _rms(xn)) * g_ref[...]
        h_ref[...] = (h * (1.0 + sc_ref[0]) + sh_ref[0]).astype(BF16)

    tps = s // tm
    return pl.pallas_call(
        body, name=name, grid=(t // tm,),
        out_shape=[jax.ShapeDtypeStruct((t, d), F32), jax.ShapeDtypeStruct((t, d), BF16)],
        in_specs=[_row_spec(tm, d), _row_spec(tm, d), _vec_spec(d), _seq_spec(d, tps),
                  _vec_spec(d), _seq_spec(d, tps), _seq_spec(d, tps)],
        out_specs=[_row_spec(tm, d), _row_spec(tm, d)],
    )(x, y, g_post, ga, g_pre, sc, sh)


def _post_res_loss(x, y, g_post, ga, target, s, name):
    t, d = x.shape
    tm = _tok_tile(s)

    def body(x_ref, y_ref, gp_ref, ga_ref, t_ref, dx_ref, loss_ref):
        yv = y_ref[...]
        err = x_ref[...] + ga_ref[0] * ((yv * _rms(yv)) * gp_ref[...]) - t_ref[...]
        dx_ref[...] = err * (1.0 / d)
        part = 0.5 * jnp.sum(jnp.mean(err * err, axis=-1, keepdims=True), axis=0, keepdims=True)

        @pl.when(pl.program_id(0) == 0)
        def _():
            loss_ref[...] = jnp.zeros_like(loss_ref)

        loss_ref[...] += part

    return pl.pallas_call(
        body, name=name, grid=(t // tm,),
        out_shape=[jax.ShapeDtypeStruct((t, d), F32), jax.ShapeDtypeStruct((8, 128), F32)],
        in_specs=[_row_spec(tm, d), _row_spec(tm, d), _vec_spec(d), _seq_spec(d, s // tm), _row_spec(tm, d)],
        out_specs=[_row_spec(tm, d), pl.BlockSpec((8, 128), lambda i: (0, 0))],
    )(x, y, g_post, ga, target)


def _post_bwd(dxn, y, g, ga, s, name):
    t, d = y.shape
    tm = _tok_tile(s)
    tps = s // tm

    def body(dx_ref, y_ref, g_ref, ga_ref, dy_ref, dg_ref, dga_ref):
        i = pl.program_id(0)
        yv = y_ref[...]
        r = _rms(yv)
        yh = yv * r
        dxv = dx_ref[...]
        dn = dxv * ga_ref[0]
        dyh = dn * g_ref[...]
        dy_ref[...] = (r * (dyh - yh * jnp.mean(dyh * yh, axis=-1, keepdims=True))).astype(BF16)

        @pl.when(i == 0)
        def _():
            dg_ref[...] = jnp.zeros_like(dg_ref)

        @pl.when(i % tps == 0)
        def _():
            dga_ref[...] = jnp.zeros_like(dga_ref)

        dg_ref[...] += jnp.sum(dn * yh, axis=0, keepdims=True)
        dga_ref[0] += jnp.sum(dxv * (yh * g_ref[...]), axis=0, keepdims=True)

    return pl.pallas_call(
        body, name=name, grid=(t // tm,),
        out_shape=[jax.ShapeDtypeStruct((t, d), BF16), jax.ShapeDtypeStruct((1, d), F32),
                   jax.ShapeDtypeStruct((t // s, 1, d), F32)],
        in_specs=[_row_spec(tm, d), _row_spec(tm, d), _vec_spec(d), _seq_spec(d, tps)],
        out_specs=[_row_spec(tm, d), _vec_spec(d), _seq_spec(d, tps)],
    )(dxn, y, g, ga)


def _pre_bwd(dh, x, g, sc, dres, s, name):
    t, d = x.shape
    tm = _tok_tile(s)
    tps = s // tm

    def body(dh_ref, x_ref, g_ref, sc_ref, dr_ref, dx_ref, dsh_ref, dsc_ref, dg_ref):
        i = pl.program_id(0)
        xv = x_ref[...]
        r = _rms(xv)
        xh = xv * r
        dhv = dh_ref[...]
        one_sc = 1.0 + sc_ref[0]
        dxh = dhv * one_sc * g_ref[...]
        dx_ref[...] = dr_ref[...] + r * (dxh - xh * jnp.mean(dxh * xh, axis=-1, keepdims=True))

        @pl.when(i == 0)
        def _():
            dg_ref[...] = jnp.zeros_like(dg_ref)

        @pl.when(i % tps == 0)
        def _():
            dsh_ref[...] = jnp.zeros_like(dsh_ref)
            dsc_ref[...] = jnp.zeros_like(dsc_ref)

        dg_ref[...] += jnp.sum(dhv * one_sc * xh, axis=0, keepdims=True)
        dsh_ref[0] += jnp.sum(dhv, axis=0, keepdims=True)
        dsc_ref[0] += jnp.sum(dhv * (xh * g_ref[...]), axis=0, keepdims=True)

    nb = t // s
    return pl.pallas_call(
        body, name=name, grid=(t // tm,),
        out_shape=[jax.ShapeDtypeStruct((t, d), F32), jax.ShapeDtypeStruct((nb, 1, d), F32),
                   jax.ShapeDtypeStruct((nb, 1, d), F32), jax.ShapeDtypeStruct((1, d), F32)],
        in_specs=[_row_spec(tm, d), _row_spec(tm, d), _vec_spec(d), _seq_spec(d, tps), _row_spec(tm, d)],
        out_specs=[_row_spec(tm, d), _seq_spec(d, tps), _seq_spec(d, tps), _vec_spec(d)],
    )(dh, x, g, sc, dres)


def _merge(y_conv, y_att, gates, s, name):
    t, d = y_conv.shape
    tm = _tok_tile(s)

    def body(yc_ref, ya_ref, gc_ref, gt_ref, o_ref):
        f32 = lambda r: r[...].astype(F32)
        o_ref[...] = (_sig(f32(gc_ref)) * f32(yc_ref) + _sig(f32(gt_ref)) * f32(ya_ref)).astype(BF16)

    return pl.pallas_call(
        body, name=name, grid=(t // tm,),
        out_shape=jax.ShapeDtypeStruct((t, d), BF16),
        in_specs=[_row_spec(tm, d), _row_spec(tm, d), pl.BlockSpec((tm, d), lambda i: (i, 0)),
                  pl.BlockSpec((tm, d), lambda i: (i, 1))],
        out_specs=_row_spec(tm, d),
    )(y_conv, y_att, gates, gates)


def _merge_bwd(dm, y_conv, y_att, gates, s, name):
    t, d = y_conv.shape
    tm = _tok_tile(s)

    def body(dm_ref, yc_ref, ya_ref, gc_ref, gt_ref, dyc_ref, dya_ref, dg_ref):
        f32 = lambda r: r[...].astype(F32)
        dmv = f32(dm_ref)
        sc_, st_ = _sig(f32(gc_ref)), _sig(f32(gt_ref))
        dyc_ref[...] = (dmv * sc_).astype(BF16)
        dya_ref[...] = (dmv * st_).astype(BF16)
        dg_ref[:, :d] = (dmv * f32(yc_ref) * (sc_ * (1.0 - sc_))).astype(BF16)
        dg_ref[:, d:] = (dmv * f32(ya_ref) * (st_ * (1.0 - st_))).astype(BF16)

    return pl.pallas_call(
        body, name=name, grid=(t // tm,),
        out_shape=[jax.ShapeDtypeStruct((t, d), BF16), jax.ShapeDtypeStruct((t, d), BF16),
                   jax.ShapeDtypeStruct((t, 2 * d), BF16)],
        in_specs=[_row_spec(tm, d), _row_spec(tm, d), _row_spec(tm, d),
                  pl.BlockSpec((tm, d), lambda i: (i, 0)), pl.BlockSpec((tm, d), lambda i: (i, 1))],
        out_specs=[_row_spec(tm, d), _row_spec(tm, d), _row_spec(tm, 2 * d)],
    )(dm, y_conv, y_att, gates, gates)


def _swiglu(f, s, name):
    t, two = f.shape
    dff = two // 2
    tm = _tok_tile(s)

    def body(g_ref, u_ref, a_ref):
        gv = g_ref[...].astype(F32)
        a_ref[...] = (gv * _sig(gv) * u_ref[...].astype(F32)).astype(BF16)

    return pl.pallas_call(
        body, name=name, grid=(t // tm,),
        out_shape=jax.ShapeDtypeStruct((t, dff), BF16),
        in_specs=[pl.BlockSpec((tm, dff), lambda i: (i, 0)), pl.BlockSpec((tm, dff), lambda i: (i, 1))],
        out_specs=_row_spec(tm, dff),
    )(f, f)


def _swiglu_bwd(da, f, s, name):
    t, two = f.shape
    dff = two // 2
    tm = _tok_tile(s)

    def body(da_ref, g_ref, u_ref, df_ref):
        gv, dav = g_ref[...].astype(F32), da_ref[...].astype(F32)
        sg = _sig(gv)
        df_ref[:, :dff] = (dav * u_ref[...].astype(F32) * (sg * (1.0 + gv * (1.0 - sg)))).astype(BF16)
        df_ref[:, dff:] = (dav * (gv * sg)).astype(BF16)

    return pl.pallas_call(
        body, name=name, grid=(t // tm,),
        out_shape=jax.ShapeDtypeStruct((t, two), BF16),
        in_specs=[_row_spec(tm, dff), pl.BlockSpec((tm, dff), lambda i: (i, 0)),
                  pl.BlockSpec((tm, dff), lambda i: (i, 1))],
        out_specs=_row_spec(tm, two),
    )(da, f, f)


CONV_CHUNK = 64


def _conv_fwd(glu_in, conv_w, conv_b, ln_g, ln_b, nb, s, name):
    t, two = glu_in.shape
    cw = two // 2
    tt = _tile(s, (256, 128, 64))
    spt = s // tt
    ch = min(CONV_CHUNK, tt)

    def body(cur_ref, halo_ref, w_ref, b_ref, g_ref, be_ref, u_ref, cv_ref, ua_ref, ext_ref):
        j = pl.program_id(1)
        cur = cur_ref[...]
        u_cur = cur[:, :cw] * _sig(cur[:, cw:])
        hal = halo_ref[...]
        u_hal = hal[:, :cw] * _sig(hal[:, cw:])
        ext_ref[0:HALO, :] = jnp.where(j == 0, 0.0, u_hal)
        ext_ref[HALO:HALO + tt, :] = u_cur
        u_ref[...] = u_cur
        for c0 in range(0, tt, ch):
            acc = jnp.zeros((ch, cw), F32) + b_ref[...]
            for k in range(CONV_K):
                st = c0 + HALO - (CONV_K - 1) + k
                acc = acc + w_ref[k:k + 1, :] * ext_ref[st:st + ch, :]
            cv_ref[c0:c0 + ch, :] = acc
            mu = jnp.mean(acc, axis=-1, keepdims=True)
            xc = acc - mu
            yv = xc * lax.rsqrt(jnp.mean(xc * xc, axis=-1, keepdims=True) + EPS) * g_ref[...] + be_ref[...]
            ua_ref[c0:c0 + ch, :] = (yv * _sig(yv)).astype(BF16)

    cur_map = lambda b, j: (b * spt + j, 0)
    halo_map = lambda b, j: (jnp.maximum((b * s + j * tt) // HALO - 1, 0), 0)
    vec = lambda d0: pl.BlockSpec((d0, cw), lambda b, j: (0, 0))
    return pl.pallas_call(
        body, name=name, grid=(nb, spt),
        out_shape=[jax.ShapeDtypeStruct((t, cw), F32), jax.ShapeDtypeStruct((t, cw), F32),
                   jax.ShapeDtypeStruct((t, cw), BF16)],
        in_specs=[pl.BlockSpec((tt, two), cur_map), pl.BlockSpec((HALO, two), halo_map),
                  vec(CONV_K), vec(1), vec(1), vec(1)],
        out_specs=[pl.BlockSpec((tt, cw), cur_map)] * 3,
        scratch_shapes=[pltpu.VMEM((tt + HALO, cw), F32)],
    )(glu_in, glu_in, conv_w, conv_b, ln_g, ln_b)


def _convln_bwd(dua, cv, ln_g, ln_b, s, name):
    t, cw = cv.shape
    tm = _tok_tile(s)

    def body(du_ref, cv_ref, g_ref, be_ref, dcv_ref, dg_ref, dbe_ref, db_ref):
        cvv = cv_ref[...]
        mu = jnp.mean(cvv, axis=-1, keepdims=True)
        xc = cvv - mu
        rstd = lax.rsqrt(jnp.mean(xc * xc, axis=-1, keepdims=True) + EPS)
        xh = xc * rstd
        yv = xh * g_ref[...] + be_ref[...]
        sg = _sig(yv)
        dy = du_ref[...] * (sg * (1.0 + yv * (1.0 - sg)))
        dxh = dy * g_ref[...]
        dcv = rstd * (dxh - jnp.mean(dxh, axis=-1, keepdims=True)
                      - xh * jnp.mean(dxh * xh, axis=-1, keepdims=True))
        dcv_ref[...] = dcv

        @pl.when(pl.program_id(0) == 0)
        def _():
            dg_ref[...] = jnp.zeros_like(dg_ref)
            dbe_ref[...] = jnp.zeros_like(dbe_ref)
            db_ref[...] = jnp.zeros_like(db_ref)

        dg_ref[...] += jnp.sum(dy * xh, axis=0, keepdims=True)
        dbe_ref[...] += jnp.sum(dy, axis=0, keepdims=True)
        db_ref[...] += jnp.sum(dcv, axis=0, keepdims=True)

    return pl.pallas_call(
        body, name=name, grid=(t // tm,),
        out_shape=[jax.ShapeDtypeStruct((t, cw), F32)] + [jax.ShapeDtypeStruct((1, cw), F32)] * 3,
        in_specs=[_row_spec(tm, cw), _row_spec(tm, cw), _vec_spec(cw), _vec_spec(cw)],
        out_specs=[_row_spec(tm, cw), _vec_spec(cw), _vec_spec(cw), _vec_spec(cw)],
    )(dua, cv, ln_g, ln_b)


def _conv_bwd(dcv, u, glu_in, conv_w, nb, s, name):
    t, cw = dcv.shape
    tt = _tile(s, (256, 128, 64))
    spt = s // tt
    ch = min(CONV_CHUNK, tt)
    nblk = t // HALO

    def body(d_ref, dn_ref, u_ref, up_ref, glu_ref, w_ref, dglu_ref, dw_ref, dext_ref, uext_ref):
        b, j = pl.program_id(0), pl.program_id(1)
        dcur = d_ref[...]
        dext_ref[0:tt, :] = dcur
        dext_ref[tt:tt + HALO, :] = jnp.where(j == spt - 1, 0.0, dn_ref[...])
        uext_ref[0:HALO, :] = jnp.where(j == 0, 0.0, up_ref[...])
        uext_ref[HALO:HALO + tt, :] = u_ref[...]

        @pl.when((b == 0) & (j == 0))
        def _():
            dw_ref[...] = jnp.zeros_like(dw_ref)

        for c0 in range(0, tt, ch):
            dc = dext_ref[c0:c0 + ch, :]
            du = jnp.zeros((ch, cw), F32)
            for k in range(CONV_K):
                sd = c0 + (CONV_K - 1) - k
                du = du + w_ref[k:k + 1, :] * dext_ref[sd:sd + ch, :]
                su = c0 + HALO - (CONV_K - 1) + k
                dw_ref[k:k + 1, :] += jnp.sum(dc * uext_ref[su:su + ch, :], axis=0, keepdims=True)
            val = glu_ref[c0:c0 + ch, :cw]
            sg = _sig(glu_ref[c0:c0 + ch, cw:])
            dglu_ref[c0:c0 + ch, :cw] = (du * sg).astype(BF16)
            dglu_ref[c0:c0 + ch, cw:] = (du * val * (sg * (1.0 - sg))).astype(BF16)

    cur_map = lambda b, j: (b * spt + j, 0)
    prev_map = lambda b, j: (jnp.maximum((b * s + j * tt) // HALO - 1, 0), 0)
    next_map = lambda b, j: (jnp.minimum((b * s + (j + 1) * tt) // HALO, nblk - 1), 0)
    return pl.pallas_call(
        body, name=name, grid=(nb, spt),
        out_shape=[jax.ShapeDtypeStruct((t, 2 * cw), BF16), jax.ShapeDtypeStruct((HALO, cw), F32)],
        in_specs=[pl.BlockSpec((tt, cw), cur_map), pl.BlockSpec((HALO, cw), next_map),
                  pl.BlockSpec((tt, cw), cur_map), pl.BlockSpec((HALO, cw), prev_map),
                  pl.BlockSpec((tt, 2 * cw), cur_map), pl.BlockSpec((CONV_K, cw), lambda b, j: (0, 0))],
        out_specs=[pl.BlockSpec((tt, 2 * cw), cur_map), pl.BlockSpec((HALO, cw), lambda b, j: (0, 0))],
        scratch_shapes=[pltpu.VMEM((tt + HALO, cw), F32), pltpu.VMEM((tt + HALO, cw), F32)],
    )(dcv, dcv, u, u, glu_in, conv_w)


def _split(x):
    hi = x.astype(BF16)
    return hi, (x - hi.astype(F32)).astype(BF16)


def _tri(upper):
    j = lax.broadcasted_iota(jnp.int32, (KEY_BLK, KEY_BLK), 0)
    s_ = lax.broadcasted_iota(jnp.int32, (KEY_BLK, KEY_BLK), 1)
    return jnp.where(j > s_ if upper else j < s_, 1.0, 0.0).astype(BF16)


def _log_keep(z, causal):
    sp = jnp.maximum(z, 0.0) + jnp.log(1.0 + jnp.exp(-jnp.abs(z)))
    return -sp if causal is None else jnp.where(causal, -sp, 0.0)


def _attn_fwd(q, k, v, name):
    bh, s, dh = q.shape
    tq = _tile(s, (256, 128))
    scale = 1.0 / math.sqrt(dh)

    def body(q_ref, k_ref, v_ref, o_ref, car_ref, acc_ref, run_ref):
        qi = pl.program_id(1)
        qv = q_ref[0]
        lane = lax.broadcasted_iota(jnp.int32, (tq, KEY_BLK), 1)
        row = qi * tq + lax.broadcasted_iota(jnp.int32, (tq, KEY_BLK), 0)
        m_suf = _tri(True)
        nk = (qi * tq + tq) // KEY_BLK
        acc_ref[...] = jnp.zeros_like(acc_ref)
        run_ref[...] = jnp.zeros_like(run_ref)
        car_ref[...] = jnp.zeros_like(car_ref)

        def step(it, _):
            kb = nk - 1 - it
            off = pl.multiple_of(kb * KEY_BLK, KEY_BLK)
            kblk = k_ref[0, pl.ds(off, KEY_BLK), :]
            vblk = v_ref[0, pl.ds(off, KEY_BLK), :]
            z = lax.dot_general(qv, kblk, NT, preferred_element_type=F32) * scale
            causal = (off + lane) < row
            lk = _log_keep(z, causal)
            hi, lo = _split(lk)
            cs = (jnp.dot(hi, m_suf, preferred_element_type=F32)
                  + jnp.dot(lo, m_suf, preferred_element_type=F32))
            run = run_ref[...]
            a = jnp.where(causal, jnp.exp(z + lk + cs + run), 0.0)
            acc_ref[...] += jnp.dot(a.astype(BF16), vblk, preferred_element_type=F32)
            car_ref[0] = jnp.where(lane == kb, run, car_ref[0])
            run_ref[...] = run + jnp.sum(lk, axis=1, keepdims=True)
            return 0

        lax.fori_loop(0, nk, step, 0)
        o_ref[0] = acc_ref[...].astype(BF16)

    full = pl.BlockSpec((1, s, dh), lambda b, i: (b, 0, 0))
    return pl.pallas_call(
        body, name=name, grid=(bh, s // tq),
        out_shape=[jax.ShapeDtypeStruct((bh, s, dh), BF16), jax.ShapeDtypeStruct((bh, s, KEY_BLK), F32)],
        in_specs=[pl.BlockSpec((1, tq, dh), lambda b, i: (b, i, 0)), full, full],
        out_specs=[pl.BlockSpec((1, tq, dh), lambda b, i: (b, i, 0)),
                   pl.BlockSpec((1, tq, KEY_BLK), lambda b, i: (b, i, 0))],
        scratch_shapes=[pltpu.VMEM((tq, dh), F32), pltpu.VMEM((tq, KEY_BLK), F32)],
        compiler_params=pltpu.CompilerParams(dimension_semantics=("parallel", "arbitrary")),
    )(q, k, v)


def _attn_bwd(q, k, v, do, car, name):
    bh, s, dh = q.shape
    tq = _tile(s, (256, 128))
    scale = 1.0 / math.sqrt(dh)
    nkb, nqb = s // KEY_BLK, s // tq

    def body(q_ref, k_ref, v_ref, do_ref, car_ref, dq_ref, dk_ref, dv_ref, dq_acc, g_acc, dk_acc, dv_acc):
        lane = lax.broadcasted_iota(jnp.int32, (tq, KEY_BLK), 1)
        rowi = lax.broadcasted_iota(jnp.int32, (tq, KEY_BLK), 0)
        m_suf, m_pre = _tri(True), _tri(False)
        dq_acc[...] = jnp.zeros_like(dq_acc)
        g_acc[...] = jnp.zeros_like(g_acc)

        def kloop(kb, _):
            off = pl.multiple_of(kb * KEY_BLK, KEY_BLK)
            kblk = k_ref[0, pl.ds(off, KEY_BLK), :]
            vblk = v_ref[0, pl.ds(off, KEY_BLK), :]

            dk_acc[...] = jnp.zeros_like(dk_acc)
            dv_acc[...] = jnp.zeros_like(dv_acc)

            def qloop(qb, _):
                r0 = pl.multiple_of(qb * tq, tq)
                qblk = q_ref[0, pl.ds(r0, tq), :]
                doblk = do_ref[0, pl.ds(r0, tq), :]
                z = lax.dot_general(qblk, kblk, NT, preferred_element_type=F32) * scale
                causal = (off + lane) < (r0 + rowi)
                lk = _log_keep(z, causal)
                hi, lo = _split(lk)
                cs = (jnp.dot(hi, m_suf, preferred_element_type=F32)
                      + jnp.dot(lo, m_suf, preferred_element_type=F32))
                cpast = jnp.sum(jnp.where(lane == kb, car_ref[0, pl.ds(r0, tq), :], 0.0), axis=1, keepdims=True)
                lsig = z + lk
                a = jnp.where(causal, jnp.exp(lsig + cs + cpast), 0.0)
                da = lax.dot_general(doblk, vblk, NT, preferred_element_type=F32)
                g = a * da
                ghi, glo = _split(g)
                gpre = (jnp.dot(ghi, m_pre, preferred_element_type=F32)
                        + jnp.dot(glo, m_pre, preferred_element_type=F32)
                        + g_acc[pl.ds(r0, tq), :])
                dz = jnp.where(causal, g - jnp.exp(lsig) * (g + gpre), 0.0)
                ds_ = (dz * scale).astype(BF16)
                dv_acc[...] += lax.dot_general(a.astype(BF16), doblk, TN, preferred_element_type=F32)
                dk_acc[...] += lax.dot_general(ds_, qblk, TN, preferred_element_type=F32)
                dq_acc[pl.ds(r0, tq), :] += jnp.dot(ds_, kblk, preferred_element_type=F32)
                g_acc[pl.ds(r0, tq), :] += jnp.broadcast_to(jnp.sum(g, axis=1, keepdims=True), (tq, KEY_BLK))
                return 0

            lax.fori_loop(off // tq, nqb, qloop, 0)
            dk_ref[0, pl.ds(off, KEY_BLK), :] = dk_acc[...].astype(BF16)
            dv_ref[0, pl.ds(off, KEY_BLK), :] = dv_acc[...].astype(BF16)
            return 0

        lax.fori_loop(0, nkb, kloop, 0)
        dq_ref[0] = dq_acc[...].astype(BF16)

    full = pl.BlockSpec((1, s, dh), lambda b: (b, 0, 0))
    return pl.pallas_call(
        body, name=name, grid=(bh,),
        out_shape=[jax.ShapeDtypeStruct((bh, s, dh), BF16)] * 3,
        in_specs=[full, full, full, full, pl.BlockSpec((1, s, KEY_BLK), lambda b: (b, 0, 0))],
        out_specs=[full, full, full],
        scratch_shapes=[pltpu.VMEM((s, dh), F32), pltpu.VMEM((s, KEY_BLK), F32),
                        pltpu.VMEM((KEY_BLK, dh), F32), pltpu.VMEM((KEY_BLK, dh), F32)],
        compiler_params=pltpu.CompilerParams(dimension_semantics=("parallel",)),
    )(q, k, v, do, car)


LANES = 128


def _sb_fwd(qkv, nb, s, name):
    t, three_aw = qkv.shape
    aw = three_aw // 3
    npair = aw // LANES
    tq = _tile(s, (256, 128))
    ndiag = tq // KEY_BLK
    spt = s // tq
    scale = 1.0 / math.sqrt(HEAD_DIM)

    def body(q_ref, k_ref, v_ref, o_ref, car_ref, acc_ref, run_ref):
        qi = pl.program_id(2)
        lane = lax.broadcasted_iota(jnp.int32, (tq, LANES), 1)
        col = lax.broadcasted_iota(jnp.int32, (tq, KEY_BLK), 1)
        row = qi * tq + lax.broadcasted_iota(jnp.int32, (tq, KEY_BLK), 0)
        first = lane < HEAD_DIM
        q2 = q_ref[...]
        zero = jnp.zeros_like(q2)
        qh = (jnp.where(first, q2, zero), jnp.where(first, zero, q2))
        m_suf = _tri(True)
        nk = (qi * tq + tq) // KEY_BLK
        acc_ref[...] = jnp.zeros_like(acc_ref)
        run_ref[...] = jnp.zeros_like(run_ref)
        car_ref[...] = jnp.zeros_like(car_ref)

        def step(it, masked):
            kb = nk - 1 - it
            off = pl.multiple_of(kb * KEY_BLK, KEY_BLK)
            kblk = k_ref[pl.ds(off, KEY_BLK), :]
            vblk = v_ref[pl.ds(off, KEY_BLK), :]
            causal = ((off + col) < row) if masked else None
            hs = range(2)
            zs = [lax.dot_general(qh[h], kblk, NT, preferred_element_type=F32) * scale for h in hs]
            lks = [_log_keep(zs[h], causal) for h in hs]
            sp = [_split(lks[h]) for h in hs]
            css = [jnp.dot(sp[h][0], m_suf, preferred_element_type=F32)
                   + jnp.dot(sp[h][1], m_suf, preferred_element_type=F32) for h in hs]
            runs = [run_ref[h] for h in hs]
            aa = [jnp.exp(zs[h] + lks[h] + css[h] + jnp.concatenate([runs[h]] * (KEY_BLK // LANES), axis=1))
                  for h in hs]
            if masked:
                aa = [jnp.where(causal, aa[h], 0.0) for h in hs]
            for h in hs:
                acc_ref[h] += jnp.dot(aa[h].astype(BF16), vblk, preferred_element_type=F32)
            for h in hs:
                car_ref[h] = jnp.where(lane == kb, runs[h], car_ref[h])
                run_ref[h] = runs[h] + jnp.sum(lks[h], axis=1, keepdims=True)

        for it in range(ndiag):
            step(it, True)

        def rest(it, _):
            step(it, False)
            return 0

        lax.fori_loop(ndiag, nk, rest, 0)
        o_ref[...] = jnp.where(first, acc_ref[0], acc_ref[1]).astype(BF16)

    return pl.pallas_call(
        body, name=name, grid=(nb, npair, spt),
        out_shape=[jax.ShapeDtypeStruct((t, aw), BF16),
                   jax.ShapeDtypeStruct((nb * npair * 2, s, LANES), F32)],
        in_specs=[pl.BlockSpec((tq, LANES), lambda b, p, i: (b * spt + i, p)),
                  pl.BlockSpec((s, LANES), lambda b, p, i: (b, npair + p)),
                  pl.BlockSpec((s, LANES), lambda b, p, i: (b, 2 * npair + p))],
        out_specs=[pl.BlockSpec((tq, LANES), lambda b, p, i: (b * spt + i, p)),
                   pl.BlockSpec((2, tq, LANES), lambda b, p, i: (b * npair + p, i, 0))],
        scratch_shapes=[pltpu.VMEM((2, tq, LANES), F32), pltpu.VMEM((2, tq, LANES), F32)],
        compiler_params=pltpu.CompilerParams(dimension_semantics=("parallel", "parallel", "arbitrary")),
    )(qkv, qkv, qkv)


def _sb_bwd(qkv, do, car, nb, s, name):
    t, three_aw = qkv.shape
    aw = three_aw // 3
    npair = aw // LANES
    tq = _tile(s, (256, 128))
    scale = 1.0 / math.sqrt(HEAD_DIM)
    nkb, nqb = s // KEY_BLK, s // tq

    def body(q_ref, k_ref, v_ref, do_ref, car_ref, dq_ref, dk_ref, dv_ref, dq_acc, g_acc, dk_acc, dv_acc):
        lane = lax.broadcasted_iota(jnp.int32, (tq, LANES), 1)
        col = lax.broadcasted_iota(jnp.int32, (tq, KEY_BLK), 1)
        rowi = lax.broadcasted_iota(jnp.int32, (tq, KEY_BLK), 0)
        first_k = lax.broadcasted_iota(jnp.int32, (KEY_BLK, LANES), 1) < HEAD_DIM
        m_suf, m_pre = _tri(True), _tri(False)
        dq_acc[...] = jnp.zeros_like(dq_acc)
        g_acc[...] = jnp.zeros_like(g_acc)

        def kloop(kb, _):
            off = pl.multiple_of(kb * KEY_BLK, KEY_BLK)
            kblk = k_ref[pl.ds(off, KEY_BLK), :]
            vblk = v_ref[pl.ds(off, KEY_BLK), :]
            zero = jnp.zeros_like(kblk)
            kh = (jnp.where(first_k, kblk, zero), jnp.where(first_k, zero, kblk))
            vh = (jnp.where(first_k, vblk, zero), jnp.where(first_k, zero, vblk))
            dk_acc[...] = jnp.zeros_like(dk_acc)
            dv_acc[...] = jnp.zeros_like(dv_acc)

            def tile(qb, masked):
                r0 = pl.multiple_of(qb * tq, tq)
                q2 = q_ref[pl.ds(r0, tq), :]
                do2 = do_ref[pl.ds(r0, tq), :]
                causal = ((off + col) < (r0 + rowi)) if masked else None
                hs = range(2)
                wide = lambda x: jnp.concatenate([x] * (KEY_BLK // LANES), axis=1)
                zs = [lax.dot_general(q2, kh[h], NT, preferred_element_type=F32) * scale for h in hs]
                das = [lax.dot_general(do2, vh[h], NT, preferred_element_type=F32) for h in hs]
                lks = [_log_keep(zs[h], causal) for h in hs]
                sp = [_split(lks[h]) for h in hs]
                css = [jnp.dot(sp[h][0], m_suf, preferred_element_type=F32)
                       + jnp.dot(sp[h][1], m_suf, preferred_element_type=F32) for h in hs]
                cpast = [jnp.sum(jnp.where(lane == kb, car_ref[h, pl.ds(r0, tq), :], 0.0), axis=1, keepdims=True)
                         for h in hs]
                lsig = [zs[h] + lks[h] for h in hs]
                aa = [jnp.exp(lsig[h] + css[h] + cpast[h]) for h in hs]
                if masked:
                    aa = [jnp.where(causal, aa[h], 0.0) for h in hs]
                gs = [aa[h] * das[h] for h in hs]
                gsp = [_split(gs[h]) for h in hs]
                gpre = [jnp.dot(gsp[h][0], m_pre, preferred_element_type=F32)
                        + jnp.dot(gsp[h][1], m_pre, preferred_element_type=F32)
                        + wide(g_acc[h, pl.ds(r0, tq), :]) for h in hs]
                dzs = [gs[h] - jnp.exp(lsig[h]) * (gs[h] + gpre[h]) for h in hs]
                if masked:
                    dzs = [jnp.where(causal, dzs[h], 0.0) for h in hs]
                dss = [(dzs[h] * scale).astype(BF16) for h in hs]
                for h in hs:
                    dv_acc[h] += lax.dot_general(aa[h].astype(BF16), do2, TN, preferred_element_type=F32)
                for h in hs:
                    dk_acc[h] += lax.dot_general(dss[h], q2, TN, preferred_element_type=F32)
                for h in hs:
                    dq_acc[h, pl.ds(r0, tq), :] += jnp.dot(dss[h], kblk, preferred_element_type=F32)
                for h in hs:
                    g_acc[h, pl.ds(r0, tq), :] += jnp.broadcast_to(jnp.sum(gs[h], axis=1, keepdims=True),
                                                                   (tq, LANES))

            q0 = off // tq
            tile(q0, True)

            def rest(qb, _):
                tile(qb, False)
                return 0

            lax.fori_loop(q0 + 1, nqb, rest, 0)
            dk_ref[pl.ds(off, KEY_BLK), :] = jnp.where(first_k, dk_acc[0], dk_acc[1]).astype(BF16)
            dv_ref[pl.ds(off, KEY_BLK), :] = jnp.where(first_k, dv_acc[0], dv_acc[1]).astype(BF16)
            return 0

        lax.fori_loop(0, nkb, kloop, 0)
        first_q = lax.broadcasted_iota(jnp.int32, (s, LANES), 1) < HEAD_DIM
        dq_ref[...] = jnp.where(first_q, dq_acc[0], dq_acc[1]).astype(BF16)

    col = lambda j: pl.BlockSpec((s, LANES), lambda b, p: (b, j * npair + p))
    return pl.pallas_call(
        body, name=name, grid=(nb, npair),
        out_shape=[jax.ShapeDtypeStruct((t, aw), BF16)] * 3,
        in_specs=[col(0), col(1), col(2), col(0),
                  pl.BlockSpec((2, s, LANES), lambda b, p: (b * npair + p, 0, 0))],
        out_specs=[col(0), col(0), col(0)],
        scratch_shapes=[pltpu.VMEM((2, s, LANES), F32), pltpu.VMEM((2, s, LANES), F32),
                        pltpu.VMEM((2, KEY_BLK, LANES), F32), pltpu.VMEM((2, KEY_BLK, LANES), F32)],
        compiler_params=pltpu.CompilerParams(dimension_semantics=("parallel", "parallel")),
    )(qkv, qkv, qkv, do, car)


HALF_Q = 256


def _cumsum_mat(upper):
    m = _tri(upper)
    return jnp.concatenate([m, m], axis=0)


def _cum(x, mat):
    hi, lo = _split(x)
    return jnp.dot(jnp.concatenate([hi, lo], axis=1), mat, preferred_element_type=F32)


def _wide(x):
    return jnp.concatenate([x] * (KEY_BLK // LANES), axis=1)


def _sb4_fwd(qkv, nb, s, name, gather=()):
    t, three_aw = qkv.shape
    aw = three_aw // 3
    npair = aw // LANES
    tq = 2 * HALF_Q
    spt = s // tq
    scale = 1.0 / math.sqrt(HEAD_DIM)

    ng = len(gather)

    def body(*refs):
        q_ref, k_ref, v_ref = refs[:3]
        o_ref, car_ref = refs[3 + ng:5 + ng]
        acc_ref, run_ref = refs[5 + 2 * ng:7 + 2 * ng]
        if ng:
            copies = _gather_copies(refs[3:3 + ng], refs[5 + ng:5 + 2 * ng], *refs[7 + 2 * ng:])
            step_id = (pl.program_id(0) * npair + pl.program_id(1)) * spt + pl.program_id(2)
            pl.when(step_id == 0)(lambda: _start_copies(copies))
        qi = pl.program_id(2)
        lane = lax.broadcasted_iota(jnp.int32, (HALF_Q, LANES), 1)
        col = lax.broadcasted_iota(jnp.int32, (HALF_Q, KEY_BLK), 1)
        rowi = lax.broadcasted_iota(jnp.int32, (HALF_Q, KEY_BLK), 0)
        first = lane < HEAD_DIM
        mat = _cumsum_mat(True)
        qs = []
        for half in range(2):
            q2 = q_ref[half * HALF_Q:(half + 1) * HALF_Q, :] * scale
            zero = jnp.zeros_like(q2)
            qs.append((jnp.where(first, q2, zero), jnp.where(first, zero, q2)))
        acc_ref[...] = jnp.zeros_like(acc_ref)
        run_ref[...] = jnp.zeros_like(run_ref)
        car_ref[...] = jnp.zeros_like(car_ref)
        nk = (qi * tq + tq) // KEY_BLK

        def step(kb, halves):
            off = pl.multiple_of(kb * KEY_BLK, KEY_BLK)
            kblk = k_ref[pl.ds(off, KEY_BLK), :]
            vblk = v_ref[pl.ds(off, KEY_BLK), :]
            chains = [(half, masked, h) for half, masked in halves for h in range(2)]
            causal = {half: ((off + col) < (qi * tq + half * HALF_Q + rowi)) if masked else None
                      for half, masked in halves}
            zs = [lax.dot_general(qs[half][h], kblk, NT, preferred_element_type=F32) for half, _, h in chains]
            lks = [_log_keep(z, causal[c[0]]) for z, c in zip(zs, chains)]
            css = [_cum(lk, mat) for lk in lks]
            runs = [run_ref[2 * half + h] for half, _, h in chains]
            aa = [jnp.exp(z + lk + cs + _wide(run)) for z, lk, cs, run in zip(zs, lks, css, runs)]
            aa = [jnp.where(causal[c[0]], a, 0.0) if c[1] else a for a, c in zip(aa, chains)]
            for a, (half, _, h) in zip(aa, chains):
                acc_ref[2 * half + h] += jnp.dot(a.astype(BF16), vblk, preferred_element_type=F32)
            for lk, run, (half, _, h) in zip(lks, runs, chains):
                rows = pl.ds(half * HALF_Q, HALF_Q)
                car_ref[h, rows, :] = jnp.where(lane == kb, run, car_ref[h, rows, :])
                run_ref[2 * half + h] = run + jnp.sum(lk, axis=1, keepdims=True)

        step(nk - 1, ((1, True),))
        step(nk - 2, ((0, True), (1, False)))

        def rest(it, _):
            step(nk - 1 - it, ((0, False), (1, False)))
            return 0

        lax.fori_loop(2, nk, rest, 0)
        for half in range(2):
            o_ref[half * HALF_Q:(half + 1) * HALF_Q, :] = jnp.where(
                first, acc_ref[2 * half], acc_ref[2 * half + 1]).astype(BF16)
        if ng:
            pl.when(step_id == nb * npair * spt - 1)(lambda: _finish_copies(copies))

    hbm = pl.BlockSpec(memory_space=pl.ANY)
    outs = pl.pallas_call(
        body, name=name, grid=(nb, npair, spt),
        out_shape=[jax.ShapeDtypeStruct((t, aw), BF16),
                   jax.ShapeDtypeStruct((nb * npair * 2, s, LANES), F32)] + _gather_out_shapes(gather),
        in_specs=[pl.BlockSpec((tq, LANES), lambda b, p, i: (b * spt + i, p)),
                  pl.BlockSpec((s, LANES), lambda b, p, i: (b, npair + p)),
                  pl.BlockSpec((s, LANES), lambda b, p, i: (b, 2 * npair + p))] + [hbm] * ng,
        out_specs=[pl.BlockSpec((tq, LANES), lambda b, p, i: (b * spt + i, p)),
                   pl.BlockSpec((2, tq, LANES), lambda b, p, i: (b * npair + p, i, 0))] + [hbm] * ng,
        scratch_shapes=[pltpu.VMEM((4, HALF_Q, LANES), F32), pltpu.VMEM((4, HALF_Q, LANES), F32)]
        + (_copy_sems(ng) if ng else []),
        compiler_params=pltpu.CompilerParams(dimension_semantics=("arbitrary", "arbitrary", "arbitrary")),
    )(qkv, qkv, qkv, *gather)
    return outs[0], outs[1], list(outs[2:])


def _sb4_bwd(qkv, do, car, nb, s, name, exchange=()):
    t, three_aw = qkv.shape
    aw = three_aw // 3
    npair = aw // LANES
    tq = HALF_Q
    scale = 1.0 / math.sqrt(HEAD_DIM)
    nkp, nqb = s // (2 * KEY_BLK), s // tq
    nx = len(exchange)

    def body(*refs):
        q_ref, k_ref, v_ref, do_ref, car_ref = refs[:5]
        dq_ref, dk_ref, dv_ref = refs[5 + nx:8 + nx]
        dq_acc, g_acc, dk_acc, dv_acc = refs[8 + 2 * nx:12 + 2 * nx]
        if nx:
            copies = _exchange_copies(refs[5:5 + nx], refs[8 + nx:8 + 2 * nx], *refs[12 + 2 * nx:])
            step_id = pl.program_id(0) * npair + pl.program_id(1)
            pl.when(step_id == 0)(lambda: _start_copies(copies))
        lane = lax.broadcasted_iota(jnp.int32, (tq, LANES), 1)
        col = lax.broadcasted_iota(jnp.int32, (tq, KEY_BLK), 1)
        rowi = lax.broadcasted_iota(jnp.int32, (tq, KEY_BLK), 0)
        first_k = lax.broadcasted_iota(jnp.int32, (KEY_BLK, LANES), 1) < HEAD_DIM
        m_suf, m_pre = _cumsum_mat(True), _cumsum_mat(False)
        dq_acc[...] = jnp.zeros_like(dq_acc)
        g_acc[...] = jnp.zeros_like(g_acc)

        def kloop(kp, _):
            offs = [pl.multiple_of((2 * kp + blk) * KEY_BLK, KEY_BLK) for blk in range(2)]
            kblk = [k_ref[pl.ds(off, KEY_BLK), :] for off in offs]
            vblk = [v_ref[pl.ds(off, KEY_BLK), :] for off in offs]
            zero = jnp.zeros_like(kblk[0])
            kh = [(jnp.where(first_k, kb_, zero), jnp.where(first_k, zero, kb_)) for kb_ in kblk]
            vh = [(jnp.where(first_k, vb_, zero), jnp.where(first_k, zero, vb_)) for vb_ in vblk]
            k_scaled = jnp.concatenate([kblk[0] * scale, kblk[1] * scale], axis=0)
            dk_acc[...] = jnp.zeros_like(dk_acc)
            dv_acc[...] = jnp.zeros_like(dv_acc)

            def tile(qb, blocks):
                r0 = pl.multiple_of(qb * tq, tq)
                rows = pl.ds(r0, tq)
                q2 = q_ref[rows, :] * scale
                do2 = do_ref[rows, :]
                chains = [(blk, masked, h) for blk, masked in blocks for h in range(2)]
                causal = {blk: ((offs[blk] + col) < (r0 + rowi)) if masked else None for blk, masked in blocks}
                zs = [lax.dot_general(q2, kh[blk][h], NT, preferred_element_type=F32) for blk, _, h in chains]
                das = [lax.dot_general(do2, vh[blk][h], NT, preferred_element_type=F32) for blk, _, h in chains]
                lks = [_log_keep(z, causal[c[0]]) for z, c in zip(zs, chains)]
                css = [_cum(lk, m_suf) for lk in lks]
                cars = [car_ref[h, rows, :] for h in range(2)]
                cpast = [jnp.sum(jnp.where(lane == 2 * kp + blk, cars[h], 0.0), axis=1, keepdims=True)
                         for blk, _, h in chains]
                lsig = [z + lk for z, lk in zip(zs, lks)]
                aa = [jnp.exp(ls + cs + cp) for ls, cs, cp in zip(lsig, css, cpast)]
                aa = [jnp.where(causal[c[0]], a, 0.0) if c[1] else a for a, c in zip(aa, chains)]
                gs = [a * da for a, da in zip(aa, das)]
                gsum = [jnp.sum(g, axis=1, keepdims=True) for g in gs]
                gin = [_cum(g, m_pre) for g in gs]
                gold = [_wide(g_acc[h, rows, :]) for h in range(2)]
                gpre = []
                for i, (blk, _, h) in enumerate(chains):
                    left = gold[h] if blk == 0 else gold[h] + gsum[h]
                    gpre.append(gin[i] + left)
                dzs = [g - jnp.exp(ls) * (g + gp) for g, ls, gp in zip(gs, lsig, gpre)]
                dzs = [jnp.where(causal[c[0]], dz, 0.0) if c[1] else dz for dz, c in zip(dzs, chains)]
                dss = [dz.astype(BF16) for dz in dzs]
                for a, (blk, _, h) in zip(aa, chains):
                    dv_acc[2 * blk + h] += lax.dot_general(a.astype(BF16), do2, TN, preferred_element_type=F32)
                for ds_, (blk, _, h) in zip(dss, chains):
                    dk_acc[2 * blk + h] += lax.dot_general(ds_, q2, TN, preferred_element_type=F32)
                for h in range(2):
                    if len(blocks) == 2:
                        both = jnp.concatenate([dss[h], dss[2 + h]], axis=1)
                        dq_acc[h, rows, :] += jnp.dot(both, k_scaled, preferred_element_type=F32)
                        tot = gsum[h] + gsum[2 + h]
                    else:
                        dq_acc[h, rows, :] += jnp.dot(dss[h], k_scaled[:KEY_BLK], preferred_element_type=F32)
                        tot = gsum[h]
                    g_acc[h, rows, :] += jnp.broadcast_to(tot, (tq, LANES))

            tile(2 * kp, ((0, True),))
            tile(2 * kp + 1, ((0, False), (1, True)))

            def rest(qb, _):
                tile(qb, ((0, False), (1, False)))
                return 0

            lax.fori_loop(2 * kp + 2, nqb, rest, 0)
            for blk in range(2):
                dk_ref[pl.ds(offs[blk], KEY_BLK), :] = jnp.where(
                    first_k, dk_acc[2 * blk], dk_acc[2 * blk + 1]).astype(BF16)
                dv_ref[pl.ds(offs[blk], KEY_BLK), :] = jnp.where(
                    first_k, dv_acc[2 * blk], dv_acc[2 * blk + 1]).astype(BF16)
            return 0

        lax.fori_loop(0, nkp, kloop, 0)
        first_q = lax.broadcasted_iota(jnp.int32, (s, LANES), 1) < HEAD_DIM
        dq_ref[...] = jnp.where(first_q, dq_acc[0], dq_acc[1]).astype(BF16)
        if nx:
            pl.when(step_id == nb * npair - 1)(lambda: _finish_copies(copies))

    colspec = lambda j: pl.BlockSpec((s, LANES), lambda b, p: (b, j * npair + p))
    hbm = pl.BlockSpec(memory_space=pl.ANY)
    outs = pl.pallas_call(
        body, name=name, grid=(nb, npair),
        out_shape=[jax.ShapeDtypeStruct((t, aw), BF16)] * 3 + _exchange_out_shapes(exchange),
        in_specs=[colspec(0), colspec(1), colspec(2), colspec(0),
                  pl.BlockSpec((2, s, LANES), lambda b, p: (b * npair + p, 0, 0))] + [hbm] * nx,
        out_specs=[colspec(0), colspec(0), colspec(0)] + [hbm] * nx,
        scratch_shapes=[pltpu.VMEM((2, s, LANES), F32), pltpu.VMEM((2, s, LANES), F32),
                        pltpu.VMEM((4, KEY_BLK, LANES), F32), pltpu.VMEM((4, KEY_BLK, LANES), F32)]
        + (_copy_sems(nx) if nx else []),
        compiler_params=pltpu.CompilerParams(dimension_semantics=("arbitrary", "arbitrary")),
    )(qkv, qkv, qkv, do, car, *exchange)
    return outs[0], outs[1], outs[2], list(outs[3:])


def _ada_fwd(c_all, ada_w, ada_b_loc, name):
    nl, d, n6 = ada_w.shape
    nb = c_all.shape[0]

    def body(c_ref, w_ref, b_ref, o_ref):
        cv = c_ref[...]
        o_ref[0] = jnp.dot(cv * _sig(cv), w_ref[0], preferred_element_type=F32) + b_ref[0]

    return pl.pallas_call(
        body, name=name, grid=(nl,),
        out_shape=jax.ShapeDtypeStruct((nl, nb, n6), F32),
        in_specs=[pl.BlockSpec((nb, d), lambda l: (0, 0)), pl.BlockSpec((1, d, n6), lambda l: (l, 0, 0)),
                  pl.BlockSpec((1, 1, n6), lambda l: (l, 0, 0))],
        out_specs=pl.BlockSpec((1, nb, n6), lambda l: (l, 0, 0)),
    )(c_all, ada_w, ada_b_loc)


def _ada_bwd(c_all_t, dmod_loc, dmod_all, name):
    nl, nb, n6 = dmod_loc.shape
    d = c_all_t.shape[0]
    n_all = dmod_all.shape[2]

    def body(c_ref, dl_ref, da_ref, gw_ref, gb_ref):
        cv = c_ref[...]
        gw_ref[0] = jnp.dot(cv * _sig(cv), dl_ref[0], preferred_element_type=F32)
        gb_ref[0] = jnp.sum(da_ref[0], axis=0, keepdims=True)

    return pl.pallas_call(
        body, name=name, grid=(nl,),
        out_shape=[jax.ShapeDtypeStruct((nl, d, n6), F32), jax.ShapeDtypeStruct((nl, 1, n_all), F32)],
        in_specs=[pl.BlockSpec((d, nb), lambda l: (0, 0)), pl.BlockSpec((1, nb, n6), lambda l: (l, 0, 0)),
                  pl.BlockSpec((1, nb, n_all), lambda l: (l, 0, 0))],
        out_specs=[pl.BlockSpec((1, d, n6), lambda l: (l, 0, 0)), pl.BlockSpec((1, 1, n_all), lambda l: (l, 0, 0))],
    )(c_all_t, dmod_loc, dmod_all)


def _adamw(w, g, m, v, name):
    shape = w.shape
    cols = shape[-1]
    rows = w.size // cols
    tr = _tile(rows, (512, 256, 128, 64, 32, 16, 8))
    flat = lambda a: a.reshape(rows, cols)

    def body(w_ref, g_ref, m_ref, v_ref, d_ref, mo_ref, vo_ref):
        gv = g_ref[...]
        mn = ADAM_B1 * m_ref[...] + (1.0 - ADAM_B1) * gv
        vn = ADAM_B2 * v_ref[...] + (1.0 - ADAM_B2) * (gv * gv)
        m_hat = mn / (1.0 - ADAM_B1 ** ADAM_STEP)
        v_hat = vn / (1.0 - ADAM_B2 ** ADAM_STEP)
        d_ref[...] = -ADAM_LR * (m_hat / (jnp.sqrt(v_hat) + ADAM_EPS) + ADAM_WD * w_ref[...])
        mo_ref[...] = mn
        vo_ref[...] = vn

    spec = pl.BlockSpec((tr, cols), lambda i: (i, 0))
    outs = pl.pallas_call(
        body, name=name, grid=(rows // tr,),
        out_shape=[jax.ShapeDtypeStruct((rows, cols), F32)] * 3,
        in_specs=[spec] * 4, out_specs=[spec] * 3,
    )(flat(w), flat(g), flat(m), flat(v))
    return tuple(o.reshape(shape) for o in outs)


def _pack_rows(parts, d):
    flat = jnp.concatenate([p.reshape(-1) for p in parts])
    rows = -(-flat.size // d)
    rows = -(-rows // 8) * 8
    return jnp.pad(flat, (0, rows * d - flat.size)).reshape(rows, d)


def _to_heads(a, nb, s):
    h = a.shape[1] // HEAD_DIM
    return a.reshape(nb, s, h, HEAD_DIM).transpose(0, 2, 1, 3).reshape(nb * h, s, HEAD_DIM)


def _from_heads(a, nb, s):
    h = a.shape[0] // nb
    return a.reshape(nb, h, s, HEAD_DIM).transpose(0, 2, 1, 3).reshape(nb * s, h * HEAD_DIM)


def kernel(x, c, ada_w, ada_b, pre_mix_g, post_mix_g, pre_ffn_g, post_ffn_g, w_in, conv_w, conv_b, conv_ln_g, conv_ln_b, w_conv_out, w_att_out, w_o, w_ffn_in, w_ffn_out, loss_target, m_ada_w, m_ada_b, m_pre_mix_g, m_post_mix_g, m_pre_ffn_g, m_post_ffn_g, m_w_in, m_conv_w, m_conv_b, m_conv_ln_g, m_conv_ln_b, m_w_conv_out, m_w_att_out, m_w_o, m_w_ffn_in, m_w_ffn_out, v_ada_w, v_ada_b, v_pre_mix_g, v_post_mix_g, v_pre_ffn_g, v_post_ffn_g, v_w_in, v_conv_w, v_conv_b, v_conv_ln_g, v_conv_ln_b, v_w_conv_out, v_w_att_out, v_w_o, v_w_ffn_in, v_w_ffn_out):
    nb, s, d = x.shape
    nl = ada_w.shape[0]
    cw = conv_b.shape[1]
    aw = w_att_out.shape[1]
    cl = conv_w.shape[2]
    n6 = ada_w.shape[2]
    t = nb * s
    me = _my_index()

    conv_flat = conv_w.reshape(-1)
    p1 = _pack_rows([c, conv_flat], d)
    r1 = p1.shape[0]
    (p1_all,) = _all_gather([p1], "gather_c_convw", True)
    p1_all = p1_all.reshape(N_DEV, r1 * d)
    c_all = p1_all[:, :nb * d].reshape(N_DEV * nb, d)
    conv_w_all = p1_all[:, nb * d:nb * d + conv_flat.size].reshape(N_DEV, nl, CONV_K, cl)
    conv_w_all = conv_w_all.transpose(1, 2, 0, 3).reshape(nl, CONV_K, cw)

    ada_b_loc = lax.dynamic_slice_in_dim(ada_b, me * n6, n6, axis=1).reshape(nl, 1, n6)
    mod_cols = _ada_fwd(c_all, ada_w, ada_b_loc, "ada_fwd")
    (mod_g,) = _all_gather([mod_cols.reshape(-1, d)], "gather_mod", True)
    mod_g = mod_g.reshape(N_DEV, nl, N_DEV * nb, n6)
    mod_mine = lax.dynamic_slice_in_dim(mod_g, me * nb, nb, axis=2)
    mod = mod_mine.transpose(1, 2, 0, 3).reshape(nl, nb, 6 * d)
    mods = [[mod[l, :, i * d:(i + 1) * d].reshape(nb, 1, d) for i in range(6)] for l in range(nl)]

    tr_ = lambda w: jnp.swapaxes(w, 0, 1).astype(BF16)
    shards = [[tr_(w_in[l]), tr_(w_conv_out[l]), tr_(w_att_out[l]), w_o[l].astype(BF16),
               tr_(w_ffn_in[l]), w_ffn_out[l].astype(BF16)] for l in range(nl)]
    weights = [list(_all_gather(shards[0][:3], "gather_weights", False))]

    vec = lambda a, l: a[l].reshape(1, -1)
    x2 = x.reshape(t, d)
    tgt = loss_target.reshape(t, d)

    saved = []
    xin = x2
    h = _norm_mod(xin, vec(pre_mix_g, 0), mods[0][1], mods[0][0], s, "norm_mod_0")
    for l in range(nl):
        win_t, wconv_t, watt_t = weights[l][:3]
        sh1, sc1, ga1, sh2, sc2, ga2 = mods[l]
        glu_in = _mm(h, win_t, NT, F32, f"proj_glu_{l}", 0, 2 * cw)
        qkv = _mm(h, win_t, NT, BF16, f"proj_qkv_{l}", 2 * cw, 3 * aw)
        gates = _mm(h, win_t, NT, BF16, f"proj_gates_{l}", 2 * cw + 3 * aw, 2 * d)
        u, cv, ua = _conv_fwd(glu_in, conv_w_all[l], vec(conv_b, l), vec(conv_ln_g, l), vec(conv_ln_b, l),
                              nb, s, f"conv_fwd_{l}")
        late = shards[l][len(weights[l]):]
        o, car, got = _sb4_fwd(qkv, nb, s, f"attn_fwd_{l}", late + (shards[l + 1] if l + 1 < nl else []))
        weights[l] = weights[l] + got[:len(late)]
        if l + 1 < nl:
            weights.append(got[len(late):])
        wo, wffn_in_t, wffn_out = weights[l][3:]
        y_conv = _mm(ua, wconv_t, NT, BF16, f"conv_out_{l}")
        y_att = _mm(o, watt_t, NT, BF16, f"att_out_{l}")
        merged = _merge(y_conv, y_att, gates, s, f"merge_{l}")
        y = _mm(merged, wo, NN, F32, f"w_o_{l}")
        x1, h2 = _post_res_norm(xin, y, vec(post_mix_g, l), ga1, vec(pre_ffn_g, l), sc2, sh2, s, f"post_mix_{l}")
        f = _mm(h2, wffn_in_t, NT, BF16, f"ffn_in_{l}")
        a = _swiglu(f, s, f"swiglu_{l}")
        y2 = _mm(a, wffn_out, NN, F32, f"ffn_out_{l}")
        saved.append(dict(xin=xin, h=h, glu_in=glu_in, gates=gates, u=u, cv=cv, ua=ua, qkv=qkv,
                          car=car, o=o, y_conv=y_conv, y_att=y_att, merged=merged, y=y, x1=x1, h2=h2, f=f,
                          a=a, y2=y2))
        if l + 1 < nl:
            nsh1, nsc1 = mods[l + 1][0], mods[l + 1][1]
            xin, h = _post_res_norm(x1, y2, vec(post_ffn_g, l), ga2, vec(pre_mix_g, l + 1), nsc1, nsh1, s,
                                    f"post_ffn_{l}")
        else:
            dx, loss_acc = _post_res_loss(x1, y2, vec(post_ffn_g, l), ga2, tgt, s, "loss")

    slots = [None] * (6 * nl)
    small = [None] * nl
    for l in reversed(range(nl)):
        win_t, wconv_t, watt_t, wo, wffn_in_t, wffn_out = weights[l]
        sh1, sc1, ga1, sh2, sc2, ga2 = mods[l]
        sv = saved[l]
        dy2, dg_post_ffn, dga2 = _post_bwd(dx, sv["y2"], vec(post_ffn_g, l), ga2, s, f"post_ffn_bwd_{l}")
        d_wffn_out = _mm(sv["a"], dy2, TN, BF16, f"d_w_ffn_out_{l}")
        da = _mm(dy2, wffn_out, NT, BF16, f"d_a_{l}")
        df = _swiglu_bwd(da, sv["f"], s, f"swiglu_bwd_{l}")
        d_wffn_in_t = _mm(df, sv["h2"], TN, BF16, f"d_w_ffn_in_{l}")
        dh2 = _mm(df, wffn_in_t, NN, F32, f"d_h2_{l}")
        dx1, dsh2, dsc2, dg_pre_ffn = _pre_bwd(dh2, sv["x1"], vec(pre_ffn_g, l), sc2, dx, s, f"pre_ffn_bwd_{l}")
        dy, dg_post_mix, dga1 = _post_bwd(dx1, sv["y"], vec(post_mix_g, l), ga1, s, f"post_mix_bwd_{l}")
        d_wo = _mm(sv["merged"], dy, TN, BF16, f"d_w_o_{l}")
        dmerged = _mm(dy, wo, NT, BF16, f"d_merged_{l}")
        dyc, dya, dgates = _merge_bwd(dmerged, sv["y_conv"], sv["y_att"], sv["gates"], s, f"merge_bwd_{l}")
        d_wconv_t = _mm(dyc, sv["ua"], TN, BF16, f"d_w_conv_out_{l}")
        dua = _mm(dyc, wconv_t, NN, F32, f"d_ua_{l}")
        d_watt_t = _mm(dya, sv["o"], TN, BF16, f"d_w_att_out_{l}")
        do = _mm(dya, watt_t, NN, BF16, f"d_o_{l}")
        ready = [(5, d_wffn_out), (4, d_wffn_in_t), (3, d_wo), (1, d_wconv_t), (2, d_watt_t)]
        dq, dk, dv, got = _sb4_bwd(sv["qkv"], do, sv["car"], nb, s, f"attn_bwd_{l}", [g for _, g in ready])
        for (gi, _), r in zip(ready, got):
            slots[6 * l + gi] = r
        dcv, dln_g, dln_b, dconv_b = _convln_bwd(dua, sv["cv"], vec(conv_ln_g, l), vec(conv_ln_b, l), s,
                                                 f"convln_bwd_{l}")
        dglu, dconv_w = _conv_bwd(dcv, sv["u"], sv["glu_in"], conv_w_all[l], nb, s, f"conv_bwd_{l}")
        dproj = jnp.concatenate([dglu, dq, dk, dv, dgates], axis=1)
        d_win_t = _mm(dproj, sv["h"], TN, BF16, f"d_w_in_{l}")
        dh, (slots[6 * l],) = _mm(dproj, win_t, NN, F32, f"d_h_{l}", exchange=[d_win_t])
        dx, dsh1, dsc1, dg_pre_mix = _pre_bwd(dh, sv["xin"], vec(pre_mix_g, l), sc1, dx1, s, f"pre_mix_bwd_{l}")
        dmod = jnp.concatenate([dsh1, dsc1, dga1, dsh2, dsc2, dga2], axis=2)
        small[l] = [dg_pre_mix, dg_post_mix, dg_pre_ffn, dg_post_ffn, dconv_b, dln_g, dln_b, dconv_w, dmod]
    grad_x = dx.reshape(nb, s, d)

    sizes = [a.size for a in small[0]]
    p3 = _pack_rows([a for l in range(nl) for a in small[l]] + [loss_acc[0, :1]], d)
    r3 = p3.shape[0]
    (p3_all,) = _all_gather([p3], "gather_small_grads", True)
    p3_all = p3_all.reshape(N_DEV, r3, d)
    p3_sum = _sum_slots(p3_all, "sum_small_grads").reshape(-1)
    per_layer = sum(sizes)
    loss = p3_sum[nl * per_layer]
    small_sum, dmod_all = [], []
    for l in range(nl):
        off, parts = l * per_layer, []
        for sz in sizes[:-1]:
            parts.append(p3_sum[off:off + sz])
            off += sz
        small_sum.append(parts)
        dm = p3_all.reshape(N_DEV, r3 * d)[:, off:off + sizes[-1]]
        dmod_all.append(dm.reshape(N_DEV * nb, 6 * d))
    dmod_all = jnp.stack(dmod_all)
    dmod_loc = lax.dynamic_slice_in_dim(dmod_all, me * n6, n6, axis=2)
    g_ada_w, g_ada_b = _ada_bwd(c_all.T, dmod_loc, dmod_all, "ada_bwd")
    g_ada_b = g_ada_b.reshape(nl, 6 * d)

    stack_small = lambda i, shape: jnp.stack([small_sum[l][i].reshape(shape) for l in range(nl)])
    g_pre_mix, g_post_mix, g_pre_ffn, g_post_ffn = (stack_small(i, (d,)) for i in range(4))
    g_conv_b, g_ln_g, g_ln_b = (stack_small(i, (cw,)) for i in range(4, 7))
    g_conv_w_all = stack_small(7, (HALO, cw))[:, :CONV_K]
    g_conv_w = lax.dynamic_slice_in_dim(g_conv_w_all, me * cl, cl, axis=2)

    sums = [_sum_slots(r, f"sum_grads_{i}") for i, r in enumerate(slots)]
    un_t = lambda i: jnp.stack([jnp.swapaxes(sums[6 * l + i], 0, 1) for l in range(nl)])
    keep = lambda i: jnp.stack([sums[6 * l + i] for l in range(nl)])
    g_w_in, g_w_conv_out, g_w_att_out, g_w_o, g_w_ffn_in, g_w_ffn_out = (
        un_t(0), un_t(1), un_t(2), keep(3), un_t(4), keep(5))

    grads = [g_ada_w, g_ada_b, g_pre_mix, g_post_mix, g_pre_ffn, g_post_ffn, g_w_in, g_conv_w, g_conv_b,
             g_ln_g, g_ln_b, g_w_conv_out, g_w_att_out, g_w_o, g_w_ffn_in, g_w_ffn_out]
    ws = [ada_w, ada_b, pre_mix_g, post_mix_g, pre_ffn_g, post_ffn_g, w_in, conv_w, conv_b, conv_ln_g,
          conv_ln_b, w_conv_out, w_att_out, w_o, w_ffn_in, w_ffn_out]
    ms = [m_ada_w, m_ada_b, m_pre_mix_g, m_post_mix_g, m_pre_ffn_g, m_post_ffn_g, m_w_in, m_conv_w, m_conv_b,
          m_conv_ln_g, m_conv_ln_b, m_w_conv_out, m_w_att_out, m_w_o, m_w_ffn_in, m_w_ffn_out]
    vs = [v_ada_w, v_ada_b, v_pre_mix_g, v_post_mix_g, v_pre_ffn_g, v_post_ffn_g, v_w_in, v_conv_w, v_conv_b,
          v_conv_ln_g, v_conv_ln_b, v_w_conv_out, v_w_att_out, v_w_o, v_w_ffn_in, v_w_ffn_out]
    deltas, new_m, new_v = [], [], []
    for i, (w, g, m, v) in enumerate(zip(ws, grads, ms, vs)):
        dlt, mn, vn = _adamw(w, g, m, v, f"adamw_{i}")
        deltas.append(dlt)
        new_m.append(mn)
        new_v.append(vn)
    return (loss, grad_x, *grads, *deltas, *new_m, *new_v)
```

```python
import functools
import math

import jax
import jax.numpy as jnp
from jax import lax
from jax.experimental import pallas as pl
from jax.experimental.pallas import tpu as pltpu

F32 = jnp.float32
BF16 = jnp.bfloat16
MESH = pl.DeviceIdType.MESH
N_DEV = 8
EPS = 1e-6
HEAD_DIM = 64
CONV_K = 31
HALO = 32
KEY_BLK = 256
ADAM_LR, ADAM_B1, ADAM_B2, ADAM_EPS, ADAM_WD, ADAM_STEP = 0.001, 0.9, 0.999, 1e-08, 0.01, 10

NT = (((1,), (1,)), ((), ()))
NN = (((1,), (0,)), ((), ()))
TN = (((0,), (0,)), ((), ()))


def _tile(n, cands):
    for t in cands:
        if n % t == 0:
            return t
    return n


def _lane_tile(n, cap):
    best = n
    for t in range(128, min(n, cap) + 1, 128):
        if n % t == 0:
            best = t
    return best


def _sig(x):
    return 1.0 / (1.0 + jnp.exp(-x))


def _pos():
    return lax.axis_index("x"), lax.axis_index("y"), lax.axis_index("c")


def _my_index():
    x, y, c = _pos()
    return 4 * x + 2 * y + c


def _all_gather(arrs, name, in_vmem):
    n = len(arrs)

    def body(*refs):
        ins, outs = refs[:n], refs[n:2 * n]
        send_sems, recv_sems, local_sems = refs[2 * n:]
        x, y, c = _pos()
        me, sib = (x, y, c), (x, y, 1 - c)
        chips = [(1 - x, y), (x, 1 - y), (1 - x, 1 - y)]

        def rows(i, p):
            r = ins[i].shape[0]
            return outs[i].at[pl.ds((4 * p[0] + 2 * p[1] + p[2]) * r, r), :]

        def copy(i, k, block, to, src=None):
            return pltpu.make_async_remote_copy(
                src_ref=rows(i, block) if src is None else src, dst_ref=rows(i, block),
                send_sem=send_sems.at[i, k], recv_sem=recv_sems.at[i, k],
                device_id=to, device_id_type=MESH)

        mine = [pltpu.make_async_copy(ins[i], rows(i, me), local_sems.at[i]) for i in range(n)]
        for cp in mine:
            cp.start()
        first = []
        for j, chip in enumerate(chips):
            for i in range(n):
                first.append(copy(i, 1 + j, me, (*chip, c), src=ins[i]))
        for i in range(n):
            first.append(copy(i, 0, me, sib, src=ins[i]))
        for cp in first:
            cp.start()
        passed = []
        for j, chip in enumerate(chips):
            for i in range(n):
                copy(i, 1 + j, (*chip, c), me).wait_recv()
                p = copy(i, 4 + j, (*chip, c), sib)
                p.start()
                passed.append(p)
        for i in range(n):
            copy(i, 0, sib, me).wait_recv()
            for j, chip in enumerate(chips):
                copy(i, 4 + j, (*chip, 1 - c), me).wait_recv()
        for cp in first + passed:
            cp.wait_send()
        for cp in mine:
            cp.wait()

    space = pltpu.VMEM if in_vmem else pl.ANY
    spec = pl.BlockSpec(memory_space=space)
    return pl.pallas_call(
        body, name=name,
        out_shape=[jax.ShapeDtypeStruct((N_DEV * a.shape[0], a.shape[1]), a.dtype) for a in arrs],
        in_specs=[spec] * n, out_specs=[spec] * n,
        scratch_shapes=[pltpu.SemaphoreType.DMA((n, 7)), pltpu.SemaphoreType.DMA((n, 7)),
                        pltpu.SemaphoreType.DMA((n,))],
    )(*arrs)


def _exchange_blocks(arrs, name):
    n = len(arrs)

    def body(*refs):
        copies = _exchange_copies(refs[:n], refs[n:2 * n], *refs[2 * n:])
        _start_copies(copies)
        _finish_copies(copies)

    spec = pl.BlockSpec(memory_space=pl.ANY)
    return pl.pallas_call(
        body, name=name, out_shape=_exchange_out_shapes(arrs),
        in_specs=[spec] * n, out_specs=[spec] * n, scratch_shapes=_copy_sems(n),
    )(*arrs)


PEER_ORDER = (4, 2, 6, 5, 3, 7, 1)


def _peer(k):
    x, y, c = _pos()
    return (1 - x if k & 4 else x, 1 - y if k & 2 else y, 1 - c if k & 1 else c)


def _mesh_index(p):
    return 4 * p[0] + 2 * p[1] + p[2]


def _copy_sems(n):
    return [pltpu.SemaphoreType.DMA((n, 7)), pltpu.SemaphoreType.DMA((n, 7)), pltpu.SemaphoreType.DMA((n,))]


def _exchange_out_shapes(arrs):
    return [jax.ShapeDtypeStruct((N_DEV, a.shape[0] // N_DEV, a.shape[1]), a.dtype) for a in arrs]


def _gather_out_shapes(arrs):
    return [jax.ShapeDtypeStruct((N_DEV * a.shape[0], a.shape[1]), a.dtype) for a in arrs]


def _exchange_copies(ins, outs, send_sems, recv_sems, local_sems):
    n = len(ins)

    def block(i, p):
        r = outs[i].shape[1]
        return ins[i].at[pl.ds(_mesh_index(p) * r, r), :]

    mine = [pltpu.make_async_copy(block(i, _peer(0)), outs[i].at[0], local_sems.at[i]) for i in range(n)]
    sends = [pltpu.make_async_remote_copy(
        src_ref=block(i, _peer(k)), dst_ref=outs[i].at[k], send_sem=send_sems.at[i, k - 1],
        recv_sem=recv_sems.at[i, k - 1], device_id=_peer(k), device_id_type=MESH)
        for k in PEER_ORDER for i in range(n)]
    return mine, sends, sends


def _gather_copies(ins, outs, send_sems, recv_sems, local_sems):
    n = len(ins)

    def rows(i, p):
        r = ins[i].shape[0]
        return outs[i].at[pl.ds(_mesh_index(p) * r, r), :]

    def copy(i, k, landing):
        return pltpu.make_async_remote_copy(
            src_ref=ins[i], dst_ref=rows(i, landing), send_sem=send_sems.at[i, k - 1],
            recv_sem=recv_sems.at[i, k - 1], device_id=_peer(k), device_id_type=MESH)

    mine = [pltpu.make_async_copy(ins[i], rows(i, _peer(0)), local_sems.at[i]) for i in range(n)]
    sends = [copy(i, k, _peer(0)) for k in PEER_ORDER for i in range(n)]
    recvs = [copy(i, k, _peer(k)) for k in PEER_ORDER for i in range(n)]
    return mine, sends, recvs


def _start_copies(copies):
    mine, sends, _ = copies
    for cp in mine + sends:
        cp.start()


def _finish_copies(copies):
    mine, sends, recvs = copies
    for cp in recvs:
        cp.wait_recv()
    for cp in sends:
        cp.wait_send()
    for cp in mine:
        cp.wait()


def _sum_slots(r3, name):
    _, r, c = r3.shape
    tr = _tile(r, (128, 96, 64, 32, 16))

    def body(r_ref, o_ref):
        acc = r_ref[0].astype(F32)
        for k in range(1, N_DEV):
            acc = acc + r_ref[k].astype(F32)
        o_ref[...] = acc

    return pl.pallas_call(
        body, name=name, grid=(r // tr,),
        out_shape=jax.ShapeDtypeStruct((r, c), F32),
        in_specs=[pl.BlockSpec((N_DEV, tr, c), lambda i: (0, i, 0))],
        out_specs=pl.BlockSpec((tr, c), lambda i: (i, 0)),
    )(r3)


def _mm(a, b, dims, out_dtype, name, n0=0, n=None, exchange=()):
    if dims is NT:
        m, k = a.shape
        n = b.shape[0] if n is None else n
    elif dims is NN:
        m, k = a.shape
        n = b.shape[1]
    else:
        k, m = a.shape
        n = b.shape[1]
    if dims is TN:
        tm = _lane_tile(m, 1536)
        tn = _lane_tile(n, 1024)
        tk = _tile(k, (1024, 512, 256, 128))
    else:
        tm = _tile(m, (1024, 512, 256, 128))
        tn = _lane_tile(math.gcd(n, n0) if n0 else n, 1536)
        tk = _lane_tile(k, 2816)
    nk = k // tk
    noff = n0 // tn

    nx = len(exchange)
    grid = (m // tm, n // tn, nk)

    def body(*refs):
        a_ref, b_ref = refs[:2]
        o_ref = refs[2 + nx]
        scratch = refs[3 + 2 * nx:]
        if nx:
            copies = _exchange_copies(refs[2:2 + nx], refs[3 + nx:3 + 2 * nx], *scratch[-3:])
            step_id = (pl.program_id(0) * grid[1] + pl.program_id(1)) * grid[2] + pl.program_id(2)
            pl.when(step_id == 0)(lambda: _start_copies(copies))
        part =lax.dot_general(a_ref[...].astype(BF16), b_ref[...].astype(BF16), dims,
                               preferred_element_type=F32)
        if nk == 1:
            o_ref[...] = part.astype(o_ref.dtype)
        else:
            acc_ref = scratch[0]
            kk = pl.program_id(2)

            @pl.when(kk == 0)
            def _():
                acc_ref[...] = part

            @pl.when(kk > 0)
            def _():
                acc_ref[...] += part

            @pl.when(kk == nk - 1)
            def _():
                o_ref[...] = acc_ref[...].astype(o_ref.dtype)

        if nx:
            pl.when(step_id == grid[0] * grid[1] * grid[2] - 1)(lambda: _finish_copies(copies))

    if dims is NT:
        in_specs = [pl.BlockSpec((tm, tk), lambda i, j, kk: (i, kk)),
                    pl.BlockSpec((tn, tk), lambda i, j, kk: (j + noff, kk))]
    elif dims is NN:
        in_specs = [pl.BlockSpec((tm, tk), lambda i, j, kk: (i, kk)),
                    pl.BlockSpec((tk, tn), lambda i, j, kk: (kk, j))]
    else:
        in_specs = [pl.BlockSpec((tk, tm), lambda i, j, kk: (kk, i)),
                    pl.BlockSpec((tk, tn), lambda i, j, kk: (kk, j))]
    hbm = pl.BlockSpec(memory_space=pl.ANY)
    outs = pl.pallas_call(
        body, name=name, grid=grid,
        out_shape=[jax.ShapeDtypeStruct((m, n), out_dtype)] + _exchange_out_shapes(exchange),
        in_specs=in_specs + [hbm] * nx,
        out_specs=[pl.BlockSpec((tm, tn), lambda i, j, kk: (i, j))] + [hbm] * nx,
        scratch_shapes=([] if nk == 1 else [pltpu.VMEM((tm, tn), F32)]) + (_copy_sems(nx) if nx else []),
        compiler_params=pltpu.CompilerParams(
            dimension_semantics=("arbitrary",) * 3 if nx else ("parallel", "parallel", "arbitrary")),
    )(a, b, *exchange)
    return (outs[0], list(outs[1:])) if nx else outs[0]


def _tok_tile(s):
    return _tile(s, (512, 256, 128))


def _row_spec(tm, d):
    return pl.BlockSpec((tm, d), lambda i: (i, 0))


def _vec_spec(d):
    return pl.BlockSpec((1, d), lambda i: (0, 0))


def _seq_spec(d, tiles_per_seq):
    return pl.BlockSpec((1, 1, d), lambda i: (i // tiles_per_seq, 0, 0))


def _rms(x):
    return lax.rsqrt(jnp.mean(x * x, axis=-1, keepdims=True) + EPS)


def _norm_mod(x, g, sc, sh, s, name):
    t, d = x.shape
    tm = _tok_tile(s)

    def body(x_ref, g_ref, sc_ref, sh_ref, h_ref):
        xv = x_ref[...]
        h = (xv * _rms(xv)) * g_ref[...]
        h_ref[...] = (h * (1.0 + sc_ref[0]) + sh_ref[0]).astype(BF16)

    return pl.pallas_call(
        body, name=name, grid=(t // tm,),
        out_shape=jax.ShapeDtypeStruct((t, d), BF16),
        in_specs=[_row_spec(tm, d), _vec_spec(d), _seq_spec(d, s // tm), _seq_spec(d, s // tm)],
        out_specs=_row_spec(tm, d),
    )(x, g, sc, sh)


def _post_res_norm(x, y, g_post, ga, g_pre, sc, sh, s, name):
    t, d = x.shape
    tm = _tok_tile(s)

    def body(x_ref, y_ref, gp_ref, ga_ref, g_ref, sc_ref, sh_ref, xn_ref, h_ref):
        yv = y_ref[...]
        xn = x_ref[...] + ga_ref[0] * ((yv * _rms(yv)) * gp_ref[...])
        xn_ref[...] = xn
        h = (xn * _rms(xn)) * g_ref[...]
        h_ref[...] = (h * (1.0 + sc_ref[0]) + sh_ref[0]).astype(BF16)

    tps = s // tm
    return pl.pallas_call(
        body, name=name, grid=(t // tm,),
        out_shape=[jax.ShapeDtypeStruct((t, d), F32), jax.ShapeDtypeStruct((t, d), BF16)],
        in_specs=[_row_spec(tm, d), _row_spec(tm, d), _vec_spec(d), _seq_spec(d, tps),
                  _vec_spec(d), _seq_spec(d, tps), _seq_spec(d, tps)],
        out_specs=[_row_spec(tm, d), _row_spec(tm, d)],
    )(x, y, g_post, ga, g_pre, sc, sh)


def _post_res_loss(x, y, g_post, ga, target, s, name):
    t, d = x.shape
    tm = _tok_tile(s)

    def body(x_ref, y_ref, gp_ref, ga_ref, t_ref, dx_ref, loss_ref):
        yv = y_ref[...]
        err = x_ref[...] + ga_ref[0] * ((yv * _rms(yv)) * gp_ref[...]) - t_ref[...]
        dx_ref[...] = err * (1.0 / d)
        part = 0.5 * jnp.sum(jnp.mean(err * err, axis=-1, keepdims=True), axis=0, keepdims=True)

        @pl.when(pl.program_id(0) == 0)
        def _():
            loss_ref[...] = jnp.zeros_like(loss_ref)

        loss_ref[...] += part

    return pl.pallas_call(
        body, name=name, grid=(t // tm,),
        out_shape=[jax.ShapeDtypeStruct((t, d), F32), jax.ShapeDtypeStruct((8, 128), F32)],
        in_specs=[_row_spec(tm, d), _row_spec(tm, d), _vec_spec(d), _seq_spec(d, s // tm), _row_spec(tm, d)],
        out_specs=[_row_spec(tm, d), pl.BlockSpec((8, 128), lambda i: (0, 0))],
    )(x, y, g_post, ga, target)


def _post_bwd(dxn, y, g, ga, s, name):
    t, d = y.shape
    tm = _tok_tile(s)
    tps = s // tm

    def body(dx_ref, y_ref, g_ref, ga_ref, dy_ref, dg_ref, dga_ref):
        i = pl.program_id(0)
        yv = y_ref[...]
        r = _rms(yv)
        yh = yv * r
        dxv = dx_ref[...]
        dn = dxv * ga_ref[0]
        dyh = dn * g_ref[...]
        dy_ref[...] = (r * (dyh - yh * jnp.mean(dyh * yh, axis=-1, keepdims=True))).astype(BF16)

        @pl.when(i == 0)
        def _():
            dg_ref[...] = jnp.zeros_like(dg_ref)

        @pl.when(i % tps == 0)
        def _():
            dga_ref[...] = jnp.zeros_like(dga_ref)

        dg_ref[...] += jnp.sum(dn * yh, axis=0, keepdims=True)
        dga_ref[0] += jnp.sum(dxv * (yh * g_ref[...]), axis=0, keepdims=True)

    return pl.pallas_call(
        body, name=name, grid=(t // tm,),
        out_shape=[jax.ShapeDtypeStruct((t, d), BF16), jax.ShapeDtypeStruct((1, d), F32),
                   jax.ShapeDtypeStruct((t // s, 1, d), F32)],
        in_specs=[_row_spec(tm, d), _row_spec(tm, d), _vec_spec(d), _seq_spec(d, tps)],
        out_specs=[_row_spec(tm, d), _vec_spec(d), _seq_spec(d, tps)],
    )(dxn, y, g, ga)


def _pre_bwd(dh, x, g, sc, dres, s, name):
    t, d = x.shape
    tm = _tok_tile(s)
    tps = s // tm

    def body(dh_ref, x_ref, g_ref, sc_ref, dr_ref, dx_ref, dsh_ref, dsc_ref, dg_ref):
        i = pl.program_id(0)
        xv = x_ref[...]
        r = _rms(xv)
        xh = xv * r
        dhv = dh_ref[...]
        one_sc = 1.0 + sc_ref[0]
        dxh = dhv * one_sc * g_ref[...]
        dx_ref[...] = dr_ref[...] + r * (dxh - xh * jnp.mean(dxh * xh, axis=-1, keepdims=True))

        @pl.when(i == 0)
        def _():
            dg_ref[...] = jnp.zeros_like(dg_ref)

        @pl.when(i % tps == 0)
        def _():
            dsh_ref[...] = jnp.zeros_like(dsh_ref)
            dsc_ref[...] = jnp.zeros_like(dsc_ref)

        dg_ref[...] += jnp.sum(dhv * one_sc * xh, axis=0, keepdims=True)
        dsh_ref[0] += jnp.sum(dhv, axis=0, keepdims=True)
        dsc_ref[0] += jnp.sum(dhv * (xh * g_ref[...]), axis=0, keepdims=True)

    nb = t // s
    return pl.pallas_call(
        body, name=name, grid=(t // tm,),
        out_shape=[jax.ShapeDtypeStruct((t, d), F32), jax.ShapeDtypeStruct((nb, 1, d), F32),
                   jax.ShapeDtypeStruct((nb, 1, d), F32), jax.ShapeDtypeStruct((1, d), F32)],
        in_specs=[_row_spec(tm, d), _row_spec(tm, d), _vec_spec(d), _seq_spec(d, tps), _row_spec(tm, d)],
        out_specs=[_row_spec(tm, d), _seq_spec(d, tps), _seq_spec(d, tps), _vec_spec(d)],
    )(dh, x, g, sc, dres)


def _merge(y_conv, y_att, gates, s, name):
    t, d = y_conv.shape
    tm = _tok_tile(s)

    def body(yc_ref, ya_ref, gc_ref, gt_ref, o_ref):
        f32 = lambda r: r[...].astype(F32)
        o_ref[...] = (_sig(f32(gc_ref)) * f32(yc_ref) + _sig(f32(gt_ref)) * f32(ya_ref)).astype(BF16)

    return pl.pallas_call(
        body, name=name, grid=(t // tm,),
        out_shape=jax.ShapeDtypeStruct((t, d), BF16),
        in_specs=[_row_spec(tm, d), _row_spec(tm, d), pl.BlockSpec((tm, d), lambda i: (i, 0)),
                  pl.BlockSpec((tm, d), lambda i: (i, 1))],
        out_specs=_row_spec(tm, d),
    )(y_conv, y_att, gates, gates)


def _merge_bwd(dm, y_conv, y_att, gates, s, name):
    t, d = y_conv.shape
    tm = _tok_tile(s)

    def body(dm_ref, yc_ref, ya_ref, gc_ref, gt_ref, dyc_ref, dya_ref, dg_ref):
        f32 = lambda r: r[...].astype(F32)
        dmv = f32(dm_ref)
        sc_, st_ = _sig(f32(gc_ref)), _sig(f32(gt_ref))
        dyc_ref[...] = (dmv * sc_).astype(BF16)
        dya_ref[...] = (dmv * st_).astype(BF16)
        dg_ref[:, :d] = (dmv * f32(yc_ref) * (sc_ * (1.0 - sc_))).astype(BF16)
        dg_ref[:, d:] = (dmv * f32(ya_ref) * (st_ * (1.0 - st_))).astype(BF16)

    return pl.pallas_call(
        body, name=name, grid=(t // tm,),
        out_shape=[jax.ShapeDtypeStruct((t, d), BF16), jax.ShapeDtypeStruct((t, d), BF16),
                   jax.ShapeDtypeStruct((t, 2 * d), BF16)],
        in_specs=[_row_spec(tm, d), _row_spec(tm, d), _row_spec(tm, d),
                  pl.BlockSpec((tm, d), lambda i: (i, 0)), pl.BlockSpec((tm, d), lambda i: (i, 1))],
        out_specs=[_row_spec(tm, d), _row_spec(tm, d), _row_spec(tm, 2 * d)],
    )(dm, y_conv, y_att, gates, gates)


def _swiglu(f, s, name):
    t, two = f.shape
    dff = two // 2
    tm = _tok_tile(s)

    def body(g_ref, u_ref, a_ref):
        gv = g_ref[...].astype(F32)
        a_ref[...] = (gv * _sig(gv) * u_ref[...].astype(F32)).astype(BF16)

    return pl.pallas_call(
        body, name=name, grid=(t // tm,),
        out_shape=jax.ShapeDtypeStruct((t, dff), BF16),
        in_specs=[pl.BlockSpec((tm, dff), lambda i: (i, 0)), pl.BlockSpec((tm, dff), lambda i: (i, 1))],
        out_specs=_row_spec(tm, dff),
    )(f, f)


def _swiglu_bwd(da, f, s, name):
    t, two = f.shape
    dff = two // 2
    tm = _tok_tile(s)

    def body(da_ref, g_ref, u_ref, df_ref):
        gv, dav = g_ref[...].astype(F32), da_ref[...].astype(F32)
        sg = _sig(gv)
        df_ref[:, :dff] = (dav * u_ref[...].astype(F32) * (sg * (1.0 + gv * (1.0 - sg)))).astype(BF16)
        df_ref[:, dff:] = (dav * (gv * sg)).astype(BF16)

    return pl.pallas_call(
        body, name=name, grid=(t // tm,),
        out_shape=jax.ShapeDtypeStruct((t, two), BF16),
        in_specs=[_row_spec(tm, dff), pl.BlockSpec((tm, dff), lambda i: (i, 0)),
                  pl.BlockSpec((tm, dff), lambda i: (i, 1))],
        out_specs=_row_spec(tm, two),
    )(da, f, f)


CONV_CHUNK = 64


def _conv_fwd(glu_in, conv_w, conv_b, ln_g, ln_b, nb, s, name):
    t, two = glu_in.shape
    cw = two // 2
    tt = _tile(s, (256, 128, 64))
    spt = s // tt
    ch = min(CONV_CHUNK, tt)

    def body(cur_ref, halo_ref, w_ref, b_ref, g_ref, be_ref, u_ref, cv_ref, ua_ref, ext_ref):
        j = pl.program_id(1)
        cur = cur_ref[...]
        u_cur = cur[:, :cw] * _sig(cur[:, cw:])
        hal = halo_ref[...]
        u_hal = hal[:, :cw] * _sig(hal[:, cw:])
        ext_ref[0:HALO, :] = jnp.where(j == 0, 0.0, u_hal)
        ext_ref[HALO:HALO + tt, :] = u_cur
        u_ref[...] = u_cur
        for c0 in range(0, tt, ch):
            acc = jnp.zeros((ch, cw), F32) + b_ref[...]
            for k in range(CONV_K):
                st = c0 + HALO - (CONV_K - 1) + k
                acc = acc + w_ref[k:k + 1, :] * ext_ref[st:st + ch, :]
            cv_ref[c0:c0 + ch, :] = acc
            mu = jnp.mean(acc, axis=-1, keepdims=True)
            xc = acc - mu
            yv = xc * lax.rsqrt(jnp.mean(xc * xc, axis=-1, keepdims=True) + EPS) * g_ref[...] + be_ref[...]
            ua_ref[c0:c0 + ch, :] = (yv * _sig(yv)).astype(BF16)

    cur_map = lambda b, j: (b * spt + j, 0)
    halo_map = lambda b, j: (jnp.maximum((b * s + j * tt) // HALO - 1, 0), 0)
    vec = lambda d0: pl.BlockSpec((d0, cw), lambda b, j: (0, 0))
    return pl.pallas_call(
        body, name=name, grid=(nb, spt),
        out_shape=[jax.ShapeDtypeStruct((t, cw), F32), jax.ShapeDtypeStruct((t, cw), F32),
                   jax.ShapeDtypeStruct((t, cw), BF16)],
        in_specs=[pl.BlockSpec((tt, two), cur_map), pl.BlockSpec((HALO, two), halo_map),
                  vec(CONV_K), vec(1), vec(1), vec(1)],
        out_specs=[pl.BlockSpec((tt, cw), cur_map)] * 3,
        scratch_shapes=[pltpu.VMEM((tt + HALO, cw), F32)],
    )(glu_in, glu_in, conv_w, conv_b, ln_g, ln_b)


def _convln_bwd(dua, cv, ln_g, ln_b, s, name):
    t, cw = cv.shape
    tm = _tok_tile(s)

    def body(du_ref, cv_ref, g_ref, be_ref, dcv_ref, dg_ref, dbe_ref, db_ref):
        cvv = cv_ref[...]
        mu = jnp.mean(cvv, axis=-1, keepdims=True)
        xc = cvv - mu
        rstd = lax.rsqrt(jnp.mean(xc * xc, axis=-1, keepdims=True) + EPS)
        xh = xc * rstd
        yv = xh * g_ref[...] + be_ref[...]
        sg = _sig(yv)
        dy = du_ref[...] * (sg * (1.0 + yv * (1.0 - sg)))
        dxh = dy * g_ref[...]
        dcv = rstd * (dxh - jnp.mean(dxh, axis=-1, keepdims=True)
                      - xh * jnp.mean(dxh * xh, axis=-1, keepdims=True))
        dcv_ref[...] = dcv

        @pl.when(pl.program_id(0) == 0)
        def _():
            dg_ref[...] = jnp.zeros_like(dg_ref)
            dbe_ref[...] = jnp.zeros_like(dbe_ref)
            db_ref[...] = jnp.zeros_like(db_ref)

        dg_ref[...] += jnp.sum(dy * xh, axis=0, keepdims=True)
        dbe_ref[...] += jnp.sum(dy, axis=0, keepdims=True)
        db_ref[...] += jnp.sum(dcv, axis=0, keepdims=True)

    return pl.pallas_call(
        body, name=name, grid=(t // tm,),
        out_shape=[jax.ShapeDtypeStruct((t, cw), F32)] + [jax.ShapeDtypeStruct((1, cw), F32)] * 3,
        in_specs=[_row_spec(tm, cw), _row_spec(tm, cw), _vec_spec(cw), _vec_spec(cw)],
        out_specs=[_row_spec(tm, cw), _vec_spec(cw), _vec_spec(cw), _vec_spec(cw)],
    )(dua, cv, ln_g, ln_b)


def _conv_bwd(dcv, u, glu_in, conv_w, nb, s, name):
    t, cw = dcv.shape
    tt = _tile(s, (256, 128, 64))
    spt = s // tt
    ch = min(CONV_CHUNK, tt)
    nblk = t // HALO

    def body(d_ref, dn_ref, u_ref, up_ref, glu_ref, w_ref, dglu_ref, dw_ref, dext_ref, uext_ref):
        b, j = pl.program_id(0), pl.program_id(1)
        dcur = d_ref[...]
        dext_ref[0:tt, :] = dcur
        dext_ref[tt:tt + HALO, :] = jnp.where(j == spt - 1, 0.0, dn_ref[...])
        uext_ref[0:HALO, :] = jnp.where(j == 0, 0.0, up_ref[...])
        uext_ref[HALO:HALO + tt, :] = u_ref[...]

        @pl.when((b == 0) & (j == 0))
        def _():
            dw_ref[...] = jnp.zeros_like(dw_ref)

        for c0 in range(0, tt, ch):
            dc = dext_ref[c0:c0 + ch, :]
            du = jnp.zeros((ch, cw), F32)
            for k in range(CONV_K):
                sd = c0 + (CONV_K - 1) - k
                du = du + w_ref[k:k + 1, :] * dext_ref[sd:sd + ch, :]
                su = c0 + HALO - (CONV_K - 1) + k
                dw_ref[k:k + 1, :] += jnp.sum(dc * uext_ref[su:su + ch, :], axis=0, keepdims=True)
            val = glu_ref[c0:c0 + ch, :cw]
            sg = _sig(glu_ref[c0:c0 + ch, cw:])
            dglu_ref[c0:c0 + ch, :cw] = (du * sg).astype(BF16)
            dglu_ref[c0:c0 + ch, cw:] = (du * val * (sg * (1.0 - sg))).astype(BF16)

    cur_map = lambda b, j: (b * spt + j, 0)
    prev_map = lambda b, j: (jnp.maximum((b * s + j * tt) // HALO - 1, 0), 0)
    next_map = lambda b, j: (jnp.minimum((b * s + (j + 1) * tt) // HALO, nblk - 1), 0)
    return pl.pallas_call(
        body, name=name, grid=(nb, spt),
        out_shape=[jax.ShapeDtypeStruct((t, 2 * cw), BF16), jax.ShapeDtypeStruct((HALO, cw), F32)],
        in_specs=[pl.BlockSpec((tt, cw), cur_map), pl.BlockSpec((HALO, cw), next_map),
                  pl.BlockSpec((tt, cw), cur_map), pl.BlockSpec((HALO, cw), prev_map),
                  pl.BlockSpec((tt, 2 * cw), cur_map), pl.BlockSpec((CONV_K, cw), lambda b, j: (0, 0))],
        out_specs=[pl.BlockSpec((tt, 2 * cw), cur_map), pl.BlockSpec((HALO, cw), lambda b, j: (0, 0))],
        scratch_shapes=[pltpu.VMEM((tt + HALO, cw), F32), pltpu.VMEM((tt + HALO, cw), F32)],
    )(dcv, dcv, u, u, glu_in, conv_w)


def _split(x):
    hi = x.astype(BF16)
    return hi, (x - hi.astype(F32)).astype(BF16)


def _tri(upper):
    j = lax.broadcasted_iota(jnp.int32, (KEY_BLK, KEY_BLK), 0)
    s_ = lax.broadcasted_iota(jnp.int32, (KEY_BLK, KEY_BLK), 1)
    return jnp.where(j > s_ if upper else j < s_, 1.0, 0.0).astype(BF16)


def _log_keep(z, causal):
    sp = jnp.maximum(z, 0.0) + jnp.log(1.0 + jnp.exp(-jnp.abs(z)))
    return -sp if causal is None else jnp.where(causal, -sp, 0.0)


def _attn_fwd(q, k, v, name):
    bh, s, dh = q.shape
    tq = _tile(s, (256, 128))
    scale = 1.0 / math.sqrt(dh)

    def body(q_ref, k_ref, v_ref, o_ref, car_ref, acc_ref, run_ref):
        qi = pl.program_id(1)
        qv = q_ref[0]
        lane = lax.broadcasted_iota(jnp.int32, (tq, KEY_BLK), 1)
        row = qi * tq + lax.broadcasted_iota(jnp.int32, (tq, KEY_BLK), 0)
        m_suf = _tri(True)
        nk = (qi * tq + tq) // KEY_BLK
        acc_ref[...] = jnp.zeros_like(acc_ref)
        run_ref[...] = jnp.zeros_like(run_ref)
        car_ref[...] = jnp.zeros_like(car_ref)

        def step(it, _):
            kb = nk - 1 - it
            off = pl.multiple_of(kb * KEY_BLK, KEY_BLK)
            kblk = k_ref[0, pl.ds(off, KEY_BLK), :]
            vblk = v_ref[0, pl.ds(off, KEY_BLK), :]
            z = lax.dot_general(qv, kblk, NT, preferred_element_type=F32) * scale
            causal = (off + lane) < row
            lk = _log_keep(z, causal)
            hi, lo = _split(lk)
            cs = (jnp.dot(hi, m_suf, preferred_element_type=F32)
                  + jnp.dot(lo, m_suf, preferred_element_type=F32))
            run = run_ref[...]
            a = jnp.where(causal, jnp.exp(z + lk + cs + run), 0.0)
            acc_ref[...] += jnp.dot(a.astype(BF16), vblk, preferred_element_type=F32)
            car_ref[0] = jnp.where(lane == kb, run, car_ref[0])
            run_ref[...] = run + jnp.sum(lk, axis=1, keepdims=True)
            return 0

        lax.fori_loop(0, nk, step, 0)
        o_ref[0] = acc_ref[...].astype(BF16)

    full = pl.BlockSpec((1, s, dh), lambda b, i: (b, 0, 0))
    return pl.pallas_call(
        body, name=name, grid=(bh, s // tq),
        out_shape=[jax.ShapeDtypeStruct((bh, s, dh), BF16), jax.ShapeDtypeStruct((bh, s, KEY_BLK), F32)],
        in_specs=[pl.BlockSpec((1, tq, dh), lambda b, i: (b, i, 0)), full, full],
        out_specs=[pl.BlockSpec((1, tq, dh), lambda b, i: (b, i, 0)),
                   pl.BlockSpec((1, tq, KEY_BLK), lambda b, i: (b, i, 0))],
        scratch_shapes=[pltpu.VMEM((tq, dh), F32), pltpu.VMEM((tq, KEY_BLK), F32)],
        compiler_params=pltpu.CompilerParams(dimension_semantics=("parallel", "arbitrary")),
    )(q, k, v)


def _attn_bwd(q, k, v, do, car, name):
    bh, s, dh = q.shape
    tq = _tile(s, (256, 128))
    scale = 1.0 / math.sqrt(dh)
    nkb, nqb = s // KEY_BLK, s // tq

    def body(q_ref, k_ref, v_ref, do_ref, car_ref, dq_ref, dk_ref, dv_ref, dq_acc, g_acc, dk_acc, dv_acc):
        lane = lax.broadcasted_iota(jnp.int32, (tq, KEY_BLK), 1)
        rowi = lax.broadcasted_iota(jnp.int32, (tq, KEY_BLK), 0)
        m_suf, m_pre = _tri(True), _tri(False)
        dq_acc[...] = jnp.zeros_like(dq_acc)
        g_acc[...] = jnp.zeros_like(g_acc)

        def kloop(kb, _):
            off = pl.multiple_of(kb * KEY_BLK, KEY_BLK)
            kblk = k_ref[0, pl.ds(off, KEY_BLK), :]
            vblk = v_ref[0, pl.ds(off, KEY_BLK), :]

            dk_acc[...] = jnp.zeros_like(dk_acc)
            dv_acc[...] = jnp.zeros_like(dv_acc)

            def qloop(qb, _):
                r0 = pl.multiple_of(qb * tq, tq)
                qblk = q_ref[0, pl.ds(r0, tq), :]
                doblk = do_ref[0, pl.ds(r0, tq), :]
                z = lax.dot_general(qblk, kblk, NT, preferred_element_type=F32) * scale
                causal = (off + lane) < (r0 + rowi)
                lk = _log_keep(z, causal)
                hi, lo = _split(lk)
                cs = (jnp.dot(hi, m_suf, preferred_element_type=F32)
                      + jnp.dot(lo, m_suf, preferred_element_type=F32))
                cpast = jnp.sum(jnp.where(lane == kb, car_ref[0, pl.ds(r0, tq), :], 0.0), axis=1, keepdims=True)
                lsig = z + lk
                a = jnp.where(causal, jnp.exp(lsig + cs + cpast), 0.0)
                da = lax.dot_general(doblk, vblk, NT, preferred_element_type=F32)
                g = a * da
                ghi, glo = _split(g)
                gpre = (jnp.dot(ghi, m_pre, preferred_element_type=F32)
                        + jnp.dot(glo, m_pre, preferred_element_type=F32)
                        + g_acc[pl.ds(r0, tq), :])
                dz = jnp.where(causal, g - jnp.exp(lsig) * (g + gpre), 0.0)
                ds_ = (dz * scale).astype(BF16)
                dv_acc[...] += lax.dot_general(a.astype(BF16), doblk, TN, preferred_element_type=F32)
                dk_acc[...] += lax.dot_general(ds_, qblk, TN, preferred_element_type=F32)
                dq_acc[pl.ds(r0, tq), :] += jnp.dot(ds_, kblk, preferred_element_type=F32)
                g_acc[pl.ds(r0, tq), :] += jnp.broadcast_to(jnp.sum(g, axis=1, keepdims=True), (tq, KEY_BLK))
                return 0

            lax.fori_loop(off // tq, nqb, qloop, 0)
            dk_ref[0, pl.ds(off, KEY_BLK), :] = dk_acc[...].astype(BF16)
            dv_ref[0, pl.ds(off, KEY_BLK), :] = dv_acc[...].astype(BF16)
            return 0

        lax.fori_loop(0, nkb, kloop, 0)
        dq_ref[0] = dq_acc[...].astype(BF16)

    full = pl.BlockSpec((1, s, dh), lambda b: (b, 0, 0))
    return pl.pallas_call(
        body, name=name, grid=(bh,),
        out_shape=[jax.ShapeDtypeStruct((bh, s, dh), BF16)] * 3,
        in_specs=[full, full, full, full, pl.BlockSpec((1, s, KEY_BLK), lambda b: (b, 0, 0))],
        out_specs=[full, full, full],
        scratch_shapes=[pltpu.VMEM((s, dh), F32), pltpu.VMEM((s, KEY_BLK), F32),
                        pltpu.VMEM((KEY_BLK, dh), F32), pltpu.VMEM((KEY_BLK, dh), F32)],
        compiler_params=pltpu.CompilerParams(dimension_semantics=("parallel",)),
    )(q, k, v, do, car)


LANES = 128


def _sb_fwd(qkv, nb, s, name):
    t, three_aw = qkv.shape
    aw = three_aw // 3
    npair = aw // LANES
    tq = _tile(s, (256, 128))
    ndiag = tq // KEY_BLK
    spt = s // tq
    scale = 1.0 / math.sqrt(HEAD_DIM)

    def body(q_ref, k_ref, v_ref, o_ref, car_ref, acc_ref, run_ref):
        qi = pl.program_id(2)
        lane = lax.broadcasted_iota(jnp.int32, (tq, LANES), 1)
        col = lax.broadcasted_iota(jnp.int32, (tq, KEY_BLK), 1)
        row = qi * tq + lax.broadcasted_iota(jnp.int32, (tq, KEY_BLK), 0)
        first = lane < HEAD_DIM
        q2 = q_ref[...]
        zero = jnp.zeros_like(q2)
        qh = (jnp.where(first, q2, zero), jnp.where(first, zero, q2))
        m_suf = _tri(True)
        nk = (qi * tq + tq) // KEY_BLK
        acc_ref[...] = jnp.zeros_like(acc_ref)
        run_ref[...] = jnp.zeros_like(run_ref)
        car_ref[...] = jnp.zeros_like(car_ref)

        def step(it, masked):
            kb = nk - 1 - it
            off = pl.multiple_of(kb * KEY_BLK, KEY_BLK)
            kblk = k_ref[pl.ds(off, KEY_BLK), :]
            vblk = v_ref[pl.ds(off, KEY_BLK), :]
            causal = ((off + col) < row) if masked else None
            hs = range(2)
            zs = [lax.dot_general(qh[h], kblk, NT, preferred_element_type=F32) * scale for h in hs]
            lks = [_log_keep(zs[h], causal) for h in hs]
            sp = [_split(lks[h]) for h in hs]
            css = [jnp.dot(sp[h][0], m_suf, preferred_element_type=F32)
                   + jnp.dot(sp[h][1], m_suf, preferred_element_type=F32) for h in hs]
            runs = [run_ref[h] for h in hs]
            aa = [jnp.exp(zs[h] + lks[h] + css[h] + jnp.concatenate([runs[h]] * (KEY_BLK // LANES), axis=1))
                  for h in hs]
            if masked:
                aa = [jnp.where(causal, aa[h], 0.0) for h in hs]
            for h in hs:
                acc_ref[h] += jnp.dot(aa[h].astype(BF16), vblk, preferred_element_type=F32)
            for h in hs:
                car_ref[h] = jnp.where(lane == kb, runs[h], car_ref[h])
                run_ref[h] = runs[h] + jnp.sum(lks[h], axis=1, keepdims=True)

        for it in range(ndiag):
            step(it, True)

        def rest(it, _):
            step(it, False)
            return 0

        lax.fori_loop(ndiag, nk, rest, 0)
        o_ref[...] = jnp.where(first, acc_ref[0], acc_ref[1]).astype(BF16)

    return pl.pallas_call(
        body, name=name, grid=(nb, npair, spt),
        out_shape=[jax.ShapeDtypeStruct((t, aw), BF16),
                   jax.ShapeDtypeStruct((nb * npair * 2, s, LANES), F32)],
        in_specs=[pl.BlockSpec((tq, LANES), lambda b, p, i: (b * spt + i, p)),
                  pl.BlockSpec((s, LANES), lambda b, p, i: (b, npair + p)),
                  pl.BlockSpec((s, LANES), lambda b, p, i: (b, 2 * npair + p))],
        out_specs=[pl.BlockSpec((tq, LANES), lambda b, p, i: (b * spt + i, p)),
                   pl.BlockSpec((2, tq, LANES), lambda b, p, i: (b * npair + p, i, 0))],
        scratch_shapes=[pltpu.VMEM((2, tq, LANES), F32), pltpu.VMEM((2, tq, LANES), F32)],
        compiler_params=pltpu.CompilerParams(dimension_semantics=("parallel", "parallel", "arbitrary")),
    )(qkv, qkv, qkv)


def _sb_bwd(qkv, do, car, nb, s, name):
    t, three_aw = qkv.shape
    aw = three_aw // 3
    npair = aw // LANES
    tq = _tile(s, (256, 128))
    scale = 1.0 / math.sqrt(HEAD_DIM)
    nkb, nqb = s // KEY_BLK, s // tq

    def body(q_ref, k_ref, v_ref, do_ref, car_ref, dq_ref, dk_ref, dv_ref, dq_acc, g_acc, dk_acc, dv_acc):
        lane = lax.broadcasted_iota(jnp.int32, (tq, LANES), 1)
        col = lax.broadcasted_iota(jnp.int32, (tq, KEY_BLK), 1)
        rowi = lax.broadcasted_iota(jnp.int32, (tq, KEY_BLK), 0)
        first_k = lax.broadcasted_iota(jnp.int32, (KEY_BLK, LANES), 1) < HEAD_DIM
        m_suf, m_pre = _tri(True), _tri(False)
        dq_acc[...] = jnp.zeros_like(dq_acc)
        g_acc[...] = jnp.zeros_like(g_acc)

        def kloop(kb, _):
            off = pl.multiple_of(kb * KEY_BLK, KEY_BLK)
            kblk = k_ref[pl.ds(off, KEY_BLK), :]
            vblk = v_ref[pl.ds(off, KEY_BLK), :]
            zero = jnp.zeros_like(kblk)
            kh = (jnp.where(first_k, kblk, zero), jnp.where(first_k, zero, kblk))
            vh = (jnp.where(first_k, vblk, zero), jnp.where(first_k, zero, vblk))
            dk_acc[...] = jnp.zeros_like(dk_acc)
            dv_acc[...] = jnp.zeros_like(dv_acc)

            def tile(qb, masked):
                r0 = pl.multiple_of(qb * tq, tq)
                q2 = q_ref[pl.ds(r0, tq), :]
                do2 = do_ref[pl.ds(r0, tq), :]
                causal = ((off + col) < (r0 + rowi)) if masked else None
                hs = range(2)
                wide = lambda x: jnp.concatenate([x] * (KEY_BLK // LANES), axis=1)
                zs = [lax.dot_general(q2, kh[h], NT, preferred_element_type=F32) * scale for h in hs]
                das = [lax.dot_general(do2, vh[h], NT, preferred_element_type=F32) for h in hs]
                lks = [_log_keep(zs[h], causal) for h in hs]
                sp = [_split(lks[h]) for h in hs]
                css = [jnp.dot(sp[h][0], m_suf, preferred_element_type=F32)
                       + jnp.dot(sp[h][1], m_suf, preferred_element_type=F32) for h in hs]
                cpast = [jnp.sum(jnp.where(lane == kb, car_ref[h, pl.ds(r0, tq), :], 0.0), axis=1, keepdims=True)
                         for h in hs]
                lsig = [zs[h] + lks[h] for h in hs]
                aa = [jnp.exp(lsig[h] + css[h] + cpast[h]) for h in hs]
                if masked:
                    aa = [jnp.where(causal, aa[h], 0.0) for h in hs]
                gs = [aa[h] * das[h] for h in hs]
                gsp = [_split(gs[h]) for h in hs]
                gpre = [jnp.dot(gsp[h][0], m_pre, preferred_element_type=F32)
                        + jnp.dot(gsp[h][1], m_pre, preferred_element_type=F32)
                        + wide(g_acc[h, pl.ds(r0, tq), :]) for h in hs]
                dzs = [gs[h] - jnp.exp(lsig[h]) * (gs[h] + gpre[h]) for h in hs]
                if masked:
                    dzs = [jnp.where(causal, dzs[h], 0.0) for h in hs]
                dss = [(dzs[h] * scale).astype(BF16) for h in hs]
                for h in hs:
                    dv_acc[h] += lax.dot_general(aa[h].astype(BF16), do2, TN, preferred_element_type=F32)
                for h in hs:
                    dk_acc[h] += lax.dot_general(dss[h], q2, TN, preferred_element_type=F32)
                for h in hs:
                    dq_acc[h, pl.ds(r0, tq), :] += jnp.dot(dss[h], kblk, preferred_element_type=F32)
                for h in hs:
                    g_acc[h, pl.ds(r0, tq), :] += jnp.broadcast_to(jnp.sum(gs[h], axis=1, keepdims=True),
                                                                   (tq, LANES))

            q0 = off // tq
            tile(q0, True)

            def rest(qb, _):
                tile(qb, False)
                return 0

            lax.fori_loop(q0 + 1, nqb, rest, 0)
            dk_ref[pl.ds(off, KEY_BLK), :] = jnp.where(first_k, dk_acc[0], dk_acc[1]).astype(BF16)
            dv_ref[pl.ds(off, KEY_BLK), :] = jnp.where(first_k, dv_acc[0], dv_acc[1]).astype(BF16)
            return 0

        lax.fori_loop(0, nkb, kloop, 0)
        first_q = lax.broadcasted_iota(jnp.int32, (s, LANES), 1) < HEAD_DIM
        dq_ref[...] = jnp.where(first_q, dq_acc[0], dq_acc[1]).astype(BF16)

    col = lambda j: pl.BlockSpec((s, LANES), lambda b, p: (b, j * npair + p))
    return pl.pallas_call(
        body, name=name, grid=(nb, npair),
        out_shape=[jax.ShapeDtypeStruct((t, aw), BF16)] * 3,
        in_specs=[col(0), col(1), col(2), col(0),
                  pl.BlockSpec((2, s, LANES), lambda b, p: (b * npair + p, 0, 0))],
        out_specs=[col(0), col(0), col(0)],
        scratch_shapes=[pltpu.VMEM((2, s, LANES), F32), pltpu.VMEM((2, s, LANES), F32),
                        pltpu.VMEM((2, KEY_BLK, LANES), F32), pltpu.VMEM((2, KEY_BLK, LANES), F32)],
        compiler_params=pltpu.CompilerParams(dimension_semantics=("parallel", "parallel")),
    )(qkv, qkv, qkv, do, car)


HALF_Q = 256


def _cumsum_mat(upper):
    m = _tri(upper)
    return jnp.concatenate([m, m], axis=0)


def _cum(x, mat):
    hi, lo = _split(x)
    return jnp.dot(jnp.concatenate([hi, lo], axis=1), mat, preferred_element_type=F32)


def _wide(x):
    return jnp.concatenate([x] * (KEY_BLK // LANES), axis=1)


def _sb4_fwd(qkv, nb, s, name, gather=()):
    t, three_aw = qkv.shape
    aw = three_aw // 3
    npair = aw // LANES
    tq = 2 * HALF_Q
    spt = s // tq
    scale = 1.0 / math.sqrt(HEAD_DIM)

    ng = len(gather)

    def body(*refs):
        q_ref, k_ref, v_ref = refs[:3]
        o_ref, car_ref = refs[3 + ng:5 + ng]
        acc_ref, run_ref = refs[5 + 2 * ng:7 + 2 * ng]
        if ng:
            copies = _gather_copies(refs[3:3 + ng], refs[5 + ng:5 + 2 * ng], *refs[7 + 2 * ng:])
            step_id = (pl.program_id(0) * npair + pl.program_id(1)) * spt + pl.program_id(2)
            pl.when(step_id == 0)(lambda: _start_copies(copies))
        qi = pl.program_id(2)
        lane = lax.broadcasted_iota(jnp.int32, (HALF_Q, LANES), 1)
        col = lax.broadcasted_iota(jnp.int32, (HALF_Q, KEY_BLK), 1)
        rowi = lax.broadcasted_iota(jnp.int32, (HALF_Q, KEY_BLK), 0)
        first = lane < HEAD_DIM
        mat = _cumsum_mat(True)
        qs = []
        for half in range(2):
            q2 = q_ref[half * HALF_Q:(half + 1) * HALF_Q, :] * scale
            zero = jnp.zeros_like(q2)
            qs.append((jnp.where(first, q2, zero), jnp.where(first, zero, q2)))
        acc_ref[...] = jnp.zeros_like(acc_ref)
        run_ref[...] = jnp.zeros_like(run_ref)
        car_ref[...] = jnp.zeros_like(car_ref)
        nk = (qi * tq + tq) // KEY_BLK

        def step(kb, halves):
            off = pl.multiple_of(kb * KEY_BLK, KEY_BLK)
            kblk = k_ref[pl.ds(off, KEY_BLK), :]
            vblk = v_ref[pl.ds(off, KEY_BLK), :]
            chains = [(half, masked, h) for half, masked in halves for h in range(2)]
            causal = {half: ((off + col) < (qi * tq + half * HALF_Q + rowi)) if masked else None
                      for half, masked in halves}
            zs = [lax.dot_general(qs[half][h], kblk, NT, preferred_element_type=F32) for half, _, h in chains]
            lks = [_log_keep(z, causal[c[0]]) for z, c in zip(zs, chains)]
            css = [_cum(lk, mat) for lk in lks]
            runs = [run_ref[2 * half + h] for half, _, h in chains]
            aa = [jnp.exp(z + lk + cs + _wide(run)) for z, lk, cs, run in zip(zs, lks, css, runs)]
            aa = [jnp.where(causal[c[0]], a, 0.0) if c[1] else a for a, c in zip(aa, chains)]
            for a, (half, _, h) in zip(aa, chains):
                acc_ref[2 * half + h] += jnp.dot(a.astype(BF16), vblk, preferred_element_type=F32)
            for lk, run, (half, _, h) in zip(lks, runs, chains):
                rows = pl.ds(half * HALF_Q, HALF_Q)
                car_ref[h, rows, :] = jnp.where(lane == kb, run, car_ref[h, rows, :])
                run_ref[2 * half + h] = run + jnp.sum(lk, axis=1, keepdims=True)

        step(nk - 1, ((1, True),))
        step(nk - 2, ((0, True), (1, False)))

        def rest(it, _):
            step(nk - 1 - it, ((0, False), (1, False)))
            return 0

        lax.fori_loop(2, nk, rest, 0)
        for half in range(2):
            o_ref[half * HALF_Q:(half + 1) * HALF_Q, :] = jnp.where(
                first, acc_ref[2 * half], acc_ref[2 * half + 1]).astype(BF16)
        if ng:
            pl.when(step_id == nb * npair * spt - 1)(lambda: _finish_copies(copies))

    hbm = pl.BlockSpec(memory_space=pl.ANY)
    outs = pl.pallas_call(
        body, name=name, grid=(nb, npair, spt),
        out_shape=[jax.ShapeDtypeStruct((t, aw), BF16),
                   jax.ShapeDtypeStruct((nb * npair * 2, s, LANES), F32)] + _gather_out_shapes(gather),
        in_specs=[pl.BlockSpec((tq, LANES), lambda b, p, i: (b * spt + i, p)),
                  pl.BlockSpec((s, LANES), lambda b, p, i: (b, npair + p)),
                  pl.BlockSpec((s, LANES), lambda b, p, i: (b, 2 * npair + p))] + [hbm] * ng,
        out_specs=[pl.BlockSpec((tq, LANES), lambda b, p, i: (b * spt + i, p)),
                   pl.BlockSpec((2, tq, LANES), lambda b, p, i: (b * npair + p, i, 0))] + [hbm] * ng,
        scratch_shapes=[pltpu.VMEM((4, HALF_Q, LANES), F32), pltpu.VMEM((4, HALF_Q, LANES), F32)]
        + (_copy_sems(ng) if ng else []),
        compiler_params=pltpu.CompilerParams(dimension_semantics=("arbitrary", "arbitrary", "arbitrary")),
    )(qkv, qkv, qkv, *gather)
    return outs[0], outs[1], list(outs[2:])


def _sb4_bwd(qkv, do, car, nb, s, name, exchange=()):
    t, three_aw = qkv.shape
    aw = three_aw // 3
    npair = aw // LANES
    tq = HALF_Q
    scale = 1.0 / math.sqrt(HEAD_DIM)
    nkp, nqb = s // (2 * KEY_BLK), s // tq
    nx = len(exchange)

    def body(*refs):
        q_ref, k_ref, v_ref, do_ref, car_ref = refs[:5]
        dq_ref, dk_ref, dv_ref = refs[5 + nx:8 + nx]
        dq_acc, g_acc, dk_acc, dv_acc = refs[8 + 2 * nx:12 + 2 * nx]
        if nx:
            copies = _exchange_copies(refs[5:5 + nx], refs[8 + nx:8 + 2 * nx], *refs[12 + 2 * nx:])
            step_id = pl.program_id(0) * npair + pl.program_id(1)
            pl.when(step_id == 0)(lambda: _start_copies(copies))
        lane = lax.broadcasted_iota(jnp.int32, (tq, LANES), 1)
        col = lax.broadcasted_iota(jnp.int32, (tq, KEY_BLK), 1)
        rowi = lax.broadcasted_iota(jnp.int32, (tq, KEY_BLK), 0)
        first_k = lax.broadcasted_iota(jnp.int32, (KEY_BLK, LANES), 1) < HEAD_DIM
        m_suf, m_pre = _cumsum_mat(True), _tri(False)
        dq_acc[...] = jnp.zeros_like(dq_acc)
        g_acc[...] = jnp.zeros_like(g_acc)

        def kloop(kp, _):
            offs = [pl.multiple_of((2 * kp + blk) * KEY_BLK, KEY_BLK) for blk in range(2)]
            kblk = [k_ref[pl.ds(off, KEY_BLK), :] for off in offs]
            vblk = [v_ref[pl.ds(off, KEY_BLK), :] for off in offs]
            zero = jnp.zeros_like(kblk[0])
            kh = [(jnp.where(first_k, kb_, zero), jnp.where(first_k, zero, kb_)) for kb_ in kblk]
            vh = [(jnp.where(first_k, vb_, zero), jnp.where(first_k, zero, vb_)) for vb_ in vblk]
            k_scaled = jnp.concatenate([kblk[0] * scale, kblk[1] * scale], axis=0)
            dk_acc[...] = jnp.zeros_like(dk_acc)
            dv_acc[...] = jnp.zeros_like(dv_acc)

            def tile(qb, blocks):
                r0 = pl.multiple_of(qb * tq, tq)
                rows = pl.ds(r0, tq)
                q2 = q_ref[rows, :] * scale
                do2 = do_ref[rows, :]
                chains = [(blk, masked, h) for blk, masked in blocks for h in range(2)]
                causal = {blk: ((offs[blk] + col) < (r0 + rowi)) if masked else None for blk, masked in blocks}
                zs = [lax.dot_general(q2, kh[blk][h], NT, preferred_element_type=F32) for blk, _, h in chains]
                das = [lax.dot_general(do2, vh[blk][h], NT, preferred_element_type=F32) for blk, _, h in chains]
                lks = [_log_keep(z, causal[c[0]]) for z, c in zip(zs, chains)]
                css = [_cum(lk, m_suf) for lk in lks]
                cars = [car_ref[h, rows, :] for h in range(2)]
                cpast = [jnp.sum(jnp.where(lane == 2 * kp + blk, cars[h], 0.0), axis=1, keepdims=True)
                         for blk, _, h in chains]
                lsig = [z + lk for z, lk in zip(zs, lks)]
                aa = [jnp.exp(ls + cs + cp) for ls, cs, cp in zip(lsig, css, cpast)]
                aa = [jnp.where(causal[c[0]], a, 0.0) if c[1] else a for a, c in zip(aa, chains)]
                gs = [a * da for a, da in zip(aa, das)]
                gsum = [jnp.sum(g, axis=1, keepdims=True) for g in gs]
                gin = [jnp.dot(g.astype(BF16), m_pre, preferred_element_type=F32) for g in gs]
                gold = [_wide(g_acc[h, rows, :]) for h in range(2)]
                gpre = []
                for i, (blk, _, h) in enumerate(chains):
                    left = gold[h] if blk == 0 else gold[h] + gsum[h]
                    gpre.append(gin[i] + left)
                dzs = [g - jnp.exp(ls) * (g + gp) for g, ls, gp in zip(gs, lsig, gpre)]
                dzs = [jnp.where(causal[c[0]], dz, 0.0) if c[1] else dz for dz, c in zip(dzs, chains)]
                dss = [dz.astype(BF16) for dz in dzs]
                for a, (blk, _, h) in zip(aa, chains):
                    dv_acc[2 * blk + h] += lax.dot_general(a.astype(BF16), do2, TN, preferred_element_type=F32)
                for ds_, (blk, _, h) in zip(dss, chains):
                    dk_acc[2 * blk + h] += lax.dot_general(ds_, q2, TN, preferred_element_type=F32)
                for h in range(2):
                    if len(blocks) == 2:
                        both = jnp.concatenate([dss[h], dss[2 + h]], axis=1)
                        dq_acc[h, rows, :] += jnp.dot(both, k_scaled, preferred_element_type=F32)
                        tot = gsum[h] + gsum[2 + h]
                    else:
                        dq_acc[h, rows, :] += jnp.dot(dss[h], k_scaled[:KEY_BLK], preferred_element_type=F32)
                        tot = gsum[h]
                    g_acc[h, rows, :] += jnp.broadcast_to(tot, (tq, LANES))

            tile(2 * kp, ((0, True),))
            tile(2 * kp + 1, ((0, False), (1, True)))

            def rest(qb, _):
                tile(qb, ((0, False), (1, False)))
                return 0

            lax.fori_loop(2 * kp + 2, nqb, rest, 0)
            for blk in range(2):
                dk_ref[pl.ds(offs[blk], KEY_BLK), :] = jnp.where(
                    first_k, dk_acc[2 * blk], dk_acc[2 * blk + 1]).astype(BF16)
                dv_ref[pl.ds(offs[blk], KEY_BLK), :] = jnp.where(
                    first_k, dv_acc[2 * blk], dv_acc[2 * blk + 1]).astype(BF16)
            return 0

        lax.fori_loop(0, nkp, kloop, 0)
        first_q = lax.broadcasted_iota(jnp.int32, (s, LANES), 1) < HEAD_DIM
        dq_ref[...] = jnp.where(first_q, dq_acc[0], dq_acc[1]).astype(BF16)
        if nx:
            pl.when(step_id == nb * npair - 1)(lambda: _finish_copies(copies))

    colspec = lambda j: pl.BlockSpec((s, LANES), lambda b, p: (b, j * npair + p))
    hbm = pl.BlockSpec(memory_space=pl.ANY)
    outs = pl.pallas_call(
        body, name=name, grid=(nb, npair),
        out_shape=[jax.ShapeDtypeStruct((t, aw), BF16)] * 3 + _exchange_out_shapes(exchange),
        in_specs=[colspec(0), colspec(1), colspec(2), colspec(0),
                  pl.BlockSpec((2, s, LANES), lambda b, p: (b * npair + p, 0, 0))] + [hbm] * nx,
        out_specs=[colspec(0), colspec(0), colspec(0)] + [hbm] * nx,
        scratch_shapes=[pltpu.VMEM((2, s, LANES), F32), pltpu.VMEM((2, s, LANES), F32),
                        pltpu.VMEM((4, KEY_BLK, LANES), F32), pltpu.VMEM((4, KEY_BLK, LANES), F32)]
        + (_copy_sems(nx) if nx else []),
        compiler_params=pltpu.CompilerParams(dimension_semantics=("arbitrary", "arbitrary")),
    )(qkv, qkv, qkv, do, car, *exchange)
    return outs[0], outs[1], outs[2], list(outs[3:])


def _ada_fwd(c_all, ada_w, ada_b_loc, name):
    nl, d, n6 = ada_w.shape
    nb = c_all.shape[0]

    def body(c_ref, w_ref, b_ref, o_ref):
        cv = c_ref[...]
        o_ref[0] = jnp.dot(cv * _sig(cv), w_ref[0], preferred_element_type=F32) + b_ref[0]

    return pl.pallas_call(
        body, name=name, grid=(nl,),
        out_shape=jax.ShapeDtypeStruct((nl, nb, n6), F32),
        in_specs=[pl.BlockSpec((nb, d), lambda l: (0, 0)), pl.BlockSpec((1, d, n6), lambda l: (l, 0, 0)),
                  pl.BlockSpec((1, 1, n6), lambda l: (l, 0, 0))],
        out_specs=pl.BlockSpec((1, nb, n6), lambda l: (l, 0, 0)),
    )(c_all, ada_w, ada_b_loc)


def _ada_bwd(c_all_t, dmod_loc, dmod_all, name):
    nl, nb, n6 = dmod_loc.shape
    d = c_all_t.shape[0]
    n_all = dmod_all.shape[2]

    def body(c_ref, dl_ref, da_ref, gw_ref, gb_ref):
        cv = c_ref[...]
        gw_ref[0] = jnp.dot(cv * _sig(cv), dl_ref[0], preferred_element_type=F32)
        gb_ref[0] = jnp.sum(da_ref[0], axis=0, keepdims=True)

    return pl.pallas_call(
        body, name=name, grid=(nl,),
        out_shape=[jax.ShapeDtypeStruct((nl, d, n6), F32), jax.ShapeDtypeStruct((nl, 1, n_all), F32)],
        in_specs=[pl.BlockSpec((d, nb), lambda l: (0, 0)), pl.BlockSpec((1, nb, n6), lambda l: (l, 0, 0)),
                  pl.BlockSpec((1, nb, n_all), lambda l: (l, 0, 0))],
        out_specs=[pl.BlockSpec((1, d, n6), lambda l: (l, 0, 0)), pl.BlockSpec((1, 1, n_all), lambda l: (l, 0, 0))],
    )(c_all_t, dmod_loc, dmod_all)


def _adamw(w, g, m, v, name):
    shape = w.shape
    cols = shape[-1]
    rows = w.size // cols
    tr = _tile(rows, (512, 256, 128, 64, 32, 16, 8))
    flat = lambda a: a.reshape(rows, cols)

    def body(w_ref, g_ref, m_ref, v_ref, d_ref, mo_ref, vo_ref):
        gv = g_ref[...]
        mn = ADAM_B1 * m_ref[...] + (1.0 - ADAM_B1) * gv
        vn = ADAM_B2 * v_ref[...] + (1.0 - ADAM_B2) * (gv * gv)
        m_hat = mn / (1.0 - ADAM_B1 ** ADAM_STEP)
        v_hat = vn / (1.0 - ADAM_B2 ** ADAM_STEP)
        d_ref[...] = -ADAM_LR * (m_hat / (jnp.sqrt(v_hat) + ADAM_EPS) + ADAM_WD * w_ref[...])
        mo_ref[...] = mn
        vo_ref[...] = vn

    spec = pl.BlockSpec((tr, cols), lambda i: (i, 0))
    outs = pl.pallas_call(
        body, name=name, grid=(rows // tr,),
        out_shape=[jax.ShapeDtypeStruct((rows, cols), F32)] * 3,
        in_specs=[spec] * 4, out_specs=[spec] * 3,
    )(flat(w), flat(g), flat(m), flat(v))
    return tuple(o.reshape(shape) for o in outs)


def _pack_rows(parts, d):
    flat = jnp.concatenate([p.reshape(-1) for p in parts])
    rows = -(-flat.size // d)
    rows = -(-rows // 8) * 8
    return jnp.pad(flat, (0, rows * d - flat.size)).reshape(rows, d)


def _to_heads(a, nb, s):
    h = a.shape[1] // HEAD_DIM
    return a.reshape(nb, s, h, HEAD_DIM).transpose(0, 2, 1, 3).reshape(nb * h, s, HEAD_DIM)


def _from_heads(a, nb, s):
    h = a.shape[0] // nb
    return a.reshape(nb, h, s, HEAD_DIM).transpose(0, 2, 1, 3).reshape(nb * s, h * HEAD_DIM)


def kernel(x, c, ada_w, ada_b, pre_mix_g, post_mix_g, pre_ffn_g, post_ffn_g, w_in, conv_w, conv_b, conv_ln_g, conv_ln_b, w_conv_out, w_att_out, w_o, w_ffn_in, w_ffn_out, loss_target, m_ada_w, m_ada_b, m_pre_mix_g, m_post_mix_g, m_pre_ffn_g, m_post_ffn_g, m_w_in, m_conv_w, m_conv_b, m_conv_ln_g, m_conv_ln_b, m_w_conv_out, m_w_att_out, m_w_o, m_w_ffn_in, m_w_ffn_out, v_ada_w, v_ada_b, v_pre_mix_g, v_post_mix_g, v_pre_ffn_g, v_post_ffn_g, v_w_in, v_conv_w, v_conv_b, v_conv_ln_g, v_conv_ln_b, v_w_conv_out, v_w_att_out, v_w_o, v_w_ffn_in, v_w_ffn_out):
    nb, s, d = x.shape
    nl = ada_w.shape[0]
    cw = conv_b.shape[1]
    aw = w_att_out.shape[1]
    cl = conv_w.shape[2]
    n6 = ada_w.shape[2]
    t = nb * s
    me = _my_index()

    conv_flat = conv_w.reshape(-1)
    p1 = _pack_rows([c, conv_flat], d)
    r1 = p1.shape[0]
    (p1_all,) = _all_gather([p1], "gather_c_convw", True)
    p1_all = p1_all.reshape(N_DEV, r1 * d)
    c_all = p1_all[:, :nb * d].reshape(N_DEV * nb, d)
    conv_w_all = p1_all[:, nb * d:nb * d + conv_flat.size].reshape(N_DEV, nl, CONV_K, cl)
    conv_w_all = conv_w_all.transpose(1, 2, 0, 3).reshape(nl, CONV_K, cw)

    ada_b_loc = lax.dynamic_slice_in_dim(ada_b, me * n6, n6, axis=1).reshape(nl, 1, n6)
    mod_cols = _ada_fwd(c_all, ada_w, ada_b_loc, "ada_fwd")
    (mod_g,) = _all_gather([mod_cols.reshape(-1, d)], "gather_mod", True)
    mod_g = mod_g.reshape(N_DEV, nl, N_DEV * nb, n6)
    mod_mine = lax.dynamic_slice_in_dim(mod_g, me * nb, nb, axis=2)
    mod = mod_mine.transpose(1, 2, 0, 3).reshape(nl, nb, 6 * d)
    mods = [[mod[l, :, i * d:(i + 1) * d].reshape(nb, 1, d) for i in range(6)] for l in range(nl)]

    tr_ = lambda w: jnp.swapaxes(w, 0, 1).astype(BF16)
    shards = [[tr_(w_in[l]), tr_(w_conv_out[l]), tr_(w_att_out[l]), w_o[l].astype(BF16),
               tr_(w_ffn_in[l]), w_ffn_out[l].astype(BF16)] for l in range(nl)]
    weights = [list(_all_gather(shards[0][:3], "gather_weights", False))]

    vec = lambda a, l: a[l].reshape(1, -1)
    x2 = x.reshape(t, d)
    tgt = loss_target.reshape(t, d)

    saved = []
    xin = x2
    h = _norm_mod(xin, vec(pre_mix_g, 0), mods[0][1], mods[0][0], s, "norm_mod_0")
    for l in range(nl):
        win_t, wconv_t, watt_t = weights[l][:3]
        sh1, sc1, ga1, sh2, sc2, ga2 = mods[l]
        glu_in = _mm(h, win_t, NT, F32, f"proj_glu_{l}", 0, 2 * cw)
        qkv = _mm(h, win_t, NT, BF16, f"proj_qkv_{l}", 2 * cw, 3 * aw)
        gates = _mm(h, win_t, NT, BF16, f"proj_gates_{l}", 2 * cw + 3 * aw, 2 * d)
        u, cv, ua = _conv_fwd(glu_in, conv_w_all[l], vec(conv_b, l), vec(conv_ln_g, l), vec(conv_ln_b, l),
                              nb, s, f"conv_fwd_{l}")
        late = shards[l][len(weights[l]):]
        o, car, got = _sb4_fwd(qkv, nb, s, f"attn_fwd_{l}", late + (shards[l + 1] if l + 1 < nl else []))
        weights[l] = weights[l] + got[:len(late)]
        if l + 1 < nl:
            weights.append(got[len(late):])
        wo, wffn_in_t, wffn_out = weights[l][3:]
        y_conv = _mm(ua, wconv_t, NT, BF16, f"conv_out_{l}")
        y_att = _mm(o, watt_t, NT, BF16, f"att_out_{l}")
        merged = _merge(y_conv, y_att, gates, s, f"merge_{l}")
        y = _mm(merged, wo, NN, F32, f"w_o_{l}")
        x1, h2 = _post_res_norm(xin, y, vec(post_mix_g, l), ga1, vec(pre_ffn_g, l), sc2, sh2, s, f"post_mix_{l}")
        f = _mm(h2, wffn_in_t, NT, BF16, f"ffn_in_{l}")
        a = _swiglu(f, s, f"swiglu_{l}")
        y2 = _mm(a, wffn_out, NN, F32, f"ffn_out_{l}")
        saved.append(dict(xin=xin, h=h, glu_in=glu_in, gates=gates, u=u, cv=cv, ua=ua, qkv=qkv,
                          car=car, o=o, y_conv=y_conv, y_att=y_att, merged=merged, y=y, x1=x1, h2=h2, f=f,
                          a=a, y2=y2))
        if l + 1 < nl:
            nsh1, nsc1 = mods[l + 1][0], mods[l + 1][1]
            xin, h = _post_res_norm(x1, y2, vec(post_ffn_g, l), ga2, vec(pre_mix_g, l + 1), nsc1, nsh1, s,
                                    f"post_ffn_{l}")
        else:
            dx, loss_acc = _post_res_loss(x1, y2, vec(post_ffn_g, l), ga2, tgt, s, "loss")

    slots = [None] * (6 * nl)
    small = [None] * nl
    for l in reversed(range(nl)):
        win_t, wconv_t, watt_t, wo, wffn_in_t, wffn_out = weights[l]
        sh1, sc1, ga1, sh2, sc2, ga2 = mods[l]
        sv = saved[l]
        dy2, dg_post_ffn, dga2 = _post_bwd(dx, sv["y2"], vec(post_ffn_g, l), ga2, s, f"post_ffn_bwd_{l}")
        d_wffn_out = _mm(sv["a"], dy2, TN, BF16, f"d_w_ffn_out_{l}")
        da = _mm(dy2, wffn_out, NT, BF16, f"d_a_{l}")
        df = _swiglu_bwd(da, sv["f"], s, f"swiglu_bwd_{l}")
        d_wffn_in_t = _mm(df, sv["h2"], TN, BF16, f"d_w_ffn_in_{l}")
        dh2 = _mm(df, wffn_in_t, NN, F32, f"d_h2_{l}")
        dx1, dsh2, dsc2, dg_pre_ffn = _pre_bwd(dh2, sv["x1"], vec(pre_ffn_g, l), sc2, dx, s, f"pre_ffn_bwd_{l}")
        dy, dg_post_mix, dga1 = _post_bwd(dx1, sv["y"], vec(post_mix_g, l), ga1, s, f"post_mix_bwd_{l}")
        d_wo = _mm(sv["merged"], dy, TN, BF16, f"d_w_o_{l}")
        dmerged = _mm(dy, wo, NT, BF16, f"d_merged_{l}")
        dyc, dya, dgates = _merge_bwd(dmerged, sv["y_conv"], sv["y_att"], sv["gates"], s, f"merge_bwd_{l}")
        d_wconv_t = _mm(dyc, sv["ua"], TN, BF16, f"d_w_conv_out_{l}")
        dua = _mm(dyc, wconv_t, NN, F32, f"d_ua_{l}")
        d_watt_t = _mm(dya, sv["o"], TN, BF16, f"d_w_att_out_{l}")
        do = _mm(dya, watt_t, NN, BF16, f"d_o_{l}")
        ready = [(5, d_wffn_out), (4, d_wffn_in_t), (3, d_wo), (1, d_wconv_t), (2, d_watt_t)]
        dq, dk, dv, got = _sb4_bwd(sv["qkv"], do, sv["car"], nb, s, f"attn_bwd_{l}", [g for _, g in ready])
        for (gi, _), r in zip(ready, got):
            slots[6 * l + gi] = r
        dcv, dln_g, dln_b, dconv_b = _convln_bwd(dua, sv["cv"], vec(conv_ln_g, l), vec(conv_ln_b, l), s,
                                                 f"convln_bwd_{l}")
        dglu, dconv_w = _conv_bwd(dcv, sv["u"], sv["glu_in"], conv_w_all[l], nb, s, f"conv_bwd_{l}")
        dproj = jnp.concatenate([dglu, dq, dk, dv, dgates], axis=1)
        d_win_t = _mm(dproj, sv["h"], TN, BF16, f"d_w_in_{l}")
        dh, (slots[6 * l],) = _mm(dproj, win_t, NN, F32, f"d_h_{l}", exchange=[d_win_t])
        dx, dsh1, dsc1, dg_pre_mix = _pre_bwd(dh, sv["xin"], vec(pre_mix_g, l), sc1, dx1, s, f"pre_mix_bwd_{l}")
        dmod = jnp.concatenate([dsh1, dsc1, dga1, dsh2, dsc2, dga2], axis=2)
        small[l] = [dg_pre_mix, dg_post_mix, dg_pre_ffn, dg_post_ffn, dconv_b, dln_g, dln_b, dconv_w, dmod]
    grad_x = dx.reshape(nb, s, d)

    sizes = [a.size for a in small[0]]
    p3 = _pack_rows([a for l in range(nl) for a in small[l]] + [loss_acc[0, :1]], d)
    r3 = p3.shape[0]
    (p3_all,) = _all_gather([p3], "gather_small_grads", True)
    p3_all = p3_all.reshape(N_DEV, r3, d)
    p3_sum = _sum_slots(p3_all, "sum_small_grads").reshape(-1)
    per_layer = sum(sizes)
    loss = p3_sum[nl * per_layer]
    small_sum, dmod_all = [], []
    for l in range(nl):
        off, parts = l * per_layer, []
        for sz in sizes[:-1]:
            parts.append(p3_sum[off:off + sz])
            off += sz
        small_sum.append(parts)
        dm = p3_all.reshape(N_DEV, r3 * d)[:, off:off + sizes[-1]]
        dmod_all.append(dm.reshape(N_DEV * nb, 6 * d))
    dmod_all = jnp.stack(dmod_all)
    dmod_loc = lax.dynamic_slice_in_dim(dmod_all, me * n6, n6, axis=2)
    g_ada_w, g_ada_b = _ada_bwd(c_all.T, dmod_loc, dmod_all, "ada_bwd")
    g_ada_b = g_ada_b.reshape(nl, 6 * d)

    stack_small = lambda i, shape: jnp.stack([small_sum[l][i].reshape(shape) for l in range(nl)])
    g_pre_mix, g_post_mix, g_pre_ffn, g_post_ffn = (stack_small(i, (d,)) for i in range(4))
    g_conv_b, g_ln_g, g_ln_b = (stack_small(i, (cw,)) for i in range(4, 7))
    g_conv_w_all = stack_small(7, (HALO, cw))[:, :CONV_K]
    g_conv_w = lax.dynamic_slice_in_dim(g_conv_w_all, me * cl, cl, axis=2)

    sums = [_sum_slots(r, f"sum_grads_{i}") for i, r in enumerate(slots)]
    un_t = lambda i: jnp.stack([jnp.swapaxes(sums[6 * l + i], 0, 1) for l in range(nl)])
    keep = lambda i: jnp.stack([sums[6 * l + i] for l in range(nl)])
    g_w_in, g_w_conv_out, g_w_att_out, g_w_o, g_w_ffn_in, g_w_ffn_out = (
        un_t(0), un_t(1), un_t(2), keep(3), un_t(4), keep(5))

    grads = [g_ada_w, g_ada_b, g_pre_mix, g_post_mix, g_pre_ffn, g_post_ffn, g_w_in, g_conv_w, g_conv_b,
             g_ln_g, g_ln_b, g_w_conv_out, g_w_att_out, g_w_o, g_w_ffn_in, g_w_ffn_out]
    ws = [ada_w, ada_b, pre_mix_g, post_mix_g, pre_ffn_g, post_ffn_g, w_in, conv_w, conv_b, conv_ln_g,
          conv_ln_b, w_conv_out, w_att_out, w_o, w_ffn_in, w_ffn_out]
    ms = [m_ada_w, m_ada_b, m_pre_mix_g, m_post_mix_g, m_pre_ffn_g, m_post_ffn_g, m_w_in, m_conv_w, m_conv_b,
          m_conv_ln_g, m_conv_ln_b, m_w_conv_out, m_w_att_out, m_w_o, m_w_ffn_in, m_w_ffn_out]
    vs = [v_ada_w, v_ada_b, v_pre_mix_g, v_post_mix_g, v_pre_ffn_g, v_post_ffn_g, v_w_in, v_conv_w, v_conv_b,
          v_conv_ln_g, v_conv_ln_b, v_w_conv_out, v_w_att_out, v_w_o, v_w_ffn_in, v_w_ffn_out]
    deltas, new_m, new_v = [], [], []
    for i, (w, g, m, v) in enumerate(zip(ws, grads, ms, vs)):
        dlt, mn, vn = _adamw(w, g, m, v, f"adamw_{i}")
        deltas.append(dlt)
        new_m.append(mn)
        new_v.append(vn)
    return (loss, grad_x, *grads, *deltas, *new_m, *new_v)
```

```python
import functools
import math

import jax
import jax.numpy as jnp
from jax import lax
from jax.experimental import pallas as pl
from jax.experimental.pallas import tpu as pltpu

F32 = jnp.float32
BF16 = jnp.bfloat16
MESH = pl.DeviceIdType.MESH
N_DEV = 8
EPS = 1e-6
HEAD_DIM = 64
CONV_K = 31
HALO = 32
KEY_BLK = 256
ADAM_LR, ADAM_B1, ADAM_B2, ADAM_EPS, ADAM_WD, ADAM_STEP = 0.001, 0.9, 0.999, 1e-08, 0.01, 10

NT = (((1,), (1,)), ((), ()))
NN = (((1,), (0,)), ((), ()))
TN = (((0,), (0,)), ((), ()))


def _tile(n, cands):
    for t in cands:
        if n % t == 0:
            return t
    return n


def _lane_tile(n, cap):
    best = n
    for t in range(128, min(n, cap) + 1, 128):
        if n % t == 0:
            best = t
    return best


def _sig(x):
    return 1.0 / (1.0 + jnp.exp(-x))


def _pos():
    return lax.axis_index("x"), lax.axis_index("y"), lax.axis_index("c")


def _my_index():
    x, y, c = _pos()
    return 4 * x + 2 * y + c


def _all_gather(arrs, name, in_vmem):
    n = len(arrs)

    def body(*refs):
        ins, outs = refs[:n], refs[n:2 * n]
        send_sems, recv_sems, local_sems = refs[2 * n:]
        x, y, c = _pos()
        me, sib = (x, y, c), (x, y, 1 - c)
        chips = [(1 - x, y), (x, 1 - y), (1 - x, 1 - y)]

        def rows(i, p):
            r = ins[i].shape[0]
            return outs[i].at[pl.ds((4 * p[0] + 2 * p[1] + p[2]) * r, r), :]

        def copy(i, k, block, to, src=None):
            return pltpu.make_async_remote_copy(
                src_ref=rows(i, block) if src is None else src, dst_ref=rows(i, block),
                send_sem=send_sems.at[i, k], recv_sem=recv_sems.at[i, k],
                device_id=to, device_id_type=MESH)

        mine = [pltpu.make_async_copy(ins[i], rows(i, me), local_sems.at[i]) for i in range(n)]
        for cp in mine:
            cp.start()
        first = []
        for j, chip in enumerate(chips):
            for i in range(n):
                first.append(copy(i, 1 + j, me, (*chip, c), src=ins[i]))
        for i in range(n):
            first.append(copy(i, 0, me, sib, src=ins[i]))
        for cp in first:
            cp.start()
        passed = []
        for j, chip in enumerate(chips):
            for i in range(n):
                copy(i, 1 + j, (*chip, c), me).wait_recv()
                p = copy(i, 4 + j, (*chip, c), sib)
                p.start()
                passed.append(p)
        for i in range(n):
            copy(i, 0, sib, me).wait_recv()
            for j, chip in enumerate(chips):
                copy(i, 4 + j, (*chip, 1 - c), me).wait_recv()
        for cp in first + passed:
            cp.wait_send()
        for cp in mine:
            cp.wait()

    space = pltpu.VMEM if in_vmem else pl.ANY
    spec = pl.BlockSpec(memory_space=space)
    return pl.pallas_call(
        body, name=name,
        out_shape=[jax.ShapeDtypeStruct((N_DEV * a.shape[0], a.shape[1]), a.dtype) for a in arrs],
        in_specs=[spec] * n, out_specs=[spec] * n,
        scratch_shapes=[pltpu.SemaphoreType.DMA((n, 7)), pltpu.SemaphoreType.DMA((n, 7)),
                        pltpu.SemaphoreType.DMA((n,))],
    )(*arrs)


def _exchange_blocks(arrs, name):
    n = len(arrs)

    def body(*refs):
        copies = _exchange_copies(refs[:n], refs[n:2 * n], *refs[2 * n:])
        _start_copies(copies)
        _finish_copies(copies)

    spec = pl.BlockSpec(memory_space=pl.ANY)
    return pl.pallas_call(
        body, name=name, out_shape=_exchange_out_shapes(arrs),
        in_specs=[spec] * n, out_specs=[spec] * n, scratch_shapes=_copy_sems(n),
    )(*arrs)


PEER_ORDER = (4, 2, 6, 5, 3, 7, 1)


def _peer(k):
    x, y, c = _pos()
    return (1 - x if k & 4 else x, 1 - y if k & 2 else y, 1 - c if k & 1 else c)


def _mesh_index(p):
    return 4 * p[0] + 2 * p[1] + p[2]


def _copy_sems(n):
    return [pltpu.SemaphoreType.DMA((n, 7)), pltpu.SemaphoreType.DMA((n, 7)), pltpu.SemaphoreType.DMA((n,))]


def _exchange_out_shapes(arrs):
    return [jax.ShapeDtypeStruct((N_DEV, a.shape[0] // N_DEV, a.shape[1]), a.dtype) for a in arrs]


def _gather_out_shapes(arrs):
    return [jax.ShapeDtypeStruct((N_DEV * a.shape[0], a.shape[1]), a.dtype) for a in arrs]


def _exchange_copies(ins, outs, send_sems, recv_sems, local_sems):
    n = len(ins)

    def block(i, p):
        r = outs[i].shape[1]
        return ins[i].at[pl.ds(_mesh_index(p) * r, r), :]

    mine = [pltpu.make_async_copy(block(i, _peer(0)), outs[i].at[0], local_sems.at[i]) for i in range(n)]
    sends = [pltpu.make_async_remote_copy(
        src_ref=block(i, _peer(k)), dst_ref=outs[i].at[k], send_sem=send_sems.at[i, k - 1],
        recv_sem=recv_sems.at[i, k - 1], device_id=_peer(k), device_id_type=MESH)
        for k in PEER_ORDER for i in range(n)]
    return mine, sends, sends


def _gather_copies(ins, outs, send_sems, recv_sems, local_sems):
    n = len(ins)

    def rows(i, p):
        r = ins[i].shape[0]
        return outs[i].at[pl.ds(_mesh_index(p) * r, r), :]

    def copy(i, k, landing):
        return pltpu.make_async_remote_copy(
            src_ref=ins[i], dst_ref=rows(i, landing), send_sem=send_sems.at[i, k - 1],
            recv_sem=recv_sems.at[i, k - 1], device_id=_peer(k), device_id_type=MESH)

    mine = [pltpu.make_async_copy(ins[i], rows(i, _peer(0)), local_sems.at[i]) for i in range(n)]
    sends = [copy(i, k, _peer(0)) for k in PEER_ORDER for i in range(n)]
    recvs = [copy(i, k, _peer(k)) for k in PEER_ORDER for i in range(n)]
    return mine, sends, recvs


def _start_copies(copies):
    mine, sends, _ = copies
    for cp in mine + sends:
        cp.start()


def _finish_copies(copies):
    mine, sends, recvs = copies
    for cp in recvs:
        cp.wait_recv()
    for cp in sends:
        cp.wait_send()
    for cp in mine:
        cp.wait()


def _sum_slots(r3, name):
    _, r, c = r3.shape
    tr = _tile(r, (128, 96, 64, 32, 16))

    def body(r_ref, o_ref):
        acc = r_ref[0].astype(F32)
        for k in range(1, N_DEV):
            acc = acc + r_ref[k].astype(F32)
        o_ref[...] = acc

    return pl.pallas_call(
        body, name=name, grid=(r // tr,),
        out_shape=jax.ShapeDtypeStruct((r, c), F32),
        in_specs=[pl.BlockSpec((N_DEV, tr, c), lambda i: (0, i, 0))],
        out_specs=pl.BlockSpec((tr, c), lambda i: (i, 0)),
    )(r3)


def _mm(a, b, dims, out_dtype, name, n0=0, n=None, exchange=(), gather=()):
    if dims is NT:
        m, k = a.shape
        n = b.shape[0] if n is None else n
    elif dims is NN:
        m, k = a.shape
        n = b.shape[1]
    else:
        k, m = a.shape
        n = b.shape[1]
    if dims is TN:
        tm = _lane_tile(m, 1536)
        tn = _lane_tile(n, 1024)
        tk = _tile(k, (1024, 512, 256, 128))
    else:
        tm = _tile(m, (1024, 512, 256, 128))
        tn = _lane_tile(math.gcd(n, n0) if n0 else n, 1536)
        tk = _lane_tile(k, 2816)
    nk = k // tk
    noff = n0 // tn

    moved = list(exchange) or list(gather)
    make_copies = _exchange_copies if exchange else _gather_copies
    nx = len(moved)
    grid = (m // tm, n // tn, nk)

    def body(*refs):
        a_ref, b_ref = refs[:2]
        o_ref = refs[2 + nx]
        scratch = refs[3 + 2 * nx:]
        if nx:
            copies = make_copies(refs[2:2 + nx], refs[3 + nx:3 + 2 * nx], *scratch[-3:])
            step_id = (pl.program_id(0) * grid[1] + pl.program_id(1)) * grid[2] + pl.program_id(2)
            pl.when(step_id == 0)(lambda: _start_copies(copies))
        part =lax.dot_general(a_ref[...].astype(BF16), b_ref[...].astype(BF16), dims,
                               preferred_element_type=F32)
        if nk == 1:
            o_ref[...] = part.astype(o_ref.dtype)
        else:
            acc_ref = scratch[0]
            kk = pl.program_id(2)

            @pl.when(kk == 0)
            def _():
                acc_ref[...] = part

            @pl.when(kk > 0)
            def _():
                acc_ref[...] += part

            @pl.when(kk == nk - 1)
            def _():
                o_ref[...] = acc_ref[...].astype(o_ref.dtype)

        if nx:
            pl.when(step_id == grid[0] * grid[1] * grid[2] - 1)(lambda: _finish_copies(copies))

    if dims is NT:
        in_specs = [pl.BlockSpec((tm, tk), lambda i, j, kk: (i, kk)),
                    pl.BlockSpec((tn, tk), lambda i, j, kk: (j + noff, kk))]
    elif dims is NN:
        in_specs = [pl.BlockSpec((tm, tk), lambda i, j, kk: (i, kk)),
                    pl.BlockSpec((tk, tn), lambda i, j, kk: (kk, j))]
    else:
        in_specs = [pl.BlockSpec((tk, tm), lambda i, j, kk: (kk, i)),
                    pl.BlockSpec((tk, tn), lambda i, j, kk: (kk, j))]
    hbm = pl.BlockSpec(memory_space=pl.ANY)
    outs = pl.pallas_call(
        body, name=name, grid=grid,
        out_shape=[jax.ShapeDtypeStruct((m, n), out_dtype)]
        + (_exchange_out_shapes(moved) if exchange else _gather_out_shapes(moved)),
        in_specs=in_specs + [hbm] * nx,
        out_specs=[pl.BlockSpec((tm, tn), lambda i, j, kk: (i, j))] + [hbm] * nx,
        scratch_shapes=([] if nk == 1 else [pltpu.VMEM((tm, tn), F32)]) + (_copy_sems(nx) if nx else []),
        compiler_params=pltpu.CompilerParams(
            dimension_semantics=("arbitrary",) * 3 if nx else ("parallel", "parallel", "arbitrary")),
    )(a, b, *moved)
    return (outs[0], list(outs[1:])) if nx else outs[0]


def _tok_tile(s):
    return _tile(s, (512, 256, 128))


def _row_spec(tm, d):
    return pl.BlockSpec((tm, d), lambda i: (i, 0))


def _vec_spec(d):
    return pl.BlockSpec((1, d), lambda i: (0, 0))


def _seq_spec(d, tiles_per_seq):
    return pl.BlockSpec((1, 1, d), lambda i: (i // tiles_per_seq, 0, 0))


def _rms(x):
    return lax.rsqrt(jnp.mean(x * x, axis=-1, keepdims=True) + EPS)


def _norm_mod(x, g, sc, sh, s, name):
    t, d = x.shape
    tm = _tok_tile(s)

    def body(x_ref, g_ref, sc_ref, sh_ref, h_ref):
        xv = x_ref[...]
        h = (xv * _rms(xv)) * g_ref[...]
        h_ref[...] = (h * (1.0 + sc_ref[0]) + sh_ref[0]).astype(BF16)

    return pl.pallas_call(
        body, name=name, grid=(t // tm,),
        out_shape=jax.ShapeDtypeStruct((t, d), BF16),
        in_specs=[_row_spec(tm, d), _vec_spec(d), _seq_spec(d, s // tm), _seq_spec(d, s // tm)],
        out_specs=_row_spec(tm, d),
    )(x, g, sc, sh)


def _post_res_norm(x, y, g_post, ga, g_pre, sc, sh, s, name):
    t, d = x.shape
    tm = _tok_tile(s)

    def body(x_ref, y_ref, gp_ref, ga_ref, g_ref, sc_ref, sh_ref, xn_ref, h_ref):
        yv = y_ref[...]
        xn = x_ref[...] + ga_ref[0] * ((yv * _rms(yv)) * gp_ref[...])
        xn_ref[...] = xn
        h = (xn * _rms(xn)) * g_ref[...]
        h_ref[...] = (h * (1.0 + sc_ref[0]) + sh_ref[0]).astype(BF16)

    tps = s // tm
    return pl.pallas_call(
        body, name=name, grid=(t // tm,),
        out_shape=[jax.ShapeDtypeStruct((t, d), F32), jax.ShapeDtypeStruct((t, d), BF16)],
        in_specs=[_row_spec(tm, d), _row_spec(tm, d), _vec_spec(d), _seq_spec(d, tps),
                  _vec_spec(d), _seq_spec(d, tps), _seq_spec(d, tps)],
        out_specs=[_row_spec(tm, d), _row_spec(tm, d)],
    )(x, y, g_post, ga, g_pre, sc, sh)


def _post_res_loss(x, y, g_post, ga, target, s, name):
    t, d = x.shape
    tm = _tok_tile(s)

    def body(x_ref, y_ref, gp_ref, ga_ref, t_ref, dx_ref, loss_ref):
        yv = y_ref[...]
        err = x_ref[...] + ga_ref[0] * ((yv * _rms(yv)) * gp_ref[...]) - t_ref[...]
        dx_ref[...] = err * (1.0 / d)
        part = 0.5 * jnp.sum(jnp.mean(err * err, axis=-1, keepdims=True), axis=0, keepdims=True)

        @pl.when(pl.program_id(0) == 0)
        def _():
            loss_ref[...] = jnp.zeros_like(loss_ref)

        loss_ref[...] += part

    return pl.pallas_call(
        body, name=name, grid=(t // tm,),
        out_shape=[jax.ShapeDtypeStruct((t, d), F32), jax.ShapeDtypeStruct((8, 128), F32)],
        in_specs=[_row_spec(tm, d), _row_spec(tm, d), _vec_spec(d), _seq_spec(d, s // tm), _row_spec(tm, d)],
        out_specs=[_row_spec(tm, d), pl.BlockSpec((8, 128), lambda i: (0, 0))],
    )(x, y, g_post, ga, target)


def _post_bwd(dxn, y, g, ga, s, name):
    t, d = y.shape
    tm = _tok_tile(s)
    tps = s // tm

    def body(dx_ref, y_ref, g_ref, ga_ref, dy_ref, dg_ref, dga_ref):
        i = pl.program_id(0)
        yv = y_ref[...]
        r = _rms(yv)
        yh = yv * r
        dxv = dx_ref[...]
        dn = dxv * ga_ref[0]
        dyh = dn * g_ref[...]
        dy_ref[...] = (r * (dyh - yh * jnp.mean(dyh * yh, axis=-1, keepdims=True))).astype(BF16)

        @pl.when(i == 0)
        def _():
            dg_ref[...] = jnp.zeros_like(dg_ref)

        @pl.when(i % tps == 0)
        def _():
            dga_ref[...] = jnp.zeros_like(dga_ref)

        dg_ref[...] += jnp.sum(dn * yh, axis=0, keepdims=True)
        dga_ref[0] += jnp.sum(dxv * (yh * g_ref[...]), axis=0, keepdims=True)

    return pl.pallas_call(
        body, name=name, grid=(t // tm,),
        out_shape=[jax.ShapeDtypeStruct((t, d), BF16), jax.ShapeDtypeStruct((1, d), F32),
                   jax.ShapeDtypeStruct((t // s, 1, d), F32)],
        in_specs=[_row_spec(tm, d), _row_spec(tm, d), _vec_spec(d), _seq_spec(d, tps)],
        out_specs=[_row_spec(tm, d), _vec_spec(d), _seq_spec(d, tps)],
    )(dxn, y, g, ga)


def _pre_bwd(dh, x, g, sc, dres, s, name):
    t, d = x.shape
    tm = _tok_tile(s)
    tps = s // tm

    def body(dh_ref, x_ref, g_ref, sc_ref, dr_ref, dx_ref, dsh_ref, dsc_ref, dg_ref):
        i = pl.program_id(0)
        xv = x_ref[...]
        r = _rms(xv)
        xh = xv * r
        dhv = dh_ref[...]
        one_sc = 1.0 + sc_ref[0]
        dxh = dhv * one_sc * g_ref[...]
        dx_ref[...] = dr_ref[...] + r * (dxh - xh * jnp.mean(dxh * xh, axis=-1, keepdims=True))

        @pl.when(i == 0)
        def _():
            dg_ref[...] = jnp.zeros_like(dg_ref)

        @pl.when(i % tps == 0)
        def _():
            dsh_ref[...] = jnp.zeros_like(dsh_ref)
            dsc_ref[...] = jnp.zeros_like(dsc_ref)

        dg_ref[...] += jnp.sum(dhv * one_sc * xh, axis=0, keepdims=True)
        dsh_ref[0] += jnp.sum(dhv, axis=0, keepdims=True)
        dsc_ref[0] += jnp.sum(dhv * (xh * g_ref[...]), axis=0, keepdims=True)

    nb = t // s
    return pl.pallas_call(
        body, name=name, grid=(t // tm,),
        out_shape=[jax.ShapeDtypeStruct((t, d), F32), jax.ShapeDtypeStruct((nb, 1, d), F32),
                   jax.ShapeDtypeStruct((nb, 1, d), F32), jax.ShapeDtypeStruct((1, d), F32)],
        in_specs=[_row_spec(tm, d), _row_spec(tm, d), _vec_spec(d), _seq_spec(d, tps), _row_spec(tm, d)],
        out_specs=[_row_spec(tm, d), _seq_spec(d, tps), _seq_spec(d, tps), _vec_spec(d)],
    )(dh, x, g, sc, dres)


def _merge(y_conv, y_att, gates, s, name):
    t, d = y_conv.shape
    tm = _tok_tile(s)

    def body(yc_ref, ya_ref, gc_ref, gt_ref, o_ref):
        f32 = lambda r: r[...].astype(F32)
        o_ref[...] = (_sig(f32(gc_ref)) * f32(yc_ref) + _sig(f32(gt_ref)) * f32(ya_ref)).astype(BF16)

    return pl.pallas_call(
        body, name=name, grid=(t // tm,),
        out_shape=jax.ShapeDtypeStruct((t, d), BF16),
        in_specs=[_row_spec(tm, d), _row_spec(tm, d), pl.BlockSpec((tm, d), lambda i: (i, 0)),
                  pl.BlockSpec((tm, d), lambda i: (i, 1))],
        out_specs=_row_spec(tm, d),
    )(y_conv, y_att, gates, gates)


def _merge_bwd(dm, y_conv, y_att, gates, s, name):
    t, d = y_conv.shape
    tm = _tok_tile(s)

    def body(dm_ref, yc_ref, ya_ref, gc_ref, gt_ref, dyc_ref, dya_ref, dg_ref):
        f32 = lambda r: r[...].astype(F32)
        dmv = f32(dm_ref)
        sc_, st_ = _sig(f32(gc_ref)), _sig(f32(gt_ref))
        dyc_ref[...] = (dmv * sc_).astype(BF16)
        dya_ref[...] = (dmv * st_).astype(BF16)
        dg_ref[:, :d] = (dmv * f32(yc_ref) * (sc_ * (1.0 - sc_))).astype(BF16)
        dg_ref[:, d:] = (dmv * f32(ya_ref) * (st_ * (1.0 - st_))).astype(BF16)

    return pl.pallas_call(
        body, name=name, grid=(t // tm,),
        out_shape=[jax.ShapeDtypeStruct((t, d), BF16), jax.ShapeDtypeStruct((t, d), BF16),
                   jax.ShapeDtypeStruct((t, 2 * d), BF16)],
        in_specs=[_row_spec(tm, d), _row_spec(tm, d), _row_spec(tm, d),
                  pl.BlockSpec((tm, d), lambda i: (i, 0)), pl.BlockSpec((tm, d), lambda i: (i, 1))],
        out_specs=[_row_spec(tm, d), _row_spec(tm, d), _row_spec(tm, 2 * d)],
    )(dm, y_conv, y_att, gates, gates)


def _swiglu(f, s, name):
    t, two = f.shape
    dff = two // 2
    tm = _tok_tile(s)

    def body(g_ref, u_ref, a_ref):
        gv = g_ref[...].astype(F32)
        a_ref[...] = (gv * _sig(gv) * u_ref[...].astype(F32)).astype(BF16)

    return pl.pallas_call(
        body, name=name, grid=(t // tm,),
        out_shape=jax.ShapeDtypeStruct((t, dff), BF16),
        in_specs=[pl.BlockSpec((tm, dff), lambda i: (i, 0)), pl.BlockSpec((tm, dff), lambda i: (i, 1))],
        out_specs=_row_spec(tm, dff),
    )(f, f)


def _swiglu_bwd(da, f, s, name):
    t, two = f.shape
    dff = two // 2
    tm = _tok_tile(s)

    def body(da_ref, g_ref, u_ref, df_ref):
        gv, dav = g_ref[...].astype(F32), da_ref[...].astype(F32)
        sg = _sig(gv)
        df_ref[:, :dff] = (dav * u_ref[...].astype(F32) * (sg * (1.0 + gv * (1.0 - sg)))).astype(BF16)
        df_ref[:, dff:] = (dav * (gv * sg)).astype(BF16)

    return pl.pallas_call(
        body, name=name, grid=(t // tm,),
        out_shape=jax.ShapeDtypeStruct((t, two), BF16),
        in_specs=[_row_spec(tm, dff), pl.BlockSpec((tm, dff), lambda i: (i, 0)),
                  pl.BlockSpec((tm, dff), lambda i: (i, 1))],
        out_specs=_row_spec(tm, two),
    )(da, f, f)


CONV_CHUNK = 64


def _conv_fwd(glu_in, conv_w, conv_b, ln_g, ln_b, nb, s, name):
    t, two = glu_in.shape
    cw = two // 2
    tt = _tile(s, (256, 128, 64))
    spt = s // tt
    ch = min(CONV_CHUNK, tt)

    def body(cur_ref, halo_ref, w_ref, b_ref, g_ref, be_ref, u_ref, cv_ref, ua_ref, ext_ref):
        j = pl.program_id(1)
        cur = cur_ref[...]
        u_cur = cur[:, :cw] * _sig(cur[:, cw:])
        hal = halo_ref[...]
        u_hal = hal[:, :cw] * _sig(hal[:, cw:])
        ext_ref[0:HALO, :] = jnp.where(j == 0, 0.0, u_hal)
        ext_ref[HALO:HALO + tt, :] = u_cur
        u_ref[...] = u_cur
        for c0 in range(0, tt, ch):
            acc = jnp.zeros((ch, cw), F32) + b_ref[...]
            for k in range(CONV_K):
                st = c0 + HALO - (CONV_K - 1) + k
                acc = acc + w_ref[k:k + 1, :] * ext_ref[st:st + ch, :]
            cv_ref[c0:c0 + ch, :] = acc
            mu = jnp.mean(acc, axis=-1, keepdims=True)
            xc = acc - mu
            yv = xc * lax.rsqrt(jnp.mean(xc * xc, axis=-1, keepdims=True) + EPS) * g_ref[...] + be_ref[...]
            ua_ref[c0:c0 + ch, :] = (yv * _sig(yv)).astype(BF16)

    cur_map = lambda b, j: (b * spt + j, 0)
    halo_map = lambda b, j: (jnp.maximum((b * s + j * tt) // HALO - 1, 0), 0)
    vec = lambda d0: pl.BlockSpec((d0, cw), lambda b, j: (0, 0))
    return pl.pallas_call(
        body, name=name, grid=(nb, spt),
        out_shape=[jax.ShapeDtypeStruct((t, cw), F32), jax.ShapeDtypeStruct((t, cw), F32),
                   jax.ShapeDtypeStruct((t, cw), BF16)],
        in_specs=[pl.BlockSpec((tt, two), cur_map), pl.BlockSpec((HALO, two), halo_map),
                  vec(CONV_K), vec(1), vec(1), vec(1)],
        out_specs=[pl.BlockSpec((tt, cw), cur_map)] * 3,
        scratch_shapes=[pltpu.VMEM((tt + HALO, cw), F32)],
    )(glu_in, glu_in, conv_w, conv_b, ln_g, ln_b)


def _convln_bwd(dua, cv, ln_g, ln_b, s, name):
    t, cw = cv.shape
    tm = _tok_tile(s)

    def body(du_ref, cv_ref, g_ref, be_ref, dcv_ref, dg_ref, dbe_ref, db_ref):
        cvv = cv_ref[...]
        mu = jnp.mean(cvv, axis=-1, keepdims=True)
        xc = cvv - mu
        rstd = lax.rsqrt(jnp.mean(xc * xc, axis=-1, keepdims=True) + EPS)
        xh = xc * rstd
        yv = xh * g_ref[...] + be_ref[...]
        sg = _sig(yv)
        dy = du_ref[...] * (sg * (1.0 + yv * (1.0 - sg)))
        dxh = dy * g_ref[...]
        dcv = rstd * (dxh - jnp.mean(dxh, axis=-1, keepdims=True)
                      - xh * jnp.mean(dxh * xh, axis=-1, keepdims=True))
        dcv_ref[...] = dcv

        @pl.when(pl.program_id(0) == 0)
        def _():
            dg_ref[...] = jnp.zeros_like(dg_ref)
            dbe_ref[...] = jnp.zeros_like(dbe_ref)
            db_ref[...] = jnp.zeros_like(db_ref)

        dg_ref[...] += jnp.sum(dy * xh, axis=0, keepdims=True)
        dbe_ref[...] += jnp.sum(dy, axis=0, keepdims=True)
        db_ref[...] += jnp.sum(dcv, axis=0, keepdims=True)

    return pl.pallas_call(
        body, name=name, grid=(t // tm,),
        out_shape=[jax.ShapeDtypeStruct((t, cw), F32)] + [jax.ShapeDtypeStruct((1, cw), F32)] * 3,
        in_specs=[_row_spec(tm, cw), _row_spec(tm, cw), _vec_spec(cw), _vec_spec(cw)],
        out_specs=[_row_spec(tm, cw), _vec_spec(cw), _vec_spec(cw), _vec_spec(cw)],
    )(dua, cv, ln_g, ln_b)


def _conv_bwd(dcv, u, glu_in, conv_w, nb, s, name):
    t, cw = dcv.shape
    tt = _tile(s, (256, 128, 64))
    spt = s // tt
    ch = min(CONV_CHUNK, tt)
    nblk = t // HALO

    def body(d_ref, dn_ref, u_ref, up_ref, glu_ref, w_ref, dglu_ref, dw_ref, dext_ref, uext_ref):
        b, j = pl.program_id(0), pl.program_id(1)
        dcur = d_ref[...]
        dext_ref[0:tt, :] = dcur
        dext_ref[tt:tt + HALO, :] = jnp.where(j == spt - 1, 0.0, dn_ref[...])
        uext_ref[0:HALO, :] = jnp.where(j == 0, 0.0, up_ref[...])
        uext_ref[HALO:HALO + tt, :] = u_ref[...]

        @pl.when((b == 0) & (j == 0))
        def _():
            dw_ref[...] = jnp.zeros_like(dw_ref)

        for c0 in range(0, tt, ch):
            dc = dext_ref[c0:c0 + ch, :]
            du = jnp.zeros((ch, cw), F32)
            for k in range(CONV_K):
                sd = c0 + (CONV_K - 1) - k
                du = du + w_ref[k:k + 1, :] * dext_ref[sd:sd + ch, :]
                su = c0 + HALO - (CONV_K - 1) + k
                dw_ref[k:k + 1, :] += jnp.sum(dc * uext_ref[su:su + ch, :], axis=0, keepdims=True)
            val = glu_ref[c0:c0 + ch, :cw]
            sg = _sig(glu_ref[c0:c0 + ch, cw:])
            dglu_ref[c0:c0 + ch, :cw] = (du * sg).astype(BF16)
            dglu_ref[c0:c0 + ch, cw:] = (du * val * (sg * (1.0 - sg))).astype(BF16)

    cur_map = lambda b, j: (b * spt + j, 0)
    prev_map = lambda b, j: (jnp.maximum((b * s + j * tt) // HALO - 1, 0), 0)
    next_map = lambda b, j: (jnp.minimum((b * s + (j + 1) * tt) // HALO, nblk - 1), 0)
    return pl.pallas_call(
        body, name=name, grid=(nb, spt),
        out_shape=[jax.ShapeDtypeStruct((t, 2 * cw), BF16), jax.ShapeDtypeStruct((HALO, cw), F32)],
        in_specs=[pl.BlockSpec((tt, cw), cur_map), pl.BlockSpec((HALO, cw), next_map),
                  pl.BlockSpec((tt, cw), cur_map), pl.BlockSpec((HALO, cw), prev_map),
                  pl.BlockSpec((tt, 2 * cw), cur_map), pl.BlockSpec((CONV_K, cw), lambda b, j: (0, 0))],
        out_specs=[pl.BlockSpec((tt, 2 * cw), cur_map), pl.BlockSpec((HALO, cw), lambda b, j: (0, 0))],
        scratch_shapes=[pltpu.VMEM((tt + HALO, cw), F32), pltpu.VMEM((tt + HALO, cw), F32)],
    )(dcv, dcv, u, u, glu_in, conv_w)


def _split(x):
    hi = x.astype(BF16)
    return hi, (x - hi.astype(F32)).astype(BF16)


def _tri(upper):
    j = lax.broadcasted_iota(jnp.int32, (KEY_BLK, KEY_BLK), 0)
    s_ = lax.broadcasted_iota(jnp.int32, (KEY_BLK, KEY_BLK), 1)
    return jnp.where(j > s_ if upper else j < s_, 1.0, 0.0).astype(BF16)


def _log_keep(z, causal):
    sp = jnp.maximum(z, 0.0) + jnp.log(1.0 + jnp.exp(-jnp.abs(z)))
    return -sp if causal is None else jnp.where(causal, -sp, 0.0)


def _attn_fwd(q, k, v, name):
    bh, s, dh = q.shape
    tq = _tile(s, (256, 128))
    scale = 1.0 / math.sqrt(dh)

    def body(q_ref, k_ref, v_ref, o_ref, car_ref, acc_ref, run_ref):
        qi = pl.program_id(1)
        qv = q_ref[0]
        lane = lax.broadcasted_iota(jnp.int32, (tq, KEY_BLK), 1)
        row = qi * tq + lax.broadcasted_iota(jnp.int32, (tq, KEY_BLK), 0)
        m_suf = _tri(True)
        nk = (qi * tq + tq) // KEY_BLK
        acc_ref[...] = jnp.zeros_like(acc_ref)
        run_ref[...] = jnp.zeros_like(run_ref)
        car_ref[...] = jnp.zeros_like(car_ref)

        def step(it, _):
            kb = nk - 1 - it
            off = pl.multiple_of(kb * KEY_BLK, KEY_BLK)
            kblk = k_ref[0, pl.ds(off, KEY_BLK), :]
            vblk = v_ref[0, pl.ds(off, KEY_BLK), :]
            z = lax.dot_general(qv, kblk, NT, preferred_element_type=F32) * scale
            causal = (off + lane) < row
            lk = _log_keep(z, causal)
            hi, lo = _split(lk)
            cs = (jnp.dot(hi, m_suf, preferred_element_type=F32)
                  + jnp.dot(lo, m_suf, preferred_element_type=F32))
            run = run_ref[...]
            a = jnp.where(causal, jnp.exp(z + lk + cs + run), 0.0)
            acc_ref[...] += jnp.dot(a.astype(BF16), vblk, preferred_element_type=F32)
            car_ref[0] = jnp.where(lane == kb, run, car_ref[0])
            run_ref[...] = run + jnp.sum(lk, axis=1, keepdims=True)
            return 0

        lax.fori_loop(0, nk, step, 0)
        o_ref[0] = acc_ref[...].astype(BF16)

    full = pl.BlockSpec((1, s, dh), lambda b, i: (b, 0, 0))
    return pl.pallas_call(
        body, name=name, grid=(bh, s // tq),
        out_shape=[jax.ShapeDtypeStruct((bh, s, dh), BF16), jax.ShapeDtypeStruct((bh, s, KEY_BLK), F32)],
        in_specs=[pl.BlockSpec((1, tq, dh), lambda b, i: (b, i, 0)), full, full],
        out_specs=[pl.BlockSpec((1, tq, dh), lambda b, i: (b, i, 0)),
                   pl.BlockSpec((1, tq, KEY_BLK), lambda b, i: (b, i, 0))],
        scratch_shapes=[pltpu.VMEM((tq, dh), F32), pltpu.VMEM((tq, KEY_BLK), F32)],
        compiler_params=pltpu.CompilerParams(dimension_semantics=("parallel", "arbitrary")),
    )(q, k, v)


def _attn_bwd(q, k, v, do, car, name):
    bh, s, dh = q.shape
    tq = _tile(s, (256, 128))
    scale = 1.0 / math.sqrt(dh)
    nkb, nqb = s // KEY_BLK, s // tq

    def body(q_ref, k_ref, v_ref, do_ref, car_ref, dq_ref, dk_ref, dv_ref, dq_acc, g_acc, dk_acc, dv_acc):
        lane = lax.broadcasted_iota(jnp.int32, (tq, KEY_BLK), 1)
        rowi = lax.broadcasted_iota(jnp.int32, (tq, KEY_BLK), 0)
        m_suf, m_pre = _tri(True), _tri(False)
        dq_acc[...] = jnp.zeros_like(dq_acc)
        g_acc[...] = jnp.zeros_like(g_acc)

        def kloop(kb, _):
            off = pl.multiple_of(kb * KEY_BLK, KEY_BLK)
            kblk = k_ref[0, pl.ds(off, KEY_BLK), :]
            vblk = v_ref[0, pl.ds(off, KEY_BLK), :]

            dk_acc[...] = jnp.zeros_like(dk_acc)
            dv_acc[...] = jnp.zeros_like(dv_acc)

            def qloop(qb, _):
                r0 = pl.multiple_of(qb * tq, tq)
                qblk = q_ref[0, pl.ds(r0, tq), :]
                doblk = do_ref[0, pl.ds(r0, tq), :]
                z = lax.dot_general(qblk, kblk, NT, preferred_element_type=F32) * scale
                causal = (off + lane) < (r0 + rowi)
                lk = _log_keep(z, causal)
                hi, lo = _split(lk)
                cs = (jnp.dot(hi, m_suf, preferred_element_type=F32)
                      + jnp.dot(lo, m_suf, preferred_element_type=F32))
                cpast = jnp.sum(jnp.where(lane == kb, car_ref[0, pl.ds(r0, tq), :], 0.0), axis=1, keepdims=True)
                lsig = z + lk
                a = jnp.where(causal, jnp.exp(lsig + cs + cpast), 0.0)
                da = lax.dot_general(doblk, vblk, NT, preferred_element_type=F32)
                g = a * da
                ghi, glo = _split(g)
                gpre = (jnp.dot(ghi, m_pre, preferred_element_type=F32)
                        + jnp.dot(glo, m_pre, preferred_element_type=F32)
                        + g_acc[pl.ds(r0, tq), :])
                dz = jnp.where(causal, g - jnp.exp(lsig) * (g + gpre), 0.0)
                ds_ = (dz * scale).astype(BF16)
                dv_acc[...] += lax.dot_general(a.astype(BF16), doblk, TN, preferred_element_type=F32)
                dk_acc[...] += lax.dot_general(ds_, qblk, TN, preferred_element_type=F32)
                dq_acc[pl.ds(r0, tq), :] += jnp.dot(ds_, kblk, preferred_element_type=F32)
                g_acc[pl.ds(r0, tq), :] += jnp.broadcast_to(jnp.sum(g, axis=1, keepdims=True), (tq, KEY_BLK))
                return 0

            lax.fori_loop(off // tq, nqb, qloop, 0)
            dk_ref[0, pl.ds(off, KEY_BLK), :] = dk_acc[...].astype(BF16)
            dv_ref[0, pl.ds(off, KEY_BLK), :] = dv_acc[...].astype(BF16)
            return 0

        lax.fori_loop(0, nkb, kloop, 0)
        dq_ref[0] = dq_acc[...].astype(BF16)

    full = pl.BlockSpec((1, s, dh), lambda b: (b, 0, 0))
    return pl.pallas_call(
        body, name=name, grid=(bh,),
        out_shape=[jax.ShapeDtypeStruct((bh, s, dh), BF16)] * 3,
        in_specs=[full, full, full, full, pl.BlockSpec((1, s, KEY_BLK), lambda b: (b, 0, 0))],
        out_specs=[full, full, full],
        scratch_shapes=[pltpu.VMEM((s, dh), F32), pltpu.VMEM((s, KEY_BLK), F32),
                        pltpu.VMEM((KEY_BLK, dh), F32), pltpu.VMEM((KEY_BLK, dh), F32)],
        compiler_params=pltpu.CompilerParams(dimension_semantics=("parallel",)),
    )(q, k, v, do, car)


LANES = 128


def _sb_fwd(qkv, nb, s, name):
    t, three_aw = qkv.shape
    aw = three_aw // 3
    npair = aw // LANES
    tq = _tile(s, (256, 128))
    ndiag = tq // KEY_BLK
    spt = s // tq
    scale = 1.0 / math.sqrt(HEAD_DIM)

    def body(q_ref, k_ref, v_ref, o_ref, car_ref, acc_ref, run_ref):
        qi = pl.program_id(2)
        lane = lax.broadcasted_iota(jnp.int32, (tq, LANES), 1)
        col = lax.broadcasted_iota(jnp.int32, (tq, KEY_BLK), 1)
        row = qi * tq + lax.broadcasted_iota(jnp.int32, (tq, KEY_BLK), 0)
        first = lane < HEAD_DIM
        q2 = q_ref[...]
        zero = jnp.zeros_like(q2)
        qh = (jnp.where(first, q2, zero), jnp.where(first, zero, q2))
        m_suf = _tri(True)
        nk = (qi * tq + tq) // KEY_BLK
        acc_ref[...] = jnp.zeros_like(acc_ref)
        run_ref[...] = jnp.zeros_like(run_ref)
        car_ref[...] = jnp.zeros_like(car_ref)

        def step(it, masked):
            kb = nk - 1 - it
            off = pl.multiple_of(kb * KEY_BLK, KEY_BLK)
            kblk = k_ref[pl.ds(off, KEY_BLK), :]
            vblk = v_ref[pl.ds(off, KEY_BLK), :]
            causal = ((off + col) < row) if masked else None
            hs = range(2)
            zs = [lax.dot_general(qh[h], kblk, NT, preferred_element_type=F32) * scale for h in hs]
            lks = [_log_keep(zs[h], causal) for h in hs]
            sp = [_split(lks[h]) for h in hs]
            css = [jnp.dot(sp[h][0], m_suf, preferred_element_type=F32)
                   + jnp.dot(sp[h][1], m_suf, preferred_element_type=F32) for h in hs]
            runs = [run_ref[h] for h in hs]
            aa = [jnp.exp(zs[h] + lks[h] + css[h] + jnp.concatenate([runs[h]] * (KEY_BLK // LANES), axis=1))
                  for h in hs]
            if masked:
                aa = [jnp.where(causal, aa[h], 0.0) for h in hs]
            for h in hs:
                acc_ref[h] += jnp.dot(aa[h].astype(BF16), vblk, preferred_element_type=F32)
            for h in hs:
                car_ref[h] = jnp.where(lane == kb, runs[h], car_ref[h])
                run_ref[h] = runs[h] + jnp.sum(lks[h], axis=1, keepdims=True)

        for it in range(ndiag):
            step(it, True)

        def rest(it, _):
            step(it, False)
            return 0

        lax.fori_loop(ndiag, nk, rest, 0)
        o_ref[...] = jnp.where(first, acc_ref[0], acc_ref[1]).astype(BF16)

    return pl.pallas_call(
        body, name=name, grid=(nb, npair, spt),
        out_shape=[jax.ShapeDtypeStruct((t, aw), BF16),
                   jax.ShapeDtypeStruct((nb * npair * 2, s, LANES), F32)],
        in_specs=[pl.BlockSpec((tq, LANES), lambda b, p, i: (b * spt + i, p)),
                  pl.BlockSpec((s, LANES), lambda b, p, i: (b, npair + p)),
                  pl.BlockSpec((s, LANES), lambda b, p, i: (b, 2 * npair + p))],
        out_specs=[pl.BlockSpec((tq, LANES), lambda b, p, i: (b * spt + i, p)),
                   pl.BlockSpec((2, tq, LANES), lambda b, p, i: (b * npair + p, i, 0))],
        scratch_shapes=[pltpu.VMEM((2, tq, LANES), F32), pltpu.VMEM((2, tq, LANES), F32)],
        compiler_params=pltpu.CompilerParams(dimension_semantics=("parallel", "parallel", "arbitrary")),
    )(qkv, qkv, qkv)


def _sb_bwd(qkv, do, car, nb, s, name):
    t, three_aw = qkv.shape
    aw = three_aw // 3
    npair = aw // LANES
    tq = _tile(s, (256, 128))
    scale = 1.0 / math.sqrt(HEAD_DIM)
    nkb, nqb = s // KEY_BLK, s // tq

    def body(q_ref, k_ref, v_ref, do_ref, car_ref, dq_ref, dk_ref, dv_ref, dq_acc, g_acc, dk_acc, dv_acc):
        lane = lax.broadcasted_iota(jnp.int32, (tq, LANES), 1)
        col = lax.broadcasted_iota(jnp.int32, (tq, KEY_BLK), 1)
        rowi = lax.broadcasted_iota(jnp.int32, (tq, KEY_BLK), 0)
        first_k = lax.broadcasted_iota(jnp.int32, (KEY_BLK, LANES), 1) < HEAD_DIM
        m_suf, m_pre = _tri(True), _tri(False)
        dq_acc[...] = jnp.zeros_like(dq_acc)
        g_acc[...] = jnp.zeros_like(g_acc)

        def kloop(kb, _):
            off = pl.multiple_of(kb * KEY_BLK, KEY_BLK)
            kblk = k_ref[pl.ds(off, KEY_BLK), :]
            vblk = v_ref[pl.ds(off, KEY_BLK), :]
            zero = jnp.zeros_like(kblk)
            kh = (jnp.where(first_k, kblk, zero), jnp.where(first_k, zero, kblk))
            vh = (jnp.where(first_k, vblk, zero), jnp.where(first_k, zero, vblk))
            dk_acc[...] = jnp.zeros_like(dk_acc)
            dv_acc[...] = jnp.zeros_like(dv_acc)

            def tile(qb, masked):
                r0 = pl.multiple_of(qb * tq, tq)
                q2 = q_ref[pl.ds(r0, tq), :]
                do2 = do_ref[pl.ds(r0, tq), :]
                causal = ((off + col) < (r0 + rowi)) if masked else None
                hs = range(2)
                wide = lambda x: jnp.concatenate([x] * (KEY_BLK // LANES), axis=1)
                zs = [lax.dot_general(q2, kh[h], NT, preferred_element_type=F32) * scale for h in hs]
                das = [lax.dot_general(do2, vh[h], NT, preferred_element_type=F32) for h in hs]
                lks = [_log_keep(zs[h], causal) for h in hs]
                sp = [_split(lks[h]) for h in hs]
                css = [jnp.dot(sp[h][0], m_suf, preferred_element_type=F32)
                       + jnp.dot(sp[h][1], m_suf, preferred_element_type=F32) for h in hs]
                cpast = [jnp.sum(jnp.where(lane == kb, car_ref[h, pl.ds(r0, tq), :], 0.0), axis=1, keepdims=True)
                         for h in hs]
                lsig = [zs[h] + lks[h] for h in hs]
                aa = [jnp.exp(lsig[h] + css[h] + cpast[h]) for h in hs]
                if masked:
                    aa = [jnp.where(causal, aa[h], 0.0) for h in hs]
                gs = [aa[h] * das[h] for h in hs]
                gsp = [_split(gs[h]) for h in hs]
                gpre = [jnp.dot(gsp[h][0], m_pre, preferred_element_type=F32)
                        + jnp.dot(gsp[h][1], m_pre, preferred_element_type=F32)
                        + wide(g_acc[h, pl.ds(r0, tq), :]) for h in hs]
                dzs = [gs[h] - jnp.exp(lsig[h]) * (gs[h] + gpre[h]) for h in hs]
                if masked:
                    dzs = [jnp.where(causal, dzs[h], 0.0) for h in hs]
                dss = [(dzs[h] * scale).astype(BF16) for h in hs]
                for h in hs:
                    dv_acc[h] += lax.dot_general(aa[h].astype(BF16), do2, TN, preferred_element_type=F32)
                for h in hs:
                    dk_acc[h] += lax.dot_general(dss[h], q2, TN, preferred_element_type=F32)
                for h in hs:
                    dq_acc[h, pl.ds(r0, tq), :] += jnp.dot(dss[h], kblk, preferred_element_type=F32)
                for h in hs:
                    g_acc[h, pl.ds(r0, tq), :] += jnp.broadcast_to(jnp.sum(gs[h], axis=1, keepdims=True),
                                                                   (tq, LANES))

            q0 = off // tq
            tile(q0, True)

            def rest(qb, _):
                tile(qb, False)
                return 0

            lax.fori_loop(q0 + 1, nqb, rest, 0)
            dk_ref[pl.ds(off, KEY_BLK), :] = jnp.where(first_k, dk_acc[0], dk_acc[1]).astype(BF16)
            dv_ref[pl.ds(off, KEY_BLK), :] = jnp.where(first_k, dv_acc[0], dv_acc[1]).astype(BF16)
            return 0

        lax.fori_loop(0, nkb, kloop, 0)
        first_q = lax.broadcasted_iota(jnp.int32, (s, LANES), 1) < HEAD_DIM
        dq_ref[...] = jnp.where(first_q, dq_acc[0], dq_acc[1]).astype(BF16)

    col = lambda j: pl.BlockSpec((s, LANES), lambda b, p: (b, j * npair + p))
    return pl.pallas_call(
        body, name=name, grid=(nb, npair),
        out_shape=[jax.ShapeDtypeStruct((t, aw), BF16)] * 3,
        in_specs=[col(0), col(1), col(2), col(0),
                  pl.BlockSpec((2, s, LANES), lambda b, p: (b * npair + p, 0, 0))],
        out_specs=[col(0), col(0), col(0)],
        scratch_shapes=[pltpu.VMEM((2, s, LANES), F32), pltpu.VMEM((2, s, LANES), F32),
                        pltpu.VMEM((2, KEY_BLK, LANES), F32), pltpu.VMEM((2, KEY_BLK, LANES), F32)],
        compiler_params=pltpu.CompilerParams(dimension_semantics=("parallel", "parallel")),
    )(qkv, qkv, qkv, do, car)


HALF_Q = 256
DEAD = -105.0
SKIPPED = -1e30


def _any_live(x):
    return (jnp.max(x) > DEAD).astype(jnp.int32)


def _cumsum_mat(upper):
    m = _tri(upper)
    return jnp.concatenate([m, m], axis=0)


def _cum(x, mat):
    hi, lo = _split(x)
    return jnp.dot(jnp.concatenate([hi, lo], axis=1), mat, preferred_element_type=F32)


def _wide(x):
    return jnp.concatenate([x] * (KEY_BLK // LANES), axis=1)


def _sb4_fwd(qkv, nb, s, name, gather=()):
    t, three_aw = qkv.shape
    aw = three_aw // 3
    npair = aw // LANES
    tq = 2 * HALF_Q
    spt = s // tq
    scale = 1.0 / math.sqrt(HEAD_DIM)

    ng = len(gather)

    def body(*refs):
        q_ref, k_ref, v_ref = refs[:3]
        o_ref, car_ref = refs[3 + ng:5 + ng]
        acc_ref, run_ref = refs[5 + 2 * ng:7 + 2 * ng]
        if ng:
            copies = _gather_copies(refs[3:3 + ng], refs[5 + ng:5 + 2 * ng], *refs[7 + 2 * ng:])
            step_id = (pl.program_id(0) * npair + pl.program_id(1)) * spt + pl.program_id(2)
            pl.when(step_id == 0)(lambda: _start_copies(copies))
        qi = pl.program_id(2)
        lane = lax.broadcasted_iota(jnp.int32, (HALF_Q, LANES), 1)
        col = lax.broadcasted_iota(jnp.int32, (HALF_Q, KEY_BLK), 1)
        rowi = lax.broadcasted_iota(jnp.int32, (HALF_Q, KEY_BLK), 0)
        first = lane < HEAD_DIM
        mat = _cumsum_mat(True)
        qs = []
        for half in range(2):
            q2 = q_ref[half * HALF_Q:(half + 1) * HALF_Q, :] * scale
            zero = jnp.zeros_like(q2)
            qs.append((jnp.where(first, q2, zero), jnp.where(first, zero, q2)))
        acc_ref[...] = jnp.zeros_like(acc_ref)
        run_ref[...] = jnp.zeros_like(run_ref)
        car_ref[...] = jnp.zeros_like(car_ref)
        nk = (qi * tq + tq) // KEY_BLK

        def step(kb, halves):
            off = pl.multiple_of(kb * KEY_BLK, KEY_BLK)
            kblk = k_ref[pl.ds(off, KEY_BLK), :]
            vblk = v_ref[pl.ds(off, KEY_BLK), :]
            chains = [(half, masked, h) for half, masked in halves for h in range(2)]
            causal = {half: ((off + col) < (qi * tq + half * HALF_Q + rowi)) if masked else None
                      for half, masked in halves}
            zs = [lax.dot_general(qs[half][h], kblk, NT, preferred_element_type=F32) for half, _, h in chains]
            lks = [_log_keep(z, causal[c[0]]) for z, c in zip(zs, chains)]
            css = [_cum(lk, mat) for lk in lks]
            runs = [run_ref[2 * half + h] for half, _, h in chains]
            aa = [jnp.exp(z + lk + cs + _wide(run)) for z, lk, cs, run in zip(zs, lks, css, runs)]
            aa = [jnp.where(causal[c[0]], a, 0.0) if c[1] else a for a, c in zip(aa, chains)]
            for a, (half, _, h) in zip(aa, chains):
                acc_ref[2 * half + h] += jnp.dot(a.astype(BF16), vblk, preferred_element_type=F32)
            for lk, run, (half, _, h) in zip(lks, runs, chains):
                rows = pl.ds(half * HALF_Q, HALF_Q)
                car_ref[h, rows, :] = jnp.where(lane == kb, run, car_ref[h, rows, :])
                run_ref[2 * half + h] = run + jnp.sum(lk, axis=1, keepdims=True)

        step(nk - 1, ((1, True),))
        step(nk - 2, ((0, True), (1, False)))

        def more(c):
            return (c[0] < nk) & (c[1] > 0)

        def rest(c):
            step(nk - 1 - c[0], ((0, False), (1, False)))
            return c[0] + 1, _any_live(run_ref[...])

        visited, _ = lax.while_loop(more, rest, (jnp.int32(2), _any_live(run_ref[...])))
        unvisited = lane < (nk - visited)
        for h in range(2):
            car_ref[h] = jnp.where(jnp.concatenate([unvisited, unvisited], axis=0), SKIPPED, car_ref[h])
        for half in range(2):
            o_ref[half * HALF_Q:(half + 1) * HALF_Q, :] = jnp.where(
                first, acc_ref[2 * half], acc_ref[2 * half + 1]).astype(BF16)
        if ng:
            pl.when(step_id == nb * npair * spt - 1)(lambda: _finish_copies(copies))

    hbm = pl.BlockSpec(memory_space=pl.ANY)
    outs = pl.pallas_call(
        body, name=name, grid=(nb, npair, spt),
        out_shape=[jax.ShapeDtypeStruct((t, aw), BF16),
                   jax.ShapeDtypeStruct((nb * npair * 2, s, LANES), F32)] + _gather_out_shapes(gather),
        in_specs=[pl.BlockSpec((tq, LANES), lambda b, p, i: (b * spt + i, p)),
                  pl.BlockSpec((s, LANES), lambda b, p, i: (b, npair + p)),
                  pl.BlockSpec((s, LANES), lambda b, p, i: (b, 2 * npair + p))] + [hbm] * ng,
        out_specs=[pl.BlockSpec((tq, LANES), lambda b, p, i: (b * spt + i, p)),
                   pl.BlockSpec((2, tq, LANES), lambda b, p, i: (b * npair + p, i, 0))] + [hbm] * ng,
        scratch_shapes=[pltpu.VMEM((4, HALF_Q, LANES), F32), pltpu.VMEM((4, HALF_Q, LANES), F32)]
        + (_copy_sems(ng) if ng else []),
        compiler_params=pltpu.CompilerParams(dimension_semantics=("arbitrary", "arbitrary", "arbitrary")),
    )(qkv, qkv, qkv, *gather)
    return outs[0], outs[1], list(outs[2:])


def _sb4_bwd(qkv, do, car, nb, s, name, exchange=()):
    t, three_aw = qkv.shape
    aw = three_aw // 3
    npair = aw // LANES
    tq = HALF_Q
    scale = 1.0 / math.sqrt(HEAD_DIM)
    nkp, nqb = s // (2 * KEY_BLK), s // tq
    nx = len(exchange)

    def body(*refs):
        q_ref, k_ref, v_ref, do_ref, car_ref = refs[:5]
        dq_ref, dk_ref, dv_ref = refs[5 + nx:8 + nx]
        dq_acc, g_acc, dk_acc, dv_acc = refs[8 + 2 * nx:12 + 2 * nx]
        if nx:
            copies = _exchange_copies(refs[5:5 + nx], refs[8 + nx:8 + 2 * nx], *refs[12 + 2 * nx:])
            step_id = pl.program_id(0) * npair + pl.program_id(1)
            pl.when(step_id == 0)(lambda: _start_copies(copies))
        lane = lax.broadcasted_iota(jnp.int32, (tq, LANES), 1)
        col = lax.broadcasted_iota(jnp.int32, (tq, KEY_BLK), 1)
        rowi = lax.broadcasted_iota(jnp.int32, (tq, KEY_BLK), 0)
        first_k = lax.broadcasted_iota(jnp.int32, (KEY_BLK, LANES), 1) < HEAD_DIM
        m_suf, m_pre = _cumsum_mat(True), _tri(False)
        dq_acc[...] = jnp.zeros_like(dq_acc)
        g_acc[...] = jnp.zeros_like(g_acc)

        def kloop(kp, _):
            offs = [pl.multiple_of((2 * kp + blk) * KEY_BLK, KEY_BLK) for blk in range(2)]
            kblk = [k_ref[pl.ds(off, KEY_BLK), :] for off in offs]
            vblk = [v_ref[pl.ds(off, KEY_BLK), :] for off in offs]
            zero = jnp.zeros_like(kblk[0])
            kh = [(jnp.where(first_k, kb_, zero), jnp.where(first_k, zero, kb_)) for kb_ in kblk]
            vh = [(jnp.where(first_k, vb_, zero), jnp.where(first_k, zero, vb_)) for vb_ in vblk]
            k_scaled = jnp.concatenate([kblk[0] * scale, kblk[1] * scale], axis=0)
            dk_acc[...] = jnp.zeros_like(dk_acc)
            dv_acc[...] = jnp.zeros_like(dv_acc)

            def tile(qb, blocks):
                r0 = pl.multiple_of(qb * tq, tq)
                rows = pl.ds(r0, tq)
                q2 = q_ref[rows, :] * scale
                do2 = do_ref[rows, :]
                chains = [(blk, masked, h) for blk, masked in blocks for h in range(2)]
                causal = {blk: ((offs[blk] + col) < (r0 + rowi)) if masked else None for blk, masked in blocks}
                zs = [lax.dot_general(q2, kh[blk][h], NT, preferred_element_type=F32) for blk, _, h in chains]
                das = [lax.dot_general(do2, vh[blk][h], NT, preferred_element_type=F32) for blk, _, h in chains]
                lks = [_log_keep(z, causal[c[0]]) for z, c in zip(zs, chains)]
                css = [_cum(lk, m_suf) for lk in lks]
                cars = [car_ref[h, rows, :] for h in range(2)]
                cpast = [jnp.sum(jnp.where(lane == 2 * kp + blk, cars[h], 0.0), axis=1, keepdims=True)
                         for blk, _, h in chains]
                lsig = [z + lk for z, lk in zip(zs, lks)]
                aa = [jnp.exp(ls + cs + cp) for ls, cs, cp in zip(lsig, css, cpast)]
                aa = [jnp.where(causal[c[0]], a, 0.0) if c[1] else a for a, c in zip(aa, chains)]
                gs = [a * da for a, da in zip(aa, das)]
                gsum = [jnp.sum(g, axis=1, keepdims=True) for g in gs]
                gin = [jnp.dot(g.astype(BF16), m_pre, preferred_element_type=F32) for g in gs]
                gold = [_wide(g_acc[h, rows, :]) for h in range(2)]
                gpre = []
                for i, (blk, _, h) in enumerate(chains):
                    left = gold[h] if blk == 0 else gold[h] + gsum[h]
                    gpre.append(gin[i] + left)
                dzs = [g - jnp.exp(ls) * (g + gp) for g, ls, gp in zip(gs, lsig, gpre)]
                dzs = [jnp.where(causal[c[0]], dz, 0.0) if c[1] else dz for dz, c in zip(dzs, chains)]
                dss = [dz.astype(BF16) for dz in dzs]
                for a, (blk, _, h) in zip(aa, chains):
                    dv_acc[2 * blk + h] += lax.dot_general(a.astype(BF16), do2, TN, preferred_element_type=F32)
                for ds_, (blk, _, h) in zip(dss, chains):
                    dk_acc[2 * blk + h] += lax.dot_general(ds_, q2, TN, preferred_element_type=F32)
                for h in range(2):
                    if len(blocks) == 2:
                        both = jnp.concatenate([dss[h], dss[2 + h]], axis=1)
                        dq_acc[h, rows, :] += jnp.dot(both, k_scaled, preferred_element_type=F32)
                        tot = gsum[h] + gsum[2 + h]
                    else:
                        dq_acc[h, rows, :] += jnp.dot(dss[h], k_scaled[:KEY_BLK], preferred_element_type=F32)
                        tot = gsum[h]
                    g_acc[h, rows, :] += jnp.broadcast_to(tot, (tq, LANES))

            tile(2 * kp, ((0, True),))
            tile(2 * kp + 1, ((0, False), (1, True)))

            def rest(qb, _):
                rows = pl.ds(pl.multiple_of(qb * tq, tq), tq)
                near = jnp.where(lane == 2 * kp + 1, jnp.maximum(car_ref[0, rows, :], car_ref[1, rows, :]), SKIPPED)
                pl.when(_any_live(near) > 0)(lambda: tile(qb, ((0, False), (1, False))))
                return 0

            lax.fori_loop(2 * kp + 2, nqb, rest, 0)
            for blk in range(2):
                dk_ref[pl.ds(offs[blk], KEY_BLK), :] = jnp.where(
                    first_k, dk_acc[2 * blk], dk_acc[2 * blk + 1]).astype(BF16)
                dv_ref[pl.ds(offs[blk], KEY_BLK), :] = jnp.where(
                    first_k, dv_acc[2 * blk], dv_acc[2 * blk + 1]).astype(BF16)
            return 0

        lax.fori_loop(0, nkp, kloop, 0)
        first_q = lax.broadcasted_iota(jnp.int32, (s, LANES), 1) < HEAD_DIM
        dq_ref[...] = jnp.where(first_q, dq_acc[0], dq_acc[1]).astype(BF16)
        if nx:
            pl.when(step_id == nb * npair - 1)(lambda: _finish_copies(copies))

    colspec = lambda j: pl.BlockSpec((s, LANES), lambda b, p: (b, j * npair + p))
    hbm = pl.BlockSpec(memory_space=pl.ANY)
    outs = pl.pallas_call(
        body, name=name, grid=(nb, npair),
        out_shape=[jax.ShapeDtypeStruct((t, aw), BF16)] * 3 + _exchange_out_shapes(exchange),
        in_specs=[colspec(0), colspec(1), colspec(2), colspec(0),
                  pl.BlockSpec((2, s, LANES), lambda b, p: (b * npair + p, 0, 0))] + [hbm] * nx,
        out_specs=[colspec(0), colspec(0), colspec(0)] + [hbm] * nx,
        scratch_shapes=[pltpu.VMEM((2, s, LANES), F32), pltpu.VMEM((2, s, LANES), F32),
                        pltpu.VMEM((4, KEY_BLK, LANES), F32), pltpu.VMEM((4, KEY_BLK, LANES), F32)]
        + (_copy_sems(nx) if nx else []),
        compiler_params=pltpu.CompilerParams(dimension_semantics=("arbitrary", "arbitrary")),
    )(qkv, qkv, qkv, do, car, *exchange)
    return outs[0], outs[1], outs[2], list(outs[3:])


def _ada_fwd(c_all, ada_w, ada_b_loc, name):
    nl, d, n6 = ada_w.shape
    nb = c_all.shape[0]

    def body(c_ref, w_ref, b_ref, o_ref):
        cv = c_ref[...]
        o_ref[0] = jnp.dot(cv * _sig(cv), w_ref[0], preferred_element_type=F32) + b_ref[0]

    return pl.pallas_call(
        body, name=name, grid=(nl,),
        out_shape=jax.ShapeDtypeStruct((nl, nb, n6), F32),
        in_specs=[pl.BlockSpec((nb, d), lambda l: (0, 0)), pl.BlockSpec((1, d, n6), lambda l: (l, 0, 0)),
                  pl.BlockSpec((1, 1, n6), lambda l: (l, 0, 0))],
        out_specs=pl.BlockSpec((1, nb, n6), lambda l: (l, 0, 0)),
    )(c_all, ada_w, ada_b_loc)


def _ada_bwd(c_all_t, dmod_loc, dmod_all, name):
    nl, nb, n6 = dmod_loc.shape
    d = c_all_t.shape[0]
    n_all = dmod_all.shape[2]

    def body(c_ref, dl_ref, da_ref, gw_ref, gb_ref):
        cv = c_ref[...]
        gw_ref[0] = jnp.dot(cv * _sig(cv), dl_ref[0], preferred_element_type=F32)
        gb_ref[0] = jnp.sum(da_ref[0], axis=0, keepdims=True)

    return pl.pallas_call(
        body, name=name, grid=(nl,),
        out_shape=[jax.ShapeDtypeStruct((nl, d, n6), F32), jax.ShapeDtypeStruct((nl, 1, n_all), F32)],
        in_specs=[pl.BlockSpec((d, nb), lambda l: (0, 0)), pl.BlockSpec((1, nb, n6), lambda l: (l, 0, 0)),
                  pl.BlockSpec((1, nb, n_all), lambda l: (l, 0, 0))],
        out_specs=[pl.BlockSpec((1, d, n6), lambda l: (l, 0, 0)), pl.BlockSpec((1, 1, n_all), lambda l: (l, 0, 0))],
    )(c_all_t, dmod_loc, dmod_all)


def _adamw(w, g, m, v, name):
    shape = w.shape
    cols = shape[-1]
    rows = w.size // cols
    tr = _tile(rows, (512, 256, 128, 64, 32, 16, 8))
    flat = lambda a: a.reshape(rows, cols)

    def body(w_ref, g_ref, m_ref, v_ref, d_ref, mo_ref, vo_ref):
        gv = g_ref[...]
        mn = ADAM_B1 * m_ref[...] + (1.0 - ADAM_B1) * gv
        vn = ADAM_B2 * v_ref[...] + (1.0 - ADAM_B2) * (gv * gv)
        m_hat = mn / (1.0 - ADAM_B1 ** ADAM_STEP)
        v_hat = vn / (1.0 - ADAM_B2 ** ADAM_STEP)
        d_ref[...] = -ADAM_LR * (m_hat / (jnp.sqrt(v_hat) + ADAM_EPS) + ADAM_WD * w_ref[...])
        mo_ref[...] = mn
        vo_ref[...] = vn

    spec = pl.BlockSpec((tr, cols), lambda i: (i, 0))
    outs = pl.pallas_call(
        body, name=name, grid=(rows // tr,),
        out_shape=[jax.ShapeDtypeStruct((rows, cols), F32)] * 3,
        in_specs=[spec] * 4, out_specs=[spec] * 3,
    )(flat(w), flat(g), flat(m), flat(v))
    return tuple(o.reshape(shape) for o in outs)


def _pack_rows(parts, d):
    flat = jnp.concatenate([p.reshape(-1) for p in parts])
    rows = -(-flat.size // d)
    rows = -(-rows // 8) * 8
    return jnp.pad(flat, (0, rows * d - flat.size)).reshape(rows, d)


def _to_heads(a, nb, s):
    h = a.shape[1] // HEAD_DIM
    return a.reshape(nb, s, h, HEAD_DIM).transpose(0, 2, 1, 3).reshape(nb * h, s, HEAD_DIM)


def _from_heads(a, nb, s):
    h = a.shape[0] // nb
    return a.reshape(nb, h, s, HEAD_DIM).transpose(0, 2, 1, 3).reshape(nb * s, h * HEAD_DIM)


def kernel(x, c, ada_w, ada_b, pre_mix_g, post_mix_g, pre_ffn_g, post_ffn_g, w_in, conv_w, conv_b, conv_ln_g, conv_ln_b, w_conv_out, w_att_out, w_o, w_ffn_in, w_ffn_out, loss_target, m_ada_w, m_ada_b, m_pre_mix_g, m_post_mix_g, m_pre_ffn_g, m_post_ffn_g, m_w_in, m_conv_w, m_conv_b, m_conv_ln_g, m_conv_ln_b, m_w_conv_out, m_w_att_out, m_w_o, m_w_ffn_in, m_w_ffn_out, v_ada_w, v_ada_b, v_pre_mix_g, v_post_mix_g, v_pre_ffn_g, v_post_ffn_g, v_w_in, v_conv_w, v_conv_b, v_conv_ln_g, v_conv_ln_b, v_w_conv_out, v_w_att_out, v_w_o, v_w_ffn_in, v_w_ffn_out):
    nb, s, d = x.shape
    nl = ada_w.shape[0]
    cw = conv_b.shape[1]
    aw = w_att_out.shape[1]
    cl = conv_w.shape[2]
    n6 = ada_w.shape[2]
    t = nb * s
    me = _my_index()

    conv_flat = conv_w.reshape(-1)
    p1 = _pack_rows([c, conv_flat], d)
    r1 = p1.shape[0]
    (p1_all,) = _all_gather([p1], "gather_c_convw", True)
    p1_all = p1_all.reshape(N_DEV, r1 * d)
    c_all = p1_all[:, :nb * d].reshape(N_DEV * nb, d)
    conv_w_all = p1_all[:, nb * d:nb * d + conv_flat.size].reshape(N_DEV, nl, CONV_K, cl)
    conv_w_all = conv_w_all.transpose(1, 2, 0, 3).reshape(nl, CONV_K, cw)

    ada_b_loc = lax.dynamic_slice_in_dim(ada_b, me * n6, n6, axis=1).reshape(nl, 1, n6)
    mod_cols = _ada_fwd(c_all, ada_w, ada_b_loc, "ada_fwd")
    (mod_g,) = _all_gather([mod_cols.reshape(-1, d)], "gather_mod", True)
    mod_g = mod_g.reshape(N_DEV, nl, N_DEV * nb, n6)
    mod_mine = lax.dynamic_slice_in_dim(mod_g, me * nb, nb, axis=2)
    mod = mod_mine.transpose(1, 2, 0, 3).reshape(nl, nb, 6 * d)
    mods = [[mod[l, :, i * d:(i + 1) * d].reshape(nb, 1, d) for i in range(6)] for l in range(nl)]

    tr_ = lambda w: jnp.swapaxes(w, 0, 1).astype(BF16)
    shards = [[tr_(w_in[l]), tr_(w_conv_out[l]), tr_(w_att_out[l]), w_o[l].astype(BF16),
               tr_(w_ffn_in[l]), w_ffn_out[l].astype(BF16)] for l in range(nl)]
    weights = [list(_all_gather(shards[0][:3], "gather_weights", False))]

    vec = lambda a, l: a[l].reshape(1, -1)
    x2 = x.reshape(t, d)
    tgt = loss_target.reshape(t, d)

    saved = []
    xin = x2
    h = _norm_mod(xin, vec(pre_mix_g, 0), mods[0][1], mods[0][0], s, "norm_mod_0")
    for l in range(nl):
        win_t, wconv_t, watt_t = weights[l][:3]
        sh1, sc1, ga1, sh2, sc2, ga2 = mods[l]
        glu_in = _mm(h, win_t, NT, F32, f"proj_glu_{l}", 0, 2 * cw)
        qkv = _mm(h, win_t, NT, BF16, f"proj_qkv_{l}", 2 * cw, 3 * aw)
        gates = _mm(h, win_t, NT, BF16, f"proj_gates_{l}", 2 * cw + 3 * aw, 2 * d)
        u, cv, ua = _conv_fwd(glu_in, conv_w_all[l], vec(conv_b, l), vec(conv_ln_g, l), vec(conv_ln_b, l),
                              nb, s, f"conv_fwd_{l}")
        o, car, late = _sb4_fwd(qkv, nb, s, f"attn_fwd_{l}", shards[l][3:])
        weights[l] = weights[l] + late
        wo, wffn_in_t, wffn_out = late
        y_conv = _mm(ua, wconv_t, NT, BF16, f"conv_out_{l}")
        y_att = _mm(o, watt_t, NT, BF16, f"att_out_{l}")
        merged = _merge(y_conv, y_att, gates, s, f"merge_{l}")
        y = _mm(merged, wo, NN, F32, f"w_o_{l}")
        x1, h2 = _post_res_norm(xin, y, vec(post_mix_g, l), ga1, vec(pre_ffn_g, l), sc2, sh2, s, f"post_mix_{l}")
        if l + 1 < nl:
            f, early = _mm(h2, wffn_in_t, NT, BF16, f"ffn_in_{l}", gather=shards[l + 1][:3])
            weights.append(early)
        else:
            f = _mm(h2, wffn_in_t, NT, BF16, f"ffn_in_{l}")
        a = _swiglu(f, s, f"swiglu_{l}")
        y2 = _mm(a, wffn_out, NN, F32, f"ffn_out_{l}")
        saved.append(dict(xin=xin, h=h, glu_in=glu_in, gates=gates, u=u, cv=cv, ua=ua, qkv=qkv,
                          car=car, o=o, y_conv=y_conv, y_att=y_att, merged=merged, y=y, x1=x1, h2=h2, f=f,
                          a=a, y2=y2))
        if l + 1 < nl:
            nsh1, nsc1 = mods[l + 1][0], mods[l + 1][1]
            xin, h = _post_res_norm(x1, y2, vec(post_ffn_g, l), ga2, vec(pre_mix_g, l + 1), nsc1, nsh1, s,
                                    f"post_ffn_{l}")
        else:
            dx, loss_acc = _post_res_loss(x1, y2, vec(post_ffn_g, l), ga2, tgt, s, "loss")

    slots = [None] * (6 * nl)
    small = [None] * nl
    for l in reversed(range(nl)):
        win_t, wconv_t, watt_t, wo, wffn_in_t, wffn_out = weights[l]
        sh1, sc1, ga1, sh2, sc2, ga2 = mods[l]
        sv = saved[l]
        dy2, dg_post_ffn, dga2 = _post_bwd(dx, sv["y2"], vec(post_ffn_g, l), ga2, s, f"post_ffn_bwd_{l}")
        d_wffn_out = _mm(sv["a"], dy2, TN, BF16, f"d_w_ffn_out_{l}")
        da = _mm(dy2, wffn_out, NT, BF16, f"d_a_{l}")
        df = _swiglu_bwd(da, sv["f"], s, f"swiglu_bwd_{l}")
        d_wffn_in_t = _mm(df, sv["h2"], TN, BF16, f"d_w_ffn_in_{l}")
        dh2 = _mm(df, wffn_in_t, NN, F32, f"d_h2_{l}")
        dx1, dsh2, dsc2, dg_pre_ffn = _pre_bwd(dh2, sv["x1"], vec(pre_ffn_g, l), sc2, dx, s, f"pre_ffn_bwd_{l}")
        dy, dg_post_mix, dga1 = _post_bwd(dx1, sv["y"], vec(post_mix_g, l), ga1, s, f"post_mix_bwd_{l}")
        d_wo = _mm(sv["merged"], dy, TN, BF16, f"d_w_o_{l}")
        dmerged = _mm(dy, wo, NT, BF16, f"d_merged_{l}")
        dyc, dya, dgates = _merge_bwd(dmerged, sv["y_conv"], sv["y_att"], sv["gates"], s, f"merge_bwd_{l}")
        d_wconv_t = _mm(dyc, sv["ua"], TN, BF16, f"d_w_conv_out_{l}")
        dua = _mm(dyc, wconv_t, NN, F32, f"d_ua_{l}")
        d_watt_t = _mm(dya, sv["o"], TN, BF16, f"d_w_att_out_{l}")
        do = _mm(dya, watt_t, NN, BF16, f"d_o_{l}")
        ready = [(5, d_wffn_out), (4, d_wffn_in_t), (3, d_wo), (1, d_wconv_t), (2, d_watt_t)]
        dq, dk, dv, got = _sb4_bwd(sv["qkv"], do, sv["car"], nb, s, f"attn_bwd_{l}", [g for _, g in ready])
        for (gi, _), r in zip(ready, got):
            slots[6 * l + gi] = r
        dcv, dln_g, dln_b, dconv_b = _convln_bwd(dua, sv["cv"], vec(conv_ln_g, l), vec(conv_ln_b, l), s,
                                                 f"convln_bwd_{l}")
        dglu, dconv_w = _conv_bwd(dcv, sv["u"], sv["glu_in"], conv_w_all[l], nb, s, f"conv_bwd_{l}")
        dproj = jnp.concatenate([dglu, dq, dk, dv, dgates], axis=1)
        d_win_t = _mm(dproj, sv["h"], TN, BF16, f"d_w_in_{l}")
        dh, (slots[6 * l],) = _mm(dproj, win_t, NN, F32, f"d_h_{l}", exchange=[d_win_t])
        dx, dsh1, dsc1, dg_pre_mix = _pre_bwd(dh, sv["xin"], vec(pre_mix_g, l), sc1, dx1, s, f"pre_mix_bwd_{l}")
        dmod = jnp.concatenate([dsh1, dsc1, dga1, dsh2, dsc2, dga2], axis=2)
        small[l] = [dg_pre_mix, dg_post_mix, dg_pre_ffn, dg_post_ffn, dconv_b, dln_g, dln_b, dconv_w, dmod]
    grad_x = dx.reshape(nb, s, d)

    sizes = [a.size for a in small[0]]
    p3 = _pack_rows([a for l in range(nl) for a in small[l]] + [loss_acc[0, :1]], d)
    r3 = p3.shape[0]
    (p3_all,) = _all_gather([p3], "gather_small_grads", True)
    p3_all = p3_all.reshape(N_DEV, r3, d)
    p3_sum = _sum_slots(p3_all, "sum_small_grads").reshape(-1)
    per_layer = sum(sizes)
    loss = p3_sum[nl * per_layer]
    small_sum, dmod_all = [], []
    for l in range(nl):
        off, parts = l * per_layer, []
        for sz in sizes[:-1]:
            parts.append(p3_sum[off:off + sz])
            off += sz
        small_sum.append(parts)
        dm = p3_all.reshape(N_DEV, r3 * d)[:, off:off + sizes[-1]]
        dmod_all.append(dm.reshape(N_DEV * nb, 6 * d))
    dmod_all = jnp.stack(dmod_all)
    dmod_loc = lax.dynamic_slice_in_dim(dmod_all, me * n6, n6, axis=2)
    g_ada_w, g_ada_b = _ada_bwd(c_all.T, dmod_loc, dmod_all, "ada_bwd")
    g_ada_b = g_ada_b.reshape(nl, 6 * d)

    stack_small = lambda i, shape: jnp.stack([small_sum[l][i].reshape(shape) for l in range(nl)])
    g_pre_mix, g_post_mix, g_pre_ffn, g_post_ffn = (stack_small(i, (d,)) for i in range(4))
    g_conv_b, g_ln_g, g_ln_b = (stack_small(i, (cw,)) for i in range(4, 7))
    g_conv_w_all = stack_small(7, (HALO, cw))[:, :CONV_K]
    g_conv_w = lax.dynamic_slice_in_dim(g_conv_w_all, me * cl, cl, axis=2)

    sums = [_sum_slots(r, f"sum_grads_{i}") for i, r in enumerate(slots)]
    un_t = lambda i: jnp.stack([jnp.swapaxes(sums[6 * l + i], 0, 1) for l in range(nl)])
    keep = lambda i: jnp.stack([sums[6 * l + i] for l in range(nl)])
    g_w_in, g_w_conv_out, g_w_att_out, g_w_o, g_w_ffn_in, g_w_ffn_out = (
        un_t(0), un_t(1), un_t(2), keep(3), un_t(4), keep(5))

    grads = [g_ada_w, g_ada_b, g_pre_mix, g_post_mix, g_pre_ffn, g_post_ffn, g_w_in, g_conv_w, g_conv_b,
             g_ln_g, g_ln_b, g_w_conv_out, g_w_att_out, g_w_o, g_w_ffn_in, g_w_ffn_out]
    ws = [ada_w, ada_b, pre_mix_g, post_mix_g, pre_ffn_g, post_ffn_g, w_in, conv_w, conv_b, conv_ln_g,
          conv_ln_b, w_conv_out, w_att_out, w_o, w_ffn_in, w_ffn_out]
    ms = [m_ada_w, m_ada_b, m_pre_mix_g, m_post_mix_g, m_pre_ffn_g, m_post_ffn_g, m_w_in, m_conv_w, m_conv_b,
          m_conv_ln_g, m_conv_ln_b, m_w_conv_out, m_w_att_out, m_w_o, m_w_ffn_in, m_w_ffn_out]
    vs = [v_ada_w, v_ada_b, v_pre_mix_g, v_post_mix_g, v_pre_ffn_g, v_post_ffn_g, v_w_in, v_conv_w, v_conv_b,
          v_conv_ln_g, v_conv_ln_b, v_w_conv_out, v_w_att_out, v_w_o, v_w_ffn_in, v_w_ffn_out]
    deltas, new_m, new_v = [], [], []
    for i, (w, g, m, v) in enumerate(zip(ws, grads, ms, vs)):
        dlt, mn, vn = _adamw(w, g, m, v, f"adamw_{i}")
        deltas.append(dlt)
        new_m.append(mn)
        new_v.append(vn)
    return (loss, grad_x, *grads, *deltas, *new_m, *new_v)
```

```python
import math

import jax
import jax.numpy as jnp
from jax import lax
from jax.experimental import pallas as pl
from jax.experimental.pallas import tpu as pltpu

F32 = jnp.float32
BF16 = jnp.bfloat16
MESH = pl.DeviceIdType.MESH
N_DEV = 8
EPS = 1e-6
HEAD_DIM = 64
CONV_K = 31
HALO = 32
KEY_BLK = 256
ADAM_LR, ADAM_B1, ADAM_B2, ADAM_EPS, ADAM_WD, ADAM_STEP = 0.001, 0.9, 0.999, 1e-08, 0.01, 10

NT = (((1,), (1,)), ((), ()))
NN = (((1,), (0,)), ((), ()))
TN = (((0,), (0,)), ((), ()))


def _tile(n, cands):
    for t in cands:
        if n % t == 0:
            return t
    return n


def _lane_tile(n, cap):
    best = n
    for t in range(128, min(n, cap) + 1, 128):
        if n % t == 0:
            best = t
    return best


def _sig(x):
    return 1.0 / (1.0 + jnp.exp(-x))


def _pos():
    return lax.axis_index("x"), lax.axis_index("y"), lax.axis_index("c")


def _my_index():
    x, y, c = _pos()
    return 4 * x + 2 * y + c


def _all_gather(arrs, name, in_vmem):
    n = len(arrs)

    def body(*refs):
        ins, outs = refs[:n], refs[n:2 * n]
        send_sems, recv_sems, local_sems = refs[2 * n:]
        x, y, c = _pos()
        me, sib = (x, y, c), (x, y, 1 - c)
        chips = [(1 - x, y), (x, 1 - y), (1 - x, 1 - y)]

        def rows(i, p):
            r = ins[i].shape[0]
            return outs[i].at[pl.ds((4 * p[0] + 2 * p[1] + p[2]) * r, r), :]

        def copy(i, k, block, to, src=None):
            return pltpu.make_async_remote_copy(
                src_ref=rows(i, block) if src is None else src, dst_ref=rows(i, block),
                send_sem=send_sems.at[i, k], recv_sem=recv_sems.at[i, k],
                device_id=to, device_id_type=MESH)

        mine = [pltpu.make_async_copy(ins[i], rows(i, me), local_sems.at[i]) for i in range(n)]
        for cp in mine:
            cp.start()
        first = []
        for j, chip in enumerate(chips):
            for i in range(n):
                first.append(copy(i, 1 + j, me, (*chip, c), src=ins[i]))
        for i in range(n):
            first.append(copy(i, 0, me, sib, src=ins[i]))
        for cp in first:
            cp.start()
        passed = []
        for j, chip in enumerate(chips):
            for i in range(n):
                copy(i, 1 + j, (*chip, c), me).wait_recv()
                p = copy(i, 4 + j, (*chip, c), sib)
                p.start()
                passed.append(p)
        for i in range(n):
            copy(i, 0, sib, me).wait_recv()
            for j, chip in enumerate(chips):
                copy(i, 4 + j, (*chip, 1 - c), me).wait_recv()
        for cp in first + passed:
            cp.wait_send()
        for cp in mine:
            cp.wait()

    space = pltpu.VMEM if in_vmem else pl.ANY
    spec = pl.BlockSpec(memory_space=space)
    return pl.pallas_call(
        body, name=name,
        out_shape=[jax.ShapeDtypeStruct((N_DEV * a.shape[0], a.shape[1]), a.dtype) for a in arrs],
        in_specs=[spec] * n, out_specs=[spec] * n,
        scratch_shapes=[pltpu.SemaphoreType.DMA((n, 7)), pltpu.SemaphoreType.DMA((n, 7)),
                        pltpu.SemaphoreType.DMA((n,))],
    )(*arrs)


PEER_ORDER = (4, 2, 6, 5, 3, 7, 1)


def _peer(k):
    x, y, c = _pos()
    return (1 - x if k & 4 else x, 1 - y if k & 2 else y, 1 - c if k & 1 else c)


def _mesh_index(p):
    return 4 * p[0] + 2 * p[1] + p[2]


def _copy_sems(n):
    return [pltpu.SemaphoreType.DMA((n, 7)), pltpu.SemaphoreType.DMA((n, 7)), pltpu.SemaphoreType.DMA((n,))]


def _exchange_out_shapes(arrs):
    return [jax.ShapeDtypeStruct((N_DEV, a.shape[0] // N_DEV, a.shape[1]), a.dtype) for a in arrs]


def _gather_out_shapes(arrs):
    return [jax.ShapeDtypeStruct((N_DEV * a.shape[0], a.shape[1]), a.dtype) for a in arrs]


def _exchange_copies(ins, outs, send_sems, recv_sems, local_sems):
    n = len(ins)

    def block(i, p):
        r = outs[i].shape[1]
        return ins[i].at[pl.ds(_mesh_index(p) * r, r), :]

    mine = [pltpu.make_async_copy(block(i, _peer(0)), outs[i].at[0], local_sems.at[i]) for i in range(n)]
    sends = [pltpu.make_async_remote_copy(
        src_ref=block(i, _peer(k)), dst_ref=outs[i].at[k], send_sem=send_sems.at[i, k - 1],
        recv_sem=recv_sems.at[i, k - 1], device_id=_peer(k), device_id_type=MESH)
        for k in PEER_ORDER for i in range(n)]
    return mine, sends, sends


def _gather_copies(ins, outs, send_sems, recv_sems, local_sems):
    n = len(ins)

    def rows(i, p):
        r = ins[i].shape[0]
        return outs[i].at[pl.ds(_mesh_index(p) * r, r), :]

    def copy(i, k, landing):
        return pltpu.make_async_remote_copy(
            src_ref=ins[i], dst_ref=rows(i, landing), send_sem=send_sems.at[i, k - 1],
            recv_sem=recv_sems.at[i, k - 1], device_id=_peer(k), device_id_type=MESH)

    mine = [pltpu.make_async_copy(ins[i], rows(i, _peer(0)), local_sems.at[i]) for i in range(n)]
    sends = [copy(i, k, _peer(0)) for k in PEER_ORDER for i in range(n)]
    recvs = [copy(i, k, _peer(k)) for k in PEER_ORDER for i in range(n)]
    return mine, sends, recvs


def _start_copies(copies):
    mine, sends, _ = copies
    for cp in mine + sends:
        cp.start()


def _finish_copies(copies):
    mine, sends, recvs = copies
    for cp in recvs:
        cp.wait_recv()
    for cp in sends:
        cp.wait_send()
    for cp in mine:
        cp.wait()


def _sum_slots(r3, name):
    _, r, c = r3.shape
    tr = _tile(r, (128, 96, 64, 32, 16))

    def body(r_ref, o_ref):
        acc = r_ref[0].astype(F32)
        for k in range(1, N_DEV):
            acc = acc + r_ref[k].astype(F32)
        o_ref[...] = acc

    return pl.pallas_call(
        body, name=name, grid=(r // tr,),
        out_shape=jax.ShapeDtypeStruct((r, c), F32),
        in_specs=[pl.BlockSpec((N_DEV, tr, c), lambda i: (0, i, 0))],
        out_specs=pl.BlockSpec((tr, c), lambda i: (i, 0)),
    )(r3)


def _mm(a, b, dims, out_dtype, name, n0=0, n=None, exchange=(), gather=()):
    if dims is NT:
        m, k = a.shape
        n = b.shape[0] if n is None else n
    elif dims is NN:
        m, k = a.shape
        n = b.shape[1]
    else:
        k, m = a.shape
        n = b.shape[1]
    if dims is TN:
        tm = _lane_tile(m, 1536)
        tn = _lane_tile(n, 1024)
        tk = _tile(k, (1024, 512, 256, 128))
    else:
        tm = _tile(m, (1024, 512, 256, 128))
        tn = _lane_tile(math.gcd(n, n0) if n0 else n, 1536)
        tk = _lane_tile(k, 2816)
    nk = k // tk
    noff = n0 // tn

    moved = list(exchange) or list(gather)
    make_copies = _exchange_copies if exchange else _gather_copies
    nx = len(moved)
    grid = (m // tm, n // tn, nk)

    def body(*refs):
        a_ref, b_ref = refs[:2]
        o_ref = refs[2 + nx]
        scratch = refs[3 + 2 * nx:]
        if nx:
            copies = make_copies(refs[2:2 + nx], refs[3 + nx:3 + 2 * nx], *scratch[-3:])
            step_id = (pl.program_id(0) * grid[1] + pl.program_id(1)) * grid[2] + pl.program_id(2)
            pl.when(step_id == 0)(lambda: _start_copies(copies))
        part =lax.dot_general(a_ref[...].astype(BF16), b_ref[...].astype(BF16), dims,
                               preferred_element_type=F32)
        if nk == 1:
            o_ref[...] = part.astype(o_ref.dtype)
        else:
            acc_ref = scratch[0]
            kk = pl.program_id(2)

            @pl.when(kk == 0)
            def _():
                acc_ref[...] = part

            @pl.when(kk > 0)
            def _():
                acc_ref[...] += part

            @pl.when(kk == nk - 1)
            def _():
                o_ref[...] = acc_ref[...].astype(o_ref.dtype)

        if nx:
            pl.when(step_id == grid[0] * grid[1] * grid[2] - 1)(lambda: _finish_copies(copies))

    if dims is NT:
        in_specs = [pl.BlockSpec((tm, tk), lambda i, j, kk: (i, kk)),
                    pl.BlockSpec((tn, tk), lambda i, j, kk: (j + noff, kk))]
    elif dims is NN:
        in_specs = [pl.BlockSpec((tm, tk), lambda i, j, kk: (i, kk)),
                    pl.BlockSpec((tk, tn), lambda i, j, kk: (kk, j))]
    else:
        in_specs = [pl.BlockSpec((tk, tm), lambda i, j, kk: (kk, i)),
                    pl.BlockSpec((tk, tn), lambda i, j, kk: (kk, j))]
    hbm = pl.BlockSpec(memory_space=pl.ANY)
    outs = pl.pallas_call(
        body, name=name, grid=grid,
        out_shape=[jax.ShapeDtypeStruct((m, n), out_dtype)]
        + (_exchange_out_shapes(moved) if exchange else _gather_out_shapes(moved)),
        in_specs=in_specs + [hbm] * nx,
        out_specs=[pl.BlockSpec((tm, tn), lambda i, j, kk: (i, j))] + [hbm] * nx,
        scratch_shapes=([] if nk == 1 else [pltpu.VMEM((tm, tn), F32)]) + (_copy_sems(nx) if nx else []),
        compiler_params=pltpu.CompilerParams(
            dimension_semantics=("arbitrary",) * 3 if nx else ("parallel", "parallel", "arbitrary")),
    )(a, b, *moved)
    return (outs[0], list(outs[1:])) if nx else outs[0]


def _tok_tile(s):
    return _tile(s, (512, 256, 128))


def _row_spec(tm, d):
    return pl.BlockSpec((tm, d), lambda i: (i, 0))


def _vec_spec(d):
    return pl.BlockSpec((1, d), lambda i: (0, 0))


def _seq_spec(d, tiles_per_seq):
    return pl.BlockSpec((1, 1, d), lambda i: (i // tiles_per_seq, 0, 0))


def _rms(x):
    return lax.rsqrt(jnp.mean(x * x, axis=-1, keepdims=True) + EPS)


def _norm_mod(x, g, sc, sh, s, name):
    t, d = x.shape
    tm = _tok_tile(s)

    def body(x_ref, g_ref, sc_ref, sh_ref, h_ref):
        xv = x_ref[...]
        h = (xv * _rms(xv)) * g_ref[...]
        h_ref[...] = (h * (1.0 + sc_ref[0]) + sh_ref[0]).astype(BF16)

    return pl.pallas_call(
        body, name=name, grid=(t // tm,),
        out_shape=jax.ShapeDtypeStruct((t, d), BF16),
        in_specs=[_row_spec(tm, d), _vec_spec(d), _seq_spec(d, s // tm), _seq_spec(d, s // tm)],
        out_specs=_row_spec(tm, d),
    )(x, g, sc, sh)


def _post_res_norm(x, y, g_post, ga, g_pre, sc, sh, s, name):
    t, d = x.shape
    tm = _tok_tile(s)

    def body(x_ref, y_ref, gp_ref, ga_ref, g_ref, sc_ref, sh_ref, xn_ref, h_ref):
        yv = y_ref[...]
        xn = x_ref[...] + ga_ref[0] * ((yv * _rms(yv)) * gp_ref[...])
        xn_ref[...] = xn
        h = (xn * _rms(xn)) * g_ref[...]
        h_ref[...] = (h * (1.0 + sc_ref[0]) + sh_ref[0]).astype(BF16)

    tps = s // tm
    return pl.pallas_call(
        body, name=name, grid=(t // tm,),
        out_shape=[jax.ShapeDtypeStruct((t, d), F32), jax.ShapeDtypeStruct((t, d), BF16)],
        in_specs=[_row_spec(tm, d), _row_spec(tm, d), _vec_spec(d), _seq_spec(d, tps),
                  _vec_spec(d), _seq_spec(d, tps), _seq_spec(d, tps)],
        out_specs=[_row_spec(tm, d), _row_spec(tm, d)],
    )(x, y, g_post, ga, g_pre, sc, sh)


def _post_res_loss(x, y, g_post, ga, target, s, name):
    t, d = x.shape
    tm = _tok_tile(s)

    def body(x_ref, y_ref, gp_ref, ga_ref, t_ref, dx_ref, loss_ref):
        yv = y_ref[...]
        err = x_ref[...] + ga_ref[0] * ((yv * _rms(yv)) * gp_ref[...]) - t_ref[...]
        dx_ref[...] = err * (1.0 / d)
        part = 0.5 * jnp.sum(jnp.mean(err * err, axis=-1, keepdims=True), axis=0, keepdims=True)

        @pl.when(pl.program_id(0) == 0)
        def _():
            loss_ref[...] = jnp.zeros_like(loss_ref)

        loss_ref[...] += part

    return pl.pallas_call(
        body, name=name, grid=(t // tm,),
        out_shape=[jax.ShapeDtypeStruct((t, d), F32), jax.ShapeDtypeStruct((8, 128), F32)],
        in_specs=[_row_spec(tm, d), _row_spec(tm, d), _vec_spec(d), _seq_spec(d, s // tm), _row_spec(tm, d)],
        out_specs=[_row_spec(tm, d), pl.BlockSpec((8, 128), lambda i: (0, 0))],
    )(x, y, g_post, ga, target)


def _post_bwd(dxn, y, g, ga, s, name):
    t, d = y.shape
    tm = _tok_tile(s)
    tps = s // tm

    def body(dx_ref, y_ref, g_ref, ga_ref, dy_ref, dg_ref, dga_ref):
        i = pl.program_id(0)
        yv = y_ref[...]
        r = _rms(yv)
        yh = yv * r
        dxv = dx_ref[...]
        dn = dxv * ga_ref[0]
        dyh = dn * g_ref[...]
        dy_ref[...] = (r * (dyh - yh * jnp.mean(dyh * yh, axis=-1, keepdims=True))).astype(BF16)

        @pl.when(i == 0)
        def _():
            dg_ref[...] = jnp.zeros_like(dg_ref)

        @pl.when(i % tps == 0)
        def _():
            dga_ref[...] = jnp.zeros_like(dga_ref)

        dg_ref[...] += jnp.sum(dn * yh, axis=0, keepdims=True)
        dga_ref[0] += jnp.sum(dxv * (yh * g_ref[...]), axis=0, keepdims=True)

    return pl.pallas_call(
        body, name=name, grid=(t // tm,),
        out_shape=[jax.ShapeDtypeStruct((t, d), BF16), jax.ShapeDtypeStruct((1, d), F32),
                   jax.ShapeDtypeStruct((t // s, 1, d), F32)],
        in_specs=[_row_spec(tm, d), _row_spec(tm, d), _vec_spec(d), _seq_spec(d, tps)],
        out_specs=[_row_spec(tm, d), _vec_spec(d), _seq_spec(d, tps)],
    )(dxn, y, g, ga)


def _pre_bwd(dh, x, g, sc, dres, s, name):
    t, d = x.shape
    tm = _tok_tile(s)
    tps = s // tm

    def body(dh_ref, x_ref, g_ref, sc_ref, dr_ref, dx_ref, dsh_ref, dsc_ref, dg_ref):
        i = pl.program_id(0)
        xv = x_ref[...]
        r = _rms(xv)
        xh = xv * r
        dhv = dh_ref[...]
        one_sc = 1.0 + sc_ref[0]
        dxh = dhv * one_sc * g_ref[...]
        dx_ref[...] = dr_ref[...] + r * (dxh - xh * jnp.mean(dxh * xh, axis=-1, keepdims=True))

        @pl.when(i == 0)
        def _():
            dg_ref[...] = jnp.zeros_like(dg_ref)

        @pl.when(i % tps == 0)
        def _():
            dsh_ref[...] = jnp.zeros_like(dsh_ref)
            dsc_ref[...] = jnp.zeros_like(dsc_ref)

        dg_ref[...] += jnp.sum(dhv * one_sc * xh, axis=0, keepdims=True)
        dsh_ref[0] += jnp.sum(dhv, axis=0, keepdims=True)
        dsc_ref[0] += jnp.sum(dhv * (xh * g_ref[...]), axis=0, keepdims=True)

    nb = t // s
    return pl.pallas_call(
        body, name=name, grid=(t // tm,),
        out_shape=[jax.ShapeDtypeStruct((t, d), F32), jax.ShapeDtypeStruct((nb, 1, d), F32),
                   jax.ShapeDtypeStruct((nb, 1, d), F32), jax.ShapeDtypeStruct((1, d), F32)],
        in_specs=[_row_spec(tm, d), _row_spec(tm, d), _vec_spec(d), _seq_spec(d, tps), _row_spec(tm, d)],
        out_specs=[_row_spec(tm, d), _seq_spec(d, tps), _seq_spec(d, tps), _vec_spec(d)],
    )(dh, x, g, sc, dres)


def _merge(y_conv, y_att, gates, s, name):
    t, d = y_conv.shape
    tm = _tok_tile(s)

    def body(yc_ref, ya_ref, gc_ref, gt_ref, o_ref):
        f32 = lambda r: r[...].astype(F32)
        o_ref[...] = (_sig(f32(gc_ref)) * f32(yc_ref) + _sig(f32(gt_ref)) * f32(ya_ref)).astype(BF16)

    return pl.pallas_call(
        body, name=name, grid=(t // tm,),
        out_shape=jax.ShapeDtypeStruct((t, d), BF16),
        in_specs=[_row_spec(tm, d), _row_spec(tm, d), pl.BlockSpec((tm, d), lambda i: (i, 0)),
                  pl.BlockSpec((tm, d), lambda i: (i, 1))],
        out_specs=_row_spec(tm, d),
    )(y_conv, y_att, gates, gates)


def _merge_bwd(dm, y_conv, y_att, gates, s, name):
    t, d = y_conv.shape
    tm = _tok_tile(s)

    def body(dm_ref, yc_ref, ya_ref, gc_ref, gt_ref, dyc_ref, dya_ref, dg_ref):
        f32 = lambda r: r[...].astype(F32)
        dmv = f32(dm_ref)
        sc_, st_ = _sig(f32(gc_ref)), _sig(f32(gt_ref))
        dyc_ref[...] = (dmv * sc_).astype(BF16)
        dya_ref[...] = (dmv * st_).astype(BF16)
        dg_ref[:, :d] = (dmv * f32(yc_ref) * (sc_ * (1.0 - sc_))).astype(BF16)
        dg_ref[:, d:] = (dmv * f32(ya_ref) * (st_ * (1.0 - st_))).astype(BF16)

    return pl.pallas_call(
        body, name=name, grid=(t // tm,),
        out_shape=[jax.ShapeDtypeStruct((t, d), BF16), jax.ShapeDtypeStruct((t, d), BF16),
                   jax.ShapeDtypeStruct((t, 2 * d), BF16)],
        in_specs=[_row_spec(tm, d), _row_spec(tm, d), _row_spec(tm, d),
                  pl.BlockSpec((tm, d), lambda i: (i, 0)), pl.BlockSpec((tm, d), lambda i: (i, 1))],
        out_specs=[_row_spec(tm, d), _row_spec(tm, d), _row_spec(tm, 2 * d)],
    )(dm, y_conv, y_att, gates, gates)


def _swiglu(f, s, name):
    t, two = f.shape
    dff = two // 2
    tm = _tok_tile(s)

    def body(g_ref, u_ref, a_ref):
        gv = g_ref[...].astype(F32)
        a_ref[...] = (gv * _sig(gv) * u_ref[...].astype(F32)).astype(BF16)

    return pl.pallas_call(
        body, name=name, grid=(t // tm,),
        out_shape=jax.ShapeDtypeStruct((t, dff), BF16),
        in_specs=[pl.BlockSpec((tm, dff), lambda i: (i, 0)), pl.BlockSpec((tm, dff), lambda i: (i, 1))],
        out_specs=_row_spec(tm, dff),
    )(f, f)


def _swiglu_bwd(da, f, s, name):
    t, two = f.shape
    dff = two // 2
    tm = _tok_tile(s)

    def body(da_ref, g_ref, u_ref, df_ref):
        gv, dav = g_ref[...].astype(F32), da_ref[...].astype(F32)
        sg = _sig(gv)
        df_ref[:, :dff] = (dav * u_ref[...].astype(F32) * (sg * (1.0 + gv * (1.0 - sg)))).astype(BF16)
        df_ref[:, dff:] = (dav * (gv * sg)).astype(BF16)

    return pl.pallas_call(
        body, name=name, grid=(t // tm,),
        out_shape=jax.ShapeDtypeStruct((t, two), BF16),
        in_specs=[_row_spec(tm, dff), pl.BlockSpec((tm, dff), lambda i: (i, 0)),
                  pl.BlockSpec((tm, dff), lambda i: (i, 1))],
        out_specs=_row_spec(tm, two),
    )(da, f, f)


CONV_CHUNK = 64
SUB = 8


def _tap_groups(offsets):
    groups = {}
    for k, o in enumerate(offsets):
        groups.setdefault(o % SUB, []).append((k, o - o % SUB))
    return groups


def _conv_fwd(glu_in, conv_w, conv_b, ln_g, ln_b, nb, s, name):
    t, two = glu_in.shape
    cw = two // 2
    tt = _tile(s, (256, 128, 64))
    spt = s // tt
    ch = min(CONV_CHUNK, tt)

    groups = _tap_groups([HALO - (CONV_K - 1) + k for k in range(CONV_K)])

    def body(cur_ref, halo_ref, w_ref, b_ref, g_ref, be_ref, u_ref, cv_ref, ua_ref, ext_ref, part_ref):
        j = pl.program_id(1)
        cur = cur_ref[...]
        u_cur = cur[:, :cw] * _sig(cur[:, cw:])
        hal = halo_ref[...]
        u_hal = hal[:, :cw] * _sig(hal[:, cw:])
        ext_ref[0:HALO, :] = jnp.where(j == 0, 0.0, u_hal)
        ext_ref[HALO:HALO + tt, :] = u_cur
        ext_ref[HALO + tt:, :] = jnp.zeros((SUB, cw), F32)
        u_ref[...] = u_cur
        for c0 in range(0, tt, ch):
            acc = jnp.zeros((ch, cw), F32) + b_ref[...]
            for r, taps in groups.items():
                part = None
                for k, base in taps:
                    term = w_ref[k:k + 1, :] * ext_ref[c0 + base:c0 + base + ch + SUB, :]
                    part = term if part is None else part + term
                if r == 0:
                    acc = acc + part[:ch]
                else:
                    part_ref[r] = part
                    acc = acc + part_ref[r, r:r + ch, :]
            cv_ref[c0:c0 + ch, :] = acc
            mu = jnp.mean(acc, axis=-1, keepdims=True)
            xc = acc - mu
            yv = xc * lax.rsqrt(jnp.mean(xc * xc, axis=-1, keepdims=True) + EPS) * g_ref[...] + be_ref[...]
            ua_ref[c0:c0 + ch, :] = (yv * _sig(yv)).astype(BF16)

    cur_map = lambda b, j: (b * spt + j, 0)
    halo_map = lambda b, j: (jnp.maximum((b * s + j * tt) // HALO - 1, 0), 0)
    vec = lambda d0: pl.BlockSpec((d0, cw), lambda b, j: (0, 0))
    return pl.pallas_call(
        body, name=name, grid=(nb, spt),
        out_shape=[jax.ShapeDtypeStruct((t, cw), F32), jax.ShapeDtypeStruct((t, cw), F32),
                   jax.ShapeDtypeStruct((t, cw), BF16)],
        in_specs=[pl.BlockSpec((tt, two), cur_map), pl.BlockSpec((HALO, two), halo_map),
                  vec(CONV_K), vec(1), vec(1), vec(1)],
        out_specs=[pl.BlockSpec((tt, cw), cur_map)] * 3,
        scratch_shapes=[pltpu.VMEM((tt + HALO + SUB, cw), F32), pltpu.VMEM((SUB, ch + SUB, cw), F32)],
    )(glu_in, glu_in, conv_w, conv_b, ln_g, ln_b)


def _convln_bwd(dua, cv, ln_g, ln_b, s, name):
    t, cw = cv.shape
    tm = _tok_tile(s)

    def body(du_ref, cv_ref, g_ref, be_ref, dcv_ref, dg_ref, dbe_ref, db_ref):
        cvv = cv_ref[...]
        mu = jnp.mean(cvv, axis=-1, keepdims=True)
        xc = cvv - mu
        rstd = lax.rsqrt(jnp.mean(xc * xc, axis=-1, keepdims=True) + EPS)
        xh = xc * rstd
        yv = xh * g_ref[...] + be_ref[...]
        sg = _sig(yv)
        dy = du_ref[...] * (sg * (1.0 + yv * (1.0 - sg)))
        dxh = dy * g_ref[...]
        dcv = rstd * (dxh - jnp.mean(dxh, axis=-1, keepdims=True)
                      - xh * jnp.mean(dxh * xh, axis=-1, keepdims=True))
        dcv_ref[...] = dcv

        @pl.when(pl.program_id(0) == 0)
        def _():
            dg_ref[...] = jnp.zeros_like(dg_ref)
            dbe_ref[...] = jnp.zeros_like(dbe_ref)
            db_ref[...] = jnp.zeros_like(db_ref)

        dg_ref[...] += jnp.sum(dy * xh, axis=0, keepdims=True)
        dbe_ref[...] += jnp.sum(dy, axis=0, keepdims=True)
        db_ref[...] += jnp.sum(dcv, axis=0, keepdims=True)

    return pl.pallas_call(
        body, name=name, grid=(t // tm,),
        out_shape=[jax.ShapeDtypeStruct((t, cw), F32)] + [jax.ShapeDtypeStruct((1, cw), F32)] * 3,
        in_specs=[_row_spec(tm, cw), _row_spec(tm, cw), _vec_spec(cw), _vec_spec(cw)],
        out_specs=[_row_spec(tm, cw), _vec_spec(cw), _vec_spec(cw), _vec_spec(cw)],
    )(dua, cv, ln_g, ln_b)


def _conv_bwd(dcv, u, glu_in, conv_w, nb, s, name):
    t, cw = dcv.shape
    tt = _tile(s, (256, 128, 64))
    spt = s // tt
    ch = min(CONV_CHUNK, tt)
    nblk = t // HALO

    d_groups = _tap_groups([(CONV_K - 1) - k for k in range(CONV_K)])
    u_groups = _tap_groups([HALO - (CONV_K - 1) + k for k in range(CONV_K)])

    def body(d_ref, dn_ref, u_ref, up_ref, glu_ref, w_ref, dglu_ref, dw_ref, dext_ref, uext_ref, part_ref, dwp_ref):
        b, j = pl.program_id(0), pl.program_id(1)
        dcur = d_ref[...]
        dext_ref[0:tt, :] = dcur
        dext_ref[tt:tt + HALO, :] = jnp.where(j == spt - 1, 0.0, dn_ref[...])
        uext_ref[0:HALO, :] = jnp.where(j == 0, 0.0, up_ref[...])
        uext_ref[HALO:HALO + tt, :] = u_ref[...]
        uext_ref[HALO + tt:, :] = jnp.zeros((SUB, cw), F32)

        @pl.when((b == 0) & (j == 0))
        def _():
            dwp_ref[...] = jnp.zeros_like(dwp_ref)

        for c0 in range(0, tt, ch):
            dc = dext_ref[c0:c0 + ch, :]
            du = jnp.zeros((ch, cw), F32)
            for r, taps in d_groups.items():
                part = None
                for k, base in taps:
                    term = w_ref[k:k + 1, :] * dext_ref[c0 + base:c0 + base + ch + SUB, :]
                    part = term if part is None else part + term
                if r == 0:
                    du = du + part[:ch]
                else:
                    part_ref[r] = part
                    du = du + part_ref[r, r:r + ch, :]
            for r, taps in u_groups.items():
                ush = uext_ref[c0 + r:c0 + r + ch + HALO, :]
                for k, base in taps:
                    prod = dc * ush[base:base + ch]
                    rows8 = prod[0:SUB]
                    for i in range(1, ch // SUB):
                        rows8 = rows8 + prod[i * SUB:(i + 1) * SUB]
                    dwp_ref[k] += rows8
            val = glu_ref[c0:c0 + ch, :cw]
            sg = _sig(glu_ref[c0:c0 + ch, cw:])
            dglu_ref[c0:c0 + ch, :cw] = (du * sg).astype(BF16)
            dglu_ref[c0:c0 + ch, cw:] = (du * val * (sg * (1.0 - sg))).astype(BF16)

        @pl.when((b == nb - 1) & (j == spt - 1))
        def _():
            dw_ref[...] = jnp.sum(dwp_ref[...], axis=1)

    cur_map = lambda b, j: (b * spt + j, 0)
    prev_map = lambda b, j: (jnp.maximum((b * s + j * tt) // HALO - 1, 0), 0)
    next_map = lambda b, j: (jnp.minimum((b * s + (j + 1) * tt) // HALO, nblk - 1), 0)
    return pl.pallas_call(
        body, name=name, grid=(nb, spt),
        out_shape=[jax.ShapeDtypeStruct((t, 2 * cw), BF16), jax.ShapeDtypeStruct((HALO, cw), F32)],
        in_specs=[pl.BlockSpec((tt, cw), cur_map), pl.BlockSpec((HALO, cw), next_map),
                  pl.BlockSpec((tt, cw), cur_map), pl.BlockSpec((HALO, cw), prev_map),
                  pl.BlockSpec((tt, 2 * cw), cur_map), pl.BlockSpec((CONV_K, cw), lambda b, j: (0, 0))],
        out_specs=[pl.BlockSpec((tt, 2 * cw), cur_map), pl.BlockSpec((HALO, cw), lambda b, j: (0, 0))],
        scratch_shapes=[pltpu.VMEM((tt + HALO, cw), F32), pltpu.VMEM((tt + HALO + SUB, cw), F32),
                        pltpu.VMEM((SUB, ch + SUB, cw), F32), pltpu.VMEM((HALO, SUB, cw), F32)],
    )(dcv, dcv, u, u, glu_in, conv_w)


def _split(x):
    hi = x.astype(BF16)
    return hi, (x - hi.astype(F32)).astype(BF16)


def _tri(upper):
    j = lax.broadcasted_iota(jnp.int32, (KEY_BLK, KEY_BLK), 0)
    s_ = lax.broadcasted_iota(jnp.int32, (KEY_BLK, KEY_BLK), 1)
    return jnp.where(j > s_ if upper else j < s_, 1.0, 0.0).astype(BF16)


def _log_keep(z, causal):
    sp = jnp.maximum(z, 0.0) + jnp.log(1.0 + jnp.exp(-jnp.abs(z)))
    return -sp if causal is None else jnp.where(causal, -sp, 0.0)


LANES = 128


HALF_Q = 256
DEAD = -105.0
SKIPPED = -1e30


def _any_live(x):
    return (jnp.max(x) > DEAD).astype(jnp.int32)


def _cumsum_mat(upper):
    m = _tri(upper)
    return jnp.concatenate([m, m], axis=0)


def _cum(x, mat):
    hi, lo = _split(x)
    return jnp.dot(jnp.concatenate([hi, lo], axis=1), mat, preferred_element_type=F32)


def _wide(x):
    return jnp.concatenate([x] * (KEY_BLK // LANES), axis=1)


def _sb4_fwd(qkv, nb, s, name, gather=()):
    t, three_aw = qkv.shape
    aw = three_aw // 3
    npair = aw // LANES
    tq = 2 * HALF_Q
    spt = s // tq
    scale = 1.0 / math.sqrt(HEAD_DIM)

    ng = len(gather)

    def body(*refs):
        q_ref, k_ref, v_ref = refs[:3]
        o_ref, car_ref = refs[3 + ng:5 + ng]
        acc_ref, run_ref = refs[5 + 2 * ng:7 + 2 * ng]
        if ng:
            copies = _gather_copies(refs[3:3 + ng], refs[5 + ng:5 + 2 * ng], *refs[7 + 2 * ng:])
            step_id = (pl.program_id(0) * npair + pl.program_id(1)) * spt + pl.program_id(2)
            pl.when(step_id == 0)(lambda: _start_copies(copies))
        qi = pl.program_id(2)
        lane = lax.broadcasted_iota(jnp.int32, (HALF_Q, LANES), 1)
        col = lax.broadcasted_iota(jnp.int32, (HALF_Q, KEY_BLK), 1)
        rowi = lax.broadcasted_iota(jnp.int32, (HALF_Q, KEY_BLK), 0)
        first = lane < HEAD_DIM
        mat = _cumsum_mat(True)
        qs = []
        for half in range(2):
            q2 = q_ref[half * HALF_Q:(half + 1) * HALF_Q, :] * scale
            zero = jnp.zeros_like(q2)
            qs.append((jnp.where(first, q2, zero), jnp.where(first, zero, q2)))
        acc_ref[...] = jnp.zeros_like(acc_ref)
        run_ref[...] = jnp.zeros_like(run_ref)
        car_ref[...] = jnp.zeros_like(car_ref)
        nk = (qi * tq + tq) // KEY_BLK

        def step(kb, halves):
            off = pl.multiple_of(kb * KEY_BLK, KEY_BLK)
            kblk = k_ref[pl.ds(off, KEY_BLK), :]
            vblk = v_ref[pl.ds(off, KEY_BLK), :]
            chains = [(half, masked, h) for half, masked in halves for h in range(2)]
            causal = {half: ((off + col) < (qi * tq + half * HALF_Q + rowi)) if masked else None
                      for half, masked in halves}
            zs = [lax.dot_general(qs[half][h], kblk, NT, preferred_element_type=F32) for half, _, h in chains]
            lks = [_log_keep(z, causal[c[0]]) for z, c in zip(zs, chains)]
            css = [_cum(lk, mat) for lk in lks]
            runs = [run_ref[2 * half + h] for half, _, h in chains]
            aa = [jnp.exp(z + lk + cs + _wide(run)) for z, lk, cs, run in zip(zs, lks, css, runs)]
            aa = [jnp.where(causal[c[0]], a, 0.0) if c[1] else a for a, c in zip(aa, chains)]
            for a, (half, _, h) in zip(aa, chains):
                acc_ref[2 * half + h] += jnp.dot(a.astype(BF16), vblk, preferred_element_type=F32)
            for lk, run, (half, _, h) in zip(lks, runs, chains):
                rows = pl.ds(half * HALF_Q, HALF_Q)
                car_ref[h, rows, :] = jnp.where(lane == kb, run, car_ref[h, rows, :])
                run_ref[2 * half + h] = run + jnp.sum(lk, axis=1, keepdims=True)

        step(nk - 1, ((1, True),))
        step(nk - 2, ((0, True), (1, False)))

        def more(c):
            return (c[0] < nk) & (c[1] > 0)

        def rest(c):
            step(nk - 1 - c[0], ((0, False), (1, False)))
            return c[0] + 1, _any_live(run_ref[...])

        visited, _ = lax.while_loop(more, rest, (jnp.int32(2), _any_live(run_ref[...])))
        unvisited = lane < (nk - visited)
        for h in range(2):
            car_ref[h] = jnp.where(jnp.concatenate([unvisited, unvisited], axis=0), SKIPPED, car_ref[h])
        for half in range(2):
            o_ref[half * HALF_Q:(half + 1) * HALF_Q, :] = jnp.where(
                first, acc_ref[2 * half], acc_ref[2 * half + 1]).astype(BF16)
        if ng:
            pl.when(step_id == nb * npair * spt - 1)(lambda: _finish_copies(copies))

    hbm = pl.BlockSpec(memory_space=pl.ANY)
    outs = pl.pallas_call(
        body, name=name, grid=(nb, npair, spt),
        out_shape=[jax.ShapeDtypeStruct((t, aw), BF16),
                   jax.ShapeDtypeStruct((nb * npair * 2, s, LANES), F32)] + _gather_out_shapes(gather),
        in_specs=[pl.BlockSpec((tq, LANES), lambda b, p, i: (b * spt + i, p)),
                  pl.BlockSpec((s, LANES), lambda b, p, i: (b, npair + p)),
                  pl.BlockSpec((s, LANES), lambda b, p, i: (b, 2 * npair + p))] + [hbm] * ng,
        out_specs=[pl.BlockSpec((tq, LANES), lambda b, p, i: (b * spt + i, p)),
                   pl.BlockSpec((2, tq, LANES), lambda b, p, i: (b * npair + p, i, 0))] + [hbm] * ng,
        scratch_shapes=[pltpu.VMEM((4, HALF_Q, LANES), F32), pltpu.VMEM((4, HALF_Q, LANES), F32)]
        + (_copy_sems(ng) if ng else []),
        compiler_params=pltpu.CompilerParams(dimension_semantics=("arbitrary", "arbitrary", "arbitrary")),
    )(qkv, qkv, qkv, *gather)
    return outs[0], outs[1], list(outs[2:])


def _sb4_bwd(qkv, do, car, nb, s, name, exchange=()):
    t, three_aw = qkv.shape
    aw = three_aw // 3
    npair = aw // LANES
    tq = HALF_Q
    scale = 1.0 / math.sqrt(HEAD_DIM)
    nkp, nqb = s // (2 * KEY_BLK), s // tq
    nx = len(exchange)

    def body(*refs):
        q_ref, k_ref, v_ref, do_ref, car_ref = refs[:5]
        dq_ref, dk_ref, dv_ref = refs[5 + nx:8 + nx]
        dq_acc, g_acc, dk_acc, dv_acc = refs[8 + 2 * nx:12 + 2 * nx]
        if nx:
            copies = _exchange_copies(refs[5:5 + nx], refs[8 + nx:8 + 2 * nx], *refs[12 + 2 * nx:])
            step_id = pl.program_id(0) * npair + pl.program_id(1)
            pl.when(step_id == 0)(lambda: _start_copies(copies))
        lane = lax.broadcasted_iota(jnp.int32, (tq, LANES), 1)
        col = lax.broadcasted_iota(jnp.int32, (tq, KEY_BLK), 1)
        rowi = lax.broadcasted_iota(jnp.int32, (tq, KEY_BLK), 0)
        first_k = lax.broadcasted_iota(jnp.int32, (KEY_BLK, LANES), 1) < HEAD_DIM
        m_suf, m_pre = _cumsum_mat(True), _tri(False)
        dq_acc[...] = jnp.zeros_like(dq_acc)
        g_acc[...] = jnp.zeros_like(g_acc)

        def kloop(kp, _):
            offs = [pl.multiple_of((2 * kp + blk) * KEY_BLK, KEY_BLK) for blk in range(2)]
            kblk = [k_ref[pl.ds(off, KEY_BLK), :] for off in offs]
            vblk = [v_ref[pl.ds(off, KEY_BLK), :] for off in offs]
            zero = jnp.zeros_like(kblk[0])
            kh = [(jnp.where(first_k, kb_, zero), jnp.where(first_k, zero, kb_)) for kb_ in kblk]
            vh = [(jnp.where(first_k, vb_, zero), jnp.where(first_k, zero, vb_)) for vb_ in vblk]
            k_scaled = jnp.concatenate([kblk[0] * scale, kblk[1] * scale], axis=0)
            dk_acc[...] = jnp.zeros_like(dk_acc)
            dv_acc[...] = jnp.zeros_like(dv_acc)

            def tile(qb, blocks):
                r0 = pl.multiple_of(qb * tq, tq)
                rows = pl.ds(r0, tq)
                q2 = q_ref[rows, :] * scale
                do2 = do_ref[rows, :]
                chains = [(blk, masked, h) for blk, masked in blocks for h in range(2)]
                causal = {blk: ((offs[blk] + col) < (r0 + rowi)) if masked else None for blk, masked in blocks}
                zs = [lax.dot_general(q2, kh[blk][h], NT, preferred_element_type=F32) for blk, _, h in chains]
                das = [lax.dot_general(do2, vh[blk][h], NT, preferred_element_type=F32) for blk, _, h in chains]
                lks = [_log_keep(z, causal[c[0]]) for z, c in zip(zs, chains)]
                css = [_cum(lk, m_suf) for lk in lks]
                cars = [car_ref[h, rows, :] for h in range(2)]
                cpast = [jnp.sum(jnp.where(lane == 2 * kp + blk, cars[h], 0.0), axis=1, keepdims=True)
                         for blk, _, h in chains]
                lsig = [z + lk for z, lk in zip(zs, lks)]
                aa = [jnp.exp(ls + cs + cp) for ls, cs, cp in zip(lsig, css, cpast)]
                aa = [jnp.where(causal[c[0]], a, 0.0) if c[1] else a for a, c in zip(aa, chains)]
                gs = [a * da for a, da in zip(aa, das)]
                gsum = [jnp.sum(g, axis=1, keepdims=True) for g in gs]
                gin = [jnp.dot(g.astype(BF16), m_pre, preferred_element_type=F32) for g in gs]
                gold = [_wide(g_acc[h, rows, :]) for h in range(2)]
                gpre = []
                for i, (blk, _, h) in enumerate(chains):
                    left = gold[h] if blk == 0 else gold[h] + gsum[h]
                    gpre.append(gin[i] + left)
                dzs = [g - jnp.exp(ls) * (g + gp) for g, ls, gp in zip(gs, lsig, gpre)]
                dzs = [jnp.where(causal[c[0]], dz, 0.0) if c[1] else dz for dz, c in zip(dzs, chains)]
                dss = [dz.astype(BF16) for dz in dzs]
                for a, (blk, _, h) in zip(aa, chains):
                    dv_acc[2 * blk + h] += lax.dot_general(a.astype(BF16), do2, TN, preferred_element_type=F32)
                for ds_, (blk, _, h) in zip(dss, chains):
                    dk_acc[2 * blk + h] += lax.dot_general(ds_, q2, TN, preferred_element_type=F32)
                for h in range(2):
                    if len(blocks) == 2:
                        both = jnp.concatenate([dss[h], dss[2 + h]], axis=1)
                        dq_acc[h, rows, :] += jnp.dot(both, k_scaled, preferred_element_type=F32)
                        tot = gsum[h] + gsum[2 + h]
                    else:
                        dq_acc[h, rows, :] += jnp.dot(dss[h], k_scaled[:KEY_BLK], preferred_element_type=F32)
                        tot = gsum[h]
                    g_acc[h, rows, :] += jnp.broadcast_to(tot, (tq, LANES))

            tile(2 * kp, ((0, True),))
            tile(2 * kp + 1, ((0, False), (1, True)))

            def rest(qb, _):
                rows = pl.ds(pl.multiple_of(qb * tq, tq), tq)
                near = jnp.where(lane == 2 * kp + 1, jnp.maximum(car_ref[0, rows, :], car_ref[1, rows, :]), SKIPPED)
                pl.when(_any_live(near) > 0)(lambda: tile(qb, ((0, False), (1, False))))
                return 0

            lax.fori_loop(2 * kp + 2, nqb, rest, 0)
            for blk in range(2):
                dk_ref[pl.ds(offs[blk], KEY_BLK), :] = jnp.where(
                    first_k, dk_acc[2 * blk], dk_acc[2 * blk + 1]).astype(BF16)
                dv_ref[pl.ds(offs[blk], KEY_BLK), :] = jnp.where(
                    first_k, dv_acc[2 * blk], dv_acc[2 * blk + 1]).astype(BF16)
            return 0

        lax.fori_loop(0, nkp, kloop, 0)
        first_q = lax.broadcasted_iota(jnp.int32, (s, LANES), 1) < HEAD_DIM
        dq_ref[...] = jnp.where(first_q, dq_acc[0], dq_acc[1]).astype(BF16)
        if nx:
            pl.when(step_id == nb * npair - 1)(lambda: _finish_copies(copies))

    colspec = lambda j: pl.BlockSpec((s, LANES), lambda b, p: (b, j * npair + p))
    hbm = pl.BlockSpec(memory_space=pl.ANY)
    outs = pl.pallas_call(
        body, name=name, grid=(nb, npair),
        out_shape=[jax.ShapeDtypeStruct((t, aw), BF16)] * 3 + _exchange_out_shapes(exchange),
        in_specs=[colspec(0), colspec(1), colspec(2), colspec(0),
                  pl.BlockSpec((2, s, LANES), lambda b, p: (b * npair + p, 0, 0))] + [hbm] * nx,
        out_specs=[colspec(0), colspec(0), colspec(0)] + [hbm] * nx,
        scratch_shapes=[pltpu.VMEM((2, s, LANES), F32), pltpu.VMEM((2, s, LANES), F32),
                        pltpu.VMEM((4, KEY_BLK, LANES), F32), pltpu.VMEM((4, KEY_BLK, LANES), F32)]
        + (_copy_sems(nx) if nx else []),
        compiler_params=pltpu.CompilerParams(dimension_semantics=("arbitrary", "arbitrary")),
    )(qkv, qkv, qkv, do, car, *exchange)
    return outs[0], outs[1], outs[2], list(outs[3:])


def _ada_fwd(c_all, ada_w, ada_b_loc, name):
    nl, d, n6 = ada_w.shape
    nb = c_all.shape[0]

    def body(c_ref, w_ref, b_ref, o_ref):
        cv = c_ref[...]
        o_ref[0] = jnp.dot(cv * _sig(cv), w_ref[0], preferred_element_type=F32) + b_ref[0]

    return pl.pallas_call(
        body, name=name, grid=(nl,),
        out_shape=jax.ShapeDtypeStruct((nl, nb, n6), F32),
        in_specs=[pl.BlockSpec((nb, d), lambda l: (0, 0)), pl.BlockSpec((1, d, n6), lambda l: (l, 0, 0)),
                  pl.BlockSpec((1, 1, n6), lambda l: (l, 0, 0))],
        out_specs=pl.BlockSpec((1, nb, n6), lambda l: (l, 0, 0)),
    )(c_all, ada_w, ada_b_loc)


def _ada_bwd(c_all_t, dmod_loc, dmod_all, name):
    nl, nb, n6 = dmod_loc.shape
    d = c_all_t.shape[0]
    n_all = dmod_all.shape[2]

    def body(c_ref, dl_ref, da_ref, gw_ref, gb_ref):
        cv = c_ref[...]
        gw_ref[0] = jnp.dot(cv * _sig(cv), dl_ref[0], preferred_element_type=F32)
        gb_ref[0] = jnp.sum(da_ref[0], axis=0, keepdims=True)

    return pl.pallas_call(
        body, name=name, grid=(nl,),
        out_shape=[jax.ShapeDtypeStruct((nl, d, n6), F32), jax.ShapeDtypeStruct((nl, 1, n_all), F32)],
        in_specs=[pl.BlockSpec((d, nb), lambda l: (0, 0)), pl.BlockSpec((1, nb, n6), lambda l: (l, 0, 0)),
                  pl.BlockSpec((1, nb, n_all), lambda l: (l, 0, 0))],
        out_specs=[pl.BlockSpec((1, d, n6), lambda l: (l, 0, 0)), pl.BlockSpec((1, 1, n_all), lambda l: (l, 0, 0))],
    )(c_all_t, dmod_loc, dmod_all)


def _adamw(w, g, m, v, name):
    shape = w.shape
    cols = shape[-1]
    rows = w.size // cols
    tr = _tile(rows, (512, 256, 128, 64, 32, 16, 8))
    flat = lambda a: a.reshape(rows, cols)

    def body(w_ref, g_ref, m_ref, v_ref, d_ref, mo_ref, vo_ref):
        gv = g_ref[...]
        mn = ADAM_B1 * m_ref[...] + (1.0 - ADAM_B1) * gv
        vn = ADAM_B2 * v_ref[...] + (1.0 - ADAM_B2) * (gv * gv)
        m_hat = mn / (1.0 - ADAM_B1 ** ADAM_STEP)
        v_hat = vn / (1.0 - ADAM_B2 ** ADAM_STEP)
        d_ref[...] = -ADAM_LR * (m_hat / (jnp.sqrt(v_hat) + ADAM_EPS) + ADAM_WD * w_ref[...])
        mo_ref[...] = mn
        vo_ref[...] = vn

    spec = pl.BlockSpec((tr, cols), lambda i: (i, 0))
    outs = pl.pallas_call(
        body, name=name, grid=(rows // tr,),
        out_shape=[jax.ShapeDtypeStruct((rows, cols), F32)] * 3,
        in_specs=[spec] * 4, out_specs=[spec] * 3,
    )(flat(w), flat(g), flat(m), flat(v))
    return tuple(o.reshape(shape) for o in outs)


def _pack_rows(parts, d):
    flat = jnp.concatenate([p.reshape(-1) for p in parts])
    rows = -(-flat.size // d)
    rows = -(-rows // 8) * 8
    return jnp.pad(flat, (0, rows * d - flat.size)).reshape(rows, d)


def kernel(x, c, ada_w, ada_b, pre_mix_g, post_mix_g, pre_ffn_g, post_ffn_g, w_in, conv_w, conv_b, conv_ln_g, conv_ln_b, w_conv_out, w_att_out, w_o, w_ffn_in, w_ffn_out, loss_target, m_ada_w, m_ada_b, m_pre_mix_g, m_post_mix_g, m_pre_ffn_g, m_post_ffn_g, m_w_in, m_conv_w, m_conv_b, m_conv_ln_g, m_conv_ln_b, m_w_conv_out, m_w_att_out, m_w_o, m_w_ffn_in, m_w_ffn_out, v_ada_w, v_ada_b, v_pre_mix_g, v_post_mix_g, v_pre_ffn_g, v_post_ffn_g, v_w_in, v_conv_w, v_conv_b, v_conv_ln_g, v_conv_ln_b, v_w_conv_out, v_w_att_out, v_w_o, v_w_ffn_in, v_w_ffn_out):
    nb, s, d = x.shape
    nl = ada_w.shape[0]
    cw = conv_b.shape[1]
    aw = w_att_out.shape[1]
    cl = conv_w.shape[2]
    n6 = ada_w.shape[2]
    t = nb * s
    me = _my_index()

    conv_flat = conv_w.reshape(-1)
    p1 = _pack_rows([c, conv_flat], d)
    r1 = p1.shape[0]
    (p1_all,) = _all_gather([p1], "gather_c_convw", True)
    p1_all = p1_all.reshape(N_DEV, r1 * d)
    c_all = p1_all[:, :nb * d].reshape(N_DEV * nb, d)
    conv_w_all = p1_all[:, nb * d:nb * d + conv_flat.size].reshape(N_DEV, nl, CONV_K, cl)
    conv_w_all = conv_w_all.transpose(1, 2, 0, 3).reshape(nl, CONV_K, cw)

    ada_b_loc = lax.dynamic_slice_in_dim(ada_b, me * n6, n6, axis=1).reshape(nl, 1, n6)
    mod_cols = _ada_fwd(c_all, ada_w, ada_b_loc, "ada_fwd")
    (mod_g,) = _all_gather([mod_cols.reshape(-1, d)], "gather_mod", True)
    mod_g = mod_g.reshape(N_DEV, nl, N_DEV * nb, n6)
    mod_mine = lax.dynamic_slice_in_dim(mod_g, me * nb, nb, axis=2)
    mod = mod_mine.transpose(1, 2, 0, 3).reshape(nl, nb, 6 * d)
    mods = [[mod[l, :, i * d:(i + 1) * d].reshape(nb, 1, d) for i in range(6)] for l in range(nl)]

    tr_ = lambda w: jnp.swapaxes(w, 0, 1).astype(BF16)
    shards = [[tr_(w_in[l]), tr_(w_conv_out[l]), tr_(w_att_out[l]), w_o[l].astype(BF16),
               tr_(w_ffn_in[l]), w_ffn_out[l].astype(BF16)] for l in range(nl)]
    weights = [list(_all_gather(shards[0][:3], "gather_weights", False))]

    vec = lambda a, l: a[l].reshape(1, -1)
    x2 = x.reshape(t, d)
    tgt = loss_target.reshape(t, d)

    saved = []
    xin = x2
    h = _norm_mod(xin, vec(pre_mix_g, 0), mods[0][1], mods[0][0], s, "norm_mod_0")
    for l in range(nl):
        win_t, wconv_t, watt_t = weights[l][:3]
        sh1, sc1, ga1, sh2, sc2, ga2 = mods[l]
        glu_in = _mm(h, win_t, NT, F32, f"proj_glu_{l}", 0, 2 * cw)
        qkv = _mm(h, win_t, NT, BF16, f"proj_qkv_{l}", 2 * cw, 3 * aw)
        gates = _mm(h, win_t, NT, BF16, f"proj_gates_{l}", 2 * cw + 3 * aw, 2 * d)
        u, cv, ua = _conv_fwd(glu_in, conv_w_all[l], vec(conv_b, l), vec(conv_ln_g, l), vec(conv_ln_b, l),
                              nb, s, f"conv_fwd_{l}")
        o, car, late = _sb4_fwd(qkv, nb, s, f"attn_fwd_{l}", shards[l][3:])
        weights[l] = weights[l] + late
        wo, wffn_in_t, wffn_out = late
        y_conv = _mm(ua, wconv_t, NT, BF16, f"conv_out_{l}")
        y_att = _mm(o, watt_t, NT, BF16, f"att_out_{l}")
        merged = _merge(y_conv, y_att, gates, s, f"merge_{l}")
        y = _mm(merged, wo, NN, F32, f"w_o_{l}")
        x1, h2 = _post_res_norm(xin, y, vec(post_mix_g, l), ga1, vec(pre_ffn_g, l), sc2, sh2, s, f"post_mix_{l}")
        if l + 1 < nl:
            f, early = _mm(h2, wffn_in_t, NT, BF16, f"ffn_in_{l}", gather=shards[l + 1][:3])
            weights.append(early)
        else:
            f = _mm(h2, wffn_in_t, NT, BF16, f"ffn_in_{l}")
        a = _swiglu(f, s, f"swiglu_{l}")
        y2 = _mm(a, wffn_out, NN, F32, f"ffn_out_{l}")
        saved.append(dict(xin=xin, h=h, glu_in=glu_in, gates=gates, u=u, cv=cv, ua=ua, qkv=qkv,
                          car=car, o=o, y_conv=y_conv, y_att=y_att, merged=merged, y=y, x1=x1, h2=h2, f=f,
                          a=a, y2=y2))
        if l + 1 < nl:
            nsh1, nsc1 = mods[l + 1][0], mods[l + 1][1]
            xin, h = _post_res_norm(x1, y2, vec(post_ffn_g, l), ga2, vec(pre_mix_g, l + 1), nsc1, nsh1, s,
                                    f"post_ffn_{l}")
        else:
            dx, loss_acc = _post_res_loss(x1, y2, vec(post_ffn_g, l), ga2, tgt, s, "loss")

    slots = [None] * (6 * nl)
    small = [None] * nl
    for l in reversed(range(nl)):
        win_t, wconv_t, watt_t, wo, wffn_in_t, wffn_out = weights[l]
        sh1, sc1, ga1, sh2, sc2, ga2 = mods[l]
        sv = saved[l]
        dy2, dg_post_ffn, dga2 = _post_bwd(dx, sv["y2"], vec(post_ffn_g, l), ga2, s, f"post_ffn_bwd_{l}")
        d_wffn_out = _mm(sv["a"], dy2, TN, BF16, f"d_w_ffn_out_{l}")
        da = _mm(dy2, wffn_out, NT, BF16, f"d_a_{l}")
        df = _swiglu_bwd(da, sv["f"], s, f"swiglu_bwd_{l}")
        d_wffn_in_t = _mm(df, sv["h2"], TN, BF16, f"d_w_ffn_in_{l}")
        dh2 = _mm(df, wffn_in_t, NN, F32, f"d_h2_{l}")
        dx1, dsh2, dsc2, dg_pre_ffn = _pre_bwd(dh2, sv["x1"], vec(pre_ffn_g, l), sc2, dx, s, f"pre_ffn_bwd_{l}")
        dy, dg_post_mix, dga1 = _post_bwd(dx1, sv["y"], vec(post_mix_g, l), ga1, s, f"post_mix_bwd_{l}")
        d_wo = _mm(sv["merged"], dy, TN, BF16, f"d_w_o_{l}")
        dmerged = _mm(dy, wo, NT, BF16, f"d_merged_{l}")
        dyc, dya, dgates = _merge_bwd(dmerged, sv["y_conv"], sv["y_att"], sv["gates"], s, f"merge_bwd_{l}")
        d_wconv_t = _mm(dyc, sv["ua"], TN, BF16, f"d_w_conv_out_{l}")
        dua = _mm(dyc, wconv_t, NN, F32, f"d_ua_{l}")
        d_watt_t = _mm(dya, sv["o"], TN, BF16, f"d_w_att_out_{l}")
        do = _mm(dya, watt_t, NN, BF16, f"d_o_{l}")
        ready = [(5, d_wffn_out), (4, d_wffn_in_t), (3, d_wo), (1, d_wconv_t), (2, d_watt_t)]
        dq, dk, dv, got = _sb4_bwd(sv["qkv"], do, sv["car"], nb, s, f"attn_bwd_{l}", [g for _, g in ready])
        for (gi, _), r in zip(ready, got):
            slots[6 * l + gi] = r
        dcv, dln_g, dln_b, dconv_b = _convln_bwd(dua, sv["cv"], vec(conv_ln_g, l), vec(conv_ln_b, l), s,
                                                 f"convln_bwd_{l}")
        dglu, dconv_w = _conv_bwd(dcv, sv["u"], sv["glu_in"], conv_w_all[l], nb, s, f"conv_bwd_{l}")
        dproj = jnp.concatenate([dglu, dq, dk, dv, dgates], axis=1)
        d_win_t = _mm(dproj, sv["h"], TN, BF16, f"d_w_in_{l}")
        dh, (slots[6 * l],) = _mm(dproj, win_t, NN, F32, f"d_h_{l}", exchange=[d_win_t])
        dx, dsh1, dsc1, dg_pre_mix = _pre_bwd(dh, sv["xin"], vec(pre_mix_g, l), sc1, dx1, s, f"pre_mix_bwd_{l}")
        dmod = jnp.concatenate([dsh1, dsc1, dga1, dsh2, dsc2, dga2], axis=2)
        small[l] = [dg_pre_mix, dg_post_mix, dg_pre_ffn, dg_post_ffn, dconv_b, dln_g, dln_b, dconv_w, dmod]
    grad_x = dx.reshape(nb, s, d)

    sizes = [a.size for a in small[0]]
    p3 = _pack_rows([a for l in range(nl) for a in small[l]] + [loss_acc[0, :1]], d)
    r3 = p3.shape[0]
    (p3_all,) = _all_gather([p3], "gather_small_grads", True)
    p3_all = p3_all.reshape(N_DEV, r3, d)
    p3_sum = _sum_slots(p3_all, "sum_small_grads").reshape(-1)
    per_layer = sum(sizes)
    loss = p3_sum[nl * per_layer]
    small_sum, dmod_all = [], []
    for l in range(nl):
        off, parts = l * per_layer, []
        for sz in sizes[:-1]:
            parts.append(p3_sum[off:off + sz])
            off += sz
        small_sum.append(parts)
        dm = p3_all.reshape(N_DEV, r3 * d)[:, off:off + sizes[-1]]
        dmod_all.append(dm.reshape(N_DEV * nb, 6 * d))
    dmod_all = jnp.stack(dmod_all)
    dmod_loc = lax.dynamic_slice_in_dim(dmod_all, me * n6, n6, axis=2)
    g_ada_w, g_ada_b = _ada_bwd(c_all.T, dmod_loc, dmod_all, "ada_bwd")
    g_ada_b = g_ada_b.reshape(nl, 6 * d)

    stack_small = lambda i, shape: jnp.stack([small_sum[l][i].reshape(shape) for l in range(nl)])
    g_pre_mix, g_post_mix, g_pre_ffn, g_post_ffn = (stack_small(i, (d,)) for i in range(4))
    g_conv_b, g_ln_g, g_ln_b = (stack_small(i, (cw,)) for i in range(4, 7))
    g_conv_w_all = stack_small(7, (HALO, cw))[:, :CONV_K]
    g_conv_w = lax.dynamic_slice_in_dim(g_conv_w_all, me * cl, cl, axis=2)

    sums = [_sum_slots(r, f"sum_grads_{i}") for i, r in enumerate(slots)]
    un_t = lambda i: jnp.stack([jnp.swapaxes(sums[6 * l + i], 0, 1) for l in range(nl)])
    keep = lambda i: jnp.stack([sums[6 * l + i] for l in range(nl)])
    g_w_in, g_w_conv_out, g_w_att_out, g_w_o, g_w_ffn_in, g_w_ffn_out = (
        un_t(0), un_t(1), un_t(2), keep(3), un_t(4), keep(5))

    grads = [g_ada_w, g_ada_b, g_pre_mix, g_post_mix, g_pre_ffn, g_post_ffn, g_w_in, g_conv_w, g_conv_b,
             g_ln_g, g_ln_b, g_w_conv_out, g_w_att_out, g_w_o, g_w_ffn_in, g_w_ffn_out]
    ws = [ada_w, ada_b, pre_mix_g, post_mix_g, pre_ffn_g, post_ffn_g, w_in, conv_w, conv_b, conv_ln_g,
          conv_ln_b, w_conv_out, w_att_out, w_o, w_ffn_in, w_ffn_out]
    ms = [m_ada_w, m_ada_b, m_pre_mix_g, m_post_mix_g, m_pre_ffn_g, m_post_ffn_g, m_w_in, m_conv_w, m_conv_b,
          m_conv_ln_g, m_conv_ln_b, m_w_conv_out, m_w_att_out, m_w_o, m_w_ffn_in, m_w_ffn_out]
    vs = [v_ada_w, v_ada_b, v_pre_mix_g, v_post_mix_g, v_pre_ffn_g, v_post_ffn_g, v_w_in, v_conv_w, v_conv_b,
          v_conv_ln_g, v_conv_ln_b, v_w_conv_out, v_w_att_out, v_w_o, v_w_ffn_in, v_w_ffn_out]
    deltas, new_m, new_v = [], [], []
    for i, (w, g, m, v) in enumerate(zip(ws, grads, ms, vs)):
        dlt, mn, vn = _adamw(w, g, m, v, f"adamw_{i}")
        deltas.append(dlt)
        new_m.append(mn)
        new_v.append(vn)
    return (loss, grad_x, *grads, *deltas, *new_m, *new_v)
```

```python
import math

import jax
import jax.numpy as jnp
from jax import lax
from jax.experimental import pallas as pl
from jax.experimental.pallas import tpu as pltpu

F32 = jnp.float32
BF16 = jnp.bfloat16
MESH = pl.DeviceIdType.MESH
N_DEV = 8
EPS = 1e-6
HEAD_DIM = 64
CONV_K = 31
HALO = 32
KEY_BLK = 256
ADAM_LR, ADAM_B1, ADAM_B2, ADAM_EPS, ADAM_WD, ADAM_STEP = 0.001, 0.9, 0.999, 1e-08, 0.01, 10

NT = (((1,), (1,)), ((), ()))
NN = (((1,), (0,)), ((), ()))
TN = (((0,), (0,)), ((), ()))


def _tile(n, cands):
    for t in cands:
        if n % t == 0:
            return t
    return n


def _lane_tile(n, cap):
    best = n
    for t in range(128, min(n, cap) + 1, 128):
        if n % t == 0:
            best = t
    return best


def _sig(x):
    return 1.0 / (1.0 + jnp.exp(-x))


def _pos():
    return lax.axis_index("x"), lax.axis_index("y"), lax.axis_index("c")


def _my_index():
    x, y, c = _pos()
    return 4 * x + 2 * y + c


def _all_gather(arrs, name, in_vmem):
    n = len(arrs)

    def body(*refs):
        ins, outs = refs[:n], refs[n:2 * n]
        send_sems, recv_sems, local_sems = refs[2 * n:]
        x, y, c = _pos()
        me, sib = (x, y, c), (x, y, 1 - c)
        chips = [(1 - x, y), (x, 1 - y), (1 - x, 1 - y)]

        def rows(i, p):
            r = ins[i].shape[0]
            return outs[i].at[pl.ds((4 * p[0] + 2 * p[1] + p[2]) * r, r), :]

        def copy(i, k, block, to, src=None):
            return pltpu.make_async_remote_copy(
                src_ref=rows(i, block) if src is None else src, dst_ref=rows(i, block),
                send_sem=send_sems.at[i, k], recv_sem=recv_sems.at[i, k],
                device_id=to, device_id_type=MESH)

        mine = [pltpu.make_async_copy(ins[i], rows(i, me), local_sems.at[i]) for i in range(n)]
        for cp in mine:
            cp.start()
        first = []
        for j, chip in enumerate(chips):
            for i in range(n):
                first.append(copy(i, 1 + j, me, (*chip, c), src=ins[i]))
        for i in range(n):
            first.append(copy(i, 0, me, sib, src=ins[i]))
        for cp in first:
            cp.start()
        passed = []
        for j, chip in enumerate(chips):
            for i in range(n):
                copy(i, 1 + j, (*chip, c), me).wait_recv()
                p = copy(i, 4 + j, (*chip, c), sib)
                p.start()
                passed.append(p)
        for i in range(n):
            copy(i, 0, sib, me).wait_recv()
            for j, chip in enumerate(chips):
                copy(i, 4 + j, (*chip, 1 - c), me).wait_recv()
        for cp in first + passed:
            cp.wait_send()
        for cp in mine:
            cp.wait()

    space = pltpu.VMEM if in_vmem else pl.ANY
    spec = pl.BlockSpec(memory_space=space)
    return pl.pallas_call(
        body, name=name,
        out_shape=[jax.ShapeDtypeStruct((N_DEV * a.shape[0], a.shape[1]), a.dtype) for a in arrs],
        in_specs=[spec] * n, out_specs=[spec] * n,
        scratch_shapes=[pltpu.SemaphoreType.DMA((n, 7)), pltpu.SemaphoreType.DMA((n, 7)),
                        pltpu.SemaphoreType.DMA((n,))],
    )(*arrs)


PEER_ORDER = (4, 2, 6, 5, 3, 7, 1)


def _peer(k):
    x, y, c = _pos()
    return (1 - x if k & 4 else x, 1 - y if k & 2 else y, 1 - c if k & 1 else c)


def _mesh_index(p):
    return 4 * p[0] + 2 * p[1] + p[2]


def _copy_sems(n):
    return [pltpu.SemaphoreType.DMA((n, 7)), pltpu.SemaphoreType.DMA((n, 7)), pltpu.SemaphoreType.DMA((n,))]


def _exchange_out_shapes(arrs):
    return [jax.ShapeDtypeStruct((N_DEV, a.shape[0] // N_DEV, a.shape[1]), a.dtype) for a in arrs]


def _gather_out_shapes(arrs):
    return [jax.ShapeDtypeStruct((N_DEV * a.shape[0], a.shape[1]), a.dtype) for a in arrs]


def _exchange_copies(ins, outs, send_sems, recv_sems, local_sems):
    n = len(ins)

    def block(i, p):
        r = outs[i].shape[1]
        return ins[i].at[pl.ds(_mesh_index(p) * r, r), :]

    mine = [pltpu.make_async_copy(block(i, _peer(0)), outs[i].at[0], local_sems.at[i]) for i in range(n)]
    sends = [pltpu.make_async_remote_copy(
        src_ref=block(i, _peer(k)), dst_ref=outs[i].at[k], send_sem=send_sems.at[i, k - 1],
        recv_sem=recv_sems.at[i, k - 1], device_id=_peer(k), device_id_type=MESH)
        for k in PEER_ORDER for i in range(n)]
    return mine, sends, sends


def _gather_copies(ins, outs, send_sems, recv_sems, local_sems):
    n = len(ins)

    def rows(i, p):
        r = ins[i].shape[0]
        return outs[i].at[pl.ds(_mesh_index(p) * r, r), :]

    def copy(i, k, landing):
        return pltpu.make_async_remote_copy(
            src_ref=ins[i], dst_ref=rows(i, landing), send_sem=send_sems.at[i, k - 1],
            recv_sem=recv_sems.at[i, k - 1], device_id=_peer(k), device_id_type=MESH)

    mine = [pltpu.make_async_copy(ins[i], rows(i, _peer(0)), local_sems.at[i]) for i in range(n)]
    sends = [copy(i, k, _peer(0)) for k in PEER_ORDER for i in range(n)]
    recvs = [copy(i, k, _peer(k)) for k in PEER_ORDER for i in range(n)]
    return mine, sends, recvs


def _start_copies(copies):
    mine, sends, _ = copies
    for cp in mine + sends:
        cp.start()


def _finish_copies(copies):
    mine, sends, recvs = copies
    for cp in recvs:
        cp.wait_recv()
    for cp in sends:
        cp.wait_send()
    for cp in mine:
        cp.wait()


def _sum_slots(r3, name):
    _, r, c = r3.shape
    tr = _tile(r, (128, 96, 64, 32, 16))

    def body(r_ref, o_ref):
        acc = r_ref[0].astype(F32)
        for k in range(1, N_DEV):
            acc = acc + r_ref[k].astype(F32)
        o_ref[...] = acc

    return pl.pallas_call(
        body, name=name, grid=(r // tr,),
        out_shape=jax.ShapeDtypeStruct((r, c), F32),
        in_specs=[pl.BlockSpec((N_DEV, tr, c), lambda i: (0, i, 0))],
        out_specs=pl.BlockSpec((tr, c), lambda i: (i, 0)),
    )(r3)


def _mm(a, b, dims, out_dtype, name, n0=0, n=None, exchange=(), gather=()):
    if dims is NT:
        m, k = a.shape
        n = b.shape[0] if n is None else n
    elif dims is NN:
        m, k = a.shape
        n = b.shape[1]
    else:
        k, m = a.shape
        n = b.shape[1]
    if dims is TN:
        tm = _lane_tile(m, 1536)
        tn = _lane_tile(n, 1024)
        tk = _tile(k, (1024, 512, 256, 128))
    else:
        tm = _tile(m, (1024, 512, 256, 128))
        tn = _lane_tile(math.gcd(n, n0) if n0 else n, 1536)
        tk = _lane_tile(k, 2816)
    nk = k // tk
    noff = n0 // tn

    moved = list(exchange) or list(gather)
    make_copies = _exchange_copies if exchange else _gather_copies
    nx = len(moved)
    grid = (m // tm, n // tn, nk)

    def body(*refs):
        a_ref, b_ref = refs[:2]
        o_ref = refs[2 + nx]
        scratch = refs[3 + 2 * nx:]
        if nx:
            copies = make_copies(refs[2:2 + nx], refs[3 + nx:3 + 2 * nx], *scratch[-3:])
            step_id = (pl.program_id(0) * grid[1] + pl.program_id(1)) * grid[2] + pl.program_id(2)
            pl.when(step_id == 0)(lambda: _start_copies(copies))
        part =lax.dot_general(a_ref[...].astype(BF16), b_ref[...].astype(BF16), dims,
                               preferred_element_type=F32)
        if nk == 1:
            o_ref[...] = part.astype(o_ref.dtype)
        else:
            acc_ref = scratch[0]
            kk = pl.program_id(2)

            @pl.when(kk == 0)
            def _():
                acc_ref[...] = part

            @pl.when(kk > 0)
            def _():
                acc_ref[...] += part

            @pl.when(kk == nk - 1)
            def _():
                o_ref[...] = acc_ref[...].astype(o_ref.dtype)

        if nx:
            pl.when(step_id == grid[0] * grid[1] * grid[2] - 1)(lambda: _finish_copies(copies))

    if dims is NT:
        in_specs = [pl.BlockSpec((tm, tk), lambda i, j, kk: (i, kk)),
                    pl.BlockSpec((tn, tk), lambda i, j, kk: (j + noff, kk))]
    elif dims is NN:
        in_specs = [pl.BlockSpec((tm, tk), lambda i, j, kk: (i, kk)),
                    pl.BlockSpec((tk, tn), lambda i, j, kk: (kk, j))]
    else:
        in_specs = [pl.BlockSpec((tk, tm), lambda i, j, kk: (kk, i)),
                    pl.BlockSpec((tk, tn), lambda i, j, kk: (kk, j))]
    hbm = pl.BlockSpec(memory_space=pl.ANY)
    outs = pl.pallas_call(
        body, name=name, grid=grid,
        out_shape=[jax.ShapeDtypeStruct((m, n), out_dtype)]
        + (_exchange_out_shapes(moved) if exchange else _gather_out_shapes(moved)),
        in_specs=in_specs + [hbm] * nx,
        out_specs=[pl.BlockSpec((tm, tn), lambda i, j, kk: (i, j))] + [hbm] * nx,
        scratch_shapes=([] if nk == 1 else [pltpu.VMEM((tm, tn), F32)]) + (_copy_sems(nx) if nx else []),
        compiler_params=pltpu.CompilerParams(
            dimension_semantics=("arbitrary",) * 3 if nx else ("parallel", "parallel", "arbitrary")),
    )(a, b, *moved)
    return (outs[0], list(outs[1:])) if nx else outs[0]


def _tok_tile(s):
    return _tile(s, (512, 256, 128))


def _row_spec(tm, d):
    return pl.BlockSpec((tm, d), lambda i: (i, 0))


def _vec_spec(d):
    return pl.BlockSpec((1, d), lambda i: (0, 0))


def _seq_spec(d, tiles_per_seq):
    return pl.BlockSpec((1, 1, d), lambda i: (i // tiles_per_seq, 0, 0))


def _rms(x):
    return lax.rsqrt(jnp.mean(x * x, axis=-1, keepdims=True) + EPS)


def _norm_mod(x, g, sc, sh, s, name):
    t, d = x.shape
    tm = _tok_tile(s)

    def body(x_ref, g_ref, sc_ref, sh_ref, h_ref):
        xv = x_ref[...]
        h = (xv * _rms(xv)) * g_ref[...]
        h_ref[...] = (h * (1.0 + sc_ref[0]) + sh_ref[0]).astype(BF16)

    return pl.pallas_call(
        body, name=name, grid=(t // tm,),
        out_shape=jax.ShapeDtypeStruct((t, d), BF16),
        in_specs=[_row_spec(tm, d), _vec_spec(d), _seq_spec(d, s // tm), _seq_spec(d, s // tm)],
        out_specs=_row_spec(tm, d),
    )(x, g, sc, sh)


def _post_res_norm(x, y, g_post, ga, g_pre, sc, sh, s, name):
    t, d = x.shape
    tm = _tok_tile(s)

    def body(x_ref, y_ref, gp_ref, ga_ref, g_ref, sc_ref, sh_ref, xn_ref, h_ref):
        yv = y_ref[...]
        xn = x_ref[...] + ga_ref[0] * ((yv * _rms(yv)) * gp_ref[...])
        xn_ref[...] = xn
        h = (xn * _rms(xn)) * g_ref[...]
        h_ref[...] = (h * (1.0 + sc_ref[0]) + sh_ref[0]).astype(BF16)

    tps = s // tm
    return pl.pallas_call(
        body, name=name, grid=(t // tm,),
        out_shape=[jax.ShapeDtypeStruct((t, d), F32), jax.ShapeDtypeStruct((t, d), BF16)],
        in_specs=[_row_spec(tm, d), _row_spec(tm, d), _vec_spec(d), _seq_spec(d, tps),
                  _vec_spec(d), _seq_spec(d, tps), _seq_spec(d, tps)],
        out_specs=[_row_spec(tm, d), _row_spec(tm, d)],
    )(x, y, g_post, ga, g_pre, sc, sh)


def _post_res_loss(x, y, g_post, ga, target, s, name):
    t, d = x.shape
    tm = _tok_tile(s)

    def body(x_ref, y_ref, gp_ref, ga_ref, t_ref, dx_ref, loss_ref):
        yv = y_ref[...]
        err = x_ref[...] + ga_ref[0] * ((yv * _rms(yv)) * gp_ref[...]) - t_ref[...]
        dx_ref[...] = err * (1.0 / d)
        part = 0.5 * jnp.sum(jnp.mean(err * err, axis=-1, keepdims=True), axis=0, keepdims=True)

        @pl.when(pl.program_id(0) == 0)
        def _():
            loss_ref[...] = jnp.zeros_like(loss_ref)

        loss_ref[...] += part

    return pl.pallas_call(
        body, name=name, grid=(t // tm,),
        out_shape=[jax.ShapeDtypeStruct((t, d), F32), jax.ShapeDtypeStruct((8, 128), F32)],
        in_specs=[_row_spec(tm, d), _row_spec(tm, d), _vec_spec(d), _seq_spec(d, s // tm), _row_spec(tm, d)],
        out_specs=[_row_spec(tm, d), pl.BlockSpec((8, 128), lambda i: (0, 0))],
    )(x, y, g_post, ga, target)


def _post_bwd(dxn, y, g, ga, s, name):
    t, d = y.shape
    tm = _tok_tile(s)
    tps = s // tm

    def body(dx_ref, y_ref, g_ref, ga_ref, dy_ref, dg_ref, dga_ref):
        i = pl.program_id(0)
        yv = y_ref[...]
        r = _rms(yv)
        yh = yv * r
        dxv = dx_ref[...]
        dn = dxv * ga_ref[0]
        dyh = dn * g_ref[...]
        dy_ref[...] = (r * (dyh - yh * jnp.mean(dyh * yh, axis=-1, keepdims=True))).astype(BF16)

        @pl.when(i == 0)
        def _():
            dg_ref[...] = jnp.zeros_like(dg_ref)

        @pl.when(i % tps == 0)
        def _():
            dga_ref[...] = jnp.zeros_like(dga_ref)

        dg_ref[...] += jnp.sum(dn * yh, axis=0, keepdims=True)
        dga_ref[0] += jnp.sum(dxv * (yh * g_ref[...]), axis=0, keepdims=True)

    return pl.pallas_call(
        body, name=name, grid=(t // tm,),
        out_shape=[jax.ShapeDtypeStruct((t, d), BF16), jax.ShapeDtypeStruct((1, d), F32),
                   jax.ShapeDtypeStruct((t // s, 1, d), F32)],
        in_specs=[_row_spec(tm, d), _row_spec(tm, d), _vec_spec(d), _seq_spec(d, tps)],
        out_specs=[_row_spec(tm, d), _vec_spec(d), _seq_spec(d, tps)],
    )(dxn, y, g, ga)


def _pre_bwd(dh, x, g, sc, dres, s, name):
    t, d = x.shape
    tm = _tok_tile(s)
    tps = s // tm

    def body(dh_ref, x_ref, g_ref, sc_ref, dr_ref, dx_ref, dsh_ref, dsc_ref, dg_ref):
        i = pl.program_id(0)
        xv = x_ref[...]
        r = _rms(xv)
        xh = xv * r
        dhv = dh_ref[...]
        one_sc = 1.0 + sc_ref[0]
        dxh = dhv * one_sc * g_ref[...]
        dx_ref[...] = dr_ref[...] + r * (dxh - xh * jnp.mean(dxh * xh, axis=-1, keepdims=True))

        @pl.when(i == 0)
        def _():
            dg_ref[...] = jnp.zeros_like(dg_ref)

        @pl.when(i % tps == 0)
        def _():
            dsh_ref[...] = jnp.zeros_like(dsh_ref)
            dsc_ref[...] = jnp.zeros_like(dsc_ref)

        dg_ref[...] += jnp.sum(dhv * one_sc * xh, axis=0, keepdims=True)
        dsh_ref[0] += jnp.sum(dhv, axis=0, keepdims=True)
        dsc_ref[0] += jnp.sum(dhv * (xh * g_ref[...]), axis=0, keepdims=True)

    nb = t // s
    return pl.pallas_call(
        body, name=name, grid=(t // tm,),
        out_shape=[jax.ShapeDtypeStruct((t, d), F32), jax.ShapeDtypeStruct((nb, 1, d), F32),
                   jax.ShapeDtypeStruct((nb, 1, d), F32), jax.ShapeDtypeStruct((1, d), F32)],
        in_specs=[_row_spec(tm, d), _row_spec(tm, d), _vec_spec(d), _seq_spec(d, tps), _row_spec(tm, d)],
        out_specs=[_row_spec(tm, d), _seq_spec(d, tps), _seq_spec(d, tps), _vec_spec(d)],
    )(dh, x, g, sc, dres)


def _merge(y_conv, y_att, gates, s, name):
    t, d = y_conv.shape
    tm = _tok_tile(s)

    def body(yc_ref, ya_ref, gc_ref, gt_ref, o_ref):
        f32 = lambda r: r[...].astype(F32)
        o_ref[...] = (_sig(f32(gc_ref)) * f32(yc_ref) + _sig(f32(gt_ref)) * f32(ya_ref)).astype(BF16)

    return pl.pallas_call(
        body, name=name, grid=(t // tm,),
        out_shape=jax.ShapeDtypeStruct((t, d), BF16),
        in_specs=[_row_spec(tm, d), _row_spec(tm, d), pl.BlockSpec((tm, d), lambda i: (i, 0)),
                  pl.BlockSpec((tm, d), lambda i: (i, 1))],
        out_specs=_row_spec(tm, d),
    )(y_conv, y_att, gates, gates)


def _merge_bwd(dm, y_conv, y_att, gates, s, name):
    t, d = y_conv.shape
    tm = _tok_tile(s)

    def body(dm_ref, yc_ref, ya_ref, gc_ref, gt_ref, dyc_ref, dya_ref, dg_ref):
        f32 = lambda r: r[...].astype(F32)
        dmv = f32(dm_ref)
        sc_, st_ = _sig(f32(gc_ref)), _sig(f32(gt_ref))
        dyc_ref[...] = (dmv * sc_).astype(BF16)
        dya_ref[...] = (dmv * st_).astype(BF16)
        dg_ref[:, :d] = (dmv * f32(yc_ref) * (sc_ * (1.0 - sc_))).astype(BF16)
        dg_ref[:, d:] = (dmv * f32(ya_ref) * (st_ * (1.0 - st_))).astype(BF16)

    return pl.pallas_call(
        body, name=name, grid=(t // tm,),
        out_shape=[jax.ShapeDtypeStruct((t, d), BF16), jax.ShapeDtypeStruct((t, d), BF16),
                   jax.ShapeDtypeStruct((t, 2 * d), BF16)],
        in_specs=[_row_spec(tm, d), _row_spec(tm, d), _row_spec(tm, d),
                  pl.BlockSpec((tm, d), lambda i: (i, 0)), pl.BlockSpec((tm, d), lambda i: (i, 1))],
        out_specs=[_row_spec(tm, d), _row_spec(tm, d), _row_spec(tm, 2 * d)],
    )(dm, y_conv, y_att, gates, gates)


def _swiglu(f, s, name):
    t, two = f.shape
    dff = two // 2
    tm = _tok_tile(s)

    def body(g_ref, u_ref, a_ref):
        gv = g_ref[...].astype(F32)
        a_ref[...] = (gv * _sig(gv) * u_ref[...].astype(F32)).astype(BF16)

    return pl.pallas_call(
        body, name=name, grid=(t // tm,),
        out_shape=jax.ShapeDtypeStruct((t, dff), BF16),
        in_specs=[pl.BlockSpec((tm, dff), lambda i: (i, 0)), pl.BlockSpec((tm, dff), lambda i: (i, 1))],
        out_specs=_row_spec(tm, dff),
    )(f, f)


def _swiglu_bwd(da, f, s, name):
    t, two = f.shape
    dff = two // 2
    tm = _tok_tile(s)

    def body(da_ref, g_ref, u_ref, df_ref):
        gv, dav = g_ref[...].astype(F32), da_ref[...].astype(F32)
        sg = _sig(gv)
        df_ref[:, :dff] = (dav * u_ref[...].astype(F32) * (sg * (1.0 + gv * (1.0 - sg)))).astype(BF16)
        df_ref[:, dff:] = (dav * (gv * sg)).astype(BF16)

    return pl.pallas_call(
        body, name=name, grid=(t // tm,),
        out_shape=jax.ShapeDtypeStruct((t, two), BF16),
        in_specs=[_row_spec(tm, dff), pl.BlockSpec((tm, dff), lambda i: (i, 0)),
                  pl.BlockSpec((tm, dff), lambda i: (i, 1))],
        out_specs=_row_spec(tm, two),
    )(da, f, f)


CONV_CHUNK = 64
SUB = 8


def _tap_groups(offsets):
    groups = {}
    for k, o in enumerate(offsets):
        groups.setdefault(o % SUB, []).append((k, o - o % SUB))
    return groups


def _conv_fwd(glu_in, conv_w, conv_b, ln_g, ln_b, nb, s, name):
    t, two = glu_in.shape
    cw = two // 2
    tt = _tile(s, (256, 128, 64))
    spt = s // tt
    ch = min(CONV_CHUNK, tt)

    groups = _tap_groups([HALO - (CONV_K - 1) + k for k in range(CONV_K)])

    def body(cur_ref, halo_ref, w_ref, b_ref, g_ref, be_ref, u_ref, cv_ref, ua_ref, ext_ref, part_ref):
        j = pl.program_id(1)
        cur = cur_ref[...]
        u_cur = cur[:, :cw] * _sig(cur[:, cw:])
        hal = halo_ref[...]
        u_hal = hal[:, :cw] * _sig(hal[:, cw:])
        ext_ref[0:HALO, :] = jnp.where(j == 0, 0.0, u_hal)
        ext_ref[HALO:HALO + tt, :] = u_cur
        ext_ref[HALO + tt:, :] = jnp.zeros((SUB, cw), F32)
        u_ref[...] = u_cur
        for c0 in range(0, tt, ch):
            acc = jnp.zeros((ch, cw), F32) + b_ref[...]
            for r, taps in groups.items():
                part = None
                for k, base in taps:
                    term = w_ref[k:k + 1, :] * ext_ref[c0 + base:c0 + base + ch + SUB, :]
                    part = term if part is None else part + term
                if r == 0:
                    acc = acc + part[:ch]
                else:
                    part_ref[r] = part
                    acc = acc + part_ref[r, r:r + ch, :]
            cv_ref[c0:c0 + ch, :] = acc
            mu = jnp.mean(acc, axis=-1, keepdims=True)
            xc = acc - mu
            yv = xc * lax.rsqrt(jnp.mean(xc * xc, axis=-1, keepdims=True) + EPS) * g_ref[...] + be_ref[...]
            ua_ref[c0:c0 + ch, :] = (yv * _sig(yv)).astype(BF16)

    cur_map = lambda b, j: (b * spt + j, 0)
    halo_map = lambda b, j: (jnp.maximum((b * s + j * tt) // HALO - 1, 0), 0)
    vec = lambda d0: pl.BlockSpec((d0, cw), lambda b, j: (0, 0))
    return pl.pallas_call(
        body, name=name, grid=(nb, spt),
        out_shape=[jax.ShapeDtypeStruct((t, cw), F32), jax.ShapeDtypeStruct((t, cw), F32),
                   jax.ShapeDtypeStruct((t, cw), BF16)],
        in_specs=[pl.BlockSpec((tt, two), cur_map), pl.BlockSpec((HALO, two), halo_map),
                  vec(CONV_K), vec(1), vec(1), vec(1)],
        out_specs=[pl.BlockSpec((tt, cw), cur_map)] * 3,
        scratch_shapes=[pltpu.VMEM((tt + HALO + SUB, cw), F32), pltpu.VMEM((SUB, ch + SUB, cw), F32)],
    )(glu_in, glu_in, conv_w, conv_b, ln_g, ln_b)


def _convln_bwd(dua, cv, ln_g, ln_b, s, name):
    t, cw = cv.shape
    tm = _tok_tile(s)

    def body(du_ref, cv_ref, g_ref, be_ref, dcv_ref, dg_ref, dbe_ref, db_ref):
        cvv = cv_ref[...]
        mu = jnp.mean(cvv, axis=-1, keepdims=True)
        xc = cvv - mu
        rstd = lax.rsqrt(jnp.mean(xc * xc, axis=-1, keepdims=True) + EPS)
        xh = xc * rstd
        yv = xh * g_ref[...] + be_ref[...]
        sg = _sig(yv)
        dy = du_ref[...] * (sg * (1.0 + yv * (1.0 - sg)))
        dxh = dy * g_ref[...]
        dcv = rstd * (dxh - jnp.mean(dxh, axis=-1, keepdims=True)
                      - xh * jnp.mean(dxh * xh, axis=-1, keepdims=True))
        dcv_ref[...] = dcv

        @pl.when(pl.program_id(0) == 0)
        def _():
            dg_ref[...] = jnp.zeros_like(dg_ref)
            dbe_ref[...] = jnp.zeros_like(dbe_ref)
            db_ref[...] = jnp.zeros_like(db_ref)

        dg_ref[...] += jnp.sum(dy * xh, axis=0, keepdims=True)
        dbe_ref[...] += jnp.sum(dy, axis=0, keepdims=True)
        db_ref[...] += jnp.sum(dcv, axis=0, keepdims=True)

    return pl.pallas_call(
        body, name=name, grid=(t // tm,),
        out_shape=[jax.ShapeDtypeStruct((t, cw), F32)] + [jax.ShapeDtypeStruct((1, cw), F32)] * 3,
        in_specs=[_row_spec(tm, cw), _row_spec(tm, cw), _vec_spec(cw), _vec_spec(cw)],
        out_specs=[_row_spec(tm, cw), _vec_spec(cw), _vec_spec(cw), _vec_spec(cw)],
    )(dua, cv, ln_g, ln_b)


def _conv_bwd(dcv, u, glu_in, conv_w, nb, s, name):
    t, cw = dcv.shape
    tt = _tile(s, (256, 128, 64))
    spt = s // tt
    ch = min(CONV_CHUNK, tt)
    nblk = t // HALO

    d_groups = _tap_groups([(CONV_K - 1) - k for k in range(CONV_K)])
    u_groups = _tap_groups([HALO - (CONV_K - 1) + k for k in range(CONV_K)])

    def body(d_ref, dn_ref, u_ref, up_ref, glu_ref, w_ref, dglu_ref, dw_ref, dext_ref, uext_ref, part_ref, dwp_ref):
        b, j = pl.program_id(0), pl.program_id(1)
        dcur = d_ref[...]
        dext_ref[0:tt, :] = dcur
        dext_ref[tt:tt + HALO, :] = jnp.where(j == spt - 1, 0.0, dn_ref[...])
        uext_ref[0:HALO, :] = jnp.where(j == 0, 0.0, up_ref[...])
        uext_ref[HALO:HALO + tt, :] = u_ref[...]
        uext_ref[HALO + tt:, :] = jnp.zeros((SUB, cw), F32)

        @pl.when((b == 0) & (j == 0))
        def _():
            dwp_ref[...] = jnp.zeros_like(dwp_ref)

        for c0 in range(0, tt, ch):
            dc = dext_ref[c0:c0 + ch, :]
            du = jnp.zeros((ch, cw), F32)
            for r, taps in d_groups.items():
                part = None
                for k, base in taps:
                    term = w_ref[k:k + 1, :] * dext_ref[c0 + base:c0 + base + ch + SUB, :]
                    part = term if part is None else part + term
                if r == 0:
                    du = du + part[:ch]
                else:
                    part_ref[r] = part
                    du = du + part_ref[r, r:r + ch, :]
            for r, taps in u_groups.items():
                ush = uext_ref[c0 + r:c0 + r + ch + HALO, :]
                for k, base in taps:
                    prod = dc * ush[base:base + ch]
                    rows8 = prod[0:SUB]
                    for i in range(1, ch // SUB):
                        rows8 = rows8 + prod[i * SUB:(i + 1) * SUB]
                    dwp_ref[k] += rows8
            val = glu_ref[c0:c0 + ch, :cw]
            sg = _sig(glu_ref[c0:c0 + ch, cw:])
            dglu_ref[c0:c0 + ch, :cw] = (du * sg).astype(BF16)
            dglu_ref[c0:c0 + ch, cw:] = (du * val * (sg * (1.0 - sg))).astype(BF16)

        @pl.when((b == nb - 1) & (j == spt - 1))
        def _():
            dw_ref[...] = jnp.sum(dwp_ref[...], axis=1)

    cur_map = lambda b, j: (b * spt + j, 0)
    prev_map = lambda b, j: (jnp.maximum((b * s + j * tt) // HALO - 1, 0), 0)
    next_map = lambda b, j: (jnp.minimum((b * s + (j + 1) * tt) // HALO, nblk - 1), 0)
    return pl.pallas_call(
        body, name=name, grid=(nb, spt),
        out_shape=[jax.ShapeDtypeStruct((t, 2 * cw), BF16), jax.ShapeDtypeStruct((HALO, cw), F32)],
        in_specs=[pl.BlockSpec((tt, cw), cur_map), pl.BlockSpec((HALO, cw), next_map),
                  pl.BlockSpec((tt, cw), cur_map), pl.BlockSpec((HALO, cw), prev_map),
                  pl.BlockSpec((tt, 2 * cw), cur_map), pl.BlockSpec((CONV_K, cw), lambda b, j: (0, 0))],
        out_specs=[pl.BlockSpec((tt, 2 * cw), cur_map), pl.BlockSpec((HALO, cw), lambda b, j: (0, 0))],
        scratch_shapes=[pltpu.VMEM((tt + HALO, cw), F32), pltpu.VMEM((tt + HALO + SUB, cw), F32),
                        pltpu.VMEM((SUB, ch + SUB, cw), F32), pltpu.VMEM((HALO, SUB, cw), F32)],
    )(dcv, dcv, u, u, glu_in, conv_w)


def _split(x):
    hi = x.astype(BF16)
    return hi, (x - hi.astype(F32)).astype(BF16)


def _tri(upper):
    j = lax.broadcasted_iota(jnp.int32, (KEY_BLK, KEY_BLK), 0)
    s_ = lax.broadcasted_iota(jnp.int32, (KEY_BLK, KEY_BLK), 1)
    return jnp.where(j > s_ if upper else j < s_, 1.0, 0.0).astype(BF16)


def _log_keep(z, causal):
    sp = jnp.maximum(z, 0.0) + jnp.log(1.0 + jnp.exp(-jnp.abs(z)))
    return -sp if causal is None else jnp.where(causal, -sp, 0.0)


LANES = 128


HALF_Q = 256
DEAD = -105.0
SKIPPED = -1e30


def _any_live(x):
    return (jnp.max(x) > DEAD).astype(jnp.int32)


def _cumsum_mat(upper):
    m = _tri(upper)
    return jnp.concatenate([m, m], axis=0)


def _cum(x, mat):
    hi, lo = _split(x)
    return jnp.dot(jnp.concatenate([hi, lo], axis=1), mat, preferred_element_type=F32)


def _wide(x):
    return jnp.concatenate([x] * (KEY_BLK // LANES), axis=1)


def _sb4_fwd(qkv, nb, s, name, gather=()):
    t, three_aw = qkv.shape
    aw = three_aw // 3
    npair = aw // LANES
    tq = 2 * HALF_Q
    spt = s // tq
    scale = 1.0 / math.sqrt(HEAD_DIM)

    ng = len(gather)

    def body(*refs):
        q_ref, k_ref, v_ref = refs[:3]
        o_ref, car_ref = refs[3 + ng:5 + ng]
        acc_ref, run_ref = refs[5 + 2 * ng:7 + 2 * ng]
        if ng:
            copies = _gather_copies(refs[3:3 + ng], refs[5 + ng:5 + 2 * ng], *refs[7 + 2 * ng:])
            step_id = (pl.program_id(0) * npair + pl.program_id(1)) * spt + pl.program_id(2)
            pl.when(step_id == 0)(lambda: _start_copies(copies))
        qi = pl.program_id(2)
        lane = lax.broadcasted_iota(jnp.int32, (HALF_Q, LANES), 1)
        col = lax.broadcasted_iota(jnp.int32, (HALF_Q, KEY_BLK), 1)
        rowi = lax.broadcasted_iota(jnp.int32, (HALF_Q, KEY_BLK), 0)
        first = lane < HEAD_DIM
        mat = _cumsum_mat(True)
        qs = []
        for half in range(2):
            q2 = q_ref[half * HALF_Q:(half + 1) * HALF_Q, :] * scale
            zero = jnp.zeros_like(q2)
            qs.append((jnp.where(first, q2, zero), jnp.where(first, zero, q2)))
        acc_ref[...] = jnp.zeros_like(acc_ref)
        run_ref[...] = jnp.zeros_like(run_ref)
        car_ref[...] = jnp.zeros_like(car_ref)
        nk = (qi * tq + tq) // KEY_BLK

        def step(kb, halves):
            off = pl.multiple_of(kb * KEY_BLK, KEY_BLK)
            kblk = k_ref[pl.ds(off, KEY_BLK), :]
            vblk = v_ref[pl.ds(off, KEY_BLK), :]
            chains = [(half, masked, h) for half, masked in halves for h in range(2)]
            causal = {half: ((off + col) < (qi * tq + half * HALF_Q + rowi)) if masked else None
                      for half, masked in halves}
            zs = [lax.dot_general(qs[half][h], kblk, NT, preferred_element_type=F32) for half, _, h in chains]
            lks = [_log_keep(z, causal[c[0]]) for z, c in zip(zs, chains)]
            css = [_cum(lk, mat) for lk in lks]
            runs = [run_ref[2 * half + h] for half, _, h in chains]
            aa = [jnp.exp(z + lk + cs + _wide(run)) for z, lk, cs, run in zip(zs, lks, css, runs)]
            aa = [jnp.where(causal[c[0]], a, 0.0) if c[1] else a for a, c in zip(aa, chains)]
            for a, (half, _, h) in zip(aa, chains):
                acc_ref[2 * half + h] += jnp.dot(a.astype(BF16), vblk, preferred_element_type=F32)
            for lk, run, (half, _, h) in zip(lks, runs, chains):
                rows = pl.ds(half * HALF_Q, HALF_Q)
                car_ref[h, rows, :] = jnp.where(lane == kb, run, car_ref[h, rows, :])
                run_ref[2 * half + h] = run + jnp.sum(lk, axis=1, keepdims=True)

        step(nk - 1, ((1, True),))
        step(nk - 2, ((0, True), (1, False)))

        def more(c):
            return (c[0] < nk) & (c[1] > 0)

        def rest(c):
            kb = nk - 1 - c[0]
            lower_live = _any_live(run_ref[2:4]) > 0
            pl.when(lower_live)(lambda: step(kb, ((0, False), (1, False))))

            @pl.when(jnp.logical_not(lower_live))
            def _():
                step(kb, ((0, False),))
                for h in range(2):
                    rows = pl.ds(HALF_Q, HALF_Q)
                    car_ref[h, rows, :] = jnp.where(lane == kb, SKIPPED, car_ref[h, rows, :])

            return c[0] + 1, _any_live(run_ref[...])

        visited, _ = lax.while_loop(more, rest, (jnp.int32(2), _any_live(run_ref[...])))
        unvisited = lane < (nk - visited)
        for h in range(2):
            car_ref[h] = jnp.where(jnp.concatenate([unvisited, unvisited], axis=0), SKIPPED, car_ref[h])
        for half in range(2):
            o_ref[half * HALF_Q:(half + 1) * HALF_Q, :] = jnp.where(
                first, acc_ref[2 * half], acc_ref[2 * half + 1]).astype(BF16)
        if ng:
            pl.when(step_id == nb * npair * spt - 1)(lambda: _finish_copies(copies))

    hbm = pl.BlockSpec(memory_space=pl.ANY)
    outs = pl.pallas_call(
        body, name=name, grid=(nb, npair, spt),
        out_shape=[jax.ShapeDtypeStruct((t, aw), BF16),
                   jax.ShapeDtypeStruct((nb * npair * 2, s, LANES), F32)] + _gather_out_shapes(gather),
        in_specs=[pl.BlockSpec((tq, LANES), lambda b, p, i: (b * spt + i, p)),
                  pl.BlockSpec((s, LANES), lambda b, p, i: (b, npair + p)),
                  pl.BlockSpec((s, LANES), lambda b, p, i: (b, 2 * npair + p))] + [hbm] * ng,
        out_specs=[pl.BlockSpec((tq, LANES), lambda b, p, i: (b * spt + i, p)),
                   pl.BlockSpec((2, tq, LANES), lambda b, p, i: (b * npair + p, i, 0))] + [hbm] * ng,
        scratch_shapes=[pltpu.VMEM((4, HALF_Q, LANES), F32), pltpu.VMEM((4, HALF_Q, LANES), F32)]
        + (_copy_sems(ng) if ng else []),
        compiler_params=pltpu.CompilerParams(dimension_semantics=("arbitrary", "arbitrary", "arbitrary")),
    )(qkv, qkv, qkv, *gather)
    return outs[0], outs[1], list(outs[2:])


def _sb4_bwd(qkv, do, car, nb, s, name, exchange=()):
    t, three_aw = qkv.shape
    aw = three_aw // 3
    npair = aw // LANES
    tq = HALF_Q
    scale = 1.0 / math.sqrt(HEAD_DIM)
    nkp, nqb = s // (2 * KEY_BLK), s // tq
    nx = len(exchange)

    def body(*refs):
        q_ref, k_ref, v_ref, do_ref, car_ref = refs[:5]
        dq_ref, dk_ref, dv_ref = refs[5 + nx:8 + nx]
        dq_acc, g_acc, dk_acc, dv_acc = refs[8 + 2 * nx:12 + 2 * nx]
        if nx:
            copies = _exchange_copies(refs[5:5 + nx], refs[8 + nx:8 + 2 * nx], *refs[12 + 2 * nx:])
            step_id = pl.program_id(0) * npair + pl.program_id(1)
            pl.when(step_id == 0)(lambda: _start_copies(copies))
        lane = lax.broadcasted_iota(jnp.int32, (tq, LANES), 1)
        col = lax.broadcasted_iota(jnp.int32, (tq, KEY_BLK), 1)
        rowi = lax.broadcasted_iota(jnp.int32, (tq, KEY_BLK), 0)
        first_k = lax.broadcasted_iota(jnp.int32, (KEY_BLK, LANES), 1) < HEAD_DIM
        m_suf, m_pre = _cumsum_mat(True), _tri(False)
        dq_acc[...] = jnp.zeros_like(dq_acc)
        g_acc[...] = jnp.zeros_like(g_acc)

        def kloop(kp, _):
            offs = [pl.multiple_of((2 * kp + blk) * KEY_BLK, KEY_BLK) for blk in range(2)]
            kblk = [k_ref[pl.ds(off, KEY_BLK), :] for off in offs]
            vblk = [v_ref[pl.ds(off, KEY_BLK), :] for off in offs]
            zero = jnp.zeros_like(kblk[0])
            kh = [(jnp.where(first_k, kb_, zero), jnp.where(first_k, zero, kb_)) for kb_ in kblk]
            vh = [(jnp.where(first_k, vb_, zero), jnp.where(first_k, zero, vb_)) for vb_ in vblk]
            k_scaled = jnp.concatenate([kblk[0] * scale, kblk[1] * scale], axis=0)
            dk_acc[...] = jnp.zeros_like(dk_acc)
            dv_acc[...] = jnp.zeros_like(dv_acc)

            def tile(qb, blocks):
                r0 = pl.multiple_of(qb * tq, tq)
                rows = pl.ds(r0, tq)
                q2 = q_ref[rows, :] * scale
                do2 = do_ref[rows, :]
                chains = [(blk, masked, h) for blk, masked in blocks for h in range(2)]
                causal = {blk: ((offs[blk] + col) < (r0 + rowi)) if masked else None for blk, masked in blocks}
                zs = [lax.dot_general(q2, kh[blk][h], NT, preferred_element_type=F32) for blk, _, h in chains]
                das = [lax.dot_general(do2, vh[blk][h], NT, preferred_element_type=F32) for blk, _, h in chains]
                lks = [_log_keep(z, causal[c[0]]) for z, c in zip(zs, chains)]
                css = [_cum(lk, m_suf) for lk in lks]
                cars = [car_ref[h, rows, :] for h in range(2)]
                cpast = [jnp.sum(jnp.where(lane == 2 * kp + blk, cars[h], 0.0), axis=1, keepdims=True)
                         for blk, _, h in chains]
                lsig = [z + lk for z, lk in zip(zs, lks)]
                aa = [jnp.exp(ls + cs + cp) for ls, cs, cp in zip(lsig, css, cpast)]
                aa = [jnp.where(causal[c[0]], a, 0.0) if c[1] else a for a, c in zip(aa, chains)]
                gs = [a * da for a, da in zip(aa, das)]
                gsum = [jnp.sum(g, axis=1, keepdims=True) for g in gs]
                gin = [jnp.dot(g.astype(BF16), m_pre, preferred_element_type=F32) for g in gs]
                gold = [_wide(g_acc[h, rows, :]) for h in range(2)]
                gpre = []
                for i, (blk, _, h) in enumerate(chains):
                    left = gold[h] + gsum[h] if blk == 1 and len(blocks) == 2 else gold[h]
                    gpre.append(gin[i] + left)
                dzs = [g - jnp.exp(ls) * (g + gp) for g, ls, gp in zip(gs, lsig, gpre)]
                dzs = [jnp.where(causal[c[0]], dz, 0.0) if c[1] else dz for dz, c in zip(dzs, chains)]
                dss = [dz.astype(BF16) for dz in dzs]
                for a, (blk, _, h) in zip(aa, chains):
                    dv_acc[2 * blk + h] += lax.dot_general(a.astype(BF16), do2, TN, preferred_element_type=F32)
                for ds_, (blk, _, h) in zip(dss, chains):
                    dk_acc[2 * blk + h] += lax.dot_general(ds_, q2, TN, preferred_element_type=F32)
                for h in range(2):
                    if len(blocks) == 2:
                        both = jnp.concatenate([dss[h], dss[2 + h]], axis=1)
                        dq_acc[h, rows, :] += jnp.dot(both, k_scaled, preferred_element_type=F32)
                        tot = gsum[h] + gsum[2 + h]
                    else:
                        only = blocks[0][0]
                        dq_acc[h, rows, :] += jnp.dot(dss[h], k_scaled[only * KEY_BLK:(only + 1) * KEY_BLK],
                                                      preferred_element_type=F32)
                        tot = gsum[h]
                    g_acc[h, rows, :] += jnp.broadcast_to(tot, (tq, LANES))

            tile(2 * kp, ((0, True),))
            tile(2 * kp + 1, ((0, False), (1, True)))

            def rest(qb, _):
                rows = pl.ds(pl.multiple_of(qb * tq, tq), tq)
                sums = jnp.maximum(car_ref[0, rows, :], car_ref[1, rows, :])
                near_live = _any_live(jnp.where(lane == 2 * kp + 1, sums, SKIPPED)) > 0
                far_live = _any_live(jnp.where(lane == 2 * kp, sums, SKIPPED)) > 0
                pl.when(far_live)(lambda: tile(qb, ((0, False), (1, False))))
                pl.when(near_live & jnp.logical_not(far_live))(lambda: tile(qb, ((1, False),)))
                return 0

            lax.fori_loop(2 * kp + 2, nqb, rest, 0)
            for blk in range(2):
                dk_ref[pl.ds(offs[blk], KEY_BLK), :] = jnp.where(
                    first_k, dk_acc[2 * blk], dk_acc[2 * blk + 1]).astype(BF16)
                dv_ref[pl.ds(offs[blk], KEY_BLK), :] = jnp.where(
                    first_k, dv_acc[2 * blk], dv_acc[2 * blk + 1]).astype(BF16)
            return 0

        lax.fori_loop(0, nkp, kloop, 0)
        first_q = lax.broadcasted_iota(jnp.int32, (s, LANES), 1) < HEAD_DIM
        dq_ref[...] = jnp.where(first_q, dq_acc[0], dq_acc[1]).astype(BF16)
        if nx:
            pl.when(step_id == nb * npair - 1)(lambda: _finish_copies(copies))

    colspec = lambda j: pl.BlockSpec((s, LANES), lambda b, p: (b, j * npair + p))
    hbm = pl.BlockSpec(memory_space=pl.ANY)
    outs = pl.pallas_call(
        body, name=name, grid=(nb, npair),
        out_shape=[jax.ShapeDtypeStruct((t, aw), BF16)] * 3 + _exchange_out_shapes(exchange),
        in_specs=[colspec(0), colspec(1), colspec(2), colspec(0),
                  pl.BlockSpec((2, s, LANES), lambda b, p: (b * npair + p, 0, 0))] + [hbm] * nx,
        out_specs=[colspec(0), colspec(0), colspec(0)] + [hbm] * nx,
        scratch_shapes=[pltpu.VMEM((2, s, LANES), F32), pltpu.VMEM((2, s, LANES), F32),
                        pltpu.VMEM((4, KEY_BLK, LANES), F32), pltpu.VMEM((4, KEY_BLK, LANES), F32)]
        + (_copy_sems(nx) if nx else []),
        compiler_params=pltpu.CompilerParams(dimension_semantics=("arbitrary", "arbitrary")),
    )(qkv, qkv, qkv, do, car, *exchange)
    return outs[0], outs[1], outs[2], list(outs[3:])


def _ada_fwd(c_all, ada_w, ada_b_loc, name):
    nl, d, n6 = ada_w.shape
    nb = c_all.shape[0]

    def body(c_ref, w_ref, b_ref, o_ref):
        cv = c_ref[...]
        o_ref[0] = jnp.dot(cv * _sig(cv), w_ref[0], preferred_element_type=F32) + b_ref[0]

    return pl.pallas_call(
        body, name=name, grid=(nl,),
        out_shape=jax.ShapeDtypeStruct((nl, nb, n6), F32),
        in_specs=[pl.BlockSpec((nb, d), lambda l: (0, 0)), pl.BlockSpec((1, d, n6), lambda l: (l, 0, 0)),
                  pl.BlockSpec((1, 1, n6), lambda l: (l, 0, 0))],
        out_specs=pl.BlockSpec((1, nb, n6), lambda l: (l, 0, 0)),
    )(c_all, ada_w, ada_b_loc)


def _ada_bwd(c_all_t, dmod_loc, dmod_all, name):
    nl, nb, n6 = dmod_loc.shape
    d = c_all_t.shape[0]
    n_all = dmod_all.shape[2]

    def body(c_ref, dl_ref, da_ref, gw_ref, gb_ref):
        cv = c_ref[...]
        gw_ref[0] = jnp.dot(cv * _sig(cv), dl_ref[0], preferred_element_type=F32)
        gb_ref[0] = jnp.sum(da_ref[0], axis=0, keepdims=True)

    return pl.pallas_call(
        body, name=name, grid=(nl,),
        out_shape=[jax.ShapeDtypeStruct((nl, d, n6), F32), jax.ShapeDtypeStruct((nl, 1, n_all), F32)],
        in_specs=[pl.BlockSpec((d, nb), lambda l: (0, 0)), pl.BlockSpec((1, nb, n6), lambda l: (l, 0, 0)),
                  pl.BlockSpec((1, nb, n_all), lambda l: (l, 0, 0))],
        out_specs=[pl.BlockSpec((1, d, n6), lambda l: (l, 0, 0)), pl.BlockSpec((1, 1, n_all), lambda l: (l, 0, 0))],
    )(c_all_t, dmod_loc, dmod_all)


def _adamw(w, g, m, v, name):
    shape = w.shape
    cols = shape[-1]
    rows = w.size // cols
    tr = _tile(rows, (512, 256, 128, 64, 32, 16, 8))
    flat = lambda a: a.reshape(rows, cols)

    def body(w_ref, g_ref, m_ref, v_ref, d_ref, mo_ref, vo_ref):
        gv = g_ref[...]
        mn = ADAM_B1 * m_ref[...] + (1.0 - ADAM_B1) * gv
        vn = ADAM_B2 * v_ref[...] + (1.0 - ADAM_B2) * (gv * gv)
        m_hat = mn / (1.0 - ADAM_B1 ** ADAM_STEP)
        v_hat = vn / (1.0 - ADAM_B2 ** ADAM_STEP)
        d_ref[...] = -ADAM_LR * (m_hat / (jnp.sqrt(v_hat) + ADAM_EPS) + ADAM_WD * w_ref[...])
        mo_ref[...] = mn
        vo_ref[...] = vn

    spec = pl.BlockSpec((tr, cols), lambda i: (i, 0))
    outs = pl.pallas_call(
        body, name=name, grid=(rows // tr,),
        out_shape=[jax.ShapeDtypeStruct((rows, cols), F32)] * 3,
        in_specs=[spec] * 4, out_specs=[spec] * 3,
    )(flat(w), flat(g), flat(m), flat(v))
    return tuple(o.reshape(shape) for o in outs)


def _pack_rows(parts, d):
    flat = jnp.concatenate([p.reshape(-1) for p in parts])
    rows = -(-flat.size // d)
    rows = -(-rows // 8) * 8
    return jnp.pad(flat, (0, rows * d - flat.size)).reshape(rows, d)


def kernel(x, c, ada_w, ada_b, pre_mix_g, post_mix_g, pre_ffn_g, post_ffn_g, w_in, conv_w, conv_b, conv_ln_g, conv_ln_b, w_conv_out, w_att_out, w_o, w_ffn_in, w_ffn_out, loss_target, m_ada_w, m_ada_b, m_pre_mix_g, m_post_mix_g, m_pre_ffn_g, m_post_ffn_g, m_w_in, m_conv_w, m_conv_b, m_conv_ln_g, m_conv_ln_b, m_w_conv_out, m_w_att_out, m_w_o, m_w_ffn_in, m_w_ffn_out, v_ada_w, v_ada_b, v_pre_mix_g, v_post_mix_g, v_pre_ffn_g, v_post_ffn_g, v_w_in, v_conv_w, v_conv_b, v_conv_ln_g, v_conv_ln_b, v_w_conv_out, v_w_att_out, v_w_o, v_w_ffn_in, v_w_ffn_out):
    nb, s, d = x.shape
    nl = ada_w.shape[0]
    cw = conv_b.shape[1]
    aw = w_att_out.shape[1]
    cl = conv_w.shape[2]
    n6 = ada_w.shape[2]
    t = nb * s
    me = _my_index()

    conv_flat = conv_w.reshape(-1)
    p1 = _pack_rows([c, conv_flat], d)
    r1 = p1.shape[0]
    (p1_all,) = _all_gather([p1], "gather_c_convw", True)
    p1_all = p1_all.reshape(N_DEV, r1 * d)
    c_all = p1_all[:, :nb * d].reshape(N_DEV * nb, d)
    conv_w_all = p1_all[:, nb * d:nb * d + conv_flat.size].reshape(N_DEV, nl, CONV_K, cl)
    conv_w_all = conv_w_all.transpose(1, 2, 0, 3).reshape(nl, CONV_K, cw)

    ada_b_loc = lax.dynamic_slice_in_dim(ada_b, me * n6, n6, axis=1).reshape(nl, 1, n6)
    mod_cols = _ada_fwd(c_all, ada_w, ada_b_loc, "ada_fwd")
    (mod_g,) = _all_gather([mod_cols.reshape(-1, d)], "gather_mod", True)
    mod_g = mod_g.reshape(N_DEV, nl, N_DEV * nb, n6)
    mod_mine = lax.dynamic_slice_in_dim(mod_g, me * nb, nb, axis=2)
    mod = mod_mine.transpose(1, 2, 0, 3).reshape(nl, nb, 6 * d)
    mods = [[mod[l, :, i * d:(i + 1) * d].reshape(nb, 1, d) for i in range(6)] for l in range(nl)]

    tr_ = lambda w: jnp.swapaxes(w, 0, 1).astype(BF16)
    shards = [[tr_(w_in[l]), tr_(w_conv_out[l]), tr_(w_att_out[l]), w_o[l].astype(BF16),
               tr_(w_ffn_in[l]), w_ffn_out[l].astype(BF16)] for l in range(nl)]
    weights = [list(_all_gather(shards[0][:3], "gather_weights", False))]

    vec = lambda a, l: a[l].reshape(1, -1)
    x2 = x.reshape(t, d)
    tgt = loss_target.reshape(t, d)

    saved = []
    xin = x2
    h = _norm_mod(xin, vec(pre_mix_g, 0), mods[0][1], mods[0][0], s, "norm_mod_0")
    for l in range(nl):
        win_t, wconv_t, watt_t = weights[l][:3]
        sh1, sc1, ga1, sh2, sc2, ga2 = mods[l]
        glu_in = _mm(h, win_t, NT, F32, f"proj_glu_{l}", 0, 2 * cw)
        qkv = _mm(h, win_t, NT, BF16, f"proj_qkv_{l}", 2 * cw, 3 * aw)
        gates = _mm(h, win_t, NT, BF16, f"proj_gates_{l}", 2 * cw + 3 * aw, 2 * d)
        u, cv, ua = _conv_fwd(glu_in, conv_w_all[l], vec(conv_b, l), vec(conv_ln_g, l), vec(conv_ln_b, l),
                              nb, s, f"conv_fwd_{l}")
        o, car, late = _sb4_fwd(qkv, nb, s, f"attn_fwd_{l}", shards[l][3:])
        weights[l] = weights[l] + late
        wo, wffn_in_t, wffn_out = late
        y_conv = _mm(ua, wconv_t, NT, BF16, f"conv_out_{l}")
        y_att = _mm(o, watt_t, NT, BF16, f"att_out_{l}")
        merged = _merge(y_conv, y_att, gates, s, f"merge_{l}")
        y = _mm(merged, wo, NN, F32, f"w_o_{l}")
        x1, h2 = _post_res_norm(xin, y, vec(post_mix_g, l), ga1, vec(pre_ffn_g, l), sc2, sh2, s, f"post_mix_{l}")
        if l + 1 < nl:
            f, early = _mm(h2, wffn_in_t, NT, BF16, f"ffn_in_{l}", gather=shards[l + 1][:3])
            weights.append(early)
        else:
            f = _mm(h2, wffn_in_t, NT, BF16, f"ffn_in_{l}")
        a = _swiglu(f, s, f"swiglu_{l}")
        y2 = _mm(a, wffn_out, NN, F32, f"ffn_out_{l}")
        saved.append(dict(xin=xin, h=h, glu_in=glu_in, gates=gates, u=u, cv=cv, ua=ua, qkv=qkv,
                          car=car, o=o, y_conv=y_conv, y_att=y_att, merged=merged, y=y, x1=x1, h2=h2, f=f,
                          a=a, y2=y2))
        if l + 1 < nl:
            nsh1, nsc1 = mods[l + 1][0], mods[l + 1][1]
            xin, h = _post_res_norm(x1, y2, vec(post_ffn_g, l), ga2, vec(pre_mix_g, l + 1), nsc1, nsh1, s,
                                    f"post_ffn_{l}")
        else:
            dx, loss_acc = _post_res_loss(x1, y2, vec(post_ffn_g, l), ga2, tgt, s, "loss")

    slots = [None] * (6 * nl)
    small = [None] * nl
    for l in reversed(range(nl)):
        win_t, wconv_t, watt_t, wo, wffn_in_t, wffn_out = weights[l]
        sh1, sc1, ga1, sh2, sc2, ga2 = mods[l]
        sv = saved[l]
        dy2, dg_post_ffn, dga2 = _post_bwd(dx, sv["y2"], vec(post_ffn_g, l), ga2, s, f"post_ffn_bwd_{l}")
        d_wffn_out = _mm(sv["a"], dy2, TN, BF16, f"d_w_ffn_out_{l}")
        da = _mm(dy2, wffn_out, NT, BF16, f"d_a_{l}")
        df = _swiglu_bwd(da, sv["f"], s, f"swiglu_bwd_{l}")
        d_wffn_in_t = _mm(df, sv["h2"], TN, BF16, f"d_w_ffn_in_{l}")
        dh2 = _mm(df, wffn_in_t, NN, F32, f"d_h2_{l}")
        dx1, dsh2, dsc2, dg_pre_ffn = _pre_bwd(dh2, sv["x1"], vec(pre_ffn_g, l), sc2, dx, s, f"pre_ffn_bwd_{l}")
        dy, dg_post_mix, dga1 = _post_bwd(dx1, sv["y"], vec(post_mix_g, l), ga1, s, f"post_mix_bwd_{l}")
        d_wo = _mm(sv["merged"], dy, TN, BF16, f"d_w_o_{l}")
        dmerged = _mm(dy, wo, NT, BF16, f"d_merged_{l}")
        dyc, dya, dgates = _merge_bwd(dmerged, sv["y_conv"], sv["y_att"], sv["gates"], s, f"merge_bwd_{l}")
        d_wconv_t = _mm(dyc, sv["ua"], TN, BF16, f"d_w_conv_out_{l}")
        dua = _mm(dyc, wconv_t, NN, F32, f"d_ua_{l}")
        d_watt_t = _mm(dya, sv["o"], TN, BF16, f"d_w_att_out_{l}")
        do = _mm(dya, watt_t, NN, BF16, f"d_o_{l}")
        ready = [(5, d_wffn_out), (4, d_wffn_in_t), (3, d_wo), (1, d_wconv_t), (2, d_watt_t)]
        dq, dk, dv, got = _sb4_bwd(sv["qkv"], do, sv["car"], nb, s, f"attn_bwd_{l}", [g for _, g in ready])
        for (gi, _), r in zip(ready, got):
            slots[6 * l + gi] = r
        dcv, dln_g, dln_b, dconv_b = _convln_bwd(dua, sv["cv"], vec(conv_ln_g, l), vec(conv_ln_b, l), s,
                                                 f"convln_bwd_{l}")
        dglu, dconv_w = _conv_bwd(dcv, sv["u"], sv["glu_in"], conv_w_all[l], nb, s, f"conv_bwd_{l}")
        dproj = jnp.concatenate([dglu, dq, dk, dv, dgates], axis=1)
        d_win_t = _mm(dproj, sv["h"], TN, BF16, f"d_w_in_{l}")
        dh, (slots[6 * l],) = _mm(dproj, win_t, NN, F32, f"d_h_{l}", exchange=[d_win_t])
        dx, dsh1, dsc1, dg_pre_mix = _pre_bwd(dh, sv["xin"], vec(pre_mix_g, l), sc1, dx1, s, f"pre_mix_bwd_{l}")
        dmod = jnp.concatenate([dsh1, dsc1, dga1, dsh2, dsc2, dga2], axis=2)
        small[l] = [dg_pre_mix, dg_post_mix, dg_pre_ffn, dg_post_ffn, dconv_b, dln_g, dln_b, dconv_w, dmod]
    grad_x = dx.reshape(nb, s, d)

    sizes = [a.size for a in small[0]]
    p3 = _pack_rows([a for l in range(nl) for a in small[l]] + [loss_acc[0, :1]], d)
    r3 = p3.shape[0]
    (p3_all,) = _all_gather([p3], "gather_small_grads", True)
    p3_all = p3_all.reshape(N_DEV, r3, d)
    p3_sum = _sum_slots(p3_all, "sum_small_grads").reshape(-1)
    per_layer = sum(sizes)
    loss = p3_sum[nl * per_layer]
    small_sum, dmod_all = [], []
    for l in range(nl):
        off, parts = l * per_layer, []
        for sz in sizes[:-1]:
            parts.append(p3_sum[off:off + sz])
            off += sz
        small_sum.append(parts)
        dm = p3_all.reshape(N_DEV, r3 * d)[:, off:off + sizes[-1]]
        dmod_all.append(dm.reshape(N_DEV * nb, 6 * d))
    dmod_all = jnp.stack(dmod_all)
    dmod_loc = lax.dynamic_slice_in_dim(dmod_all, me * n6, n6, axis=2)
    g_ada_w, g_ada_b = _ada_bwd(c_all.T, dmod_loc, dmod_all, "ada_bwd")
    g_ada_b = g_ada_b.reshape(nl, 6 * d)

    stack_small = lambda i, shape: jnp.stack([small_sum[l][i].reshape(shape) for l in range(nl)])
    g_pre_mix, g_post_mix, g_pre_ffn, g_post_ffn = (stack_small(i, (d,)) for i in range(4))
    g_conv_b, g_ln_g, g_ln_b = (stack_small(i, (cw,)) for i in range(4, 7))
    g_conv_w_all = stack_small(7, (HALO, cw))[:, :CONV_K]
    g_conv_w = lax.dynamic_slice_in_dim(g_conv_w_all, me * cl, cl, axis=2)

    sums = [_sum_slots(r, f"sum_grads_{i}") for i, r in enumerate(slots)]
    un_t = lambda i: jnp.stack([jnp.swapaxes(sums[6 * l + i], 0, 1) for l in range(nl)])
    keep = lambda i: jnp.stack([sums[6 * l + i] for l in range(nl)])
    g_w_in, g_w_conv_out, g_w_att_out, g_w_o, g_w_ffn_in, g_w_ffn_out = (
        un_t(0), un_t(1), un_t(2), keep(3), un_t(4), keep(5))

    grads = [g_ada_w, g_ada_b, g_pre_mix, g_post_mix, g_pre_ffn, g_post_ffn, g_w_in, g_conv_w, g_conv_b,
             g_ln_g, g_ln_b, g_w_conv_out, g_w_att_out, g_w_o, g_w_ffn_in, g_w_ffn_out]
    ws = [ada_w, ada_b, pre_mix_g, post_mix_g, pre_ffn_g, post_ffn_g, w_in, conv_w, conv_b, conv_ln_g,
          conv_ln_b, w_conv_out, w_att_out, w_o, w_ffn_in, w_ffn_out]
    ms = [m_ada_w, m_ada_b, m_pre_mix_g, m_post_mix_g, m_pre_ffn_g, m_post_ffn_g, m_w_in, m_conv_w, m_conv_b,
          m_conv_ln_g, m_conv_ln_b, m_w_conv_out, m_w_att_out, m_w_o, m_w_ffn_in, m_w_ffn_out]
    vs = [v_ada_w, v_ada_b, v_pre_mix_g, v_post_mix_g, v_pre_ffn_g, v_post_ffn_g, v_w_in, v_conv_w, v_conv_b,
          v_conv_ln_g, v_conv_ln_b, v_w_conv_out, v_w_att_out, v_w_o, v_w_ffn_in, v_w_ffn_out]
    deltas, new_m, new_v = [], [], []
    for i, (w, g, m, v) in enumerate(zip(ws, grads, ms, vs)):
        dlt, mn, vn = _adamw(w, g, m, v, f"adamw_{i}")
        deltas.append(dlt)
        new_m.append(mn)
        new_v.append(vn)
    return (loss, grad_x, *grads, *deltas, *new_m, *new_v)
```

```python
import math

import jax
import jax.numpy as jnp
from jax import lax
from jax.experimental import pallas as pl
from jax.experimental.pallas import tpu as pltpu

F32 = jnp.float32
BF16 = jnp.bfloat16
MESH = pl.DeviceIdType.MESH
N_DEV = 8
EPS = 1e-6
HEAD_DIM = 64
CONV_K = 31
HALO = 32
KEY_BLK = 256
ADAM_LR, ADAM_B1, ADAM_B2, ADAM_EPS, ADAM_WD, ADAM_STEP = 0.001, 0.9, 0.999, 1e-08, 0.01, 10

NT = (((1,), (1,)), ((), ()))
NN = (((1,), (0,)), ((), ()))
TN = (((0,), (0,)), ((), ()))


def _tile(n, cands):
    for t in cands:
        if n % t == 0:
            return t
    return n


def _lane_tile(n, cap):
    best = n
    for t in range(128, min(n, cap) + 1, 128):
        if n % t == 0:
            best = t
    return best


def _sig(x):
    return 1.0 / (1.0 + jnp.exp(-x))


def _pos():
    return lax.axis_index("x"), lax.axis_index("y"), lax.axis_index("c")


def _my_index():
    x, y, c = _pos()
    return 4 * x + 2 * y + c


def _all_gather(arrs, name, in_vmem):
    n = len(arrs)

    def body(*refs):
        ins, outs = refs[:n], refs[n:2 * n]
        send_sems, recv_sems, local_sems = refs[2 * n:]
        x, y, c = _pos()
        me, sib = (x, y, c), (x, y, 1 - c)
        chips = [(1 - x, y), (x, 1 - y), (1 - x, 1 - y)]

        def rows(i, p):
            r = ins[i].shape[0]
            return outs[i].at[pl.ds((4 * p[0] + 2 * p[1] + p[2]) * r, r), :]

        def copy(i, k, block, to, src=None):
            return pltpu.make_async_remote_copy(
                src_ref=rows(i, block) if src is None else src, dst_ref=rows(i, block),
                send_sem=send_sems.at[i, k], recv_sem=recv_sems.at[i, k],
                device_id=to, device_id_type=MESH)

        mine = [pltpu.make_async_copy(ins[i], rows(i, me), local_sems.at[i]) for i in range(n)]
        for cp in mine:
            cp.start()
        first = []
        for j, chip in enumerate(chips):
            for i in range(n):
                first.append(copy(i, 1 + j, me, (*chip, c), src=ins[i]))
        for i in range(n):
            first.append(copy(i, 0, me, sib, src=ins[i]))
        for cp in first:
            cp.start()
        passed = []
        for j, chip in enumerate(chips):
            for i in range(n):
                copy(i, 1 + j, (*chip, c), me).wait_recv()
                p = copy(i, 4 + j, (*chip, c), sib)
                p.start()
                passed.append(p)
        for i in range(n):
            copy(i, 0, sib, me).wait_recv()
            for j, chip in enumerate(chips):
                copy(i, 4 + j, (*chip, 1 - c), me).wait_recv()
        for cp in first + passed:
            cp.wait_send()
        for cp in mine:
            cp.wait()

    space = pltpu.VMEM if in_vmem else pl.ANY
    spec = pl.BlockSpec(memory_space=space)
    return pl.pallas_call(
        body, name=name,
        out_shape=[jax.ShapeDtypeStruct((N_DEV * a.shape[0], a.shape[1]), a.dtype) for a in arrs],
        in_specs=[spec] * n, out_specs=[spec] * n,
        scratch_shapes=[pltpu.SemaphoreType.DMA((n, 7)), pltpu.SemaphoreType.DMA((n, 7)),
                        pltpu.SemaphoreType.DMA((n,))],
    )(*arrs)


PEER_ORDER = (4, 2, 6, 5, 3, 7, 1)


def _peer(k):
    x, y, c = _pos()
    return (1 - x if k & 4 else x, 1 - y if k & 2 else y, 1 - c if k & 1 else c)


def _mesh_index(p):
    return 4 * p[0] + 2 * p[1] + p[2]


def _copy_sems(n):
    return [pltpu.SemaphoreType.DMA((n, 7)), pltpu.SemaphoreType.DMA((n, 7)), pltpu.SemaphoreType.DMA((n,))]


def _exchange_out_shapes(arrs):
    return [jax.ShapeDtypeStruct((N_DEV, a.shape[0] // N_DEV, a.shape[1]), a.dtype) for a in arrs]


def _gather_out_shapes(arrs):
    return [jax.ShapeDtypeStruct((N_DEV * a.shape[0], a.shape[1]), a.dtype) for a in arrs]


def _exchange_copies(ins, outs, send_sems, recv_sems, local_sems):
    n = len(ins)

    def block(i, p):
        r = outs[i].shape[1]
        return ins[i].at[pl.ds(_mesh_index(p) * r, r), :]

    mine = [pltpu.make_async_copy(block(i, _peer(0)), outs[i].at[0], local_sems.at[i]) for i in range(n)]
    sends = [pltpu.make_async_remote_copy(
        src_ref=block(i, _peer(k)), dst_ref=outs[i].at[k], send_sem=send_sems.at[i, k - 1],
        recv_sem=recv_sems.at[i, k - 1], device_id=_peer(k), device_id_type=MESH)
        for k in PEER_ORDER for i in range(n)]
    return mine, sends, sends


def _gather_copies(ins, outs, send_sems, recv_sems, local_sems):
    n = len(ins)

    def rows(i, p):
        r = ins[i].shape[0]
        return outs[i].at[pl.ds(_mesh_index(p) * r, r), :]

    def copy(i, k, landing):
        return pltpu.make_async_remote_copy(
            src_ref=ins[i], dst_ref=rows(i, landing), send_sem=send_sems.at[i, k - 1],
            recv_sem=recv_sems.at[i, k - 1], device_id=_peer(k), device_id_type=MESH)

    mine = [pltpu.make_async_copy(ins[i], rows(i, _peer(0)), local_sems.at[i]) for i in range(n)]
    sends = [copy(i, k, _peer(0)) for k in PEER_ORDER for i in range(n)]
    recvs = [copy(i, k, _peer(k)) for k in PEER_ORDER for i in range(n)]
    return mine, sends, recvs


def _start_copies(copies):
    mine, sends, _ = copies
    for cp in mine + sends:
        cp.start()


def _finish_copies(copies):
    mine, sends, recvs = copies
    for cp in recvs:
        cp.wait_recv()
    for cp in sends:
        cp.wait_send()
    for cp in mine:
        cp.wait()


def _sum_slots(r3, name):
    _, r, c = r3.shape
    tr = _tile(r, (128, 96, 64, 32, 16))

    def body(r_ref, o_ref):
        acc = r_ref[0].astype(F32)
        for k in range(1, N_DEV):
            acc = acc + r_ref[k].astype(F32)
        o_ref[...] = acc

    return pl.pallas_call(
        body, name=name, grid=(r // tr,),
        out_shape=jax.ShapeDtypeStruct((r, c), F32),
        in_specs=[pl.BlockSpec((N_DEV, tr, c), lambda i: (0, i, 0))],
        out_specs=pl.BlockSpec((tr, c), lambda i: (i, 0)),
    )(r3)


def _mm(a, b, dims, out_dtype, name, n0=0, n=None, exchange=(), gather=()):
    if dims is NT:
        m, k = a.shape
        n = b.shape[0] if n is None else n
    elif dims is NN:
        m, k = a.shape
        n = b.shape[1]
    else:
        k, m = a.shape
        n = b.shape[1]
    if dims is TN:
        tm = _lane_tile(m, 1536)
        tn = _lane_tile(n, 1024)
        tk = _tile(k, (1024, 512, 256, 128))
    else:
        tm = _tile(m, (1024, 512, 256, 128))
        tn = _lane_tile(math.gcd(n, n0) if n0 else n, 1536)
        tk = _lane_tile(k, 2816)
    nk = k // tk
    noff = n0 // tn

    moved = list(exchange) or list(gather)
    make_copies = _exchange_copies if exchange else _gather_copies
    nx = len(moved)
    grid = (m // tm, n // tn, nk)

    def body(*refs):
        a_ref, b_ref = refs[:2]
        o_ref = refs[2 + nx]
        scratch = refs[3 + 2 * nx:]
        if nx:
            copies = make_copies(refs[2:2 + nx], refs[3 + nx:3 + 2 * nx], *scratch[-3:])
            step_id = (pl.program_id(0) * grid[1] + pl.program_id(1)) * grid[2] + pl.program_id(2)
            pl.when(step_id == 0)(lambda: _start_copies(copies))
        part =lax.dot_general(a_ref[...].astype(BF16), b_ref[...].astype(BF16), dims,
                               preferred_element_type=F32)
        if nk == 1:
            o_ref[...] = part.astype(o_ref.dtype)
        else:
            acc_ref = scratch[0]
            kk = pl.program_id(2)

            @pl.when(kk == 0)
            def _():
                acc_ref[...] = part

            @pl.when(kk > 0)
            def _():
                acc_ref[...] += part

            @pl.when(kk == nk - 1)
            def _():
                o_ref[...] = acc_ref[...].astype(o_ref.dtype)

        if nx:
            pl.when(step_id == grid[0] * grid[1] * grid[2] - 1)(lambda: _finish_copies(copies))

    if dims is NT:
        in_specs = [pl.BlockSpec((tm, tk), lambda i, j, kk: (i, kk)),
                    pl.BlockSpec((tn, tk), lambda i, j, kk: (j + noff, kk))]
    elif dims is NN:
        in_specs = [pl.BlockSpec((tm, tk), lambda i, j, kk: (i, kk)),
                    pl.BlockSpec((tk, tn), lambda i, j, kk: (kk, j))]
    else:
        in_specs = [pl.BlockSpec((tk, tm), lambda i, j, kk: (kk, i)),
                    pl.BlockSpec((tk, tn), lambda i, j, kk: (kk, j))]
    hbm = pl.BlockSpec(memory_space=pl.ANY)
    outs = pl.pallas_call(
        body, name=name, grid=grid,
        out_shape=[jax.ShapeDtypeStruct((m, n), out_dtype)]
        + (_exchange_out_shapes(moved) if exchange else _gather_out_shapes(moved)),
        in_specs=in_specs + [hbm] * nx,
        out_specs=[pl.BlockSpec((tm, tn), lambda i, j, kk: (i, j))] + [hbm] * nx,
        scratch_shapes=([] if nk == 1 else [pltpu.VMEM((tm, tn), F32)]) + (_copy_sems(nx) if nx else []),
        compiler_params=pltpu.CompilerParams(
            dimension_semantics=("arbitrary",) * 3 if nx else ("parallel", "parallel", "arbitrary")),
    )(a, b, *moved)
    return (outs[0], list(outs[1:])) if nx else outs[0]


def _tok_tile(s):
    return _tile(s, (512, 256, 128))


def _row_spec(tm, d):
    return pl.BlockSpec((tm, d), lambda i: (i, 0))


def _vec_spec(d):
    return pl.BlockSpec((1, d), lambda i: (0, 0))


def _seq_spec(d, tiles_per_seq):
    return pl.BlockSpec((1, 1, d), lambda i: (i // tiles_per_seq, 0, 0))


def _rms(x):
    return lax.rsqrt(jnp.mean(x * x, axis=-1, keepdims=True) + EPS)


def _norm_mod(x, g, sc, sh, s, name):
    t, d = x.shape
    tm = _tok_tile(s)

    def body(x_ref, g_ref, sc_ref, sh_ref, h_ref):
        xv = x_ref[...]
        h = (xv * _rms(xv)) * g_ref[...]
        h_ref[...] = (h * (1.0 + sc_ref[0]) + sh_ref[0]).astype(BF16)

    return pl.pallas_call(
        body, name=name, grid=(t // tm,),
        out_shape=jax.ShapeDtypeStruct((t, d), BF16),
        in_specs=[_row_spec(tm, d), _vec_spec(d), _seq_spec(d, s // tm), _seq_spec(d, s // tm)],
        out_specs=_row_spec(tm, d),
    )(x, g, sc, sh)


def _post_res_norm(x, y, g_post, ga, g_pre, sc, sh, s, name):
    t, d = x.shape
    tm = _tok_tile(s)

    def body(x_ref, y_ref, gp_ref, ga_ref, g_ref, sc_ref, sh_ref, xn_ref, h_ref):
        yv = y_ref[...]
        xn = x_ref[...] + ga_ref[0] * ((yv * _rms(yv)) * gp_ref[...])
        xn_ref[...] = xn
        h = (xn * _rms(xn)) * g_ref[...]
        h_ref[...] = (h * (1.0 + sc_ref[0]) + sh_ref[0]).astype(BF16)

    tps = s // tm
    return pl.pallas_call(
        body, name=name, grid=(t // tm,),
        out_shape=[jax.ShapeDtypeStruct((t, d), F32), jax.ShapeDtypeStruct((t, d), BF16)],
        in_specs=[_row_spec(tm, d), _row_spec(tm, d), _vec_spec(d), _seq_spec(d, tps),
                  _vec_spec(d), _seq_spec(d, tps), _seq_spec(d, tps)],
        out_specs=[_row_spec(tm, d), _row_spec(tm, d)],
    )(x, y, g_post, ga, g_pre, sc, sh)


def _post_res_loss(x, y, g_post, ga, target, s, name):
    t, d = x.shape
    tm = _tok_tile(s)

    def body(x_ref, y_ref, gp_ref, ga_ref, t_ref, dx_ref, loss_ref):
        yv = y_ref[...]
        err = x_ref[...] + ga_ref[0] * ((yv * _rms(yv)) * gp_ref[...]) - t_ref[...]
        dx_ref[...] = err * (1.0 / d)
        part = 0.5 * jnp.sum(jnp.mean(err * err, axis=-1, keepdims=True), axis=0, keepdims=True)

        @pl.when(pl.program_id(0) == 0)
        def _():
            loss_ref[...] = jnp.zeros_like(loss_ref)

        loss_ref[...] += part

    return pl.pallas_call(
        body, name=name, grid=(t // tm,),
        out_shape=[jax.ShapeDtypeStruct((t, d), F32), jax.ShapeDtypeStruct((8, 128), F32)],
        in_specs=[_row_spec(tm, d), _row_spec(tm, d), _vec_spec(d), _seq_spec(d, s // tm), _row_spec(tm, d)],
        out_specs=[_row_spec(tm, d), pl.BlockSpec((8, 128), lambda i: (0, 0))],
    )(x, y, g_post, ga, target)


def _post_bwd(dxn, y, g, ga, s, name):
    t, d = y.shape
    tm = _tok_tile(s)
    tps = s // tm

    def body(dx_ref, y_ref, g_ref, ga_ref, dy_ref, dg_ref, dga_ref):
        i = pl.program_id(0)
        yv = y_ref[...]
        r = _rms(yv)
        yh = yv * r
        dxv = dx_ref[...]
        dn = dxv * ga_ref[0]
        dyh = dn * g_ref[...]
        dy_ref[...] = (r * (dyh - yh * jnp.mean(dyh * yh, axis=-1, keepdims=True))).astype(BF16)

        @pl.when(i == 0)
        def _():
            dg_ref[...] = jnp.zeros_like(dg_ref)

        @pl.when(i % tps == 0)
        def _():
            dga_ref[...] = jnp.zeros_like(dga_ref)

        dg_ref[...] += jnp.sum(dn * yh, axis=0, keepdims=True)
        dga_ref[0] += jnp.sum(dxv * (yh * g_ref[...]), axis=0, keepdims=True)

    return pl.pallas_call(
        body, name=name, grid=(t // tm,),
        out_shape=[jax.ShapeDtypeStruct((t, d), BF16), jax.ShapeDtypeStruct((1, d), F32),
                   jax.ShapeDtypeStruct((t // s, 1, d), F32)],
        in_specs=[_row_spec(tm, d), _row_spec(tm, d), _vec_spec(d), _seq_spec(d, tps)],
        out_specs=[_row_spec(tm, d), _vec_spec(d), _seq_spec(d, tps)],
    )(dxn, y, g, ga)


def _pre_bwd(dh, x, g, sc, dres, s, name):
    t, d = x.shape
    tm = _tok_tile(s)
    tps = s // tm

    def body(dh_ref, x_ref, g_ref, sc_ref, dr_ref, dx_ref, dsh_ref, dsc_ref, dg_ref):
        i = pl.program_id(0)
        xv = x_ref[...]
        r = _rms(xv)
        xh = xv * r
        dhv = dh_ref[...]
        one_sc = 1.0 + sc_ref[0]
        dxh = dhv * one_sc * g_ref[...]
        dx_ref[...] = dr_ref[...] + r * (dxh - xh * jnp.mean(dxh * xh, axis=-1, keepdims=True))

        @pl.when(i == 0)
        def _():
            dg_ref[...] = jnp.zeros_like(dg_ref)

        @pl.when(i % tps == 0)
        def _():
            dsh_ref[...] = jnp.zeros_like(dsh_ref)
            dsc_ref[...] = jnp.zeros_like(dsc_ref)

        dg_ref[...] += jnp.sum(dhv * one_sc * xh, axis=0, keepdims=True)
        dsh_ref[0] += jnp.sum(dhv, axis=0, keepdims=True)
        dsc_ref[0] += jnp.sum(dhv * (xh * g_ref[...]), axis=0, keepdims=True)

    nb = t // s
    return pl.pallas_call(
        body, name=name, grid=(t // tm,),
        out_shape=[jax.ShapeDtypeStruct((t, d), F32), jax.ShapeDtypeStruct((nb, 1, d), F32),
                   jax.ShapeDtypeStruct((nb, 1, d), F32), jax.ShapeDtypeStruct((1, d), F32)],
        in_specs=[_row_spec(tm, d), _row_spec(tm, d), _vec_spec(d), _seq_spec(d, tps), _row_spec(tm, d)],
        out_specs=[_row_spec(tm, d), _seq_spec(d, tps), _seq_spec(d, tps), _vec_spec(d)],
    )(dh, x, g, sc, dres)


def _merge(y_conv, y_att, gates, s, name):
    t, d = y_conv.shape
    tm = _tok_tile(s)

    def body(yc_ref, ya_ref, gc_ref, gt_ref, o_ref):
        f32 = lambda r: r[...].astype(F32)
        o_ref[...] = (_sig(f32(gc_ref)) * f32(yc_ref) + _sig(f32(gt_ref)) * f32(ya_ref)).astype(BF16)

    return pl.pallas_call(
        body, name=name, grid=(t // tm,),
        out_shape=jax.ShapeDtypeStruct((t, d), BF16),
        in_specs=[_row_spec(tm, d), _row_spec(tm, d), pl.BlockSpec((tm, d), lambda i: (i, 0)),
                  pl.BlockSpec((tm, d), lambda i: (i, 1))],
        out_specs=_row_spec(tm, d),
    )(y_conv, y_att, gates, gates)


def _merge_bwd(dm, y_conv, y_att, gates, s, name):
    t, d = y_conv.shape
    tm = _tok_tile(s)

    def body(dm_ref, yc_ref, ya_ref, gc_ref, gt_ref, dyc_ref, dya_ref, dg_ref):
        f32 = lambda r: r[...].astype(F32)
        dmv = f32(dm_ref)
        sc_, st_ = _sig(f32(gc_ref)), _sig(f32(gt_ref))
        dyc_ref[...] = (dmv * sc_).astype(BF16)
        dya_ref[...] = (dmv * st_).astype(BF16)
        dg_ref[:, :d] = (dmv * f32(yc_ref) * (sc_ * (1.0 - sc_))).astype(BF16)
        dg_ref[:, d:] = (dmv * f32(ya_ref) * (st_ * (1.0 - st_))).astype(BF16)

    return pl.pallas_call(
        body, name=name, grid=(t // tm,),
        out_shape=[jax.ShapeDtypeStruct((t, d), BF16), jax.ShapeDtypeStruct((t, d), BF16),
                   jax.ShapeDtypeStruct((t, 2 * d), BF16)],
        in_specs=[_row_spec(tm, d), _row_spec(tm, d), _row_spec(tm, d),
                  pl.BlockSpec((tm, d), lambda i: (i, 0)), pl.BlockSpec((tm, d), lambda i: (i, 1))],
        out_specs=[_row_spec(tm, d), _row_spec(tm, d), _row_spec(tm, 2 * d)],
    )(dm, y_conv, y_att, gates, gates)


def _swiglu(f, s, name):
    t, two = f.shape
    dff = two // 2
    tm = _tok_tile(s)

    def body(g_ref, u_ref, a_ref):
        gv = g_ref[...].astype(F32)
        a_ref[...] = (gv * _sig(gv) * u_ref[...].astype(F32)).astype(BF16)

    return pl.pallas_call(
        body, name=name, grid=(t // tm,),
        out_shape=jax.ShapeDtypeStruct((t, dff), BF16),
        in_specs=[pl.BlockSpec((tm, dff), lambda i: (i, 0)), pl.BlockSpec((tm, dff), lambda i: (i, 1))],
        out_specs=_row_spec(tm, dff),
    )(f, f)


def _swiglu_bwd(da, f, s, name):
    t, two = f.shape
    dff = two // 2
    tm = _tok_tile(s)

    def body(da_ref, g_ref, u_ref, df_ref):
        gv, dav = g_ref[...].astype(F32), da_ref[...].astype(F32)
        sg = _sig(gv)
        df_ref[:, :dff] = (dav * u_ref[...].astype(F32) * (sg * (1.0 + gv * (1.0 - sg)))).astype(BF16)
        df_ref[:, dff:] = (dav * (gv * sg)).astype(BF16)

    return pl.pallas_call(
        body, name=name, grid=(t // tm,),
        out_shape=jax.ShapeDtypeStruct((t, two), BF16),
        in_specs=[_row_spec(tm, dff), pl.BlockSpec((tm, dff), lambda i: (i, 0)),
                  pl.BlockSpec((tm, dff), lambda i: (i, 1))],
        out_specs=_row_spec(tm, two),
    )(da, f, f)


CONV_CHUNK = 64
SUB = 8


def _tap_groups(offsets):
    groups = {}
    for k, o in enumerate(offsets):
        groups.setdefault(o % SUB, []).append((k, o - o % SUB))
    return groups


def _conv_fwd(glu_in, conv_w, conv_b, ln_g, ln_b, nb, s, name):
    t, two = glu_in.shape
    cw = two // 2
    tt = _tile(s, (256, 128, 64))
    spt = s // tt
    ch = min(CONV_CHUNK, tt)

    groups = _tap_groups([HALO - (CONV_K - 1) + k for k in range(CONV_K)])

    def body(cur_ref, halo_ref, w_ref, b_ref, g_ref, be_ref, u_ref, cv_ref, ua_ref, ext_ref, part_ref):
        j = pl.program_id(1)
        cur = cur_ref[...]
        u_cur = cur[:, :cw] * _sig(cur[:, cw:])
        hal = halo_ref[...]
        u_hal = hal[:, :cw] * _sig(hal[:, cw:])
        ext_ref[0:HALO, :] = jnp.where(j == 0, 0.0, u_hal)
        ext_ref[HALO:HALO + tt, :] = u_cur
        ext_ref[HALO + tt:, :] = jnp.zeros((SUB, cw), F32)
        u_ref[...] = u_cur
        for c0 in range(0, tt, ch):
            acc = jnp.zeros((ch, cw), F32) + b_ref[...]
            for r, taps in groups.items():
                part = None
                for k, base in taps:
                    term = w_ref[k:k + 1, :] * ext_ref[c0 + base:c0 + base + ch + SUB, :]
                    part = term if part is None else part + term
                if r == 0:
                    acc = acc + part[:ch]
                else:
                    part_ref[r] = part
                    acc = acc + part_ref[r, r:r + ch, :]
            cv_ref[c0:c0 + ch, :] = acc
            mu = jnp.mean(acc, axis=-1, keepdims=True)
            xc = acc - mu
            yv = xc * lax.rsqrt(jnp.mean(xc * xc, axis=-1, keepdims=True) + EPS) * g_ref[...] + be_ref[...]
            ua_ref[c0:c0 + ch, :] = (yv * _sig(yv)).astype(BF16)

    cur_map = lambda b, j: (b * spt + j, 0)
    halo_map = lambda b, j: (jnp.maximum((b * s + j * tt) // HALO - 1, 0), 0)
    vec = lambda d0: pl.BlockSpec((d0, cw), lambda b, j: (0, 0))
    return pl.pallas_call(
        body, name=name, grid=(nb, spt),
        out_shape=[jax.ShapeDtypeStruct((t, cw), F32), jax.ShapeDtypeStruct((t, cw), F32),
                   jax.ShapeDtypeStruct((t, cw), BF16)],
        in_specs=[pl.BlockSpec((tt, two), cur_map), pl.BlockSpec((HALO, two), halo_map),
                  vec(CONV_K), vec(1), vec(1), vec(1)],
        out_specs=[pl.BlockSpec((tt, cw), cur_map)] * 3,
        scratch_shapes=[pltpu.VMEM((tt + HALO + SUB, cw), F32), pltpu.VMEM((SUB, ch + SUB, cw), F32)],
    )(glu_in, glu_in, conv_w, conv_b, ln_g, ln_b)


def _convln_bwd(dua, cv, ln_g, ln_b, s, name):
    t, cw = cv.shape
    tm = _tok_tile(s)

    def body(du_ref, cv_ref, g_ref, be_ref, dcv_ref, dg_ref, dbe_ref, db_ref):
        cvv = cv_ref[...]
        mu = jnp.mean(cvv, axis=-1, keepdims=True)
        xc = cvv - mu
        rstd = lax.rsqrt(jnp.mean(xc * xc, axis=-1, keepdims=True) + EPS)
        xh = xc * rstd
        yv = xh * g_ref[...] + be_ref[...]
        sg = _sig(yv)
        dy = du_ref[...] * (sg * (1.0 + yv * (1.0 - sg)))
        dxh = dy * g_ref[...]
        dcv = rstd * (dxh - jnp.mean(dxh, axis=-1, keepdims=True)
                      - xh * jnp.mean(dxh * xh, axis=-1, keepdims=True))
        dcv_ref[...] = dcv

        @pl.when(pl.program_id(0) == 0)
        def _():
            dg_ref[...] = jnp.zeros_like(dg_ref)
            dbe_ref[...] = jnp.zeros_like(dbe_ref)
            db_ref[...] = jnp.zeros_like(db_ref)

        dg_ref[...] += jnp.sum(dy * xh, axis=0, keepdims=True)
        dbe_ref[...] += jnp.sum(dy, axis=0, keepdims=True)
        db_ref[...] += jnp.sum(dcv, axis=0, keepdims=True)

    return pl.pallas_call(
        body, name=name, grid=(t // tm,),
        out_shape=[jax.ShapeDtypeStruct((t, cw), F32)] + [jax.ShapeDtypeStruct((1, cw), F32)] * 3,
        in_specs=[_row_spec(tm, cw), _row_spec(tm, cw), _vec_spec(cw), _vec_spec(cw)],
        out_specs=[_row_spec(tm, cw), _vec_spec(cw), _vec_spec(cw), _vec_spec(cw)],
    )(dua, cv, ln_g, ln_b)


def _conv_bwd(dcv, u, glu_in, conv_w, nb, s, name):
    t, cw = dcv.shape
    tt = _tile(s, (256, 128, 64))
    spt = s // tt
    ch = min(CONV_CHUNK, tt)
    nblk = t // HALO

    d_groups = _tap_groups([(CONV_K - 1) - k for k in range(CONV_K)])
    u_groups = _tap_groups([HALO - (CONV_K - 1) + k for k in range(CONV_K)])

    def body(d_ref, dn_ref, u_ref, up_ref, glu_ref, w_ref, dglu_ref, dw_ref, dext_ref, uext_ref, part_ref, dwp_ref):
        b, j = pl.program_id(0), pl.program_id(1)
        dcur = d_ref[...]
        dext_ref[0:tt, :] = dcur
        dext_ref[tt:tt + HALO, :] = jnp.where(j == spt - 1, 0.0, dn_ref[...])
        uext_ref[0:HALO, :] = jnp.where(j == 0, 0.0, up_ref[...])
        uext_ref[HALO:HALO + tt, :] = u_ref[...]
        uext_ref[HALO + tt:, :] = jnp.zeros((SUB, cw), F32)

        @pl.when((b == 0) & (j == 0))
        def _():
            dwp_ref[...] = jnp.zeros_like(dwp_ref)

        for c0 in range(0, tt, ch):
            dc = dext_ref[c0:c0 + ch, :]
            du = jnp.zeros((ch, cw), F32)
            for r, taps in d_groups.items():
                part = None
                for k, base in taps:
                    term = w_ref[k:k + 1, :] * dext_ref[c0 + base:c0 + base + ch + SUB, :]
                    part = term if part is None else part + term
                if r == 0:
                    du = du + part[:ch]
                else:
                    part_ref[r] = part
                    du = du + part_ref[r, r:r + ch, :]
            for r, taps in u_groups.items():
                ush = uext_ref[c0 + r:c0 + r + ch + HALO, :]
                for k, base in taps:
                    prod = dc * ush[base:base + ch]
                    rows8 = prod[0:SUB]
                    for i in range(1, ch // SUB):
                        rows8 = rows8 + prod[i * SUB:(i + 1) * SUB]
                    dwp_ref[k] += rows8
            val = glu_ref[c0:c0 + ch, :cw]
            sg = _sig(glu_ref[c0:c0 + ch, cw:])
            dglu_ref[c0:c0 + ch, :cw] = (du * sg).astype(BF16)
            dglu_ref[c0:c0 + ch, cw:] = (du * val * (sg * (1.0 - sg))).astype(BF16)

        @pl.when((b == nb - 1) & (j == spt - 1))
        def _():
            dw_ref[...] = jnp.sum(dwp_ref[...], axis=1)

    cur_map = lambda b, j: (b * spt + j, 0)
    prev_map = lambda b, j: (jnp.maximum((b * s + j * tt) // HALO - 1, 0), 0)
    next_map = lambda b, j: (jnp.minimum((b * s + (j + 1) * tt) // HALO, nblk - 1), 0)
    return pl.pallas_call(
        body, name=name, grid=(nb, spt),
        out_shape=[jax.ShapeDtypeStruct((t, 2 * cw), BF16), jax.ShapeDtypeStruct((HALO, cw), F32)],
        in_specs=[pl.BlockSpec((tt, cw), cur_map), pl.BlockSpec((HALO, cw), next_map),
                  pl.BlockSpec((tt, cw), cur_map), pl.BlockSpec((HALO, cw), prev_map),
                  pl.BlockSpec((tt, 2 * cw), cur_map), pl.BlockSpec((CONV_K, cw), lambda b, j: (0, 0))],
        out_specs=[pl.BlockSpec((tt, 2 * cw), cur_map), pl.BlockSpec((HALO, cw), lambda b, j: (0, 0))],
        scratch_shapes=[pltpu.VMEM((tt + HALO, cw), F32), pltpu.VMEM((tt + HALO + SUB, cw), F32),
                        pltpu.VMEM((SUB, ch + SUB, cw), F32), pltpu.VMEM((HALO, SUB, cw), F32)],
    )(dcv, dcv, u, u, glu_in, conv_w)


def _split(x):
    hi = x.astype(BF16)
    return hi, (x - hi.astype(F32)).astype(BF16)


def _tri(upper):
    j = lax.broadcasted_iota(jnp.int32, (KEY_BLK, KEY_BLK), 0)
    s_ = lax.broadcasted_iota(jnp.int32, (KEY_BLK, KEY_BLK), 1)
    return jnp.where(j > s_ if upper else j < s_, 1.0, 0.0).astype(BF16)


def _log_keep(z, causal):
    sp = jnp.maximum(z, 0.0) + jnp.log(1.0 + jnp.exp(-jnp.abs(z)))
    return -sp if causal is None else jnp.where(causal, -sp, 0.0)


LANES = 128


HALF_Q = 256
DEAD = -105.0
SKIPPED = -1e30


def _any_live(x):
    return (jnp.max(x) > DEAD).astype(jnp.int32)


def _cumsum_mat(upper):
    m = _tri(upper)
    return jnp.concatenate([m, m], axis=0)


def _cum(x, mat):
    hi, lo = _split(x)
    return jnp.dot(jnp.concatenate([hi, lo], axis=1), mat, preferred_element_type=F32)


def _wide(x):
    return jnp.concatenate([x] * (KEY_BLK // LANES), axis=1)


def _sb4_fwd(qkv, nb, s, name, gather=()):
    t, three_aw = qkv.shape
    aw = three_aw // 3
    npair = aw // LANES
    tq = 2 * HALF_Q
    spt = s // tq
    scale = 1.0 / math.sqrt(HEAD_DIM)

    ng = len(gather)

    def body(*refs):
        q_ref, k_ref, v_ref = refs[:3]
        o_ref, car_ref = refs[3 + ng:5 + ng]
        acc_ref, run_ref = refs[5 + 2 * ng:7 + 2 * ng]
        if ng:
            copies = _gather_copies(refs[3:3 + ng], refs[5 + ng:5 + 2 * ng], *refs[7 + 2 * ng:])
            step_id = (pl.program_id(0) * npair + pl.program_id(1)) * spt + pl.program_id(2)
            pl.when(step_id == 0)(lambda: _start_copies(copies))
        qi = pl.program_id(2)
        lane = lax.broadcasted_iota(jnp.int32, (HALF_Q, LANES), 1)
        col = lax.broadcasted_iota(jnp.int32, (HALF_Q, KEY_BLK), 1)
        rowi = lax.broadcasted_iota(jnp.int32, (HALF_Q, KEY_BLK), 0)
        first = lane < HEAD_DIM
        mat = _cumsum_mat(True)
        qs = []
        for half in range(2):
            q2 = q_ref[half * HALF_Q:(half + 1) * HALF_Q, :] * scale
            zero = jnp.zeros_like(q2)
            qs.append((jnp.where(first, q2, zero), jnp.where(first, zero, q2)))
        acc_ref[...] = jnp.zeros_like(acc_ref)
        run_ref[...] = jnp.zeros_like(run_ref)
        car_ref[...] = jnp.zeros_like(car_ref)
        nk = (qi * tq + tq) // KEY_BLK

        def step(kb, halves):
            off = pl.multiple_of(kb * KEY_BLK, KEY_BLK)
            kblk = k_ref[pl.ds(off, KEY_BLK), :]
            vblk = v_ref[pl.ds(off, KEY_BLK), :]
            chains = [(half, masked, h) for half, masked in halves for h in range(2)]
            causal = {half: ((off + col) < (qi * tq + half * HALF_Q + rowi)) if masked else None
                      for half, masked in halves}
            zs = [lax.dot_general(qs[half][h], kblk, NT, preferred_element_type=F32) for half, _, h in chains]
            lks = [_log_keep(z, causal[c[0]]) for z, c in zip(zs, chains)]
            css = [_cum(lk, mat) for lk in lks]
            runs = [run_ref[2 * half + h] for half, _, h in chains]
            aa = [jnp.exp(z + lk + cs + _wide(run)) for z, lk, cs, run in zip(zs, lks, css, runs)]
            aa = [jnp.where(causal[c[0]], a, 0.0) if c[1] else a for a, c in zip(aa, chains)]
            for a, (half, _, h) in zip(aa, chains):
                acc_ref[2 * half + h] += jnp.dot(a.astype(BF16), vblk, preferred_element_type=F32)
            for lk, run, (half, _, h) in zip(lks, runs, chains):
                rows = pl.ds(half * HALF_Q, HALF_Q)
                car_ref[h, rows, :] = jnp.where(lane == kb, run, car_ref[h, rows, :])
                run_ref[2 * half + h] = run + jnp.sum(lk, axis=1, keepdims=True)

        step(nk - 1, ((1, True),))
        step(nk - 2, ((0, True), (1, False)))

        def more(c):
            return (c[0] < nk) & (c[1] > 0)

        def rest(c):
            step(nk - 1 - c[0], ((0, False), (1, False)))
            return c[0] + 1, _any_live(run_ref[...])

        visited, _ = lax.while_loop(more, rest, (jnp.int32(2), _any_live(run_ref[...])))
        unvisited = lane < (nk - visited)
        for h in range(2):
            car_ref[h] = jnp.where(jnp.concatenate([unvisited, unvisited], axis=0), SKIPPED, car_ref[h])
        for half in range(2):
            o_ref[half * HALF_Q:(half + 1) * HALF_Q, :] = jnp.where(
                first, acc_ref[2 * half], acc_ref[2 * half + 1]).astype(BF16)
        if ng:
            pl.when(step_id == nb * npair * spt - 1)(lambda: _finish_copies(copies))

    hbm = pl.BlockSpec(memory_space=pl.ANY)
    outs = pl.pallas_call(
        body, name=name, grid=(nb, npair, spt),
        out_shape=[jax.ShapeDtypeStruct((t, aw), BF16),
                   jax.ShapeDtypeStruct((nb * npair * 2, s, LANES), F32)] + _gather_out_shapes(gather),
        in_specs=[pl.BlockSpec((tq, LANES), lambda b, p, i: (b * spt + i, p)),
                  pl.BlockSpec((s, LANES), lambda b, p, i: (b, npair + p)),
                  pl.BlockSpec((s, LANES), lambda b, p, i: (b, 2 * npair + p))] + [hbm] * ng,
        out_specs=[pl.BlockSpec((tq, LANES), lambda b, p, i: (b * spt + i, p)),
                   pl.BlockSpec((2, tq, LANES), lambda b, p, i: (b * npair + p, i, 0))] + [hbm] * ng,
        scratch_shapes=[pltpu.VMEM((4, HALF_Q, LANES), F32), pltpu.VMEM((4, HALF_Q, LANES), F32)]
        + (_copy_sems(ng) if ng else []),
        compiler_params=pltpu.CompilerParams(dimension_semantics=("arbitrary", "arbitrary", "arbitrary")),
    )(qkv, qkv, qkv, *gather)
    return outs[0], outs[1], list(outs[2:])


def _sb4_bwd(qkv, do, car, nb, s, name, exchange=()):
    t, three_aw = qkv.shape
    aw = three_aw // 3
    npair = aw // LANES
    tq = HALF_Q
    scale = 1.0 / math.sqrt(HEAD_DIM)
    nkp, nqb = s // (2 * KEY_BLK), s // tq
    nx = len(exchange)

    def body(*refs):
        q_ref, k_ref, v_ref, do_ref, car_ref = refs[:5]
        dq_ref, dk_ref, dv_ref = refs[5 + nx:8 + nx]
        dq_acc, g_acc, dk_acc, dv_acc = refs[8 + 2 * nx:12 + 2 * nx]
        if nx:
            copies = _exchange_copies(refs[5:5 + nx], refs[8 + nx:8 + 2 * nx], *refs[12 + 2 * nx:])
            step_id = pl.program_id(0) * npair + pl.program_id(1)
            pl.when(step_id == 0)(lambda: _start_copies(copies))
        lane = lax.broadcasted_iota(jnp.int32, (tq, LANES), 1)
        col = lax.broadcasted_iota(jnp.int32, (tq, KEY_BLK), 1)
        rowi = lax.broadcasted_iota(jnp.int32, (tq, KEY_BLK), 0)
        first_k = lax.broadcasted_iota(jnp.int32, (KEY_BLK, LANES), 1) < HEAD_DIM
        m_suf, m_pre = _cumsum_mat(True), _tri(False)
        dq_acc[...] = jnp.zeros_like(dq_acc)
        g_acc[...] = jnp.zeros_like(g_acc)

        def kloop(kp, _):
            offs = [pl.multiple_of((2 * kp + blk) * KEY_BLK, KEY_BLK) for blk in range(2)]
            kblk = [k_ref[pl.ds(off, KEY_BLK), :] for off in offs]
            vblk = [v_ref[pl.ds(off, KEY_BLK), :] for off in offs]
            zero = jnp.zeros_like(kblk[0])
            kh = [(jnp.where(first_k, kb_, zero), jnp.where(first_k, zero, kb_)) for kb_ in kblk]
            vh = [(jnp.where(first_k, vb_, zero), jnp.where(first_k, zero, vb_)) for vb_ in vblk]
            k_scaled = jnp.concatenate([kblk[0] * scale, kblk[1] * scale], axis=0)
            dk_acc[...] = jnp.zeros_like(dk_acc)
            dv_acc[...] = jnp.zeros_like(dv_acc)

            def tile(qb, blocks):
                r0 = pl.multiple_of(qb * tq, tq)
                rows = pl.ds(r0, tq)
                q2 = q_ref[rows, :] * scale
                do2 = do_ref[rows, :]
                chains = [(blk, masked, h) for blk, masked in blocks for h in range(2)]
                causal = {blk: ((offs[blk] + col) < (r0 + rowi)) if masked else None for blk, masked in blocks}
                zs = [lax.dot_general(q2, kh[blk][h], NT, preferred_element_type=F32) for blk, _, h in chains]
                das = [lax.dot_general(do2, vh[blk][h], NT, preferred_element_type=F32) for blk, _, h in chains]
                lks = [_log_keep(z, causal[c[0]]) for z, c in zip(zs, chains)]
                css = [_cum(lk, m_suf) for lk in lks]
                cars = [car_ref[h, rows, :] for h in range(2)]
                cpast = [jnp.sum(jnp.where(lane == 2 * kp + blk, cars[h], 0.0), axis=1, keepdims=True)
                         for blk, _, h in chains]
                lsig = [z + lk for z, lk in zip(zs, lks)]
                aa = [jnp.exp(ls + cs + cp) for ls, cs, cp in zip(lsig, css, cpast)]
                aa = [jnp.where(causal[c[0]], a, 0.0) if c[1] else a for a, c in zip(aa, chains)]
                gs = [a * da for a, da in zip(aa, das)]
                gsum = [jnp.sum(g, axis=1, keepdims=True) for g in gs]
                gin = [jnp.dot(g.astype(BF16), m_pre, preferred_element_type=F32) for g in gs]
                gold = [_wide(g_acc[h, rows, :]) for h in range(2)]
                gpre = []
                for i, (blk, _, h) in enumerate(chains):
                    left = gold[h] + gsum[h] if blk == 1 and len(blocks) == 2 else gold[h]
                    gpre.append(gin[i] + left)
                dzs = [g - jnp.exp(ls) * (g + gp) for g, ls, gp in zip(gs, lsig, gpre)]
                dzs = [jnp.where(causal[c[0]], dz, 0.0) if c[1] else dz for dz, c in zip(dzs, chains)]
                dss = [dz.astype(BF16) for dz in dzs]
                for a, (blk, _, h) in zip(aa, chains):
                    dv_acc[2 * blk + h] += lax.dot_general(a.astype(BF16), do2, TN, preferred_element_type=F32)
                for ds_, (blk, _, h) in zip(dss, chains):
                    dk_acc[2 * blk + h] += lax.dot_general(ds_, q2, TN, preferred_element_type=F32)
                for h in range(2):
                    if len(blocks) == 2:
                        both = jnp.concatenate([dss[h], dss[2 + h]], axis=1)
                        dq_acc[h, rows, :] += jnp.dot(both, k_scaled, preferred_element_type=F32)
                        tot = gsum[h] + gsum[2 + h]
                    else:
                        only = blocks[0][0]
                        dq_acc[h, rows, :] += jnp.dot(dss[h], k_scaled[only * KEY_BLK:(only + 1) * KEY_BLK],
                                                      preferred_element_type=F32)
                        tot = gsum[h]
                    g_acc[h, rows, :] += jnp.broadcast_to(tot, (tq, LANES))

            tile(2 * kp, ((0, True),))
            tile(2 * kp + 1, ((0, False), (1, True)))

            def rest(qb, _):
                rows = pl.ds(pl.multiple_of(qb * tq, tq), tq)
                sums = jnp.maximum(car_ref[0, rows, :], car_ref[1, rows, :])
                near_live = _any_live(jnp.where(lane == 2 * kp + 1, sums, SKIPPED)) > 0
                far_live = _any_live(jnp.where(lane == 2 * kp, sums, SKIPPED)) > 0
                pl.when(far_live)(lambda: tile(qb, ((0, False), (1, False))))
                pl.when(near_live & jnp.logical_not(far_live))(lambda: tile(qb, ((1, False),)))
                return 0

            lax.fori_loop(2 * kp + 2, nqb, rest, 0)
            for blk in range(2):
                dk_ref[pl.ds(offs[blk], KEY_BLK), :] = jnp.where(
                    first_k, dk_acc[2 * blk], dk_acc[2 * blk + 1]).astype(BF16)
                dv_ref[pl.ds(offs[blk], KEY_BLK), :] = jnp.where(
                    first_k, dv_acc[2 * blk], dv_acc[2 * blk + 1]).astype(BF16)
            return 0

        lax.fori_loop(0, nkp, kloop, 0)
        first_q = lax.broadcasted_iota(jnp.int32, (s, LANES), 1) < HEAD_DIM
        dq_ref[...] = jnp.where(first_q, dq_acc[0], dq_acc[1]).astype(BF16)
        if nx:
            pl.when(step_id == nb * npair - 1)(lambda: _finish_copies(copies))

    colspec = lambda j: pl.BlockSpec((s, LANES), lambda b, p: (b, j * npair + p))
    hbm = pl.BlockSpec(memory_space=pl.ANY)
    outs = pl.pallas_call(
        body, name=name, grid=(nb, npair),
        out_shape=[jax.ShapeDtypeStruct((t, aw), BF16)] * 3 + _exchange_out_shapes(exchange),
        in_specs=[colspec(0), colspec(1), colspec(2), colspec(0),
                  pl.BlockSpec((2, s, LANES), lambda b, p: (b * npair + p, 0, 0))] + [hbm] * nx,
        out_specs=[colspec(0), colspec(0), colspec(0)] + [hbm] * nx,
        scratch_shapes=[pltpu.VMEM((2, s, LANES), F32), pltpu.VMEM((2, s, LANES), F32),
                        pltpu.VMEM((4, KEY_BLK, LANES), F32), pltpu.VMEM((4, KEY_BLK, LANES), F32)]
        + (_copy_sems(nx) if nx else []),
        compiler_params=pltpu.CompilerParams(dimension_semantics=("arbitrary", "arbitrary")),
    )(qkv, qkv, qkv, do, car, *exchange)
    return outs[0], outs[1], outs[2], list(outs[3:])


def _ada_fwd(c_all, ada_w, ada_b_loc, name):
    nl, d, n6 = ada_w.shape
    nb = c_all.shape[0]

    def body(c_ref, w_ref, b_ref, o_ref):
        cv = c_ref[...]
        o_ref[0] = jnp.dot(cv * _sig(cv), w_ref[0], preferred_element_type=F32) + b_ref[0]

    return pl.pallas_call(
        body, name=name, grid=(nl,),
        out_shape=jax.ShapeDtypeStruct((nl, nb, n6), F32),
        in_specs=[pl.BlockSpec((nb, d), lambda l: (0, 0)), pl.BlockSpec((1, d, n6), lambda l: (l, 0, 0)),
                  pl.BlockSpec((1, 1, n6), lambda l: (l, 0, 0))],
        out_specs=pl.BlockSpec((1, nb, n6), lambda l: (l, 0, 0)),
    )(c_all, ada_w, ada_b_loc)


def _ada_bwd(c_all_t, dmod_loc, dmod_all, name):
    nl, nb, n6 = dmod_loc.shape
    d = c_all_t.shape[0]
    n_all = dmod_all.shape[2]

    def body(c_ref, dl_ref, da_ref, gw_ref, gb_ref):
        cv = c_ref[...]
        gw_ref[0] = jnp.dot(cv * _sig(cv), dl_ref[0], preferred_element_type=F32)
        gb_ref[0] = jnp.sum(da_ref[0], axis=0, keepdims=True)

    return pl.pallas_call(
        body, name=name, grid=(nl,),
        out_shape=[jax.ShapeDtypeStruct((nl, d, n6), F32), jax.ShapeDtypeStruct((nl, 1, n_all), F32)],
        in_specs=[pl.BlockSpec((d, nb), lambda l: (0, 0)), pl.BlockSpec((1, nb, n6), lambda l: (l, 0, 0)),
                  pl.BlockSpec((1, nb, n_all), lambda l: (l, 0, 0))],
        out_specs=[pl.BlockSpec((1, d, n6), lambda l: (l, 0, 0)), pl.BlockSpec((1, 1, n_all), lambda l: (l, 0, 0))],
    )(c_all_t, dmod_loc, dmod_all)


def _adamw(w, g, m, v, name):
    shape = w.shape
    cols = shape[-1]
    rows = w.size // cols
    tr = _tile(rows, (512, 256, 128, 64, 32, 16, 8))
    flat = lambda a: a.reshape(rows, cols)

    def body(w_ref, g_ref, m_ref, v_ref, d_ref, mo_ref, vo_ref):
        gv = g_ref[...]
        mn = ADAM_B1 * m_ref[...] + (1.0 - ADAM_B1) * gv
        vn = ADAM_B2 * v_ref[...] + (1.0 - ADAM_B2) * (gv * gv)
        m_hat = mn / (1.0 - ADAM_B1 ** ADAM_STEP)
        v_hat = vn / (1.0 - ADAM_B2 ** ADAM_STEP)
        d_ref[...] = -ADAM_LR * (m_hat / (jnp.sqrt(v_hat) + ADAM_EPS) + ADAM_WD * w_ref[...])
        mo_ref[...] = mn
        vo_ref[...] = vn

    spec = pl.BlockSpec((tr, cols), lambda i: (i, 0))
    outs = pl.pallas_call(
        body, name=name, grid=(rows // tr,),
        out_shape=[jax.ShapeDtypeStruct((rows, cols), F32)] * 3,
        in_specs=[spec] * 4, out_specs=[spec] * 3,
    )(flat(w), flat(g), flat(m), flat(v))
    return tuple(o.reshape(shape) for o in outs)


def _pack_rows(parts, d):
    flat = jnp.concatenate([p.reshape(-1) for p in parts])
    rows = -(-flat.size // d)
    rows = -(-rows // 8) * 8
    return jnp.pad(flat, (0, rows * d - flat.size)).reshape(rows, d)


def kernel(x, c, ada_w, ada_b, pre_mix_g, post_mix_g, pre_ffn_g, post_ffn_g, w_in, conv_w, conv_b, conv_ln_g, conv_ln_b, w_conv_out, w_att_out, w_o, w_ffn_in, w_ffn_out, loss_target, m_ada_w, m_ada_b, m_pre_mix_g, m_post_mix_g, m_pre_ffn_g, m_post_ffn_g, m_w_in, m_conv_w, m_conv_b, m_conv_ln_g, m_conv_ln_b, m_w_conv_out, m_w_att_out, m_w_o, m_w_ffn_in, m_w_ffn_out, v_ada_w, v_ada_b, v_pre_mix_g, v_post_mix_g, v_pre_ffn_g, v_post_ffn_g, v_w_in, v_conv_w, v_conv_b, v_conv_ln_g, v_conv_ln_b, v_w_conv_out, v_w_att_out, v_w_o, v_w_ffn_in, v_w_ffn_out):
    nb, s, d = x.shape
    nl = ada_w.shape[0]
    cw = conv_b.shape[1]
    aw = w_att_out.shape[1]
    cl = conv_w.shape[2]
    n6 = ada_w.shape[2]
    t = nb * s
    me = _my_index()

    conv_flat = conv_w.reshape(-1)
    p1 = _pack_rows([c, conv_flat], d)
    r1 = p1.shape[0]
    (p1_all,) = _all_gather([p1], "gather_c_convw", True)
    p1_all = p1_all.reshape(N_DEV, r1 * d)
    c_all = p1_all[:, :nb * d].reshape(N_DEV * nb, d)
    conv_w_all = p1_all[:, nb * d:nb * d + conv_flat.size].reshape(N_DEV, nl, CONV_K, cl)
    conv_w_all = conv_w_all.transpose(1, 2, 0, 3).reshape(nl, CONV_K, cw)

    ada_b_loc = lax.dynamic_slice_in_dim(ada_b, me * n6, n6, axis=1).reshape(nl, 1, n6)
    mod_cols = _ada_fwd(c_all, ada_w, ada_b_loc, "ada_fwd")
    (mod_g,) = _all_gather([mod_cols.reshape(-1, d)], "gather_mod", True)
    mod_g = mod_g.reshape(N_DEV, nl, N_DEV * nb, n6)
    mod_mine = lax.dynamic_slice_in_dim(mod_g, me * nb, nb, axis=2)
    mod = mod_mine.transpose(1, 2, 0, 3).reshape(nl, nb, 6 * d)
    mods = [[mod[l, :, i * d:(i + 1) * d].reshape(nb, 1, d) for i in range(6)] for l in range(nl)]

    tr_ = lambda w: jnp.swapaxes(w, 0, 1).astype(BF16)
    shards = [[tr_(w_in[l]), tr_(w_conv_out[l]), tr_(w_att_out[l]), w_o[l].astype(BF16),
               tr_(w_ffn_in[l]), w_ffn_out[l].astype(BF16)] for l in range(nl)]
    weights = [list(_all_gather(shards[0][:3], "gather_weights", False))]

    vec = lambda a, l: a[l].reshape(1, -1)
    x2 = x.reshape(t, d)
    tgt = loss_target.reshape(t, d)

    saved = []
    xin = x2
    h = _norm_mod(xin, vec(pre_mix_g, 0), mods[0][1], mods[0][0], s, "norm_mod_0")
    for l in range(nl):
        win_t, wconv_t, watt_t = weights[l][:3]
        sh1, sc1, ga1, sh2, sc2, ga2 = mods[l]
        glu_in = _mm(h, win_t, NT, F32, f"proj_glu_{l}", 0, 2 * cw)
        qkv = _mm(h, win_t, NT, BF16, f"proj_qkv_{l}", 2 * cw, 3 * aw)
        gates = _mm(h, win_t, NT, BF16, f"proj_gates_{l}", 2 * cw + 3 * aw, 2 * d)
        u, cv, ua = _conv_fwd(glu_in, conv_w_all[l], vec(conv_b, l), vec(conv_ln_g, l), vec(conv_ln_b, l),
                              nb, s, f"conv_fwd_{l}")
        o, car, late = _sb4_fwd(qkv, nb, s, f"attn_fwd_{l}", shards[l][3:])
        weights[l] = weights[l] + late
        wo, wffn_in_t, wffn_out = late
        y_conv = _mm(ua, wconv_t, NT, BF16, f"conv_out_{l}")
        y_att = _mm(o, watt_t, NT, BF16, f"att_out_{l}")
        merged = _merge(y_conv, y_att, gates, s, f"merge_{l}")
        y = _mm(merged, wo, NN, F32, f"w_o_{l}")
        x1, h2 = _post_res_norm(xin, y, vec(post_mix_g, l), ga1, vec(pre_ffn_g, l), sc2, sh2, s, f"post_mix_{l}")
        if l + 1 < nl:
            f, early = _mm(h2, wffn_in_t, NT, BF16, f"ffn_in_{l}", gather=shards[l + 1][:3])
            weights.append(early)
        else:
            f = _mm(h2, wffn_in_t, NT, BF16, f"ffn_in_{l}")
        a = _swiglu(f, s, f"swiglu_{l}")
        y2 = _mm(a, wffn_out, NN, F32, f"ffn_out_{l}")
        saved.append(dict(xin=xin, h=h, glu_in=glu_in, gates=gates, u=u, cv=cv, ua=ua, qkv=qkv,
                          car=car, o=o, y_conv=y_conv, y_att=y_att, merged=merged, y=y, x1=x1, h2=h2, f=f,
                          a=a, y2=y2))
        if l + 1 < nl:
            nsh1, nsc1 = mods[l + 1][0], mods[l + 1][1]
            xin, h = _post_res_norm(x1, y2, vec(post_ffn_g, l), ga2, vec(pre_mix_g, l + 1), nsc1, nsh1, s,
                                    f"post_ffn_{l}")
        else:
            dx, loss_acc = _post_res_loss(x1, y2, vec(post_ffn_g, l), ga2, tgt, s, "loss")

    slots = [None] * (6 * nl)
    small = [None] * nl
    for l in reversed(range(nl)):
        win_t, wconv_t, watt_t, wo, wffn_in_t, wffn_out = weights[l]
        sh1, sc1, ga1, sh2, sc2, ga2 = mods[l]
        sv = saved[l]
        dy2, dg_post_ffn, dga2 = _post_bwd(dx, sv["y2"], vec(post_ffn_g, l), ga2, s, f"post_ffn_bwd_{l}")
        d_wffn_out = _mm(sv["a"], dy2, TN, BF16, f"d_w_ffn_out_{l}")
        da = _mm(dy2, wffn_out, NT, BF16, f"d_a_{l}")
        df = _swiglu_bwd(da, sv["f"], s, f"swiglu_bwd_{l}")
        d_wffn_in_t = _mm(df, sv["h2"], TN, BF16, f"d_w_ffn_in_{l}")
        dh2 = _mm(df, wffn_in_t, NN, F32, f"d_h2_{l}")
        dx1, dsh2, dsc2, dg_pre_ffn = _pre_bwd(dh2, sv["x1"], vec(pre_ffn_g, l), sc2, dx, s, f"pre_ffn_bwd_{l}")
        dy, dg_post_mix, dga1 = _post_bwd(dx1, sv["y"], vec(post_mix_g, l), ga1, s, f"post_mix_bwd_{l}")
        d_wo = _mm(sv["merged"], dy, TN, BF16, f"d_w_o_{l}")
        dmerged = _mm(dy, wo, NT, BF16, f"d_merged_{l}")
        dyc, dya, dgates = _merge_bwd(dmerged, sv["y_conv"], sv["y_att"], sv["gates"], s, f"merge_bwd_{l}")
        d_wconv_t = _mm(dyc, sv["ua"], TN, BF16, f"d_w_conv_out_{l}")
        dua = _mm(dyc, wconv_t, NN, F32, f"d_ua_{l}")
        d_watt_t = _mm(dya, sv["o"], TN, BF16, f"d_w_att_out_{l}")
        do = _mm(dya, watt_t, NN, BF16, f"d_o_{l}")
        ready = [(5, d_wffn_out), (4, d_wffn_in_t), (3, d_wo), (1, d_wconv_t), (2, d_watt_t)]
        dq, dk, dv, got = _sb4_bwd(sv["qkv"], do, sv["car"], nb, s, f"attn_bwd_{l}", [g for _, g in ready])
        for (gi, _), r in zip(ready, got):
            slots[6 * l + gi] = r
        dcv, dln_g, dln_b, dconv_b = _convln_bwd(dua, sv["cv"], vec(conv_ln_g, l), vec(conv_ln_b, l), s,
                                                 f"convln_bwd_{l}")
        dglu, dconv_w = _conv_bwd(dcv, sv["u"], sv["glu_in"], conv_w_all[l], nb, s, f"conv_bwd_{l}")
        dproj = jnp.concatenate([dglu, dq, dk, dv, dgates], axis=1)
        d_win_t = _mm(dproj, sv["h"], TN, BF16, f"d_w_in_{l}")
        dh, (slots[6 * l],) = _mm(dproj, win_t, NN, F32, f"d_h_{l}", exchange=[d_win_t])
        dx, dsh1, dsc1, dg_pre_mix = _pre_bwd(dh, sv["xin"], vec(pre_mix_g, l), sc1, dx1, s, f"pre_mix_bwd_{l}")
        dmod = jnp.concatenate([dsh1, dsc1, dga1, dsh2, dsc2, dga2], axis=2)
        small[l] = [dg_pre_mix, dg_post_mix, dg_pre_ffn, dg_post_ffn, dconv_b, dln_g, dln_b, dconv_w, dmod]
    grad_x = dx.reshape(nb, s, d)

    sizes = [a.size for a in small[0]]
    p3 = _pack_rows([a for l in range(nl) for a in small[l]] + [loss_acc[0, :1]], d)
    r3 = p3.shape[0]
    (p3_all,) = _all_gather([p3], "gather_small_grads", True)
    p3_all = p3_all.reshape(N_DEV, r3, d)
    p3_sum = _sum_slots(p3_all, "sum_small_grads").reshape(-1)
    per_layer = sum(sizes)
    loss = p3_sum[nl * per_layer]
    small_sum, dmod_all = [], []
    for l in range(nl):
        off, parts = l * per_layer, []
        for sz in sizes[:-1]:
            parts.append(p3_sum[off:off + sz])
            off += sz
        small_sum.append(parts)
        dm = p3_all.reshape(N_DEV, r3 * d)[:, off:off + sizes[-1]]
        dmod_all.append(dm.reshape(N_DEV * nb, 6 * d))
    dmod_all = jnp.stack(dmod_all)
    dmod_loc = lax.dynamic_slice_in_dim(dmod_all, me * n6, n6, axis=2)
    g_ada_w, g_ada_b = _ada_bwd(c_all.T, dmod_loc, dmod_all, "ada_bwd")
    g_ada_b = g_ada_b.reshape(nl, 6 * d)

    stack_small = lambda i, shape: jnp.stack([small_sum[l][i].reshape(shape) for l in range(nl)])
    g_pre_mix, g_post_mix, g_pre_ffn, g_post_ffn = (stack_small(i, (d,)) for i in range(4))
    g_conv_b, g_ln_g, g_ln_b = (stack_small(i, (cw,)) for i in range(4, 7))
    g_conv_w_all = stack_small(7, (HALO, cw))[:, :CONV_K]
    g_conv_w = lax.dynamic_slice_in_dim(g_conv_w_all, me * cl, cl, axis=2)

    sums = [_sum_slots(r, f"sum_grads_{i}") for i, r in enumerate(slots)]
    un_t = lambda i: jnp.stack([jnp.swapaxes(sums[6 * l + i], 0, 1) for l in range(nl)])
    keep = lambda i: jnp.stack([sums[6 * l + i] for l in range(nl)])
    g_w_in, g_w_conv_out, g_w_att_out, g_w_o, g_w_ffn_in, g_w_ffn_out = (
        un_t(0), un_t(1), un_t(2), keep(3), un_t(4), keep(5))

    grads = [g_ada_w, g_ada_b, g_pre_mix, g_post_mix, g_pre_ffn, g_post_ffn, g_w_in, g_conv_w, g_conv_b,
             g_ln_g, g_ln_b, g_w_conv_out, g_w_att_out, g_w_o, g_w_ffn_in, g_w_ffn_out]
    ws = [ada_w, ada_b, pre_mix_g, post_mix_g, pre_ffn_g, post_ffn_g, w_in, conv_w, conv_b, conv_ln_g,
          conv_ln_b, w_conv_out, w_att_out, w_o, w_ffn_in, w_ffn_out]
    ms = [m_ada_w, m_ada_b, m_pre_mix_g, m_post_mix_g, m_pre_ffn_g, m_post_ffn_g, m_w_in, m_conv_w, m_conv_b,
          m_conv_ln_g, m_conv_ln_b, m_w_conv_out, m_w_att_out, m_w_o, m_w_ffn_in, m_w_ffn_out]
    vs = [v_ada_w, v_ada_b, v_pre_mix_g, v_post_mix_g, v_pre_ffn_g, v_post_ffn_g, v_w_in, v_conv_w, v_conv_b,
          v_conv_ln_g, v_conv_ln_b, v_w_conv_out, v_w_att_out, v_w_o, v_w_ffn_in, v_w_ffn_out]
    deltas, new_m, new_v = [], [], []
    for i, (w, g, m, v) in enumerate(zip(ws, grads, ms, vs)):
        dlt, mn, vn = _adamw(w, g, m, v, f"adamw_{i}")
        deltas.append(dlt)
        new_m.append(mn)
        new_v.append(vn)
    return (loss, grad_x, *grads, *deltas, *new_m, *new_v)
```
